```python
import math
import jax, jax.numpy as jnp
from jax import lax
import numpy as np

D_MODEL = 2048
BATCH = 2
SEQ = 4096
DEPTH = 1

MEM_LEN = 256

MIX_WIDTH = D_MODEL
POOL_WIDTH = D_MODEL // 4
POOL_WINDOWS = (2, 4, 8, 16)
POOL_GROUP = POOL_WIDTH // len(POOL_WINDOWS)
SWA_HEAD_DIM = 64
SWA_WIDTH = D_MODEL // 2
SWA_HEADS = SWA_WIDTH // SWA_HEAD_DIM
SWA_KV_HEADS = 4
SWA_KV_WIDTH = SWA_KV_HEADS * SWA_HEAD_DIM
WINDOW = 128
BLOCK = 128
ROPE_THETA = 500000.0
ROPE_DIM = SWA_HEAD_DIM // 4
MEM_HEADS = 4
MEM_WIDTH = D_MODEL // 4
MEM_HEAD_DIM = MEM_WIDTH // MEM_HEADS
IN_WIDTH = POOL_WIDTH + SWA_WIDTH + 2 * SWA_KV_WIDTH + MEM_WIDTH

PEER_HEADS = 8
N_KEYS = 128
N_EXPERTS = N_KEYS * N_KEYS
PEER_TOPK = 16
QUERY_DIM = 256
HALF_DIM = QUERY_DIM // 2
PEER_CHUNK = 128

ALPHA = (2.0 * DEPTH) ** 0.25
BETA = (8.0 * DEPTH) ** -0.25
LN_EPS = 1e-5
NEG = -1e30

kernel_name = "hymba_pool_swa_mem_peer_deepnorm"


def layer_norm(x, g, b):
    xf = x.astype(jnp.float32)
    mu = xf.mean(-1, keepdims=True)
    var = jnp.square(xf - mu).mean(-1, keepdims=True)
    y = (xf - mu) * lax.rsqrt(var + LN_EPS) * g.astype(jnp.float32) + b.astype(jnp.float32)
    return y.astype(x.dtype)


def partial_rope(t, positions):
    half = ROPE_DIM // 2
    inv_freq = ROPE_THETA ** (-jnp.arange(0, ROPE_DIM, 2, dtype=jnp.float32) / ROPE_DIM)
    ang = positions.astype(jnp.float32)[:, :, None] * inv_freq
    cos = jnp.cos(ang)[:, :, None, :]
    sin = jnp.sin(ang)[:, :, None, :]
    tf = t.astype(jnp.float32)
    t1, t2, rest = tf[..., :half], tf[..., half:ROPE_DIM], tf[..., ROPE_DIM:]
    out = jnp.concatenate([t1 * cos - t2 * sin, t2 * cos + t1 * sin, rest], axis=-1)
    return out.astype(t.dtype)


def multiscale_pool(v, w_pool, pool_scale):
    B, S, C = v.shape
    vf = v.astype(jnp.float32)
    csum = jnp.cumsum(vf, axis=1)
    outs = []
    for g, w in enumerate(POOL_WINDOWS):
        sl = slice(g * POOL_GROUP, (g + 1) * POOL_GROUP)
        cg = csum[..., sl]
        shifted = jnp.pad(cg, ((0, 0), (w, 0), (0, 0)))[:, :S]
        count = jnp.minimum(jnp.arange(1, S + 1), w).astype(jnp.float32)[None, :, None]
        outs.append((cg - shifted) / count - vf[..., sl])
    pooled = jnp.stack(outs, axis=2).astype(v.dtype)
    y = jnp.einsum('bsgc,gcd->bsgd', pooled, w_pool).reshape(B, S, C)
    return y * pool_scale


def sliding_window_attention(q, k, v, sinks):
    B, S, H, Dh = q.shape
    KV = SWA_KV_HEADS
    G = H // KV
    nb = S // BLOCK
    qb = q.reshape(B, nb, BLOCK, KV, G, Dh)
    pad = jnp.zeros((B, BLOCK, KV, Dh), k.dtype)
    kp = jnp.concatenate([pad, k], axis=1).reshape(B, nb + 1, BLOCK, KV, Dh)
    vp = jnp.concatenate([pad, v], axis=1).reshape(B, nb + 1, BLOCK, KV, Dh)
    kb = jnp.concatenate([kp[:, :-1], kp[:, 1:]], axis=2)
    vb = jnp.concatenate([vp[:, :-1], vp[:, 1:]], axis=2)
    s = jnp.einsum('bnqkgd,bnskd->bnkgqs', qb, kb).astype(jnp.float32) * (Dh ** -0.5)
    blk = jnp.arange(nb)[:, None] * BLOCK
    q_pos = blk + jnp.arange(BLOCK)[None, :]
    k_pos = blk - BLOCK + jnp.arange(2 * BLOCK)[None, :]
    rel = q_pos[:, :, None] - k_pos[:, None, :]
    valid = (rel >= 0) & (rel < WINDOW) & (k_pos[:, None, :] >= 0)
    s = jnp.where(valid[None, :, None, None], s, NEG)
    sink = sinks.astype(jnp.float32).reshape(KV, G)[None, None, :, :, None, None]
    m = jnp.maximum(s.max(-1, keepdims=True), sink)
    p = jnp.exp(s - m)
    p = p / (p.sum(-1, keepdims=True) + jnp.exp(sink - m))
    o = jnp.einsum('bnkgqs,bnskd->bnqkgd', p.astype(v.dtype), vb)
    return o.reshape(B, S, H * Dh)


def memory_attention(qm, km, vm):
    B, S = qm.shape[:2]
    s = jnp.einsum('bshd,bmhd->bhsm', qm, km).astype(jnp.float32) * (MEM_HEAD_DIM ** -0.5)
    p = jax.nn.softmax(s, axis=-1).astype(vm.dtype)
    o = jnp.einsum('bhsm,bmhd->bshd', p, vm)
    return o.reshape(B, S, MEM_WIDTH)


def peer(x, w_peer_q, sub_keys_1, sub_keys_2, expert_u, expert_v):
    B, S, D = x.shape
    q = (x @ w_peer_q).reshape(B, S, PEER_HEADS, 2, HALF_DIM)
    s1 = jnp.einsum('bshc,kc->bshk', q[..., 0, :], sub_keys_1).astype(jnp.float32)
    s2 = jnp.einsum('bshc,kc->bshk', q[..., 1, :], sub_keys_2).astype(jnp.float32)
    v1, i1 = lax.top_k(s1, PEER_TOPK)
    v2, i2 = lax.top_k(s2, PEER_TOPK)
    cand_s = (v1[..., :, None] + v2[..., None, :]).reshape(B, S, PEER_HEADS, PEER_TOPK * PEER_TOPK)
    cand_i = (i1[..., :, None] * N_KEYS + i2[..., None, :]).reshape(B, S, PEER_HEADS, PEER_TOPK * PEER_TOPK)
    top_s, pos = lax.top_k(cand_s, PEER_TOPK)
    idx = jnp.take_along_axis(cand_i, pos, axis=-1)
    gate = jax.nn.softmax(top_s, axis=-1).astype(x.dtype)
    T = B * S
    nc = T // PEER_CHUNK
    E = PEER_HEADS * PEER_TOPK
    xc = x.reshape(nc, PEER_CHUNK, D)
    ic = idx.reshape(nc, PEER_CHUNK, E)
    gc = gate.reshape(nc, PEER_CHUNK, E)

    def block(args):
        xb, ib, gb = args
        u = expert_u[ib]
        h = jnp.einsum('cd,ced->ce', xb, u)
        a = gb * jax.nn.gelu(h, approximate=False)
        return jnp.einsum('ce,ced->cd', a, expert_v[ib])

    out = lax.map(block, (xc, ic, gc))
    return out.reshape(B, S, D)


def setup_inputs(seed: int = 0) -> dict:
    key = jax.random.key(seed)
    ks = jax.random.split(key, 20)
    f32 = jnp.float32
    L, D = DEPTH, D_MODEL
    nrm = lambda k, shape, scale: jax.random.normal(k, shape, f32) * scale
    return {
        "x": nrm(ks[0], (BATCH, SEQ, D), 1.0),
        "mem": nrm(ks[1], (BATCH, MEM_LEN, D), 1.0),
        "positions": jnp.broadcast_to(jnp.arange(SEQ, dtype=jnp.int32)[None, :], (BATCH, SEQ)),
        "w_in": nrm(ks[2], (L, D, IN_WIDTH), D ** -0.5),
        "w_mem_kv": nrm(ks[3], (L, D, 2 * MEM_WIDTH), D ** -0.5),
        "w_pool": nrm(ks[4], (L, len(POOL_WINDOWS), POOL_GROUP, POOL_GROUP), POOL_GROUP ** -0.5),
        "pool_scale": 1.0 + nrm(ks[5], (L, POOL_WIDTH), 0.02),
        "attn_sinks": nrm(ks[6], (L, SWA_HEADS), 0.5),
        "w_out": nrm(ks[7], (L, MIX_WIDTH, D), BETA * MIX_WIDTH ** -0.5),
        "ln1_g": 1.0 + nrm(ks[8], (L, D), 0.02),
        "ln1_b": nrm(ks[9], (L, D), 0.02),
        "w_peer_q": nrm(ks[10], (L, D, PEER_HEADS * QUERY_DIM), D ** -0.5),
        "sub_keys_1": nrm(ks[11], (L, N_KEYS, HALF_DIM), HALF_DIM ** -0.5),
        "sub_keys_2": nrm(ks[12], (L, N_KEYS, HALF_DIM), HALF_DIM ** -0.5),
        "expert_u": nrm(ks[13], (L, N_EXPERTS, D), D ** -0.5),
        "expert_v": nrm(ks[14], (L, N_EXPERTS, D), BETA),
        "ln2_g": 1.0 + nrm(ks[15], (L, D), 0.02),
        "ln2_b": nrm(ks[16], (L, D), 0.02),
    }


def reference(x, mem, positions, w_in, w_mem_kv, w_pool, pool_scale, attn_sinks, w_out,
              ln1_g, ln1_b, w_peer_q, sub_keys_1, sub_keys_2, expert_u, expert_v, ln2_g, ln2_b):
    B, S, D = x.shape
    offs = np.cumsum([POOL_WIDTH, SWA_WIDTH, SWA_KV_WIDTH, SWA_KV_WIDTH]).tolist()
    for l in range(DEPTH):
        h = x @ w_in[l]
        pool_v, q, k, v, qm = jnp.split(h, offs, axis=-1)
        pool_out = multiscale_pool(pool_v, w_pool[l], pool_scale[l])
        q = partial_rope(q.reshape(B, S, SWA_HEADS, SWA_HEAD_DIM), positions)
        k = partial_rope(k.reshape(B, S, SWA_KV_HEADS, SWA_HEAD_DIM), positions)
        v = v.reshape(B, S, SWA_KV_HEADS, SWA_HEAD_DIM)
        swa_out = sliding_window_attention(q, k, v, attn_sinks[l])
        kvm = mem @ w_mem_kv[l]
        km, vm = jnp.split(kvm, 2, axis=-1)
        Mn = mem.shape[1]
        mem_out = memory_attention(qm.reshape(B, S, MEM_HEADS, MEM_HEAD_DIM),
                                   km.reshape(B, Mn, MEM_HEADS, MEM_HEAD_DIM),
                                   vm.reshape(B, Mn, MEM_HEADS, MEM_HEAD_DIM))
        mix = jnp.concatenate([pool_out, swa_out, mem_out], axis=-1) @ w_out[l]
        x = layer_norm(ALPHA * x + mix, ln1_g[l], ln1_b[l])
        ff = peer(x, w_peer_q[l], sub_keys_1[l], sub_keys_2[l], expert_u[l], expert_v[l])
        x = layer_norm(ALPHA * x + ff, ln2_g[l], ln2_b[l])
    return x
```

```python
import functools
import math

import numpy as np
import jax
import jax.numpy as jnp
from jax import lax
from jax.experimental import pallas as pl
from jax.experimental.pallas import tpu as pltpu

F32 = jnp.float32
BF16 = jnp.bfloat16

LANES = 128
SUBLANES = 8
VMEM_LIMIT_BYTES = 56 * 1024 * 1024

POOL_WINDOWS = (2, 4, 8, 16)
POOL_GROUP = 128
POOL_HALO = 16
SWA_HEAD_DIM = 64
SWA_HEADS = 16
SWA_KV_HEADS = 4
SWA_BLOCK = 128
ROPE_THETA = 500000.0
ROPE_DIM = 16
MEM_HEADS = 4
MEM_HEAD_DIM = 128
PEER_HEADS = 8
N_KEYS = 128
PEER_TOPK = 16
HALF_DIM = 128
LN_EPS = 1e-5
NEG = -1e30


def _params(*semantics):
    return pltpu.CompilerParams(dimension_semantics=semantics, vmem_limit_bytes=VMEM_LIMIT_BYTES)


def _matmul_kernel(a_ref, b_ref, o_ref):
    o_ref[...] = jnp.dot(a_ref[...], b_ref[...], preferred_element_type=F32).astype(o_ref.dtype)


def _matmul(a, b, out_dtype, tm, tn):
    m, k = a.shape
    n = b.shape[1]
    tm, tn = min(tm, m), min(tn, n)
    return pl.pallas_call(
        _matmul_kernel,
        grid=(m // tm, n // tn),
        in_specs=[pl.BlockSpec((tm, k), lambda i, j: (i, 0)), pl.BlockSpec((k, tn), lambda i, j: (0, j))],
        out_specs=pl.BlockSpec((tm, tn), lambda i, j: (i, j)),
        out_shape=jax.ShapeDtypeStruct((m, n), out_dtype),
        compiler_params=_params("parallel", "arbitrary"),
        name="matmul",
    )(a, b)


def _pool_kernel(v_ref, w_ref, scale_ref, o_ref, ext_ref, *, ts):
    s = pl.program_id(1)

    @pl.when(s == 0)
    def _():
        ext_ref[0:POOL_HALO, :] = jnp.zeros((POOL_HALO, ext_ref.shape[1]), F32)

    ext_ref[POOL_HALO:POOL_HALO + ts, :] = v_ref[0]
    pos = s * ts + lax.broadcasted_iota(jnp.int32, (ts, 1), 0)
    for g, w in enumerate(POOL_WINDOWS):
        cols = slice(g * POOL_GROUP, (g + 1) * POOL_GROUP)
        acc = ext_ref[POOL_HALO:POOL_HALO + ts, cols]
        for k in range(1, w):
            acc = acc + ext_ref[POOL_HALO - k:POOL_HALO - k + ts, cols]
        count = jnp.minimum(pos + 1, w).astype(F32)
        pooled = acc / count - ext_ref[POOL_HALO:POOL_HALO + ts, cols]
        y = jnp.dot(pooled.astype(BF16), w_ref[g], preferred_element_type=F32)
        o_ref[0, :, cols] = (y * scale_ref[:, cols]).astype(o_ref.dtype)
    ext_ref[0:POOL_HALO, :] = ext_ref[ts:ts + POOL_HALO, :]


def _pool(h, w_pool, pool_scale, ts):
    b, s, _ = h.shape
    width = POOL_GROUP * len(POOL_WINDOWS)
    ts = min(ts, s)
    return pl.pallas_call(
        functools.partial(_pool_kernel, ts=ts),
        grid=(b, s // ts),
        in_specs=[
            pl.BlockSpec((1, ts, width), lambda i, j: (i, j, 0)),
            pl.BlockSpec(w_pool.shape, lambda i, j: (0, 0, 0)),
            pl.BlockSpec((1, width), lambda i, j: (0, 0)),
        ],
        out_specs=pl.BlockSpec((1, ts, width), lambda i, j: (i, j, 0)),
        out_shape=jax.ShapeDtypeStruct((b, s, width), BF16),
        scratch_shapes=[pltpu.VMEM((ts + POOL_HALO, width), F32)],
        compiler_params=_params("arbitrary", "arbitrary"),
        name="pool",
    )(h, w_pool, pool_scale)


def _rope_tables(pos_ref, freq_ref, sa_ref, sb_ref):
    ang = pos_ref[0].astype(F32) * freq_ref[...]
    c, s = jnp.cos(ang), jnp.sin(ang)
    return c, s * sa_ref[...], s * sb_ref[...]


def _rope(t, tables):
    c, sa, sb = tables
    half = ROPE_DIM // 2
    out = []
    for j in range(t.shape[1] // LANES):
        x = t[:, j * LANES:(j + 1) * LANES]
        out.append(x * c + pltpu.roll(x, LANES - half, 1) * sa + pltpu.roll(x, half, 1) * sb)
    return jnp.concatenate(out, axis=1) if len(out) > 1 else out[0]


def _swa_kernel(sink_ref, q0_ref, q1_ref, k_ref, v_ref, kp_ref, vp_ref, pos_ref, posp_ref,
                freq_ref, sa_ref, sb_ref, o_ref):
    n = pl.program_id(1)
    cur = _rope_tables(pos_ref, freq_ref, sa_ref, sb_ref)
    prev = _rope_tables(posp_ref, freq_ref, sa_ref, sb_ref)
    scale = SWA_HEAD_DIM ** -0.5
    q = jnp.concatenate([_rope(q0_ref[0], cur), _rope(q1_ref[0], cur)], axis=1)
    q = (q * scale).astype(BF16)
    k = jnp.concatenate([_rope(kp_ref[0], prev), _rope(k_ref[0], cur)], axis=0).astype(BF16)
    v = jnp.concatenate([vp_ref[0], v_ref[0]], axis=0).astype(BF16)
    row = lax.broadcasted_iota(jnp.int32, (SWA_BLOCK, 2 * SWA_BLOCK), 0)
    col = lax.broadcasted_iota(jnp.int32, (SWA_BLOCK, 2 * SWA_BLOCK), 1)
    rel = row + SWA_BLOCK - col
    valid = (rel >= 0) & (rel < SWA_BLOCK) & ((col >= SWA_BLOCK) | (n > 0))
    group = SWA_HEADS // SWA_KV_HEADS
    outs = []
    for hq in range(SWA_HEADS):
        kv = hq // group
        qh = q[:, hq * SWA_HEAD_DIM:(hq + 1) * SWA_HEAD_DIM]
        kh = k[:, kv * SWA_HEAD_DIM:(kv + 1) * SWA_HEAD_DIM]
        vh = v[:, kv * SWA_HEAD_DIM:(kv + 1) * SWA_HEAD_DIM]
        sc = lax.dot_general(qh, kh, (((1,), (1,)), ((), ())), preferred_element_type=F32)
        sc = jnp.where(valid, sc, NEG)
        sink = sink_ref[hq]
        m = jnp.maximum(jnp.max(sc, axis=1, keepdims=True), sink)
        p = jnp.exp(sc - m)
        denom = jnp.sum(p, axis=1, keepdims=True) + jnp.exp(sink - m)
        o = jnp.dot(p.astype(BF16), vh, preferred_element_type=F32)
        outs.append(o / denom)
    o_ref[0] = jnp.concatenate(outs, axis=1).astype(o_ref.dtype)


def _rope_constants():
    lane = np.arange(LANES)
    d = lane % SWA_HEAD_DIM
    half = ROPE_DIM // 2
    inv_freq = np.float32(ROPE_THETA) ** (-np.arange(0, ROPE_DIM, 2, dtype=np.float32) / np.float32(ROPE_DIM))
    freq = np.where(d < ROPE_DIM, inv_freq[d % half], 0.0).astype(np.float32)
    sa = np.where(d < half, -1.0, 0.0).astype(np.float32)
    sb = np.where((d >= half) & (d < ROPE_DIM), 1.0, 0.0).astype(np.float32)
    return [jnp.asarray(a.reshape(1, LANES)) for a in (freq, sa, sb)]


def _swa(h, positions, sinks):
    b, s, _ = h.shape
    nb = s // SWA_BLOCK
    pos3 = positions.reshape(b, s, 1)
    freq, sa, sb = _rope_constants()
    blk = lambda w, c: pl.BlockSpec((1, SWA_BLOCK, w), lambda i, j: (i, j, c))
    blk_prev = lambda w, c: pl.BlockSpec((1, SWA_BLOCK, w), lambda i, j: (i, jnp.maximum(j - 1, 0), c))
    const = pl.BlockSpec((1, LANES), lambda i, j: (0, 0))
    kvw = SWA_KV_HEADS * SWA_HEAD_DIM
    return pl.pallas_call(
        _swa_kernel,
        grid=(b, nb),
        in_specs=[
            pl.BlockSpec(memory_space=pltpu.SMEM),
            blk(512, 1), blk(512, 2), blk(kvw, 6), blk(kvw, 7), blk_prev(kvw, 6), blk_prev(kvw, 7),
            blk(1, 0), blk_prev(1, 0), const, const, const,
        ],
        out_specs=pl.BlockSpec((1, SWA_BLOCK, SWA_HEADS * SWA_HEAD_DIM), lambda i, j: (i, j, 0)),
        out_shape=jax.ShapeDtypeStruct((b, s, SWA_HEADS * SWA_HEAD_DIM), BF16),
        compiler_params=_params("parallel", "arbitrary"),
        name="swa",
    )(sinks, h, h, h, h, h, h, pos3, pos3, freq, sa, sb)


def _mem_kernel(q_ref, kv_ref, o_ref):
    scale = MEM_HEAD_DIM ** -0.5
    width = MEM_HEADS * MEM_HEAD_DIM
    for hm in range(MEM_HEADS):
        cols = slice(hm * MEM_HEAD_DIM, (hm + 1) * MEM_HEAD_DIM)
        q = (q_ref[0, :, cols] * scale).astype(BF16)
        km = kv_ref[0, :, cols]
        vm = kv_ref[0, :, width + hm * MEM_HEAD_DIM:width + (hm + 1) * MEM_HEAD_DIM]
        sc = lax.dot_general(q, km, (((1,), (1,)), ((), ())), preferred_element_type=F32)
        m = jnp.max(sc, axis=1, keepdims=True)
        p = jnp.exp(sc - m)
        denom = jnp.sum(p, axis=1, keepdims=True)
        o = jnp.dot(p.astype(BF16), vm, preferred_element_type=F32)
        o_ref[0, :, cols] = (o / denom).astype(o_ref.dtype)


def _mem_attention(h, kvm, tq):
    b, s, _ = h.shape
    width = MEM_HEADS * MEM_HEAD_DIM
    tq = min(tq, s)
    return pl.pallas_call(
        _mem_kernel,
        grid=(b, s // tq),
        in_specs=[
            pl.BlockSpec((1, tq, width), lambda i, j: (i, j, 4)),
            pl.BlockSpec((1,) + kvm.shape[1:], lambda i, j: (i, 0, 0)),
        ],
        out_specs=pl.BlockSpec((1, tq, width), lambda i, j: (i, j, 0)),
        out_shape=jax.ShapeDtypeStruct((b, s, width), BF16),
        compiler_params=_params("parallel", "arbitrary"),
        name="mem_attention",
    )(h, kvm)


def _outproj_kernel(pool_ref, swa_ref, mem_ref, wp_ref, ws_ref, wm_ref, x_ref, g_ref, b_ref, o_ref, *, alpha):
    mix = jnp.dot(pool_ref[...], wp_ref[...], preferred_element_type=F32)
    mix += jnp.dot(swa_ref[...], ws_ref[...], preferred_element_type=F32)
    mix += jnp.dot(mem_ref[...], wm_ref[...], preferred_element_type=F32)
    z = alpha * x_ref[...] + mix
    mu = jnp.mean(z, axis=1, keepdims=True)
    zc = z - mu
    var = jnp.mean(zc * zc, axis=1, keepdims=True)
    y = zc * lax.rsqrt(var + LN_EPS) * g_ref[...] + b_ref[...]
    o_ref[...] = y.T


def _outproj_ln(pool_o, swa_o, mem_o, w_out, x2, g, b, alpha, tm):
    t, d = x2.shape
    tm = min(tm, t)
    wp, ws, wm = pool_o.shape[1], swa_o.shape[1], mem_o.shape[1]
    w_p, w_s, w_m = w_out[:wp], w_out[wp:wp + ws], w_out[wp + ws:]
    row = lambda w: pl.BlockSpec((tm, w), lambda i: (i, 0))
    full = lambda a: pl.BlockSpec(a.shape, lambda i: (0, 0))
    return pl.pallas_call(
        functools.partial(_outproj_kernel, alpha=alpha),
        grid=(t // tm,),
        in_specs=[row(wp), row(ws), row(wm), full(w_p), full(w_s), full(w_m), row(d),
                  pl.BlockSpec((1, d), lambda i: (0, 0)), pl.BlockSpec((1, d), lambda i: (0, 0))],
        out_specs=pl.BlockSpec((d, tm), lambda i: (0, i)),
        out_shape=jax.ShapeDtypeStruct((d, t), F32),
        compiler_params=_params("parallel"),
        name="outproj_ln",
    )(pool_o, swa_o, mem_o, w_p, w_s, w_m, x2, g, b)


def _top16_rows(s):
    n = s.shape[0]
    iota = lax.broadcasted_iota(jnp.int32, s.shape, 0).astype(F32)
    rank = jnp.full(s.shape, float(PEER_TOPK), F32)
    vals = []
    for r in range(PEER_TOPK):
        m = jnp.max(s, axis=0, keepdims=True)
        idx = jnp.min(jnp.where(s == m, iota, float(n)), axis=0, keepdims=True)
        hit = iota == idx
        rank = jnp.where(hit, float(r), rank)
        s = jnp.where(hit, -jnp.inf, s)
        vals.append(m)
    return rank, vals


_CAND_SMALL_A = PEER_TOPK // 2
_CAND_ROWS = PEER_TOPK + (_CAND_SMALL_A - 1) * SUBLANES + SUBLANES


def _cand_constants():
    flat = np.full((_CAND_ROWS, 1), 1e9, np.float32)
    valid = np.zeros((_CAND_ROWS, 1), np.float32)
    for b in range(PEER_TOPK):
        flat[b, 0], valid[b, 0] = b, 1.0
    for a in range(1, _CAND_SMALL_A):
        base = PEER_TOPK + (a - 1) * SUBLANES
        for b in range(PEER_TOPK // (a + 1)):
            flat[base + b, 0], valid[base + b, 0] = a * PEER_TOPK + b, 1.0
    base = PEER_TOPK + (_CAND_SMALL_A - 1) * SUBLANES
    for k in range(SUBLANES):
        flat[base + k, 0], valid[base + k, 0] = (_CAND_SMALL_A + k) * PEER_TOPK, 1.0
    return jnp.asarray(flat), jnp.asarray(valid)


def _select_experts(s1, s2, flat, valid):
    t = s1.shape[1]
    rank1, v1 = _top16_rows(s1)
    rank2, v2 = _top16_rows(s2)
    v2_lo = jnp.concatenate(v2[:SUBLANES], axis=0)
    v2_all = jnp.concatenate(v2, axis=0)
    v1_hi = jnp.concatenate(v1[_CAND_SMALL_A:], axis=0)
    groups = [v1[0] + v2_all]
    for a in range(1, _CAND_SMALL_A):
        groups.append(v1[a] + v2_lo)
    groups.append(v1_hi + v2[0])
    cand = jnp.concatenate(groups, axis=0)
    cand = jnp.where(valid > 0.5, cand, -jnp.inf)
    flat_b = jnp.broadcast_to(flat, cand.shape)
    hits = jnp.zeros(cand.shape, F32)
    top = []
    for r in range(PEER_TOPK):
        m = jnp.max(cand, axis=0, keepdims=True)
        pick = jnp.min(jnp.where(cand == m, flat_b, 2e9), axis=0, keepdims=True)
        hit = flat_b == pick
        hits = jnp.where(hit, 1.0, hits)
        cand = jnp.where(hit, -jnp.inf, cand)
        top.append(m)
    z = jnp.ones((1, t), F32)
    for r in range(1, PEER_TOPK):
        z = z + jnp.exp(top[r] - top[0])
    counts = [jnp.sum(hits[0:PEER_TOPK], axis=0, keepdims=True)]
    for a in range(1, _CAND_SMALL_A):
        base = PEER_TOPK + (a - 1) * SUBLANES
        counts.append(jnp.sum(hits[base:base + SUBLANES], axis=0, keepdims=True))
    base = PEER_TOPK + (_CAND_SMALL_A - 1) * SUBLANES
    for k in range(SUBLANES):
        counts.append(hits[base + k:base + k + 1])
    lim = jnp.zeros(s1.shape, F32)
    for a in range(PEER_TOPK):
        lim = jnp.where(rank1 == float(a), counts[a], lim)
    e1n = jnp.exp(s1 - v1[0]) / z
    e2 = jnp.exp(s2 - v2[0])
    return lim, e1n, rank2, e2


def _retrieve_kernel(wq_ref, x_ref, k1_ref, k2_ref, flat_ref, valid_ref,
                     lim_ref, e1_ref, r2_ref, e2_ref, q_ref, *, tq):
    q_ref[...] = jnp.dot(wq_ref[...], x_ref[...].astype(BF16), preferred_element_type=F32)
    flat, valid = flat_ref[...], valid_ref[...]

    def head(h, carry):
        for c in range(tq // LANES):
            lanes = slice(c * LANES, (c + 1) * LANES)
            r0 = pl.multiple_of(h * 2 * HALF_DIM, 2 * HALF_DIM)
            q1 = q_ref[pl.ds(r0, HALF_DIM), lanes].astype(BF16)
            q2 = q_ref[pl.ds(r0 + HALF_DIM, HALF_DIM), lanes].astype(BF16)
            s1 = jnp.dot(k1_ref[...], q1, preferred_element_type=F32)
            s2 = jnp.dot(k2_ref[...], q2, preferred_element_type=F32)
            lim, e1n, rank2, e2 = _select_experts(s1, s2, flat, valid)
            lim_ref[h, :, lanes] = lim
            e1_ref[h, :, lanes] = e1n
            r2_ref[h, :, lanes] = rank2
            e2_ref[h, :, lanes] = e2
        return carry

    lax.fori_loop(0, PEER_HEADS, head, 0)


def _retrieve(x1t, wq_t, k1, k2, tq):
    d, t = x1t.shape
    tq = min(tq, t)
    flat, valid = _cand_constants()
    full = lambda a: pl.BlockSpec(a.shape, lambda i: (0,) * a.ndim)
    out = jax.ShapeDtypeStruct((PEER_HEADS, N_KEYS, t), F32)
    out_spec = pl.BlockSpec((PEER_HEADS, N_KEYS, tq), lambda i: (0, 0, i))
    return pl.pallas_call(
        functools.partial(_retrieve_kernel, tq=tq),
        grid=(t // tq,),
        in_specs=[full(wq_t), pl.BlockSpec((d, tq), lambda i: (0, i)), full(k1), full(k2), full(flat), full(valid)],
        out_specs=[out_spec] * 4,
        out_shape=[out] * 4,
        scratch_shapes=[pltpu.VMEM((wq_t.shape[0], tq), F32)],
        compiler_params=_params("parallel"),
        name="peer_retrieve",
    )(wq_t, x1t, k1, k2, flat, valid)


def _gelu_exact(x):
    return 0.5 * x * (1.0 + lax.erf(x * (1.0 / math.sqrt(2.0))))


def _experts_kernel(x_ref, u_ref, vt_ref, lim_ref, e1_ref, r2_ref, e2_ref, g_ref, b_ref, o_ref,
                    xb_ref, h_ref, a_ref, *, alpha, tm, te):
    e = pl.program_id(1)
    keys_per_tile = te // N_KEYS

    @pl.when(e == 0)
    def _():
        xb_ref[...] = x_ref[...].astype(BF16)
        o_ref[...] = jnp.zeros(o_ref.shape, F32)

    h_ref[...] = jnp.dot(u_ref[...], xb_ref[...], preferred_element_type=F32)

    def first_key(i, carry):
        rows = pl.ds(pl.multiple_of(i * N_KEYS, N_KEYS), N_KEYS)
        for c in range(tm // LANES):
            lanes = slice(c * LANES, (c + 1) * LANES)
            gate = jnp.zeros((N_KEYS, LANES), F32)
            for hd in range(PEER_HEADS):
                lim = lim_ref[i, hd:hd + 1, lanes]
                e1 = e1_ref[i, hd:hd + 1, lanes]
                gate = gate + jnp.where(r2_ref[hd, :, lanes] < lim, e2_ref[hd, :, lanes] * e1, 0.0)
            act = _gelu_exact(h_ref[rows, lanes])
            a_ref[rows, lanes] = (gate * act).astype(BF16)
        return carry

    lax.fori_loop(0, keys_per_tile, first_key, 0)
    o_ref[...] += jnp.dot(vt_ref[...], a_ref[...], preferred_element_type=F32)

    @pl.when(e == pl.num_programs(1) - 1)
    def _():
        z = alpha * x_ref[...] + o_ref[...]
        mu = jnp.mean(z, axis=0, keepdims=True)
        zc = z - mu
        var = jnp.mean(zc * zc, axis=0, keepdims=True)
        o_ref[...] = zc * lax.rsqrt(var + LN_EPS) * g_ref[...] + b_ref[...]


def _experts(x1t, u, vt, sel, g, b, alpha, tm, te):
    d, t = x1t.shape
    n_exp = u.shape[0]
    tm, te = min(tm, t), min(te, n_exp)
    tok = pl.BlockSpec((d, tm), lambda i, j: (0, i))
    sel_spec = pl.BlockSpec((PEER_HEADS, N_KEYS, tm), lambda i, j: (0, 0, i))
    key_spec = pl.BlockSpec((te // N_KEYS, PEER_HEADS, tm), lambda i, j: (j, 0, i))
    col = pl.BlockSpec((d, 1), lambda i, j: (0, 0))
    lim, e1n, rank2, e2 = sel
    sel = (lim.transpose(1, 0, 2), e1n.transpose(1, 0, 2), rank2, e2)
    return pl.pallas_call(
        functools.partial(_experts_kernel, alpha=alpha, tm=tm, te=te),
        grid=(t // tm, n_exp // te),
        in_specs=[tok, pl.BlockSpec((te, d), lambda i, j: (j, 0)), pl.BlockSpec((d, te), lambda i, j: (0, j)),
                  key_spec, key_spec, sel_spec, sel_spec, col, col],
        out_specs=tok,
        out_shape=jax.ShapeDtypeStruct((d, t), F32),
        scratch_shapes=[pltpu.VMEM((d, tm), BF16), pltpu.VMEM((te, tm), F32), pltpu.VMEM((te, tm), BF16)],
        compiler_params=_params("parallel", "arbitrary"),
        name="peer_experts",
    )(x1t, u, vt, *sel, g, b)


def kernel(x, mem, positions, w_in, w_mem_kv, w_pool, pool_scale, attn_sinks, w_out, ln1_g, ln1_b,
           w_peer_q, sub_keys_1, sub_keys_2, expert_u, expert_v, ln2_g, ln2_b):
    bsz, seq, d = x.shape
    depth = w_in.shape[0]
    t = bsz * seq
    alpha = (2.0 * depth) ** 0.25
    for l in range(depth):
        x2 = x.reshape(t, d)
        h = _matmul(x2.astype(BF16), w_in[l].astype(BF16), F32, 1024, 512).reshape(bsz, seq, -1)
        mem2 = mem.reshape(-1, d).astype(BF16)
        kvm = _matmul(mem2, w_mem_kv[l].astype(BF16), BF16, 512, 512).reshape(bsz, mem.shape[1], -1)
        pool_o = _pool(h, w_pool[l].astype(BF16), pool_scale[l].reshape(1, -1), 512)
        swa_o = _swa(h, positions, attn_sinks[l])
        mem_o = _mem_attention(h, kvm, 512)
        x1t = _outproj_ln(pool_o.reshape(t, -1), swa_o.reshape(t, -1), mem_o.reshape(t, -1),
                          w_out[l].astype(BF16), x2, ln1_g[l].reshape(1, d), ln1_b[l].reshape(1, d), alpha, 512)
        sel = _retrieve(x1t, w_peer_q[l].T.astype(BF16), sub_keys_1[l].astype(BF16),
                        sub_keys_2[l].astype(BF16), 256)
        yt = _experts(x1t, expert_u[l].astype(BF16), expert_v[l].T.astype(BF16), sel,
                      ln2_g[l].reshape(d, 1), ln2_b[l].reshape(d, 1), alpha, 512, 512)
        x = yt.T.reshape(bsz, seq, d)
    return x
```

```python
import functools
import math

import numpy as np
import jax
import jax.numpy as jnp
from jax import lax
from jax.experimental import pallas as pl
from jax.experimental.pallas import tpu as pltpu

F32 = jnp.float32
BF16 = jnp.bfloat16

LANES = 128
SUBLANES = 8
VMEM_LIMIT_BYTES = 56 * 1024 * 1024

POOL_WINDOWS = (2, 4, 8, 16)
POOL_GROUP = 128
POOL_HALO = 16
SWA_HEAD_DIM = 64
SWA_HEADS = 16
SWA_KV_HEADS = 4
SWA_BLOCK = 128
ROPE_THETA = 500000.0
ROPE_DIM = 16
MEM_HEADS = 4
MEM_HEAD_DIM = 128
PEER_HEADS = 8
N_KEYS = 128
PEER_TOPK = 16
HALF_DIM = 128
LN_EPS = 1e-5
NEG = -1e30

EXPERT_CHUNK = 256
TOKEN_HALF = 512
GATE_ROWS = 16


def _params(*semantics):
    return pltpu.CompilerParams(dimension_semantics=semantics, vmem_limit_bytes=VMEM_LIMIT_BYTES)


def _matmul_kernel(a_ref, b_ref, o_ref):
    o_ref[...] = jnp.dot(a_ref[...], b_ref[...], preferred_element_type=F32).astype(o_ref.dtype)


def _matmul(a, b, out_dtype, tm, tn):
    m, k = a.shape
    n = b.shape[1]
    tm, tn = min(tm, m), min(tn, n)
    return pl.pallas_call(
        _matmul_kernel,
        grid=(m // tm, n // tn),
        in_specs=[pl.BlockSpec((tm, k), lambda i, j: (i, 0)), pl.BlockSpec((k, tn), lambda i, j: (0, j))],
        out_specs=pl.BlockSpec((tm, tn), lambda i, j: (i, j)),
        out_shape=jax.ShapeDtypeStruct((m, n), out_dtype),
        compiler_params=_params("parallel", "arbitrary"),
        name="matmul",
    )(a, b)


def _pool_kernel(v_ref, w_ref, scale_ref, o_ref, ext_ref, *, ts):
    s = pl.program_id(1)

    @pl.when(s == 0)
    def _():
        ext_ref[0:POOL_HALO, :] = jnp.zeros((POOL_HALO, ext_ref.shape[1]), F32)

    ext_ref[POOL_HALO:POOL_HALO + ts, :] = v_ref[0]
    pos = s * ts + lax.broadcasted_iota(jnp.int32, (ts, 1), 0)
    for g, w in enumerate(POOL_WINDOWS):
        cols = slice(g * POOL_GROUP, (g + 1) * POOL_GROUP)
        acc = ext_ref[POOL_HALO:POOL_HALO + ts, cols]
        for k in range(1, w):
            acc = acc + ext_ref[POOL_HALO - k:POOL_HALO - k + ts, cols]
        count = jnp.minimum(pos + 1, w).astype(F32)
        pooled = acc / count - ext_ref[POOL_HALO:POOL_HALO + ts, cols]
        y = jnp.dot(pooled.astype(BF16), w_ref[g], preferred_element_type=F32)
        o_ref[0, :, cols] = (y * scale_ref[:, cols]).astype(o_ref.dtype)
    ext_ref[0:POOL_HALO, :] = ext_ref[ts:ts + POOL_HALO, :]


def _pool(h, w_pool, pool_scale, ts):
    b, s, _ = h.shape
    width = POOL_GROUP * len(POOL_WINDOWS)
    ts = min(ts, s)
    return pl.pallas_call(
        functools.partial(_pool_kernel, ts=ts),
        grid=(b, s // ts),
        in_specs=[
            pl.BlockSpec((1, ts, width), lambda i, j: (i, j, 0)),
            pl.BlockSpec(w_pool.shape, lambda i, j: (0, 0, 0)),
            pl.BlockSpec((1, width), lambda i, j: (0, 0)),
        ],
        out_specs=pl.BlockSpec((1, ts, width), lambda i, j: (i, j, 0)),
        out_shape=jax.ShapeDtypeStruct((b, s, width), BF16),
        scratch_shapes=[pltpu.VMEM((ts + POOL_HALO, width), F32)],
        compiler_params=_params("arbitrary", "arbitrary"),
        name="pool",
    )(h, w_pool, pool_scale)


def _rope_tables(pos_ref, freq_ref, sa_ref, sb_ref):
    ang = pos_ref[0].astype(F32) * freq_ref[...]
    c, s = jnp.cos(ang), jnp.sin(ang)
    return c, s * sa_ref[...], s * sb_ref[...]


def _rope(t, tables):
    c, sa, sb = tables
    half = ROPE_DIM // 2
    out = []
    for j in range(t.shape[1] // LANES):
        x = t[:, j * LANES:(j + 1) * LANES]
        out.append(x * c + pltpu.roll(x, LANES - half, 1) * sa + pltpu.roll(x, half, 1) * sb)
    return jnp.concatenate(out, axis=1) if len(out) > 1 else out[0]


def _swa_kernel(sink_ref, q0_ref, q1_ref, k_ref, v_ref, kp_ref, vp_ref, pos_ref, posp_ref,
                freq_ref, sa_ref, sb_ref, o_ref):
    n = pl.program_id(1)
    cur = _rope_tables(pos_ref, freq_ref, sa_ref, sb_ref)
    prev = _rope_tables(posp_ref, freq_ref, sa_ref, sb_ref)
    scale = SWA_HEAD_DIM ** -0.5
    q = jnp.concatenate([_rope(q0_ref[0], cur), _rope(q1_ref[0], cur)], axis=1)
    q = (q * scale).astype(BF16)
    k = jnp.concatenate([_rope(kp_ref[0], prev), _rope(k_ref[0], cur)], axis=0).astype(BF16)
    v = jnp.concatenate([vp_ref[0], v_ref[0]], axis=0).astype(BF16)
    row = lax.broadcasted_iota(jnp.int32, (SWA_BLOCK, 2 * SWA_BLOCK), 0)
    col = lax.broadcasted_iota(jnp.int32, (SWA_BLOCK, 2 * SWA_BLOCK), 1)
    rel = row + SWA_BLOCK - col
    valid = (rel >= 0) & (rel < SWA_BLOCK) & ((col >= SWA_BLOCK) | (n > 0))
    group = SWA_HEADS // SWA_KV_HEADS
    outs = []
    for hq in range(SWA_HEADS):
        kv = hq // group
        qh = q[:, hq * SWA_HEAD_DIM:(hq + 1) * SWA_HEAD_DIM]
        kh = k[:, kv * SWA_HEAD_DIM:(kv + 1) * SWA_HEAD_DIM]
        vh = v[:, kv * SWA_HEAD_DIM:(kv + 1) * SWA_HEAD_DIM]
        sc = lax.dot_general(qh, kh, (((1,), (1,)), ((), ())), preferred_element_type=F32)
        sc = jnp.where(valid, sc, NEG)
        sink = sink_ref[hq]
        m = jnp.maximum(jnp.max(sc, axis=1, keepdims=True), sink)
        p = jnp.exp(sc - m)
        denom = jnp.sum(p, axis=1, keepdims=True) + jnp.exp(sink - m)
        o = jnp.dot(p.astype(BF16), vh, preferred_element_type=F32)
        outs.append(o / denom)
    o_ref[0] = jnp.concatenate(outs, axis=1).astype(o_ref.dtype)


def _rope_constants():
    lane = np.arange(LANES)
    d = lane % SWA_HEAD_DIM
    half = ROPE_DIM // 2
    inv_freq = np.float32(ROPE_THETA) ** (-np.arange(0, ROPE_DIM, 2, dtype=np.float32) / np.float32(ROPE_DIM))
    freq = np.where(d < ROPE_DIM, inv_freq[d % half], 0.0).astype(np.float32)
    sa = np.where(d < half, -1.0, 0.0).astype(np.float32)
    sb = np.where((d >= half) & (d < ROPE_DIM), 1.0, 0.0).astype(np.float32)
    return [jnp.asarray(a.reshape(1, LANES)) for a in (freq, sa, sb)]


def _swa(h, positions, sinks):
    b, s, _ = h.shape
    nb = s // SWA_BLOCK
    pos3 = positions.reshape(b, s, 1)
    freq, sa, sb = _rope_constants()
    blk = lambda w, c: pl.BlockSpec((1, SWA_BLOCK, w), lambda i, j: (i, j, c))
    blk_prev = lambda w, c: pl.BlockSpec((1, SWA_BLOCK, w), lambda i, j: (i, jnp.maximum(j - 1, 0), c))
    const = pl.BlockSpec((1, LANES), lambda i, j: (0, 0))
    kvw = SWA_KV_HEADS * SWA_HEAD_DIM
    return pl.pallas_call(
        _swa_kernel,
        grid=(b, nb),
        in_specs=[
            pl.BlockSpec(memory_space=pltpu.SMEM),
            blk(512, 1), blk(512, 2), blk(kvw, 6), blk(kvw, 7), blk_prev(kvw, 6), blk_prev(kvw, 7),
            blk(1, 0), blk_prev(1, 0), const, const, const,
        ],
        out_specs=pl.BlockSpec((1, SWA_BLOCK, SWA_HEADS * SWA_HEAD_DIM), lambda i, j: (i, j, 0)),
        out_shape=jax.ShapeDtypeStruct((b, s, SWA_HEADS * SWA_HEAD_DIM), BF16),
        compiler_params=_params("parallel", "arbitrary"),
        name="swa",
    )(sinks, h, h, h, h, h, h, pos3, pos3, freq, sa, sb)


def _mem_kernel(q_ref, kv_ref, o_ref):
    scale = MEM_HEAD_DIM ** -0.5
    width = MEM_HEADS * MEM_HEAD_DIM
    for hm in range(MEM_HEADS):
        cols = slice(hm * MEM_HEAD_DIM, (hm + 1) * MEM_HEAD_DIM)
        q = (q_ref[0, :, cols] * scale).astype(BF16)
        km = kv_ref[0, :, cols]
        vm = kv_ref[0, :, width + hm * MEM_HEAD_DIM:width + (hm + 1) * MEM_HEAD_DIM]
        sc = lax.dot_general(q, km, (((1,), (1,)), ((), ())), preferred_element_type=F32)
        m = jnp.max(sc, axis=1, keepdims=True)
        p = jnp.exp(sc - m)
        denom = jnp.sum(p, axis=1, keepdims=True)
        o = jnp.dot(p.astype(BF16), vm, preferred_element_type=F32)
        o_ref[0, :, cols] = (o / denom).astype(o_ref.dtype)


def _mem_attention(h, kvm, tq):
    b, s, _ = h.shape
    width = MEM_HEADS * MEM_HEAD_DIM
    tq = min(tq, s)
    return pl.pallas_call(
        _mem_kernel,
        grid=(b, s // tq),
        in_specs=[
            pl.BlockSpec((1, tq, width), lambda i, j: (i, j, 4)),
            pl.BlockSpec((1,) + kvm.shape[1:], lambda i, j: (i, 0, 0)),
        ],
        out_specs=pl.BlockSpec((1, tq, width), lambda i, j: (i, j, 0)),
        out_shape=jax.ShapeDtypeStruct((b, s, width), BF16),
        compiler_params=_params("parallel", "arbitrary"),
        name="mem_attention",
    )(h, kvm)


def _outproj_kernel(pool_ref, swa_ref, mem_ref, wp_ref, ws_ref, wm_ref, x_ref, g_ref, b_ref, o_ref, *, alpha):
    mix = jnp.dot(pool_ref[...], wp_ref[...], preferred_element_type=F32)
    mix += jnp.dot(swa_ref[...], ws_ref[...], preferred_element_type=F32)
    mix += jnp.dot(mem_ref[...], wm_ref[...], preferred_element_type=F32)
    z = alpha * x_ref[...] + mix
    mu = jnp.mean(z, axis=1, keepdims=True)
    zc = z - mu
    var = jnp.mean(zc * zc, axis=1, keepdims=True)
    y = zc * lax.rsqrt(var + LN_EPS) * g_ref[...] + b_ref[...]
    o_ref[...] = y.T


def _outproj_ln(pool_o, swa_o, mem_o, w_out, x2, g, b, alpha, tm):
    t, d = x2.shape
    tm = min(tm, t)
    wp, ws, wm = pool_o.shape[1], swa_o.shape[1], mem_o.shape[1]
    w_p, w_s, w_m = w_out[:wp], w_out[wp:wp + ws], w_out[wp + ws:]
    row = lambda w: pl.BlockSpec((tm, w), lambda i: (i, 0))
    full = lambda a: pl.BlockSpec(a.shape, lambda i: (0, 0))
    return pl.pallas_call(
        functools.partial(_outproj_kernel, alpha=alpha),
        grid=(t // tm,),
        in_specs=[row(wp), row(ws), row(wm), full(w_p), full(w_s), full(w_m), row(d),
                  pl.BlockSpec((1, d), lambda i: (0, 0)), pl.BlockSpec((1, d), lambda i: (0, 0))],
        out_specs=pl.BlockSpec((d, tm), lambda i: (0, i)),
        out_shape=jax.ShapeDtypeStruct((d, t), F32),
        compiler_params=_params("parallel"),
        name="outproj_ln",
    )(pool_o, swa_o, mem_o, w_p, w_s, w_m, x2, g, b)


def _top16_rows(s):
    n = s.shape[0]
    iota = lax.broadcasted_iota(jnp.int32, s.shape, 0).astype(F32)
    rank = jnp.full(s.shape, float(PEER_TOPK), F32)
    vals = []
    for r in range(PEER_TOPK):
        m = jnp.max(s, axis=0, keepdims=True)
        idx = jnp.min(jnp.where(s == m, iota, float(n)), axis=0, keepdims=True)
        hit = iota == idx
        rank = jnp.where(hit, float(r), rank)
        s = jnp.where(hit, -jnp.inf, s)
        vals.append(m)
    return rank, vals


_CAND_SMALL_A = PEER_TOPK // 2
_CAND_ROWS = PEER_TOPK + (_CAND_SMALL_A - 1) * SUBLANES + SUBLANES


def _cand_constants():
    flat = np.full((_CAND_ROWS, 1), 1e9, np.float32)
    valid = np.zeros((_CAND_ROWS, 1), np.float32)
    for b in range(PEER_TOPK):
        flat[b, 0], valid[b, 0] = b, 1.0
    for a in range(1, _CAND_SMALL_A):
        base = PEER_TOPK + (a - 1) * SUBLANES
        for b in range(PEER_TOPK // (a + 1)):
            flat[base + b, 0], valid[base + b, 0] = a * PEER_TOPK + b, 1.0
    base = PEER_TOPK + (_CAND_SMALL_A - 1) * SUBLANES
    for k in range(SUBLANES):
        flat[base + k, 0], valid[base + k, 0] = (_CAND_SMALL_A + k) * PEER_TOPK, 1.0
    return jnp.asarray(flat), jnp.asarray(valid)


def _select_experts(s1, s2, flat, valid):
    t = s1.shape[1]
    rank1, v1 = _top16_rows(s1)
    rank2, v2 = _top16_rows(s2)
    v2_lo = jnp.concatenate(v2[:SUBLANES], axis=0)
    v2_all = jnp.concatenate(v2, axis=0)
    v1_hi = jnp.concatenate(v1[_CAND_SMALL_A:], axis=0)
    groups = [v1[0] + v2_all]
    for a in range(1, _CAND_SMALL_A):
        groups.append(v1[a] + v2_lo)
    groups.append(v1_hi + v2[0])
    cand = jnp.concatenate(groups, axis=0)
    cand = jnp.where(valid > 0.5, cand, -jnp.inf)
    flat_b = jnp.broadcast_to(flat, cand.shape)
    hits = jnp.zeros(cand.shape, F32)
    top = []
    for r in range(PEER_TOPK):
        m = jnp.max(cand, axis=0, keepdims=True)
        pick = jnp.min(jnp.where(cand == m, flat_b, 2e9), axis=0, keepdims=True)
        hit = flat_b == pick
        hits = jnp.where(hit, 1.0, hits)
        cand = jnp.where(hit, -jnp.inf, cand)
        top.append(m)
    z = jnp.ones((1, t), F32)
    for r in range(1, PEER_TOPK):
        z = z + jnp.exp(top[r] - top[0])
    counts = [jnp.sum(hits[0:PEER_TOPK], axis=0, keepdims=True)]
    for a in range(1, _CAND_SMALL_A):
        base = PEER_TOPK + (a - 1) * SUBLANES
        counts.append(jnp.sum(hits[base:base + SUBLANES], axis=0, keepdims=True))
    base = PEER_TOPK + (_CAND_SMALL_A - 1) * SUBLANES
    for k in range(SUBLANES):
        counts.append(hits[base + k:base + k + 1])
    lim = jnp.zeros(s1.shape, F32)
    for a in range(PEER_TOPK):
        lim = jnp.where(rank1 == float(a), counts[a], lim)
    e1n = jnp.exp(s1 - v1[0]) / z
    e2 = jnp.exp(s2 - v2[0])
    return lim, e1n, rank2, e2


def _retrieve_kernel(wq_ref, x_ref, k1_ref, k2_ref, flat_ref, valid_ref,
                     lim_ref, e1_ref, r2_ref, e2_ref, q_ref, *, tq):
    q_ref[...] = jnp.dot(wq_ref[...], x_ref[...].astype(BF16), preferred_element_type=F32)
    flat, valid = flat_ref[...], valid_ref[...]

    def head(h, carry):
        for c in range(tq // LANES):
            lanes = slice(c * LANES, (c + 1) * LANES)
            r0 = pl.multiple_of(h * 2 * HALF_DIM, 2 * HALF_DIM)
            q1 = q_ref[pl.ds(r0, HALF_DIM), lanes].astype(BF16)
            q2 = q_ref[pl.ds(r0 + HALF_DIM, HALF_DIM), lanes].astype(BF16)
            s1 = jnp.dot(k1_ref[...], q1, preferred_element_type=F32)
            s2 = jnp.dot(k2_ref[...], q2, preferred_element_type=F32)
            lim, e1n, rank2, e2 = _select_experts(s1, s2, flat, valid)
            lim_ref[h, :, lanes] = lim
            e1_ref[h, :, lanes] = e1n
            r2_ref[h, :, lanes] = rank2.astype(BF16)
            e2_ref[h, :, lanes] = e2.astype(BF16)
        return carry

    lax.fori_loop(0, PEER_HEADS, head, 0)


def _retrieve(x1t, wq_t, k1, k2, tq):
    d, t = x1t.shape
    tq = min(tq, t)
    flat, valid = _cand_constants()
    full = lambda a: pl.BlockSpec(a.shape, lambda i: (0,) * a.ndim)
    out = jax.ShapeDtypeStruct((PEER_HEADS, N_KEYS, t), F32)
    out_packed = jax.ShapeDtypeStruct((PEER_HEADS, N_KEYS, t), BF16)
    out_spec = pl.BlockSpec((PEER_HEADS, N_KEYS, tq), lambda i: (0, 0, i))
    return pl.pallas_call(
        functools.partial(_retrieve_kernel, tq=tq),
        grid=(t // tq,),
        in_specs=[full(wq_t), pl.BlockSpec((d, tq), lambda i: (0, i)), full(k1), full(k2), full(flat), full(valid)],
        out_specs=[out_spec] * 4,
        out_shape=[out, out, out_packed, out_packed],
        scratch_shapes=[pltpu.VMEM((wq_t.shape[0], tq), F32)],
        compiler_params=_params("parallel"),
        name="peer_retrieve",
    )(wq_t, x1t, k1, k2, flat, valid)


def _gelu_exact(x):
    return 0.5 * x * (1.0 + lax.erf(x * (1.0 / math.sqrt(2.0))))


def _experts_kernel(x_ref, u_ref, vt_ref, lim_ref, e1_ref, r2_in_ref, e2_in_ref, g_ref, b_ref, o_ref,
                    xb_ref, h_ref, a_ref, rows_ref, gate_ref, r2_ref, e2_ref, *, alpha, tm, te):
    e = pl.program_id(1)

    @pl.when(e == 0)
    def _():
        xb_ref[...] = x_ref[...].astype(BF16)
        o_ref[...] = jnp.zeros(o_ref.shape, F32)
        r2_ref[...] = r2_in_ref[...]
        e2_ref[...] = e2_in_ref[...]

    halves = tm // TOKEN_HALF
    n_items = (te // EXPERT_CHUNK) * halves

    def window(k):
        p, half = divmod(k, halves)
        return slice(p * EXPERT_CHUNK, (p + 1) * EXPERT_CHUNK), slice(half * TOKEN_HALF, (half + 1) * TOKEN_HALF)

    def hidden(k):
        chunk, cols = window(k)
        h_ref[chunk, cols] = jnp.dot(u_ref[chunk, :], xb_ref[:, cols], preferred_element_type=F32)

    def gates(k):
        chunk, cols = window(k)
        keys = range(chunk.start // N_KEYS, chunk.stop // N_KEYS)
        blocks = range(0, N_KEYS, GATE_ROWS)
        for i in keys:
            for hd in range(PEER_HEADS):
                for q, ref in enumerate((lim_ref, e1_ref)):
                    row = ref[i, hd:hd + 1, cols]
                    rows_ref[q, i, hd, :, cols] = jnp.broadcast_to(row, (GATE_ROWS, TOKEN_HALF)).astype(BF16)
        for c in range(cols.start // LANES, cols.stop // LANES):
            lanes = slice(c * LANES, (c + 1) * LANES)
            gate = {i: {jb: jnp.zeros((GATE_ROWS, LANES), BF16) for jb in blocks} for i in keys}
            for hd in range(PEER_HEADS):
                lim = {i: rows_ref[0, i, hd, :, lanes] for i in keys}
                e1 = {i: rows_ref[1, i, hd, :, lanes] for i in keys}
                for jb in blocks:
                    r2 = r2_ref[hd * N_KEYS + jb:hd * N_KEYS + jb + GATE_ROWS, lanes]
                    e2 = e2_ref[hd * N_KEYS + jb:hd * N_KEYS + jb + GATE_ROWS, lanes]
                    for i in keys:
                        gate[i][jb] = gate[i][jb] + jnp.where(r2 < lim[i], e2 * e1[i], jnp.zeros_like(e2))
            for i in keys:
                for jb in blocks:
                    gate_ref[i * N_KEYS + jb:i * N_KEYS + jb + GATE_ROWS, lanes] = gate[i][jb]

    def activate(k):
        chunk, cols = window(k)
        a_ref[chunk, cols] = gate_ref[chunk, cols] * _gelu_exact(h_ref[chunk, cols]).astype(BF16)

    def project(k):
        chunk, cols = window(k)
        o_ref[:, cols] += jnp.dot(vt_ref[:, chunk], a_ref[chunk, cols], preferred_element_type=F32)

    for stage in range(n_items + 2):
        if stage < n_items:
            hidden(stage)
            gates(stage)
        if 1 <= stage <= n_items:
            activate(stage - 1)
        if stage >= 2:
            project(stage - 2)

    @pl.when(e == pl.num_programs(1) - 1)
    def _():
        z = alpha * x_ref[...] + o_ref[...]
        mu = jnp.mean(z, axis=0, keepdims=True)
        zc = z - mu
        var = jnp.mean(zc * zc, axis=0, keepdims=True)
        o_ref[...] = zc * lax.rsqrt(var + LN_EPS) * g_ref[...] + b_ref[...]


def _experts(x1t, u, vt, sel, g, b, alpha, tm, te):
    d, t = x1t.shape
    n_exp = u.shape[0]
    tm, te = min(tm, t), min(te, n_exp)
    tok = pl.BlockSpec((d, tm), lambda i, j: (0, i))
    sel_spec = pl.BlockSpec((PEER_HEADS * N_KEYS, tm), lambda i, j: (0, i))
    key_spec = pl.BlockSpec((te // N_KEYS, PEER_HEADS, tm), lambda i, j: (j, 0, i))
    col = pl.BlockSpec((d, 1), lambda i, j: (0, 0))
    lim, e1n, rank2, e2 = sel
    sel = (lim.transpose(1, 0, 2), e1n.transpose(1, 0, 2), rank2.reshape(-1, t), e2.reshape(-1, t))
    return pl.pallas_call(
        functools.partial(_experts_kernel, alpha=alpha, tm=tm, te=te),
        grid=(t // tm, n_exp // te),
        in_specs=[tok, pl.BlockSpec((te, d), lambda i, j: (j, 0)), pl.BlockSpec((d, te), lambda i, j: (0, j)),
                  key_spec, key_spec, sel_spec, sel_spec, col, col],
        out_specs=tok,
        out_shape=jax.ShapeDtypeStruct((d, t), F32),
        scratch_shapes=[pltpu.VMEM((d, tm), BF16), pltpu.VMEM((te, tm), F32), pltpu.VMEM((te, tm), BF16),
                        pltpu.VMEM((2, te // N_KEYS, PEER_HEADS, GATE_ROWS, tm), BF16),
                        pltpu.VMEM((te, tm), BF16),
                        pltpu.VMEM((PEER_HEADS * N_KEYS, tm), BF16), pltpu.VMEM((PEER_HEADS * N_KEYS, tm), BF16)],
        compiler_params=_params("parallel", "arbitrary"),
        name="peer_experts",
    )(x1t, u, vt, *sel, g, b)


def kernel(x, mem, positions, w_in, w_mem_kv, w_pool, pool_scale, attn_sinks, w_out, ln1_g, ln1_b,
           w_peer_q, sub_keys_1, sub_keys_2, expert_u, expert_v, ln2_g, ln2_b):
    bsz, seq, d = x.shape
    depth = w_in.shape[0]
    t = bsz * seq
    alpha = (2.0 * depth) ** 0.25
    for l in range(depth):
        x2 = x.reshape(t, d)
        h = _matmul(x2.astype(BF16), w_in[l].astype(BF16), F32, 1024, 512).reshape(bsz, seq, -1)
        mem2 = mem.reshape(-1, d).astype(BF16)
        kvm = _matmul(mem2, w_mem_kv[l].astype(BF16), BF16, 512, 512).reshape(bsz, mem.shape[1], -1)
        pool_o = _pool(h, w_pool[l].astype(BF16), pool_scale[l].reshape(1, -1), 512)
        swa_o = _swa(h, positions, attn_sinks[l])
        mem_o = _mem_attention(h, kvm, 512)
        x1t = _outproj_ln(pool_o.reshape(t, -1), swa_o.reshape(t, -1), mem_o.reshape(t, -1),
                          w_out[l].astype(BF16), x2, ln1_g[l].reshape(1, d), ln1_b[l].reshape(1, d), alpha, 512)
        sel = _retrieve(x1t, w_peer_q[l].T.astype(BF16), sub_keys_1[l].astype(BF16),
                        sub_keys_2[l].astype(BF16), 256)
        yt = _experts(x1t, expert_u[l].astype(BF16), expert_v[l].T.astype(BF16), sel,
                      ln2_g[l].reshape(d, 1), ln2_b[l].reshape(d, 1), alpha, 512, 512)
        x = yt.T.reshape(bsz, seq, d)
    return x
```

```python
import functools
import math

import numpy as np
import jax
import jax.numpy as jnp
from jax import lax
from jax.experimental import pallas as pl
from jax.experimental.pallas import tpu as pltpu

F32 = jnp.float32
BF16 = jnp.bfloat16

LANES = 128
SUBLANES = 8
VMEM_LIMIT_BYTES = 56 * 1024 * 1024

POOL_WINDOWS = (2, 4, 8, 16)
POOL_GROUP = 128
POOL_HALO = 16
SWA_HEAD_DIM = 64
SWA_HEADS = 16
SWA_KV_HEADS = 4
SWA_BLOCK = 128
ROPE_THETA = 500000.0
ROPE_DIM = 16
MEM_HEADS = 4
MEM_HEAD_DIM = 128
PEER_HEADS = 8
N_KEYS = 128
PEER_TOPK = 16
HALF_DIM = 128
LN_EPS = 1e-5
NEG = -1e30

EXPERT_CHUNK = 256
TOKEN_HALF = 256
GATE_ROWS = 16


def _params(*semantics):
    return pltpu.CompilerParams(dimension_semantics=semantics, vmem_limit_bytes=VMEM_LIMIT_BYTES)


def _matmul_kernel(a_ref, b_ref, o_ref):
    o_ref[...] = jnp.dot(a_ref[...], b_ref[...], preferred_element_type=F32).astype(o_ref.dtype)


def _matmul(a, b, out_dtype, tm, tn):
    m, k = a.shape
    n = b.shape[1]
    tm, tn = min(tm, m), min(tn, n)
    return pl.pallas_call(
        _matmul_kernel,
        grid=(m // tm, n // tn),
        in_specs=[pl.BlockSpec((tm, k), lambda i, j: (i, 0)), pl.BlockSpec((k, tn), lambda i, j: (0, j))],
        out_specs=pl.BlockSpec((tm, tn), lambda i, j: (i, j)),
        out_shape=jax.ShapeDtypeStruct((m, n), out_dtype),
        compiler_params=_params("parallel", "arbitrary"),
        name="matmul",
    )(a, b)


def _pool_kernel(v_ref, w_ref, scale_ref, o_ref, ext_ref, *, ts):
    s = pl.program_id(1)

    @pl.when(s == 0)
    def _():
        ext_ref[0:POOL_HALO, :] = jnp.zeros((POOL_HALO, ext_ref.shape[1]), F32)

    ext_ref[POOL_HALO:POOL_HALO + ts, :] = v_ref[0]
    pos = s * ts + lax.broadcasted_iota(jnp.int32, (ts, 1), 0)
    for g, w in enumerate(POOL_WINDOWS):
        cols = slice(g * POOL_GROUP, (g + 1) * POOL_GROUP)
        acc = ext_ref[POOL_HALO:POOL_HALO + ts, cols]
        for k in range(1, w):
            acc = acc + ext_ref[POOL_HALO - k:POOL_HALO - k + ts, cols]
        count = jnp.minimum(pos + 1, w).astype(F32)
        pooled = acc / count - ext_ref[POOL_HALO:POOL_HALO + ts, cols]
        y = jnp.dot(pooled.astype(BF16), w_ref[g], preferred_element_type=F32)
        o_ref[0, :, cols] = (y * scale_ref[:, cols]).astype(o_ref.dtype)
    ext_ref[0:POOL_HALO, :] = ext_ref[ts:ts + POOL_HALO, :]


def _pool(h, w_pool, pool_scale, ts):
    b, s, _ = h.shape
    width = POOL_GROUP * len(POOL_WINDOWS)
    ts = min(ts, s)
    return pl.pallas_call(
        functools.partial(_pool_kernel, ts=ts),
        grid=(b, s // ts),
        in_specs=[
            pl.BlockSpec((1, ts, width), lambda i, j: (i, j, 0)),
            pl.BlockSpec(w_pool.shape, lambda i, j: (0, 0, 0)),
            pl.BlockSpec((1, width), lambda i, j: (0, 0)),
        ],
        out_specs=pl.BlockSpec((1, ts, width), lambda i, j: (i, j, 0)),
        out_shape=jax.ShapeDtypeStruct((b, s, width), BF16),
        scratch_shapes=[pltpu.VMEM((ts + POOL_HALO, width), F32)],
        compiler_params=_params("arbitrary", "arbitrary"),
        name="pool",
    )(h, w_pool, pool_scale)


def _rope_tables(pos_ref, freq_ref, sa_ref, sb_ref):
    ang = pos_ref[0].astype(F32) * freq_ref[...]
    c, s = jnp.cos(ang), jnp.sin(ang)
    return c, s * sa_ref[...], s * sb_ref[...]


def _rope(t, tables):
    c, sa, sb = tables
    half = ROPE_DIM // 2
    out = []
    for j in range(t.shape[1] // LANES):
        x = t[:, j * LANES:(j + 1) * LANES]
        out.append(x * c + pltpu.roll(x, LANES - half, 1) * sa + pltpu.roll(x, half, 1) * sb)
    return jnp.concatenate(out, axis=1) if len(out) > 1 else out[0]


def _swa_kernel(sink_ref, q0_ref, q1_ref, k_ref, v_ref, kp_ref, vp_ref, pos_ref, posp_ref,
                freq_ref, sa_ref, sb_ref, o_ref):
    n = pl.program_id(1)
    cur = _rope_tables(pos_ref, freq_ref, sa_ref, sb_ref)
    prev = _rope_tables(posp_ref, freq_ref, sa_ref, sb_ref)
    scale = SWA_HEAD_DIM ** -0.5
    q = jnp.concatenate([_rope(q0_ref[0], cur), _rope(q1_ref[0], cur)], axis=1)
    q = (q * scale).astype(BF16)
    k = jnp.concatenate([_rope(kp_ref[0], prev), _rope(k_ref[0], cur)], axis=0).astype(BF16)
    v = jnp.concatenate([vp_ref[0], v_ref[0]], axis=0).astype(BF16)
    row = lax.broadcasted_iota(jnp.int32, (SWA_BLOCK, 2 * SWA_BLOCK), 0)
    col = lax.broadcasted_iota(jnp.int32, (SWA_BLOCK, 2 * SWA_BLOCK), 1)
    rel = row + SWA_BLOCK - col
    valid = (rel >= 0) & (rel < SWA_BLOCK) & ((col >= SWA_BLOCK) | (n > 0))
    group = SWA_HEADS // SWA_KV_HEADS
    outs = []
    for hq in range(SWA_HEADS):
        kv = hq // group
        qh = q[:, hq * SWA_HEAD_DIM:(hq + 1) * SWA_HEAD_DIM]
        kh = k[:, kv * SWA_HEAD_DIM:(kv + 1) * SWA_HEAD_DIM]
        vh = v[:, kv * SWA_HEAD_DIM:(kv + 1) * SWA_HEAD_DIM]
        sc = lax.dot_general(qh, kh, (((1,), (1,)), ((), ())), preferred_element_type=F32)
        sc = jnp.where(valid, sc, NEG)
        sink = sink_ref[hq]
        m = jnp.maximum(jnp.max(sc, axis=1, keepdims=True), sink)
        p = jnp.exp(sc - m)
        denom = jnp.sum(p, axis=1, keepdims=True) + jnp.exp(sink - m)
        o = jnp.dot(p.astype(BF16), vh, preferred_element_type=F32)
        outs.append(o / denom)
    o_ref[0] = jnp.concatenate(outs, axis=1).astype(o_ref.dtype)


def _rope_constants():
    lane = np.arange(LANES)
    d = lane % SWA_HEAD_DIM
    half = ROPE_DIM // 2
    inv_freq = np.float32(ROPE_THETA) ** (-np.arange(0, ROPE_DIM, 2, dtype=np.float32) / np.float32(ROPE_DIM))
    freq = np.where(d < ROPE_DIM, inv_freq[d % half], 0.0).astype(np.float32)
    sa = np.where(d < half, -1.0, 0.0).astype(np.float32)
    sb = np.where((d >= half) & (d < ROPE_DIM), 1.0, 0.0).astype(np.float32)
    return [jnp.asarray(a.reshape(1, LANES)) for a in (freq, sa, sb)]


def _swa(h, positions, sinks):
    b, s, _ = h.shape
    nb = s // SWA_BLOCK
    pos3 = positions.reshape(b, s, 1)
    freq, sa, sb = _rope_constants()
    blk = lambda w, c: pl.BlockSpec((1, SWA_BLOCK, w), lambda i, j: (i, j, c))
    blk_prev = lambda w, c: pl.BlockSpec((1, SWA_BLOCK, w), lambda i, j: (i, jnp.maximum(j - 1, 0), c))
    const = pl.BlockSpec((1, LANES), lambda i, j: (0, 0))
    kvw = SWA_KV_HEADS * SWA_HEAD_DIM
    return pl.pallas_call(
        _swa_kernel,
        grid=(b, nb),
        in_specs=[
            pl.BlockSpec(memory_space=pltpu.SMEM),
            blk(512, 1), blk(512, 2), blk(kvw, 6), blk(kvw, 7), blk_prev(kvw, 6), blk_prev(kvw, 7),
            blk(1, 0), blk_prev(1, 0), const, const, const,
        ],
        out_specs=pl.BlockSpec((1, SWA_BLOCK, SWA_HEADS * SWA_HEAD_DIM), lambda i, j: (i, j, 0)),
        out_shape=jax.ShapeDtypeStruct((b, s, SWA_HEADS * SWA_HEAD_DIM), BF16),
        compiler_params=_params("parallel", "arbitrary"),
        name="swa",
    )(sinks, h, h, h, h, h, h, pos3, pos3, freq, sa, sb)


def _mem_kernel(q_ref, kv_ref, o_ref):
    scale = MEM_HEAD_DIM ** -0.5
    width = MEM_HEADS * MEM_HEAD_DIM
    for hm in range(MEM_HEADS):
        cols = slice(hm * MEM_HEAD_DIM, (hm + 1) * MEM_HEAD_DIM)
        q = (q_ref[0, :, cols] * scale).astype(BF16)
        km = kv_ref[0, :, cols]
        vm = kv_ref[0, :, width + hm * MEM_HEAD_DIM:width + (hm + 1) * MEM_HEAD_DIM]
        sc = lax.dot_general(q, km, (((1,), (1,)), ((), ())), preferred_element_type=F32)
        m = jnp.max(sc, axis=1, keepdims=True)
        p = jnp.exp(sc - m)
        denom = jnp.sum(p, axis=1, keepdims=True)
        o = jnp.dot(p.astype(BF16), vm, preferred_element_type=F32)
        o_ref[0, :, cols] = (o / denom).astype(o_ref.dtype)


def _mem_attention(h, kvm, tq):
    b, s, _ = h.shape
    width = MEM_HEADS * MEM_HEAD_DIM
    tq = min(tq, s)
    return pl.pallas_call(
        _mem_kernel,
        grid=(b, s // tq),
        in_specs=[
            pl.BlockSpec((1, tq, width), lambda i, j: (i, j, 4)),
            pl.BlockSpec((1,) + kvm.shape[1:], lambda i, j: (i, 0, 0)),
        ],
        out_specs=pl.BlockSpec((1, tq, width), lambda i, j: (i, j, 0)),
        out_shape=jax.ShapeDtypeStruct((b, s, width), BF16),
        compiler_params=_params("parallel", "arbitrary"),
        name="mem_attention",
    )(h, kvm)


def _outproj_kernel(pool_ref, swa_ref, mem_ref, wp_ref, ws_ref, wm_ref, x_ref, g_ref, b_ref, o_ref, *, alpha):
    mix = jnp.dot(pool_ref[...], wp_ref[...], preferred_element_type=F32)
    mix += jnp.dot(swa_ref[...], ws_ref[...], preferred_element_type=F32)
    mix += jnp.dot(mem_ref[...], wm_ref[...], preferred_element_type=F32)
    z = alpha * x_ref[...] + mix
    mu = jnp.mean(z, axis=1, keepdims=True)
    zc = z - mu
    var = jnp.mean(zc * zc, axis=1, keepdims=True)
    y = zc * lax.rsqrt(var + LN_EPS) * g_ref[...] + b_ref[...]
    o_ref[...] = y.T


def _outproj_ln(pool_o, swa_o, mem_o, w_out, x2, g, b, alpha, tm):
    t, d = x2.shape
    tm = min(tm, t)
    wp, ws, wm = pool_o.shape[1], swa_o.shape[1], mem_o.shape[1]
    w_p, w_s, w_m = w_out[:wp], w_out[wp:wp + ws], w_out[wp + ws:]
    row = lambda w: pl.BlockSpec((tm, w), lambda i: (i, 0))
    full = lambda a: pl.BlockSpec(a.shape, lambda i: (0, 0))
    return pl.pallas_call(
        functools.partial(_outproj_kernel, alpha=alpha),
        grid=(t // tm,),
        in_specs=[row(wp), row(ws), row(wm), full(w_p), full(w_s), full(w_m), row(d),
                  pl.BlockSpec((1, d), lambda i: (0, 0)), pl.BlockSpec((1, d), lambda i: (0, 0))],
        out_specs=pl.BlockSpec((d, tm), lambda i: (0, i)),
        out_shape=jax.ShapeDtypeStruct((d, t), F32),
        compiler_params=_params("parallel"),
        name="outproj_ln",
    )(pool_o, swa_o, mem_o, w_p, w_s, w_m, x2, g, b)


def _top16_rows(s):
    n = s.shape[0]
    iota = lax.broadcasted_iota(jnp.int32, s.shape, 0).astype(F32)
    rank = jnp.full(s.shape, float(PEER_TOPK), F32)
    vals = []
    for r in range(PEER_TOPK):
        m = jnp.max(s, axis=0, keepdims=True)
        idx = jnp.min(jnp.where(s == m, iota, float(n)), axis=0, keepdims=True)
        hit = iota == idx
        rank = jnp.where(hit, float(r), rank)
        s = jnp.where(hit, -jnp.inf, s)
        vals.append(m)
    return rank, vals


_CAND_SMALL_A = PEER_TOPK // 2
_CAND_ROWS = PEER_TOPK + (_CAND_SMALL_A - 1) * SUBLANES + SUBLANES


def _cand_constants():
    flat = np.full((_CAND_ROWS, 1), 1e9, np.float32)
    valid = np.zeros((_CAND_ROWS, 1), np.float32)
    for b in range(PEER_TOPK):
        flat[b, 0], valid[b, 0] = b, 1.0
    for a in range(1, _CAND_SMALL_A):
        base = PEER_TOPK + (a - 1) * SUBLANES
        for b in range(PEER_TOPK // (a + 1)):
            flat[base + b, 0], valid[base + b, 0] = a * PEER_TOPK + b, 1.0
    base = PEER_TOPK + (_CAND_SMALL_A - 1) * SUBLANES
    for k in range(SUBLANES):
        flat[base + k, 0], valid[base + k, 0] = (_CAND_SMALL_A + k) * PEER_TOPK, 1.0
    return jnp.asarray(flat), jnp.asarray(valid)


def _select_experts(s1, s2, flat, valid):
    t = s1.shape[1]
    rank1, v1 = _top16_rows(s1)
    rank2, v2 = _top16_rows(s2)
    v2_lo = jnp.concatenate(v2[:SUBLANES], axis=0)
    v2_all = jnp.concatenate(v2, axis=0)
    v1_hi = jnp.concatenate(v1[_CAND_SMALL_A:], axis=0)
    groups = [v1[0] + v2_all]
    for a in range(1, _CAND_SMALL_A):
        groups.append(v1[a] + v2_lo)
    groups.append(v1_hi + v2[0])
    cand = jnp.concatenate(groups, axis=0)
    cand = jnp.where(valid > 0.5, cand, -jnp.inf)
    flat_b = jnp.broadcast_to(flat, cand.shape)
    hits = jnp.zeros(cand.shape, F32)
    top = []
    for r in range(PEER_TOPK):
        m = jnp.max(cand, axis=0, keepdims=True)
        pick = jnp.min(jnp.where(cand == m, flat_b, 2e9), axis=0, keepdims=True)
        hit = flat_b == pick
        hits = jnp.where(hit, 1.0, hits)
        cand = jnp.where(hit, -jnp.inf, cand)
        top.append(m)
    z = jnp.ones((1, t), F32)
    for r in range(1, PEER_TOPK):
        z = z + jnp.exp(top[r] - top[0])
    counts = [jnp.sum(hits[0:PEER_TOPK], axis=0, keepdims=True)]
    for a in range(1, _CAND_SMALL_A):
        base = PEER_TOPK + (a - 1) * SUBLANES
        counts.append(jnp.sum(hits[base:base + SUBLANES], axis=0, keepdims=True))
    base = PEER_TOPK + (_CAND_SMALL_A - 1) * SUBLANES
    for k in range(SUBLANES):
        counts.append(hits[base + k:base + k + 1])
    lim = jnp.zeros(s1.shape, F32)
    for a in range(PEER_TOPK):
        lim = jnp.where(rank1 == float(a), counts[a], lim)
    e1n = jnp.exp(s1 - v1[0]) / z
    e2 = jnp.exp(s2 - v2[0])
    return lim, e1n, rank2, e2


def _retrieve_kernel(wq_ref, x_ref, k1_ref, k2_ref, flat_ref, valid_ref,
                     lim_ref, e1_ref, r2_ref, e2_ref, q_ref, *, tq):
    q_ref[...] = jnp.dot(wq_ref[...], x_ref[...].astype(BF16), preferred_element_type=F32)
    flat, valid = flat_ref[...], valid_ref[...]

    def head(h, carry):
        for c in range(tq // LANES):
            lanes = slice(c * LANES, (c + 1) * LANES)
            r0 = pl.multiple_of(h * 2 * HALF_DIM, 2 * HALF_DIM)
            q1 = q_ref[pl.ds(r0, HALF_DIM), lanes].astype(BF16)
            q2 = q_ref[pl.ds(r0 + HALF_DIM, HALF_DIM), lanes].astype(BF16)
            s1 = jnp.dot(k1_ref[...], q1, preferred_element_type=F32)
            s2 = jnp.dot(k2_ref[...], q2, preferred_element_type=F32)
            lim, e1n, rank2, e2 = _select_experts(s1, s2, flat, valid)
            lim_ref[h, :, lanes] = lim
            e1_ref[h, :, lanes] = e1n
            r2_ref[h, :, lanes] = rank2.astype(BF16)
            e2_ref[h, :, lanes] = e2.astype(BF16)
        return carry

    lax.fori_loop(0, PEER_HEADS, head, 0)


def _retrieve(x1t, wq_t, k1, k2, tq):
    d, t = x1t.shape
    tq = min(tq, t)
    flat, valid = _cand_constants()
    full = lambda a: pl.BlockSpec(a.shape, lambda i: (0,) * a.ndim)
    out = jax.ShapeDtypeStruct((PEER_HEADS, N_KEYS, t), F32)
    out_packed = jax.ShapeDtypeStruct((PEER_HEADS, N_KEYS, t), BF16)
    out_spec = pl.BlockSpec((PEER_HEADS, N_KEYS, tq), lambda i: (0, 0, i))
    return pl.pallas_call(
        functools.partial(_retrieve_kernel, tq=tq),
        grid=(t // tq,),
        in_specs=[full(wq_t), pl.BlockSpec((d, tq), lambda i: (0, i)), full(k1), full(k2), full(flat), full(valid)],
        out_specs=[out_spec] * 4,
        out_shape=[out, out, out_packed, out_packed],
        scratch_shapes=[pltpu.VMEM((wq_t.shape[0], tq), F32)],
        compiler_params=_params("parallel"),
        name="peer_retrieve",
    )(wq_t, x1t, k1, k2, flat, valid)


def _gelu_exact(x):
    return 0.5 * x * (1.0 + lax.erf(x * (1.0 / math.sqrt(2.0))))


def _experts_kernel(x_ref, u_ref, vt_ref, lim_ref, e1_ref, r2_in_ref, e2_in_ref, g_ref, b_ref, o_ref,
                    xb_ref, h_ref, a_ref, rows_ref, gate_ref, r2_ref, e2_ref, *, alpha, tm, te):
    e = pl.program_id(1)

    @pl.when(e == 0)
    def _():
        xb_ref[...] = x_ref[...].astype(BF16)
        o_ref[...] = jnp.zeros(o_ref.shape, F32)
        r2_ref[...] = r2_in_ref[...]
        e2_ref[...] = e2_in_ref[...]

    halves = tm // TOKEN_HALF
    n_items = (te // EXPERT_CHUNK) * halves

    def window(k):
        p, half = divmod(k, halves)
        return slice(p * EXPERT_CHUNK, (p + 1) * EXPERT_CHUNK), slice(half * TOKEN_HALF, (half + 1) * TOKEN_HALF)

    def hidden(k):
        chunk, cols = window(k)
        h_ref[chunk, cols] = jnp.dot(u_ref[chunk, :], xb_ref[:, cols], preferred_element_type=F32)

    def gates(k):
        chunk, cols = window(k)
        keys = range(chunk.start // N_KEYS, chunk.stop // N_KEYS)
        blocks = range(0, N_KEYS, GATE_ROWS)
        for i in keys:
            for hd in range(PEER_HEADS):
                for q, ref in enumerate((lim_ref, e1_ref)):
                    row = ref[i, hd:hd + 1, cols]
                    rows_ref[q, i, hd, :, cols] = jnp.broadcast_to(row, (GATE_ROWS, TOKEN_HALF)).astype(BF16)
        for c in range(cols.start // LANES, cols.stop // LANES):
            lanes = slice(c * LANES, (c + 1) * LANES)
            gate = {i: {jb: jnp.zeros((GATE_ROWS, LANES), BF16) for jb in blocks} for i in keys}
            for hd in range(PEER_HEADS):
                lim = {i: rows_ref[0, i, hd, :, lanes] for i in keys}
                e1 = {i: rows_ref[1, i, hd, :, lanes] for i in keys}
                for jb in blocks:
                    r2 = r2_ref[hd * N_KEYS + jb:hd * N_KEYS + jb + GATE_ROWS, lanes]
                    e2 = e2_ref[hd * N_KEYS + jb:hd * N_KEYS + jb + GATE_ROWS, lanes]
                    for i in keys:
                        gate[i][jb] = gate[i][jb] + jnp.where(r2 < lim[i], e2 * e1[i], jnp.zeros_like(e2))
            for i in keys:
                for jb in blocks:
                    gate_ref[i * N_KEYS + jb:i * N_KEYS + jb + GATE_ROWS, lanes] = gate[i][jb]

    def activate(k):
        chunk, cols = window(k)
        a_ref[chunk, cols] = gate_ref[chunk, cols] * _gelu_exact(h_ref[chunk, cols]).astype(BF16)

    def project(k):
        chunk, cols = window(k)
        o_ref[:, cols] += jnp.dot(vt_ref[:, chunk], a_ref[chunk, cols], preferred_element_type=F32)

    for stage in range(n_items + 2):
        if stage < n_items:
            hidden(stage)
            gates(stage)
        if 1 <= stage <= n_items:
            activate(stage - 1)
        if stage >= 2:
            project(stage - 2)

    @pl.when(e == pl.num_programs(1) - 1)
    def _():
        z = alpha * x_ref[...] + o_ref[...]
        mu = jnp.mean(z, axis=0, keepdims=True)
        zc = z - mu
        var = jnp.mean(zc * zc, axis=0, keepdims=True)
        o_ref[...] = zc * lax.rsqrt(var + LN_EPS) * g_ref[...] + b_ref[...]


def _experts(x1t, u, vt, sel, g, b, alpha, tm, te):
    d, t = x1t.shape
    n_exp = u.shape[0]
    tm, te = min(tm, t), min(te, n_exp)
    tok = pl.BlockSpec((d, tm), lambda i, j: (0, i))
    sel_spec = pl.BlockSpec((PEER_HEADS * N_KEYS, tm), lambda i, j: (0, i))
    key_spec = pl.BlockSpec((te // N_KEYS, PEER_HEADS, tm), lambda i, j: (j, 0, i))
    col = pl.BlockSpec((d, 1), lambda i, j: (0, 0))
    lim, e1n, rank2, e2 = sel
    sel = (lim.transpose(1, 0, 2), e1n.transpose(1, 0, 2), rank2.reshape(-1, t), e2.reshape(-1, t))
    return pl.pallas_call(
        functools.partial(_experts_kernel, alpha=alpha, tm=tm, te=te),
        grid=(t // tm, n_exp // te),
        in_specs=[tok, pl.BlockSpec((te, d), lambda i, j: (j, 0)), pl.BlockSpec((d, te), lambda i, j: (0, j)),
                  key_spec, key_spec, sel_spec, sel_spec, col, col],
        out_specs=tok,
        out_shape=jax.ShapeDtypeStruct((d, t), F32),
        scratch_shapes=[pltpu.VMEM((d, tm), BF16), pltpu.VMEM((te, tm), F32), pltpu.VMEM((te, tm), BF16),
                        pltpu.VMEM((2, te // N_KEYS, PEER_HEADS, GATE_ROWS, tm), BF16),
                        pltpu.VMEM((te, tm), BF16),
                        pltpu.VMEM((PEER_HEADS * N_KEYS, tm), BF16), pltpu.VMEM((PEER_HEADS * N_KEYS, tm), BF16)],
        compiler_params=_params("parallel", "arbitrary"),
        name="peer_experts",
    )(x1t, u, vt, *sel, g, b)


def kernel(x, mem, positions, w_in, w_mem_kv, w_pool, pool_scale, attn_sinks, w_out, ln1_g, ln1_b,
           w_peer_q, sub_keys_1, sub_keys_2, expert_u, expert_v, ln2_g, ln2_b):
    bsz, seq, d = x.shape
    depth = w_in.shape[0]
    t = bsz * seq
    alpha = (2.0 * depth) ** 0.25
    for l in range(depth):
        x2 = x.reshape(t, d)
        h = _matmul(x2.astype(BF16), w_in[l].astype(BF16), F32, 1024, 512).reshape(bsz, seq, -1)
        mem2 = mem.reshape(-1, d).astype(BF16)
        kvm = _matmul(mem2, w_mem_kv[l].astype(BF16), BF16, 512, 512).reshape(bsz, mem.shape[1], -1)
        pool_o = _pool(h, w_pool[l].astype(BF16), pool_scale[l].reshape(1, -1), 512)
        swa_o = _swa(h, positions, attn_sinks[l])
        mem_o = _mem_attention(h, kvm, 512)
        x1t = _outproj_ln(pool_o.reshape(t, -1), swa_o.reshape(t, -1), mem_o.reshape(t, -1),
                          w_out[l].astype(BF16), x2, ln1_g[l].reshape(1, d), ln1_b[l].reshape(1, d), alpha, 512)
        sel = _retrieve(x1t, w_peer_q[l].T.astype(BF16), sub_keys_1[l].astype(BF16),
                        sub_keys_2[l].astype(BF16), 256)
        yt = _experts(x1t, expert_u[l].astype(BF16), expert_v[l].T.astype(BF16), sel,
                      ln2_g[l].reshape(d, 1), ln2_b[l].reshape(d, 1), alpha, 256, 512)
        x = yt.T.reshape(bsz, seq, d)
    return x
```

```python
import functools
import math

import numpy as np
import jax
import jax.numpy as jnp
from jax import lax
from jax.experimental import pallas as pl
from jax.experimental.pallas import tpu as pltpu

F32 = jnp.float32
BF16 = jnp.bfloat16

LANES = 128
SUBLANES = 8
VMEM_LIMIT_BYTES = 56 * 1024 * 1024

POOL_WINDOWS = (2, 4, 8, 16)
POOL_GROUP = 128
POOL_HALO = 16
SWA_HEAD_DIM = 64
SWA_HEADS = 16
SWA_KV_HEADS = 4
SWA_BLOCK = 128
ROPE_THETA = 500000.0
ROPE_DIM = 16
MEM_HEADS = 4
MEM_HEAD_DIM = 128
PEER_HEADS = 8
N_KEYS = 128
PEER_TOPK = 16
HALF_DIM = 128
LN_EPS = 1e-5
NEG = -1e30

EXPERT_CHUNK = 256
GATE_ROWS = 16


def _params(*semantics):
    return pltpu.CompilerParams(dimension_semantics=semantics, vmem_limit_bytes=VMEM_LIMIT_BYTES)


def _matmul_kernel(a_ref, b_ref, o_ref):
    o_ref[...] = jnp.dot(a_ref[...], b_ref[...], preferred_element_type=F32).astype(o_ref.dtype)


def _matmul(a, b, out_dtype, tm, tn):
    m, k = a.shape
    n = b.shape[1]
    tm, tn = min(tm, m), min(tn, n)
    return pl.pallas_call(
        _matmul_kernel,
        grid=(m // tm, n // tn),
        in_specs=[pl.BlockSpec((tm, k), lambda i, j: (i, 0)), pl.BlockSpec((k, tn), lambda i, j: (0, j))],
        out_specs=pl.BlockSpec((tm, tn), lambda i, j: (i, j)),
        out_shape=jax.ShapeDtypeStruct((m, n), out_dtype),
        compiler_params=_params("parallel", "arbitrary"),
        name="matmul",
    )(a, b)


def _pool_kernel(v_ref, w_ref, scale_ref, o_ref, ext_ref, *, ts):
    s = pl.program_id(1)

    @pl.when(s == 0)
    def _():
        ext_ref[0:POOL_HALO, :] = jnp.zeros((POOL_HALO, ext_ref.shape[1]), F32)

    ext_ref[POOL_HALO:POOL_HALO + ts, :] = v_ref[0]
    pos = s * ts + lax.broadcasted_iota(jnp.int32, (ts, 1), 0)
    for g, w in enumerate(POOL_WINDOWS):
        cols = slice(g * POOL_GROUP, (g + 1) * POOL_GROUP)
        acc = ext_ref[POOL_HALO:POOL_HALO + ts, cols]
        for k in range(1, w):
            acc = acc + ext_ref[POOL_HALO - k:POOL_HALO - k + ts, cols]
        count = jnp.minimum(pos + 1, w).astype(F32)
        pooled = acc / count - ext_ref[POOL_HALO:POOL_HALO + ts, cols]
        y = jnp.dot(pooled.astype(BF16), w_ref[g], preferred_element_type=F32)
        o_ref[0, :, cols] = (y * scale_ref[:, cols]).astype(o_ref.dtype)
    ext_ref[0:POOL_HALO, :] = ext_ref[ts:ts + POOL_HALO, :]


def _pool(h, w_pool, pool_scale, ts):
    b, s, _ = h.shape
    width = POOL_GROUP * len(POOL_WINDOWS)
    ts = min(ts, s)
    return pl.pallas_call(
        functools.partial(_pool_kernel, ts=ts),
        grid=(b, s // ts),
        in_specs=[
            pl.BlockSpec((1, ts, width), lambda i, j: (i, j, 0)),
            pl.BlockSpec(w_pool.shape, lambda i, j: (0, 0, 0)),
            pl.BlockSpec((1, width), lambda i, j: (0, 0)),
        ],
        out_specs=pl.BlockSpec((1, ts, width), lambda i, j: (i, j, 0)),
        out_shape=jax.ShapeDtypeStruct((b, s, width), BF16),
        scratch_shapes=[pltpu.VMEM((ts + POOL_HALO, width), F32)],
        compiler_params=_params("arbitrary", "arbitrary"),
        name="pool",
    )(h, w_pool, pool_scale)


def _rope_tables(pos_ref, freq_ref, sa_ref, sb_ref):
    ang = pos_ref[0].astype(F32) * freq_ref[...]
    c, s = jnp.cos(ang), jnp.sin(ang)
    return c, s * sa_ref[...], s * sb_ref[...]


def _rope(t, tables):
    c, sa, sb = tables
    half = ROPE_DIM // 2
    out = []
    for j in range(t.shape[1] // LANES):
        x = t[:, j * LANES:(j + 1) * LANES]
        out.append(x * c + pltpu.roll(x, LANES - half, 1) * sa + pltpu.roll(x, half, 1) * sb)
    return jnp.concatenate(out, axis=1) if len(out) > 1 else out[0]


def _swa_kernel(sink_ref, q0_ref, q1_ref, k_ref, v_ref, kp_ref, vp_ref, pos_ref, posp_ref,
                freq_ref, sa_ref, sb_ref, o_ref):
    n = pl.program_id(1)
    cur = _rope_tables(pos_ref, freq_ref, sa_ref, sb_ref)
    prev = _rope_tables(posp_ref, freq_ref, sa_ref, sb_ref)
    scale = SWA_HEAD_DIM ** -0.5
    q = jnp.concatenate([_rope(q0_ref[0], cur), _rope(q1_ref[0], cur)], axis=1)
    q = (q * scale).astype(BF16)
    k = jnp.concatenate([_rope(kp_ref[0], prev), _rope(k_ref[0], cur)], axis=0).astype(BF16)
    v = jnp.concatenate([vp_ref[0], v_ref[0]], axis=0).astype(BF16)
    row = lax.broadcasted_iota(jnp.int32, (SWA_BLOCK, 2 * SWA_BLOCK), 0)
    col = lax.broadcasted_iota(jnp.int32, (SWA_BLOCK, 2 * SWA_BLOCK), 1)
    rel = row + SWA_BLOCK - col
    valid = (rel >= 0) & (rel < SWA_BLOCK) & ((col >= SWA_BLOCK) | (n > 0))
    group = SWA_HEADS // SWA_KV_HEADS
    outs = []
    for hq in range(SWA_HEADS):
        kv = hq // group
        qh = q[:, hq * SWA_HEAD_DIM:(hq + 1) * SWA_HEAD_DIM]
        kh = k[:, kv * SWA_HEAD_DIM:(kv + 1) * SWA_HEAD_DIM]
        vh = v[:, kv * SWA_HEAD_DIM:(kv + 1) * SWA_HEAD_DIM]
        sc = lax.dot_general(qh, kh, (((1,), (1,)), ((), ())), preferred_element_type=F32)
        sc = jnp.where(valid, sc, NEG)
        sink = sink_ref[hq]
        m = jnp.maximum(jnp.max(sc, axis=1, keepdims=True), sink)
        p = jnp.exp(sc - m)
        denom = jnp.sum(p, axis=1, keepdims=True) + jnp.exp(sink - m)
        o = jnp.dot(p.astype(BF16), vh, preferred_element_type=F32)
        outs.append(o / denom)
    o_ref[0] = jnp.concatenate(outs, axis=1).astype(o_ref.dtype)


def _rope_constants():
    lane = np.arange(LANES)
    d = lane % SWA_HEAD_DIM
    half = ROPE_DIM // 2
    inv_freq = np.float32(ROPE_THETA) ** (-np.arange(0, ROPE_DIM, 2, dtype=np.float32) / np.float32(ROPE_DIM))
    freq = np.where(d < ROPE_DIM, inv_freq[d % half], 0.0).astype(np.float32)
    sa = np.where(d < half, -1.0, 0.0).astype(np.float32)
    sb = np.where((d >= half) & (d < ROPE_DIM), 1.0, 0.0).astype(np.float32)
    return [jnp.asarray(a.reshape(1, LANES)) for a in (freq, sa, sb)]


def _swa(h, positions, sinks):
    b, s, _ = h.shape
    nb = s // SWA_BLOCK
    pos3 = positions.reshape(b, s, 1)
    freq, sa, sb = _rope_constants()
    blk = lambda w, c: pl.BlockSpec((1, SWA_BLOCK, w), lambda i, j: (i, j, c))
    blk_prev = lambda w, c: pl.BlockSpec((1, SWA_BLOCK, w), lambda i, j: (i, jnp.maximum(j - 1, 0), c))
    const = pl.BlockSpec((1, LANES), lambda i, j: (0, 0))
    kvw = SWA_KV_HEADS * SWA_HEAD_DIM
    return pl.pallas_call(
        _swa_kernel,
        grid=(b, nb),
        in_specs=[
            pl.BlockSpec(memory_space=pltpu.SMEM),
            blk(512, 1), blk(512, 2), blk(kvw, 6), blk(kvw, 7), blk_prev(kvw, 6), blk_prev(kvw, 7),
            blk(1, 0), blk_prev(1, 0), const, const, const,
        ],
        out_specs=pl.BlockSpec((1, SWA_BLOCK, SWA_HEADS * SWA_HEAD_DIM), lambda i, j: (i, j, 0)),
        out_shape=jax.ShapeDtypeStruct((b, s, SWA_HEADS * SWA_HEAD_DIM), BF16),
        compiler_params=_params("parallel", "arbitrary"),
        name="swa",
    )(sinks, h, h, h, h, h, h, pos3, pos3, freq, sa, sb)


def _mem_kernel(q_ref, kv_ref, o_ref):
    scale = MEM_HEAD_DIM ** -0.5
    width = MEM_HEADS * MEM_HEAD_DIM
    for hm in range(MEM_HEADS):
        cols = slice(hm * MEM_HEAD_DIM, (hm + 1) * MEM_HEAD_DIM)
        q = (q_ref[0, :, cols] * scale).astype(BF16)
        km = kv_ref[0, :, cols]
        vm = kv_ref[0, :, width + hm * MEM_HEAD_DIM:width + (hm + 1) * MEM_HEAD_DIM]
        sc = lax.dot_general(q, km, (((1,), (1,)), ((), ())), preferred_element_type=F32)
        m = jnp.max(sc, axis=1, keepdims=True)
        p = jnp.exp(sc - m)
        denom = jnp.sum(p, axis=1, keepdims=True)
        o = jnp.dot(p.astype(BF16), vm, preferred_element_type=F32)
        o_ref[0, :, cols] = (o / denom).astype(o_ref.dtype)


def _mem_attention(h, kvm, tq):
    b, s, _ = h.shape
    width = MEM_HEADS * MEM_HEAD_DIM
    tq = min(tq, s)
    return pl.pallas_call(
        _mem_kernel,
        grid=(b, s // tq),
        in_specs=[
            pl.BlockSpec((1, tq, width), lambda i, j: (i, j, 4)),
            pl.BlockSpec((1,) + kvm.shape[1:], lambda i, j: (i, 0, 0)),
        ],
        out_specs=pl.BlockSpec((1, tq, width), lambda i, j: (i, j, 0)),
        out_shape=jax.ShapeDtypeStruct((b, s, width), BF16),
        compiler_params=_params("parallel", "arbitrary"),
        name="mem_attention",
    )(h, kvm)


def _outproj_kernel(pool_ref, swa_ref, mem_ref, wp_ref, ws_ref, wm_ref, x_ref, g_ref, b_ref, o_ref, ob_ref,
                    *, alpha):
    mix = jnp.dot(pool_ref[...], wp_ref[...], preferred_element_type=F32)
    mix += jnp.dot(swa_ref[...], ws_ref[...], preferred_element_type=F32)
    mix += jnp.dot(mem_ref[...], wm_ref[...], preferred_element_type=F32)
    z = alpha * x_ref[...] + mix
    mu = jnp.mean(z, axis=1, keepdims=True)
    zc = z - mu
    var = jnp.mean(zc * zc, axis=1, keepdims=True)
    y = zc * lax.rsqrt(var + LN_EPS) * g_ref[...] + b_ref[...]
    yt = y.T
    o_ref[...] = yt
    ob_ref[...] = yt.astype(BF16)


def _outproj_ln(pool_o, swa_o, mem_o, w_out, x2, g, b, alpha, tm):
    t, d = x2.shape
    tm = min(tm, t)
    wp, ws, wm = pool_o.shape[1], swa_o.shape[1], mem_o.shape[1]
    w_p, w_s, w_m = w_out[:wp], w_out[wp:wp + ws], w_out[wp + ws:]
    row = lambda w: pl.BlockSpec((tm, w), lambda i: (i, 0))
    full = lambda a: pl.BlockSpec(a.shape, lambda i: (0, 0))
    return pl.pallas_call(
        functools.partial(_outproj_kernel, alpha=alpha),
        grid=(t // tm,),
        in_specs=[row(wp), row(ws), row(wm), full(w_p), full(w_s), full(w_m), row(d),
                  pl.BlockSpec((1, d), lambda i: (0, 0)), pl.BlockSpec((1, d), lambda i: (0, 0))],
        out_specs=[pl.BlockSpec((d, tm), lambda i: (0, i))] * 2,
        out_shape=[jax.ShapeDtypeStruct((d, t), F32), jax.ShapeDtypeStruct((d, t), BF16)],
        compiler_params=_params("parallel"),
        name="outproj_ln",
    )(pool_o, swa_o, mem_o, w_p, w_s, w_m, x2, g, b)


def _top16_rows(s):
    n = s.shape[0]
    iota = lax.broadcasted_iota(jnp.int32, s.shape, 0).astype(F32)
    rank = jnp.full(s.shape, float(PEER_TOPK), F32)
    vals = []
    for r in range(PEER_TOPK):
        m = jnp.max(s, axis=0, keepdims=True)
        idx = jnp.min(jnp.where(s == m, iota, float(n)), axis=0, keepdims=True)
        hit = iota == idx
        rank = jnp.where(hit, float(r), rank)
        s = jnp.where(hit, -jnp.inf, s)
        vals.append(m)
    return rank, vals


_CAND_SMALL_A = PEER_TOPK // 2
_CAND_ROWS = PEER_TOPK + (_CAND_SMALL_A - 1) * SUBLANES + SUBLANES


def _cand_constants():
    flat = np.full((_CAND_ROWS, 1), 1e9, np.float32)
    valid = np.zeros((_CAND_ROWS, 1), np.float32)
    for b in range(PEER_TOPK):
        flat[b, 0], valid[b, 0] = b, 1.0
    for a in range(1, _CAND_SMALL_A):
        base = PEER_TOPK + (a - 1) * SUBLANES
        for b in range(PEER_TOPK // (a + 1)):
            flat[base + b, 0], valid[base + b, 0] = a * PEER_TOPK + b, 1.0
    base = PEER_TOPK + (_CAND_SMALL_A - 1) * SUBLANES
    for k in range(SUBLANES):
        flat[base + k, 0], valid[base + k, 0] = (_CAND_SMALL_A + k) * PEER_TOPK, 1.0
    return jnp.asarray(flat), jnp.asarray(valid)


def _select_experts(s1, s2, flat, valid):
    t = s1.shape[1]
    rank1, v1 = _top16_rows(s1)
    rank2, v2 = _top16_rows(s2)
    v2_lo = jnp.concatenate(v2[:SUBLANES], axis=0)
    v2_all = jnp.concatenate(v2, axis=0)
    v1_hi = jnp.concatenate(v1[_CAND_SMALL_A:], axis=0)
    groups = [v1[0] + v2_all]
    for a in range(1, _CAND_SMALL_A):
        groups.append(v1[a] + v2_lo)
    groups.append(v1_hi + v2[0])
    cand = jnp.concatenate(groups, axis=0)
    cand = jnp.where(valid > 0.5, cand, -jnp.inf)
    flat_b = jnp.broadcast_to(flat, cand.shape)
    hits = jnp.zeros(cand.shape, F32)
    top = []
    for r in range(PEER_TOPK):
        m = jnp.max(cand, axis=0, keepdims=True)
        pick = jnp.min(jnp.where(cand == m, flat_b, 2e9), axis=0, keepdims=True)
        hit = flat_b == pick
        hits = jnp.where(hit, 1.0, hits)
        cand = jnp.where(hit, -jnp.inf, cand)
        top.append(m)
    z = jnp.ones((1, t), F32)
    for r in range(1, PEER_TOPK):
        z = z + jnp.exp(top[r] - top[0])
    counts = [jnp.sum(hits[0:PEER_TOPK], axis=0, keepdims=True)]
    for a in range(1, _CAND_SMALL_A):
        base = PEER_TOPK + (a - 1) * SUBLANES
        counts.append(jnp.sum(hits[base:base + SUBLANES], axis=0, keepdims=True))
    base = PEER_TOPK + (_CAND_SMALL_A - 1) * SUBLANES
    for k in range(SUBLANES):
        counts.append(hits[base + k:base + k + 1])
    lim = jnp.zeros(s1.shape, F32)
    for a in range(PEER_TOPK):
        lim = jnp.where(rank1 == float(a), counts[a], lim)
    e1n = jnp.exp(s1 - v1[0]) / z
    e2 = jnp.exp(s2 - v2[0])
    return lim, e1n, rank2, e2


def _retrieve_kernel(wq_ref, x_ref, k1_ref, k2_ref, flat_ref, valid_ref,
                     lim_ref, e1_ref, r2_ref, e2_ref, q_ref, *, tq):
    q_ref[...] = jnp.dot(wq_ref[...], x_ref[...], preferred_element_type=F32)
    flat, valid = flat_ref[...], valid_ref[...]

    def head(h, carry):
        for c in range(tq // LANES):
            lanes = slice(c * LANES, (c + 1) * LANES)
            r0 = pl.multiple_of(h * 2 * HALF_DIM, 2 * HALF_DIM)
            q1 = q_ref[pl.ds(r0, HALF_DIM), lanes].astype(BF16)
            q2 = q_ref[pl.ds(r0 + HALF_DIM, HALF_DIM), lanes].astype(BF16)
            s1 = jnp.dot(k1_ref[...], q1, preferred_element_type=F32)
            s2 = jnp.dot(k2_ref[...], q2, preferred_element_type=F32)
            lim, e1n, rank2, e2 = _select_experts(s1, s2, flat, valid)
            lim_ref[h, :, lanes] = lim
            e1_ref[h, :, lanes] = e1n
            r2_ref[h, :, lanes] = rank2.astype(BF16)
            e2_ref[h, :, lanes] = e2.astype(BF16)
        return carry

    lax.fori_loop(0, PEER_HEADS, head, 0)


def _retrieve(x1t, wq_t, k1, k2, tq):
    d, t = x1t.shape
    tq = min(tq, t)
    flat, valid = _cand_constants()
    full = lambda a: pl.BlockSpec(a.shape, lambda i: (0,) * a.ndim)
    out = jax.ShapeDtypeStruct((PEER_HEADS, N_KEYS, t), F32)
    out_packed = jax.ShapeDtypeStruct((PEER_HEADS, N_KEYS, t), BF16)
    out_spec = pl.BlockSpec((PEER_HEADS, N_KEYS, tq), lambda i: (0, 0, i))
    return pl.pallas_call(
        functools.partial(_retrieve_kernel, tq=tq),
        grid=(t // tq,),
        in_specs=[full(wq_t), pl.BlockSpec((d, tq), lambda i: (0, i)), full(k1), full(k2), full(flat), full(valid)],
        out_specs=[out_spec] * 4,
        out_shape=[out, out, out_packed, out_packed],
        scratch_shapes=[pltpu.VMEM((wq_t.shape[0], tq), F32)],
        compiler_params=_params("parallel"),
        name="peer_retrieve",
    )(wq_t, x1t, k1, k2, flat, valid)


def _gelu_exact(x):
    return 0.5 * x * (1.0 + lax.erf(x * (1.0 / math.sqrt(2.0))))


def _experts_kernel(xb_ref, u_ref, vt_ref, lim_ref, e1_ref, r2_in_ref, e2_in_ref, o_ref,
                    h_ref, a_ref, rows_ref, gate_ref, r2_ref, e2_ref, *, tm, te):
    e = pl.program_id(1)

    @pl.when(e == 0)
    def _():
        o_ref[...] = jnp.zeros(o_ref.shape, F32)
        r2_ref[...] = r2_in_ref[...]
        e2_ref[...] = e2_in_ref[...]

    n_chunks = te // EXPERT_CHUNK

    def rows_of(p):
        return slice(p * EXPERT_CHUNK, (p + 1) * EXPERT_CHUNK)

    def hidden(p):
        h_ref[rows_of(p), :] = jnp.dot(u_ref[rows_of(p), :], xb_ref[...], preferred_element_type=F32)

    def gates(p):
        keys = range(p * EXPERT_CHUNK // N_KEYS, (p + 1) * EXPERT_CHUNK // N_KEYS)
        blocks = range(0, N_KEYS, GATE_ROWS)
        for i in keys:
            for hd in range(PEER_HEADS):
                for q, ref in enumerate((lim_ref, e1_ref)):
                    rows_ref[q, i, hd] = jnp.broadcast_to(ref[i, hd:hd + 1, :], (GATE_ROWS, tm)).astype(BF16)
        for c in range(tm // LANES):
            lanes = slice(c * LANES, (c + 1) * LANES)
            gate = {i: {jb: jnp.zeros((GATE_ROWS, LANES), BF16) for jb in blocks} for i in keys}
            for hd in range(PEER_HEADS):
                lim = {i: rows_ref[0, i, hd, :, lanes] for i in keys}
                e1 = {i: rows_ref[1, i, hd, :, lanes] for i in keys}
                for jb in blocks:
                    r2 = r2_ref[hd * N_KEYS + jb:hd * N_KEYS + jb + GATE_ROWS, lanes]
                    e2 = e2_ref[hd * N_KEYS + jb:hd * N_KEYS + jb + GATE_ROWS, lanes]
                    for i in keys:
                        gate[i][jb] = gate[i][jb] + jnp.where(r2 < lim[i], e2 * e1[i], jnp.zeros_like(e2))
            for i in keys:
                for jb in blocks:
                    gate_ref[i * N_KEYS + jb:i * N_KEYS + jb + GATE_ROWS, lanes] = gate[i][jb]

    hidden(0)
    for p in range(n_chunks):
        if p + 1 < n_chunks:
            hidden(p + 1)
        gates(p)
        a_ref[rows_of(p), :] = gate_ref[rows_of(p), :] * _gelu_exact(h_ref[rows_of(p), :]).astype(BF16)
        o_ref[...] += jnp.dot(vt_ref[:, rows_of(p)], a_ref[rows_of(p), :], preferred_element_type=F32)


def _experts(xbt, u, vt, sel, tm, te):
    d, t = xbt.shape
    n_exp = u.shape[0]
    tm, te = min(tm, t), min(te, n_exp)
    tok = pl.BlockSpec((d, tm), lambda i, j: (0, i))
    sel_spec = pl.BlockSpec((PEER_HEADS * N_KEYS, tm), lambda i, j: (0, i))
    key_spec = pl.BlockSpec((te // N_KEYS, PEER_HEADS, tm), lambda i, j: (j, 0, i))
    lim, e1n, rank2, e2 = sel
    sel = (lim.transpose(1, 0, 2), e1n.transpose(1, 0, 2), rank2.reshape(-1, t), e2.reshape(-1, t))
    return pl.pallas_call(
        functools.partial(_experts_kernel, tm=tm, te=te),
        grid=(t // tm, n_exp // te),
        in_specs=[tok, pl.BlockSpec((te, d), lambda i, j: (j, 0)), pl.BlockSpec((d, te), lambda i, j: (0, j)),
                  key_spec, key_spec, sel_spec, sel_spec],
        out_specs=tok,
        out_shape=jax.ShapeDtypeStruct((d, t), F32),
        scratch_shapes=[pltpu.VMEM((te, tm), F32), pltpu.VMEM((te, tm), BF16),
                        pltpu.VMEM((2, te // N_KEYS, PEER_HEADS, GATE_ROWS, tm), BF16),
                        pltpu.VMEM((te, tm), BF16),
                        pltpu.VMEM((PEER_HEADS * N_KEYS, tm), BF16), pltpu.VMEM((PEER_HEADS * N_KEYS, tm), BF16)],
        compiler_params=_params("parallel", "arbitrary"),
        name="peer_experts",
    )(xbt, u, vt, *sel)


def _ln_t_kernel(x_ref, f_ref, g_ref, b_ref, o_ref, *, alpha):
    z = (alpha * x_ref[...] + f_ref[...]).T
    mu = jnp.mean(z, axis=1, keepdims=True)
    zc = z - mu
    var = jnp.mean(zc * zc, axis=1, keepdims=True)
    o_ref[...] = zc * lax.rsqrt(var + LN_EPS) * g_ref[...] + b_ref[...]


def _residual_ln_t(x1t, fft, g, b, alpha, tm):
    d, t = x1t.shape
    tm = min(tm, t)
    tok = pl.BlockSpec((d, tm), lambda i: (0, i))
    vec = pl.BlockSpec((1, d), lambda i: (0, 0))
    return pl.pallas_call(
        functools.partial(_ln_t_kernel, alpha=alpha),
        grid=(t // tm,),
        in_specs=[tok, tok, vec, vec],
        out_specs=pl.BlockSpec((tm, d), lambda i: (i, 0)),
        out_shape=jax.ShapeDtypeStruct((t, d), F32),
        compiler_params=_params("parallel"),
        name="residual_ln",
    )(x1t, fft, g, b)


def kernel(x, mem, positions, w_in, w_mem_kv, w_pool, pool_scale, attn_sinks, w_out, ln1_g, ln1_b,
           w_peer_q, sub_keys_1, sub_keys_2, expert_u, expert_v, ln2_g, ln2_b):
    bsz, seq, d = x.shape
    depth = w_in.shape[0]
    t = bsz * seq
    alpha = (2.0 * depth) ** 0.25
    for l in range(depth):
        x2 = x.reshape(t, d)
        h = _matmul(x2.astype(BF16), w_in[l].astype(BF16), F32, 1024, 512).reshape(bsz, seq, -1)
        mem2 = mem.reshape(-1, d).astype(BF16)
        kvm = _matmul(mem2, w_mem_kv[l].astype(BF16), BF16, 512, 512).reshape(bsz, mem.shape[1], -1)
        pool_o = _pool(h, w_pool[l].astype(BF16), pool_scale[l].reshape(1, -1), 512)
        swa_o = _swa(h, positions, attn_sinks[l])
        mem_o = _mem_attention(h, kvm, 512)
        x1t, x1bt = _outproj_ln(pool_o.reshape(t, -1), swa_o.reshape(t, -1), mem_o.reshape(t, -1),
                          w_out[l].astype(BF16), x2, ln1_g[l].reshape(1, d), ln1_b[l].reshape(1, d), alpha, 512)
        sel = _retrieve(x1bt, w_peer_q[l].T.astype(BF16), sub_keys_1[l].astype(BF16),
                        sub_keys_2[l].astype(BF16), 256)
        fft = _experts(x1bt, expert_u[l].astype(BF16), expert_v[l].T.astype(BF16), sel, 1024, 512)
        x = _residual_ln_t(x1t, fft, ln2_g[l].reshape(1, d), ln2_b[l].reshape(1, d), alpha, 512).reshape(bsz, seq, d)
    return x
```

```python
import functools
import math

import numpy as np
import jax
import jax.numpy as jnp
from jax import lax
from jax.experimental import pallas as pl
from jax.experimental.pallas import tpu as pltpu

F32 = jnp.float32
BF16 = jnp.bfloat16

LANES = 128
SUBLANES = 8
VMEM_LIMIT_BYTES = 56 * 1024 * 1024

POOL_WINDOWS = (2, 4, 8, 16)
POOL_GROUP = 128
POOL_HALO = 16
SWA_HEAD_DIM = 64
SWA_HEADS = 16
SWA_KV_HEADS = 4
SWA_BLOCK = 128
ROPE_THETA = 500000.0
ROPE_DIM = 16
MEM_HEADS = 4
MEM_HEAD_DIM = 128
PEER_HEADS = 8
N_KEYS = 128
PEER_TOPK = 16
HALF_DIM = 128
LN_EPS = 1e-5
NEG = -1e30

EXPERT_CHUNK = 256
GATE_ROWS = 16


def _params(*semantics):
    return pltpu.CompilerParams(dimension_semantics=semantics, vmem_limit_bytes=VMEM_LIMIT_BYTES)


def _matmul_kernel(a_ref, b_ref, o_ref):
    o_ref[...] = jnp.dot(a_ref[...], b_ref[...], preferred_element_type=F32).astype(o_ref.dtype)


def _matmul(a, b, out_dtype, tm, tn):
    m, k = a.shape
    n = b.shape[1]
    tm, tn = min(tm, m), min(tn, n)
    return pl.pallas_call(
        _matmul_kernel,
        grid=(m // tm, n // tn),
        in_specs=[pl.BlockSpec((tm, k), lambda i, j: (i, 0)), pl.BlockSpec((k, tn), lambda i, j: (0, j))],
        out_specs=pl.BlockSpec((tm, tn), lambda i, j: (i, j)),
        out_shape=jax.ShapeDtypeStruct((m, n), out_dtype),
        compiler_params=_params("parallel", "arbitrary"),
        name="matmul",
    )(a, b)


def _pool_kernel(v_ref, w_ref, scale_ref, o_ref, ext_ref, *, ts):
    s = pl.program_id(1)

    @pl.when(s == 0)
    def _():
        ext_ref[0:POOL_HALO, :] = jnp.zeros((POOL_HALO, ext_ref.shape[1]), F32)

    ext_ref[POOL_HALO:POOL_HALO + ts, :] = v_ref[0]
    pos = s * ts + lax.broadcasted_iota(jnp.int32, (ts, 1), 0)
    for g, w in enumerate(POOL_WINDOWS):
        cols = slice(g * POOL_GROUP, (g + 1) * POOL_GROUP)
        acc = ext_ref[POOL_HALO:POOL_HALO + ts, cols]
        for k in range(1, w):
            acc = acc + ext_ref[POOL_HALO - k:POOL_HALO - k + ts, cols]
        count = jnp.minimum(pos + 1, w).astype(F32)
        pooled = acc / count - ext_ref[POOL_HALO:POOL_HALO + ts, cols]
        y = jnp.dot(pooled.astype(BF16), w_ref[g], preferred_element_type=F32)
        o_ref[0, :, cols] = (y * scale_ref[:, cols]).astype(o_ref.dtype)
    ext_ref[0:POOL_HALO, :] = ext_ref[ts:ts + POOL_HALO, :]


def _pool(h, w_pool, pool_scale, ts):
    b, s, _ = h.shape
    width = POOL_GROUP * len(POOL_WINDOWS)
    ts = min(ts, s)
    return pl.pallas_call(
        functools.partial(_pool_kernel, ts=ts),
        grid=(b, s // ts),
        in_specs=[
            pl.BlockSpec((1, ts, width), lambda i, j: (i, j, 0)),
            pl.BlockSpec(w_pool.shape, lambda i, j: (0, 0, 0)),
            pl.BlockSpec((1, width), lambda i, j: (0, 0)),
        ],
        out_specs=pl.BlockSpec((1, ts, width), lambda i, j: (i, j, 0)),
        out_shape=jax.ShapeDtypeStruct((b, s, width), BF16),
        scratch_shapes=[pltpu.VMEM((ts + POOL_HALO, width), F32)],
        compiler_params=_params("arbitrary", "arbitrary"),
        name="pool",
    )(h, w_pool, pool_scale)


def _rope_tables(pos_ref, freq_ref, sa_ref, sb_ref):
    ang = pos_ref[0].astype(F32) * freq_ref[...]
    c, s = jnp.cos(ang), jnp.sin(ang)
    return c, s * sa_ref[...], s * sb_ref[...]


def _rope(t, tables):
    c, sa, sb = tables
    half = ROPE_DIM // 2
    out = []
    for j in range(t.shape[1] // LANES):
        x = t[:, j * LANES:(j + 1) * LANES]
        out.append(x * c + pltpu.roll(x, LANES - half, 1) * sa + pltpu.roll(x, half, 1) * sb)
    return jnp.concatenate(out, axis=1) if len(out) > 1 else out[0]


def _swa_kernel(sink_ref, q0_ref, q1_ref, k_ref, v_ref, kp_ref, vp_ref, pos_ref, posp_ref,
                freq_ref, sa_ref, sb_ref, o_ref):
    n = pl.program_id(1)
    cur = _rope_tables(pos_ref, freq_ref, sa_ref, sb_ref)
    prev = _rope_tables(posp_ref, freq_ref, sa_ref, sb_ref)
    scale = SWA_HEAD_DIM ** -0.5
    q = jnp.concatenate([_rope(q0_ref[0], cur), _rope(q1_ref[0], cur)], axis=1)
    q = (q * scale).astype(BF16)
    k = jnp.concatenate([_rope(kp_ref[0], prev), _rope(k_ref[0], cur)], axis=0).astype(BF16)
    v = jnp.concatenate([vp_ref[0], v_ref[0]], axis=0).astype(BF16)
    row = lax.broadcasted_iota(jnp.int32, (SWA_BLOCK, 2 * SWA_BLOCK), 0)
    col = lax.broadcasted_iota(jnp.int32, (SWA_BLOCK, 2 * SWA_BLOCK), 1)
    rel = row + SWA_BLOCK - col
    valid = (rel >= 0) & (rel < SWA_BLOCK) & ((col >= SWA_BLOCK) | (n > 0))
    group = SWA_HEADS // SWA_KV_HEADS
    outs = []
    for hq in range(SWA_HEADS):
        kv = hq // group
        qh = q[:, hq * SWA_HEAD_DIM:(hq + 1) * SWA_HEAD_DIM]
        kh = k[:, kv * SWA_HEAD_DIM:(kv + 1) * SWA_HEAD_DIM]
        vh = v[:, kv * SWA_HEAD_DIM:(kv + 1) * SWA_HEAD_DIM]
        sc = lax.dot_general(qh, kh, (((1,), (1,)), ((), ())), preferred_element_type=F32)
        sc = jnp.where(valid, sc, NEG)
        sink = sink_ref[hq]
        m = jnp.maximum(jnp.max(sc, axis=1, keepdims=True), sink)
        p = jnp.exp(sc - m)
        denom = jnp.sum(p, axis=1, keepdims=True) + jnp.exp(sink - m)
        o = jnp.dot(p.astype(BF16), vh, preferred_element_type=F32)
        outs.append(o / denom)
    o_ref[0] = jnp.concatenate(outs, axis=1).astype(o_ref.dtype)


def _rope_constants():
    lane = np.arange(LANES)
    d = lane % SWA_HEAD_DIM
    half = ROPE_DIM // 2
    inv_freq = np.float32(ROPE_THETA) ** (-np.arange(0, ROPE_DIM, 2, dtype=np.float32) / np.float32(ROPE_DIM))
    freq = np.where(d < ROPE_DIM, inv_freq[d % half], 0.0).astype(np.float32)
    sa = np.where(d < half, -1.0, 0.0).astype(np.float32)
    sb = np.where((d >= half) & (d < ROPE_DIM), 1.0, 0.0).astype(np.float32)
    return [jnp.asarray(a.reshape(1, LANES)) for a in (freq, sa, sb)]


def _swa(h, positions, sinks):
    b, s, _ = h.shape
    nb = s // SWA_BLOCK
    pos3 = positions.reshape(b, s, 1)
    freq, sa, sb = _rope_constants()
    blk = lambda w, c: pl.BlockSpec((1, SWA_BLOCK, w), lambda i, j: (i, j, c))
    blk_prev = lambda w, c: pl.BlockSpec((1, SWA_BLOCK, w), lambda i, j: (i, jnp.maximum(j - 1, 0), c))
    const = pl.BlockSpec((1, LANES), lambda i, j: (0, 0))
    kvw = SWA_KV_HEADS * SWA_HEAD_DIM
    return pl.pallas_call(
        _swa_kernel,
        grid=(b, nb),
        in_specs=[
            pl.BlockSpec(memory_space=pltpu.SMEM),
            blk(512, 1), blk(512, 2), blk(kvw, 6), blk(kvw, 7), blk_prev(kvw, 6), blk_prev(kvw, 7),
            blk(1, 0), blk_prev(1, 0), const, const, const,
        ],
        out_specs=pl.BlockSpec((1, SWA_BLOCK, SWA_HEADS * SWA_HEAD_DIM), lambda i, j: (i, j, 0)),
        out_shape=jax.ShapeDtypeStruct((b, s, SWA_HEADS * SWA_HEAD_DIM), BF16),
        compiler_params=_params("parallel", "arbitrary"),
        name="swa",
    )(sinks, h, h, h, h, h, h, pos3, pos3, freq, sa, sb)


def _mem_kernel(q_ref, kv_ref, o_ref):
    scale = MEM_HEAD_DIM ** -0.5
    width = MEM_HEADS * MEM_HEAD_DIM
    for hm in range(MEM_HEADS):
        cols = slice(hm * MEM_HEAD_DIM, (hm + 1) * MEM_HEAD_DIM)
        q = (q_ref[0, :, cols] * scale).astype(BF16)
        km = kv_ref[0, :, cols]
        vm = kv_ref[0, :, width + hm * MEM_HEAD_DIM:width + (hm + 1) * MEM_HEAD_DIM]
        sc = lax.dot_general(q, km, (((1,), (1,)), ((), ())), preferred_element_type=F32)
        m = jnp.max(sc, axis=1, keepdims=True)
        p = jnp.exp(sc - m)
        denom = jnp.sum(p, axis=1, keepdims=True)
        o = jnp.dot(p.astype(BF16), vm, preferred_element_type=F32)
        o_ref[0, :, cols] = (o / denom).astype(o_ref.dtype)


def _mem_attention(h, kvm, tq):
    b, s, _ = h.shape
    width = MEM_HEADS * MEM_HEAD_DIM
    tq = min(tq, s)
    return pl.pallas_call(
        _mem_kernel,
        grid=(b, s // tq),
        in_specs=[
            pl.BlockSpec((1, tq, width), lambda i, j: (i, j, 4)),
            pl.BlockSpec((1,) + kvm.shape[1:], lambda i, j: (i, 0, 0)),
        ],
        out_specs=pl.BlockSpec((1, tq, width), lambda i, j: (i, j, 0)),
        out_shape=jax.ShapeDtypeStruct((b, s, width), BF16),
        compiler_params=_params("parallel", "arbitrary"),
        name="mem_attention",
    )(h, kvm)


def _outproj_kernel(pool_ref, swa_ref, mem_ref, wp_ref, ws_ref, wm_ref, x_ref, g_ref, b_ref, o_ref, ob_ref,
                    *, alpha):
    mix = jnp.dot(pool_ref[...], wp_ref[...], preferred_element_type=F32)
    mix += jnp.dot(swa_ref[...], ws_ref[...], preferred_element_type=F32)
    mix += jnp.dot(mem_ref[...], wm_ref[...], preferred_element_type=F32)
    z = alpha * x_ref[...] + mix
    mu = jnp.mean(z, axis=1, keepdims=True)
    zc = z - mu
    var = jnp.mean(zc * zc, axis=1, keepdims=True)
    y = zc * lax.rsqrt(var + LN_EPS) * g_ref[...] + b_ref[...]
    yt = y.T
    o_ref[...] = yt
    ob_ref[...] = yt.astype(BF16)


def _outproj_ln(pool_o, swa_o, mem_o, w_out, x2, g, b, alpha, tm):
    t, d = x2.shape
    tm = min(tm, t)
    wp, ws, wm = pool_o.shape[1], swa_o.shape[1], mem_o.shape[1]
    w_p, w_s, w_m = w_out[:wp], w_out[wp:wp + ws], w_out[wp + ws:]
    row = lambda w: pl.BlockSpec((tm, w), lambda i: (i, 0))
    full = lambda a: pl.BlockSpec(a.shape, lambda i: (0, 0))
    return pl.pallas_call(
        functools.partial(_outproj_kernel, alpha=alpha),
        grid=(t // tm,),
        in_specs=[row(wp), row(ws), row(wm), full(w_p), full(w_s), full(w_m), row(d),
                  pl.BlockSpec((1, d), lambda i: (0, 0)), pl.BlockSpec((1, d), lambda i: (0, 0))],
        out_specs=[pl.BlockSpec((d, tm), lambda i: (0, i))] * 2,
        out_shape=[jax.ShapeDtypeStruct((d, t), F32), jax.ShapeDtypeStruct((d, t), BF16)],
        compiler_params=_params("parallel"),
        name="outproj_ln",
    )(pool_o, swa_o, mem_o, w_p, w_s, w_m, x2, g, b)


def _top16_rows(s):
    n = s.shape[0]
    iota = lax.broadcasted_iota(jnp.int32, s.shape, 0).astype(F32)
    rank = jnp.full(s.shape, float(PEER_TOPK), F32)
    vals = []
    for r in range(PEER_TOPK):
        m = jnp.max(s, axis=0, keepdims=True)
        idx = jnp.min(jnp.where(s == m, iota, float(n)), axis=0, keepdims=True)
        hit = iota == idx
        rank = jnp.where(hit, float(r), rank)
        s = jnp.where(hit, -jnp.inf, s)
        vals.append(m)
    return rank, vals


_CAND_SMALL_A = PEER_TOPK // 2
_CAND_ROWS = PEER_TOPK + (_CAND_SMALL_A - 1) * SUBLANES + SUBLANES


def _cand_constants():
    flat = np.full((_CAND_ROWS, 1), 1e9, np.float32)
    valid = np.zeros((_CAND_ROWS, 1), np.float32)
    for b in range(PEER_TOPK):
        flat[b, 0], valid[b, 0] = b, 1.0
    for a in range(1, _CAND_SMALL_A):
        base = PEER_TOPK + (a - 1) * SUBLANES
        for b in range(PEER_TOPK // (a + 1)):
            flat[base + b, 0], valid[base + b, 0] = a * PEER_TOPK + b, 1.0
    base = PEER_TOPK + (_CAND_SMALL_A - 1) * SUBLANES
    for k in range(SUBLANES):
        flat[base + k, 0], valid[base + k, 0] = (_CAND_SMALL_A + k) * PEER_TOPK, 1.0
    return jnp.asarray(flat), jnp.asarray(valid)


def _select_experts(s1, s2, flat, valid):
    t = s1.shape[1]
    rank1, v1 = _top16_rows(s1)
    rank2, v2 = _top16_rows(s2)
    v2_lo = jnp.concatenate(v2[:SUBLANES], axis=0)
    v2_all = jnp.concatenate(v2, axis=0)
    v1_hi = jnp.concatenate(v1[_CAND_SMALL_A:], axis=0)
    groups = [v1[0] + v2_all]
    for a in range(1, _CAND_SMALL_A):
        groups.append(v1[a] + v2_lo)
    groups.append(v1_hi + v2[0])
    cand = jnp.concatenate(groups, axis=0)
    cand = jnp.where(valid > 0.5, cand, -jnp.inf)
    flat_b = jnp.broadcast_to(flat, cand.shape)
    hits = jnp.zeros(cand.shape, F32)
    top = []
    for r in range(PEER_TOPK):
        m = jnp.max(cand, axis=0, keepdims=True)
        pick = jnp.min(jnp.where(cand == m, flat_b, 2e9), axis=0, keepdims=True)
        hit = flat_b == pick
        hits = jnp.where(hit, 1.0, hits)
        cand = jnp.where(hit, -jnp.inf, cand)
        top.append(m)
    z = jnp.ones((1, t), F32)
    for r in range(1, PEER_TOPK):
        z = z + jnp.exp(top[r] - top[0])
    counts = [jnp.sum(hits[0:PEER_TOPK], axis=0, keepdims=True)]
    for a in range(1, _CAND_SMALL_A):
        base = PEER_TOPK + (a - 1) * SUBLANES
        counts.append(jnp.sum(hits[base:base + SUBLANES], axis=0, keepdims=True))
    base = PEER_TOPK + (_CAND_SMALL_A - 1) * SUBLANES
    for k in range(SUBLANES):
        counts.append(hits[base + k:base + k + 1])
    lim = jnp.zeros(s1.shape, F32)
    for a in range(PEER_TOPK):
        lim = jnp.where(rank1 == float(a), counts[a], lim)
    e1n = jnp.exp(s1 - v1[0]) / z
    e2 = jnp.exp(s2 - v2[0])
    return lim, e1n, rank2, e2


def _retrieve_kernel(wq_ref, x_ref, k1_ref, k2_ref, flat_ref, valid_ref,
                     lim_ref, e1_ref, r2_ref, e2_ref, q_ref, *, tq):
    q_ref[...] = jnp.dot(wq_ref[...], x_ref[...], preferred_element_type=F32)
    flat, valid = flat_ref[...], valid_ref[...]

    def head(h, carry):
        for c in range(tq // LANES):
            lanes = slice(c * LANES, (c + 1) * LANES)
            r0 = pl.multiple_of(h * 2 * HALF_DIM, 2 * HALF_DIM)
            q1 = q_ref[pl.ds(r0, HALF_DIM), lanes].astype(BF16)
            q2 = q_ref[pl.ds(r0 + HALF_DIM, HALF_DIM), lanes].astype(BF16)
            s1 = jnp.dot(k1_ref[...], q1, preferred_element_type=F32)
            s2 = jnp.dot(k2_ref[...], q2, preferred_element_type=F32)
            lim, e1n, rank2, e2 = _select_experts(s1, s2, flat, valid)
            lim_ref[h, :, lanes] = lim
            e1_ref[h, :, lanes] = e1n
            r2_ref[h, :, lanes] = rank2.astype(BF16)
            e2_ref[h, :, lanes] = e2.astype(BF16)
        return carry

    lax.fori_loop(0, PEER_HEADS, head, 0)


def _retrieve(x1t, wq_t, k1, k2, tq):
    d, t = x1t.shape
    tq = min(tq, t)
    flat, valid = _cand_constants()
    full = lambda a: pl.BlockSpec(a.shape, lambda i: (0,) * a.ndim)
    out = jax.ShapeDtypeStruct((PEER_HEADS, N_KEYS, t), F32)
    out_packed = jax.ShapeDtypeStruct((PEER_HEADS, N_KEYS, t), BF16)
    out_spec = pl.BlockSpec((PEER_HEADS, N_KEYS, tq), lambda i: (0, 0, i))
    return pl.pallas_call(
        functools.partial(_retrieve_kernel, tq=tq),
        grid=(t // tq,),
        in_specs=[full(wq_t), pl.BlockSpec((d, tq), lambda i: (0, i)), full(k1), full(k2), full(flat), full(valid)],
        out_specs=[out_spec] * 4,
        out_shape=[out, out, out_packed, out_packed],
        scratch_shapes=[pltpu.VMEM((wq_t.shape[0], tq), F32)],
        compiler_params=_params("parallel"),
        name="peer_retrieve",
    )(wq_t, x1t, k1, k2, flat, valid)


def _gelu_exact(x):
    return 0.5 * x * (1.0 + lax.erf(x * (1.0 / math.sqrt(2.0))))


def _experts_kernel(xb_ref, u_ref, vt_ref, lim_ref, e1_ref, r2_in_ref, e2_in_ref, o_ref,
                    h_ref, a_ref, rows_ref, gate_ref, r2_ref, e2_ref, *, tm, te):
    e = pl.program_id(1)

    @pl.when(e == 0)
    def _():
        o_ref[...] = jnp.zeros(o_ref.shape, F32)
        r2_ref[...] = r2_in_ref[...]
        e2_ref[...] = e2_in_ref[...]

    n_chunks = te // EXPERT_CHUNK

    def rows_of(p):
        return slice(p * EXPERT_CHUNK, (p + 1) * EXPERT_CHUNK)

    def gates(p):
        keys = range(p * EXPERT_CHUNK // N_KEYS, (p + 1) * EXPERT_CHUNK // N_KEYS)
        blocks = range(0, N_KEYS, GATE_ROWS)
        for i in keys:
            for hd in range(PEER_HEADS):
                for q, ref in enumerate((lim_ref, e1_ref)):
                    rows_ref[q, i, hd] = jnp.broadcast_to(ref[i, hd:hd + 1, :], (GATE_ROWS, tm)).astype(BF16)
        for c in range(tm // LANES):
            lanes = slice(c * LANES, (c + 1) * LANES)
            gate = {i: {jb: jnp.zeros((GATE_ROWS, LANES), BF16) for jb in blocks} for i in keys}
            for hd in range(PEER_HEADS):
                lim = {i: rows_ref[0, i, hd, :, lanes] for i in keys}
                e1 = {i: rows_ref[1, i, hd, :, lanes] for i in keys}
                for jb in blocks:
                    r2 = r2_ref[hd * N_KEYS + jb:hd * N_KEYS + jb + GATE_ROWS, lanes]
                    e2 = e2_ref[hd * N_KEYS + jb:hd * N_KEYS + jb + GATE_ROWS, lanes]
                    for i in keys:
                        gate[i][jb] = gate[i][jb] + jnp.where(r2 < lim[i], e2 * e1[i], jnp.zeros_like(e2))
            for i in keys:
                for jb in blocks:
                    gate_ref[i * N_KEYS + jb:i * N_KEYS + jb + GATE_ROWS, lanes] = gate[i][jb]

    for p in range(n_chunks):
        gates(p)
    h_ref[...] = jnp.dot(u_ref[...], xb_ref[...], preferred_element_type=F32)
    for p in range(n_chunks):
        a_ref[rows_of(p), :] = gate_ref[rows_of(p), :] * _gelu_exact(h_ref[rows_of(p), :]).astype(BF16)
        o_ref[...] += jnp.dot(vt_ref[:, rows_of(p)], a_ref[rows_of(p), :], preferred_element_type=F32)


def _experts(xbt, u, vt, sel, tm, te):
    d, t = xbt.shape
    n_exp = u.shape[0]
    tm, te = min(tm, t), min(te, n_exp)
    tok = pl.BlockSpec((d, tm), lambda i, j: (0, i))
    sel_spec = pl.BlockSpec((PEER_HEADS * N_KEYS, tm), lambda i, j: (0, i))
    key_spec = pl.BlockSpec((te // N_KEYS, PEER_HEADS, tm), lambda i, j: (j, 0, i))
    lim, e1n, rank2, e2 = sel
    sel = (lim.transpose(1, 0, 2), e1n.transpose(1, 0, 2), rank2.reshape(-1, t), e2.reshape(-1, t))
    return pl.pallas_call(
        functools.partial(_experts_kernel, tm=tm, te=te),
        grid=(t // tm, n_exp // te),
        in_specs=[tok, pl.BlockSpec((te, d), lambda i, j: (j, 0)), pl.BlockSpec((d, te), lambda i, j: (0, j)),
                  key_spec, key_spec, sel_spec, sel_spec],
        out_specs=tok,
        out_shape=jax.ShapeDtypeStruct((d, t), F32),
        scratch_shapes=[pltpu.VMEM((te, tm), F32), pltpu.VMEM((te, tm), BF16),
                        pltpu.VMEM((2, te // N_KEYS, PEER_HEADS, GATE_ROWS, tm), BF16),
                        pltpu.VMEM((te, tm), BF16),
                        pltpu.VMEM((PEER_HEADS * N_KEYS, tm), BF16), pltpu.VMEM((PEER_HEADS * N_KEYS, tm), BF16)],
        compiler_params=_params("parallel", "arbitrary"),
        name="peer_experts",
    )(xbt, u, vt, *sel)


def _ln_t_kernel(x_ref, f_ref, g_ref, b_ref, o_ref, *, alpha):
    z = (alpha * x_ref[...] + f_ref[...]).T
    mu = jnp.mean(z, axis=1, keepdims=True)
    zc = z - mu
    var = jnp.mean(zc * zc, axis=1, keepdims=True)
    o_ref[...] = zc * lax.rsqrt(var + LN_EPS) * g_ref[...] + b_ref[...]


def _residual_ln_t(x1t, fft, g, b, alpha, tm):
    d, t = x1t.shape
    tm = min(tm, t)
    tok = pl.BlockSpec((d, tm), lambda i: (0, i))
    vec = pl.BlockSpec((1, d), lambda i: (0, 0))
    return pl.pallas_call(
        functools.partial(_ln_t_kernel, alpha=alpha),
        grid=(t // tm,),
        in_specs=[tok, tok, vec, vec],
        out_specs=pl.BlockSpec((tm, d), lambda i: (i, 0)),
        out_shape=jax.ShapeDtypeStruct((t, d), F32),
        compiler_params=_params("parallel"),
        name="residual_ln",
    )(x1t, fft, g, b)


def kernel(x, mem, positions, w_in, w_mem_kv, w_pool, pool_scale, attn_sinks, w_out, ln1_g, ln1_b,
           w_peer_q, sub_keys_1, sub_keys_2, expert_u, expert_v, ln2_g, ln2_b):
    bsz, seq, d = x.shape
    depth = w_in.shape[0]
    t = bsz * seq
    alpha = (2.0 * depth) ** 0.25
    for l in range(depth):
        x2 = x.reshape(t, d)
        h = _matmul(x2.astype(BF16), w_in[l].astype(BF16), F32, 1024, 512).reshape(bsz, seq, -1)
        mem2 = mem.reshape(-1, d).astype(BF16)
        kvm = _matmul(mem2, w_mem_kv[l].astype(BF16), BF16, 512, 512).reshape(bsz, mem.shape[1], -1)
        pool_o = _pool(h, w_pool[l].astype(BF16), pool_scale[l].reshape(1, -1), 512)
        swa_o = _swa(h, positions, attn_sinks[l])
        mem_o = _mem_attention(h, kvm, 512)
        x1t, x1bt = _outproj_ln(pool_o.reshape(t, -1), swa_o.reshape(t, -1), mem_o.reshape(t, -1),
                          w_out[l].astype(BF16), x2, ln1_g[l].reshape(1, d), ln1_b[l].reshape(1, d), alpha, 512)
        sel = _retrieve(x1bt, w_peer_q[l].T.astype(BF16), sub_keys_1[l].astype(BF16),
                        sub_keys_2[l].astype(BF16), 256)
        fft = _experts(x1bt, expert_u[l].astype(BF16), expert_v[l].T.astype(BF16), sel, 1024, 512)
        x = _residual_ln_t(x1t, fft, ln2_g[l].reshape(1, d), ln2_b[l].reshape(1, d), alpha, 512).reshape(bsz, seq, d)
    return x
```

```python
import functools
import math

import numpy as np
import jax
import jax.numpy as jnp
from jax import lax
from jax.experimental import pallas as pl
from jax.experimental.pallas import tpu as pltpu

F32 = jnp.float32
BF16 = jnp.bfloat16

LANES = 128
SUBLANES = 8
VMEM_LIMIT_BYTES = 56 * 1024 * 1024

POOL_WINDOWS = (2, 4, 8, 16)
POOL_GROUP = 128
POOL_HALO = 16
SWA_HEAD_DIM = 64
SWA_HEADS = 16
SWA_KV_HEADS = 4
SWA_BLOCK = 128
ROPE_THETA = 500000.0
ROPE_DIM = 16
MEM_HEADS = 4
MEM_HEAD_DIM = 128
PEER_HEADS = 8
N_KEYS = 128
PEER_TOPK = 16
HALF_DIM = 128
LN_EPS = 1e-5
NEG = -1e30

EXPERT_CHUNK = 256
GATE_ROWS = 16


def _params(*semantics):
    return pltpu.CompilerParams(dimension_semantics=semantics, vmem_limit_bytes=VMEM_LIMIT_BYTES)


def _matmul_kernel(a_ref, b_ref, o_ref):
    o_ref[...] = jnp.dot(a_ref[...], b_ref[...], preferred_element_type=F32).astype(o_ref.dtype)


def _matmul(a, b, out_dtype, tm, tn):
    m, k = a.shape
    n = b.shape[1]
    tm, tn = min(tm, m), min(tn, n)
    return pl.pallas_call(
        _matmul_kernel,
        grid=(m // tm, n // tn),
        in_specs=[pl.BlockSpec((tm, k), lambda i, j: (i, 0)), pl.BlockSpec((k, tn), lambda i, j: (0, j))],
        out_specs=pl.BlockSpec((tm, tn), lambda i, j: (i, j)),
        out_shape=jax.ShapeDtypeStruct((m, n), out_dtype),
        compiler_params=_params("parallel", "arbitrary"),
        name="matmul",
    )(a, b)


def _pool_kernel(v_ref, w_ref, scale_ref, o_ref, ext_ref, *, ts):
    s = pl.program_id(1)

    @pl.when(s == 0)
    def _():
        ext_ref[0:POOL_HALO, :] = jnp.zeros((POOL_HALO, ext_ref.shape[1]), F32)

    ext_ref[POOL_HALO:POOL_HALO + ts, :] = v_ref[0]
    pos = s * ts + lax.broadcasted_iota(jnp.int32, (ts, 1), 0)
    for g, w in enumerate(POOL_WINDOWS):
        cols = slice(g * POOL_GROUP, (g + 1) * POOL_GROUP)
        acc = ext_ref[POOL_HALO:POOL_HALO + ts, cols]
        for k in range(1, w):
            acc = acc + ext_ref[POOL_HALO - k:POOL_HALO - k + ts, cols]
        count = jnp.minimum(pos + 1, w).astype(F32)
        pooled = acc / count - ext_ref[POOL_HALO:POOL_HALO + ts, cols]
        y = jnp.dot(pooled.astype(BF16), w_ref[g], preferred_element_type=F32)
        o_ref[0, :, cols] = (y * scale_ref[:, cols]).astype(o_ref.dtype)
    ext_ref[0:POOL_HALO, :] = ext_ref[ts:ts + POOL_HALO, :]


def _pool(h, w_pool, pool_scale, ts):
    b, s, _ = h.shape
    width = POOL_GROUP * len(POOL_WINDOWS)
    ts = min(ts, s)
    return pl.pallas_call(
        functools.partial(_pool_kernel, ts=ts),
        grid=(b, s // ts),
        in_specs=[
            pl.BlockSpec((1, ts, width), lambda i, j: (i, j, 0)),
            pl.BlockSpec(w_pool.shape, lambda i, j: (0, 0, 0)),
            pl.BlockSpec((1, width), lambda i, j: (0, 0)),
        ],
        out_specs=pl.BlockSpec((1, ts, width), lambda i, j: (i, j, 0)),
        out_shape=jax.ShapeDtypeStruct((b, s, width), BF16),
        scratch_shapes=[pltpu.VMEM((ts + POOL_HALO, width), F32)],
        compiler_params=_params("arbitrary", "arbitrary"),
        name="pool",
    )(h, w_pool, pool_scale)


def _rope_tables(pos_ref, freq_ref, sa_ref, sb_ref):
    ang = pos_ref[0].astype(F32) * freq_ref[...]
    c, s = jnp.cos(ang), jnp.sin(ang)
    return c, s * sa_ref[...], s * sb_ref[...]


def _rope(t, tables):
    c, sa, sb = tables
    half = ROPE_DIM // 2
    out = []
    for j in range(t.shape[1] // LANES):
        x = t[:, j * LANES:(j + 1) * LANES]
        out.append(x * c + pltpu.roll(x, LANES - half, 1) * sa + pltpu.roll(x, half, 1) * sb)
    return jnp.concatenate(out, axis=1) if len(out) > 1 else out[0]


def _swa_kernel(sink_ref, q0_ref, q1_ref, k_ref, v_ref, kp_ref, vp_ref, pos_ref, posp_ref,
                freq_ref, sa_ref, sb_ref, o_ref):
    n = pl.program_id(1)
    cur = _rope_tables(pos_ref, freq_ref, sa_ref, sb_ref)
    prev = _rope_tables(posp_ref, freq_ref, sa_ref, sb_ref)
    scale = SWA_HEAD_DIM ** -0.5
    q = jnp.concatenate([_rope(q0_ref[0], cur), _rope(q1_ref[0], cur)], axis=1)
    q = (q * scale).astype(BF16)
    k = jnp.concatenate([_rope(kp_ref[0], prev), _rope(k_ref[0], cur)], axis=0).astype(BF16)
    v = jnp.concatenate([vp_ref[0], v_ref[0]], axis=0).astype(BF16)
    row = lax.broadcasted_iota(jnp.int32, (SWA_BLOCK, 2 * SWA_BLOCK), 0)
    col = lax.broadcasted_iota(jnp.int32, (SWA_BLOCK, 2 * SWA_BLOCK), 1)
    rel = row + SWA_BLOCK - col
    valid = (rel >= 0) & (rel < SWA_BLOCK) & ((col >= SWA_BLOCK) | (n > 0))
    group = SWA_HEADS // SWA_KV_HEADS
    outs = []
    for hq in range(SWA_HEADS):
        kv = hq // group
        qh = q[:, hq * SWA_HEAD_DIM:(hq + 1) * SWA_HEAD_DIM]
        kh = k[:, kv * SWA_HEAD_DIM:(kv + 1) * SWA_HEAD_DIM]
        vh = v[:, kv * SWA_HEAD_DIM:(kv + 1) * SWA_HEAD_DIM]
        sc = lax.dot_general(qh, kh, (((1,), (1,)), ((), ())), preferred_element_type=F32)
        sc = jnp.where(valid, sc, NEG)
        sink = sink_ref[hq]
        m = jnp.maximum(jnp.max(sc, axis=1, keepdims=True), sink)
        p = jnp.exp(sc - m)
        denom = jnp.sum(p, axis=1, keepdims=True) + jnp.exp(sink - m)
        o = jnp.dot(p.astype(BF16), vh, preferred_element_type=F32)
        outs.append(o / denom)
    o_ref[0] = jnp.concatenate(outs, axis=1).astype(o_ref.dtype)


def _rope_constants():
    lane = np.arange(LANES)
    d = lane % SWA_HEAD_DIM
    half = ROPE_DIM // 2
    inv_freq = np.float32(ROPE_THETA) ** (-np.arange(0, ROPE_DIM, 2, dtype=np.float32) / np.float32(ROPE_DIM))
    freq = np.where(d < ROPE_DIM, inv_freq[d % half], 0.0).astype(np.float32)
    sa = np.where(d < half, -1.0, 0.0).astype(np.float32)
    sb = np.where((d >= half) & (d < ROPE_DIM), 1.0, 0.0).astype(np.float32)
    return [jnp.asarray(a.reshape(1, LANES)) for a in (freq, sa, sb)]


def _swa(h, positions, sinks):
    b, s, _ = h.shape
    nb = s // SWA_BLOCK
    pos3 = positions.reshape(b, s, 1)
    freq, sa, sb = _rope_constants()
    blk = lambda w, c: pl.BlockSpec((1, SWA_BLOCK, w), lambda i, j: (i, j, c))
    blk_prev = lambda w, c: pl.BlockSpec((1, SWA_BLOCK, w), lambda i, j: (i, jnp.maximum(j - 1, 0), c))
    const = pl.BlockSpec((1, LANES), lambda i, j: (0, 0))
    kvw = SWA_KV_HEADS * SWA_HEAD_DIM
    return pl.pallas_call(
        _swa_kernel,
        grid=(b, nb),
        in_specs=[
            pl.BlockSpec(memory_space=pltpu.SMEM),
            blk(512, 1), blk(512, 2), blk(kvw, 6), blk(kvw, 7), blk_prev(kvw, 6), blk_prev(kvw, 7),
            blk(1, 0), blk_prev(1, 0), const, const, const,
        ],
        out_specs=pl.BlockSpec((1, SWA_BLOCK, SWA_HEADS * SWA_HEAD_DIM), lambda i, j: (i, j, 0)),
        out_shape=jax.ShapeDtypeStruct((b, s, SWA_HEADS * SWA_HEAD_DIM), BF16),
        compiler_params=_params("parallel", "arbitrary"),
        name="swa",
    )(sinks, h, h, h, h, h, h, pos3, pos3, freq, sa, sb)


def _mem_kernel(q_ref, kv_ref, o_ref):
    scale = MEM_HEAD_DIM ** -0.5
    width = MEM_HEADS * MEM_HEAD_DIM
    for hm in range(MEM_HEADS):
        cols = slice(hm * MEM_HEAD_DIM, (hm + 1) * MEM_HEAD_DIM)
        q = (q_ref[0, :, cols] * scale).astype(BF16)
        km = kv_ref[0, :, cols]
        vm = kv_ref[0, :, width + hm * MEM_HEAD_DIM:width + (hm + 1) * MEM_HEAD_DIM]
        sc = lax.dot_general(q, km, (((1,), (1,)), ((), ())), preferred_element_type=F32)
        m = jnp.max(sc, axis=1, keepdims=True)
        p = jnp.exp(sc - m)
        denom = jnp.sum(p, axis=1, keepdims=True)
        o = jnp.dot(p.astype(BF16), vm, preferred_element_type=F32)
        o_ref[0, :, cols] = (o / denom).astype(o_ref.dtype)


def _mem_attention(h, kvm, tq):
    b, s, _ = h.shape
    width = MEM_HEADS * MEM_HEAD_DIM
    tq = min(tq, s)
    return pl.pallas_call(
        _mem_kernel,
        grid=(b, s // tq),
        in_specs=[
            pl.BlockSpec((1, tq, width), lambda i, j: (i, j, 4)),
            pl.BlockSpec((1,) + kvm.shape[1:], lambda i, j: (i, 0, 0)),
        ],
        out_specs=pl.BlockSpec((1, tq, width), lambda i, j: (i, j, 0)),
        out_shape=jax.ShapeDtypeStruct((b, s, width), BF16),
        compiler_params=_params("parallel", "arbitrary"),
        name="mem_attention",
    )(h, kvm)


def _outproj_kernel(pool_ref, swa_ref, mem_ref, wp_ref, ws_ref, wm_ref, x_ref, g_ref, b_ref, o_ref, ob_ref,
                    *, alpha):
    mix = jnp.dot(pool_ref[...], wp_ref[...], preferred_element_type=F32)
    mix += jnp.dot(swa_ref[...], ws_ref[...], preferred_element_type=F32)
    mix += jnp.dot(mem_ref[...], wm_ref[...], preferred_element_type=F32)
    z = alpha * x_ref[...] + mix
    mu = jnp.mean(z, axis=1, keepdims=True)
    zc = z - mu
    var = jnp.mean(zc * zc, axis=1, keepdims=True)
    y = zc * lax.rsqrt(var + LN_EPS) * g_ref[...] + b_ref[...]
    yt = y.T
    o_ref[...] = yt
    ob_ref[...] = yt.astype(BF16)


def _outproj_ln(pool_o, swa_o, mem_o, w_out, x2, g, b, alpha, tm):
    t, d = x2.shape
    tm = min(tm, t)
    wp, ws, wm = pool_o.shape[1], swa_o.shape[1], mem_o.shape[1]
    w_p, w_s, w_m = w_out[:wp], w_out[wp:wp + ws], w_out[wp + ws:]
    row = lambda w: pl.BlockSpec((tm, w), lambda i: (i, 0))
    full = lambda a: pl.BlockSpec(a.shape, lambda i: (0, 0))
    return pl.pallas_call(
        functools.partial(_outproj_kernel, alpha=alpha),
        grid=(t // tm,),
        in_specs=[row(wp), row(ws), row(wm), full(w_p), full(w_s), full(w_m), row(d),
                  pl.BlockSpec((1, d), lambda i: (0, 0)), pl.BlockSpec((1, d), lambda i: (0, 0))],
        out_specs=[pl.BlockSpec((d, tm), lambda i: (0, i))] * 2,
        out_shape=[jax.ShapeDtypeStruct((d, t), F32), jax.ShapeDtypeStruct((d, t), BF16)],
        compiler_params=_params("parallel"),
        name="outproj_ln",
    )(pool_o, swa_o, mem_o, w_p, w_s, w_m, x2, g, b)


def _top16_rows(s):
    n = s.shape[0]
    iota = lax.broadcasted_iota(jnp.int32, s.shape, 0).astype(F32)
    rank = jnp.full(s.shape, float(PEER_TOPK), F32)
    vals = []
    for r in range(PEER_TOPK):
        m = jnp.max(s, axis=0, keepdims=True)
        idx = jnp.min(jnp.where(s == m, iota, float(n)), axis=0, keepdims=True)
        hit = iota == idx
        rank = jnp.where(hit, float(r), rank)
        s = jnp.where(hit, -jnp.inf, s)
        vals.append(m)
    return rank, vals


_CAND_SMALL_A = PEER_TOPK // 2
_CAND_ROWS = PEER_TOPK + (_CAND_SMALL_A - 1) * SUBLANES + SUBLANES


def _cand_constants():
    flat = np.full((_CAND_ROWS, 1), 1e9, np.float32)
    valid = np.zeros((_CAND_ROWS, 1), np.float32)
    for b in range(PEER_TOPK):
        flat[b, 0], valid[b, 0] = b, 1.0
    for a in range(1, _CAND_SMALL_A):
        base = PEER_TOPK + (a - 1) * SUBLANES
        for b in range(PEER_TOPK // (a + 1)):
            flat[base + b, 0], valid[base + b, 0] = a * PEER_TOPK + b, 1.0
    base = PEER_TOPK + (_CAND_SMALL_A - 1) * SUBLANES
    for k in range(SUBLANES):
        flat[base + k, 0], valid[base + k, 0] = (_CAND_SMALL_A + k) * PEER_TOPK, 1.0
    return jnp.asarray(flat), jnp.asarray(valid)


def _select_experts(s1, s2, flat, valid):
    t = s1.shape[1]
    rank1, v1 = _top16_rows(s1)
    rank2, v2 = _top16_rows(s2)
    v2_lo = jnp.concatenate(v2[:SUBLANES], axis=0)
    v2_all = jnp.concatenate(v2, axis=0)
    v1_hi = jnp.concatenate(v1[_CAND_SMALL_A:], axis=0)
    groups = [v1[0] + v2_all]
    for a in range(1, _CAND_SMALL_A):
        groups.append(v1[a] + v2_lo)
    groups.append(v1_hi + v2[0])
    cand = jnp.concatenate(groups, axis=0)
    cand = jnp.where(valid > 0.5, cand, -jnp.inf)
    flat_b = jnp.broadcast_to(flat, cand.shape)
    hits = jnp.zeros(cand.shape, F32)
    top = []
    for r in range(PEER_TOPK):
        m = jnp.max(cand, axis=0, keepdims=True)
        pick = jnp.min(jnp.where(cand == m, flat_b, 2e9), axis=0, keepdims=True)
        hit = flat_b == pick
        hits = jnp.where(hit, 1.0, hits)
        cand = jnp.where(hit, -jnp.inf, cand)
        top.append(m)
    z = jnp.ones((1, t), F32)
    for r in range(1, PEER_TOPK):
        z = z + jnp.exp(top[r] - top[0])
    counts = [jnp.sum(hits[0:PEER_TOPK], axis=0, keepdims=True)]
    for a in range(1, _CAND_SMALL_A):
        base = PEER_TOPK + (a - 1) * SUBLANES
        counts.append(jnp.sum(hits[base:base + SUBLANES], axis=0, keepdims=True))
    base = PEER_TOPK + (_CAND_SMALL_A - 1) * SUBLANES
    for k in range(SUBLANES):
        counts.append(hits[base + k:base + k + 1])
    lim = jnp.zeros(s1.shape, F32)
    for a in range(PEER_TOPK):
        lim = jnp.where(rank1 == float(a), counts[a], lim)
    e1n = jnp.exp(s1 - v1[0]) / z
    e2 = jnp.exp(s2 - v2[0])
    return lim, e1n, rank2, e2


def _retrieve_kernel(wq_ref, x_ref, k1_ref, k2_ref, flat_ref, valid_ref,
                     lim_ref, e1_ref, r2_ref, e2_ref, q_ref, *, tq):
    q_ref[...] = jnp.dot(wq_ref[...], x_ref[...], preferred_element_type=F32)
    flat, valid = flat_ref[...], valid_ref[...]

    def head(h, carry):
        for c in range(tq // LANES):
            lanes = slice(c * LANES, (c + 1) * LANES)
            r0 = pl.multiple_of(h * 2 * HALF_DIM, 2 * HALF_DIM)
            q1 = q_ref[pl.ds(r0, HALF_DIM), lanes].astype(BF16)
            q2 = q_ref[pl.ds(r0 + HALF_DIM, HALF_DIM), lanes].astype(BF16)
            s1 = jnp.dot(k1_ref[...], q1, preferred_element_type=F32)
            s2 = jnp.dot(k2_ref[...], q2, preferred_element_type=F32)
            lim, e1n, rank2, e2 = _select_experts(s1, s2, flat, valid)
            lim_ref[h, :, lanes] = lim
            e1_ref[h, :, lanes] = e1n
            r2_ref[h, :, lanes] = rank2.astype(BF16)
            e2_ref[h, :, lanes] = e2.astype(BF16)
        return carry

    lax.fori_loop(0, PEER_HEADS, head, 0)


def _retrieve(x1t, wq_t, k1, k2, tq):
    d, t = x1t.shape
    tq = min(tq, t)
    flat, valid = _cand_constants()
    full = lambda a: pl.BlockSpec(a.shape, lambda i: (0,) * a.ndim)
    out = jax.ShapeDtypeStruct((PEER_HEADS, N_KEYS, t), F32)
    out_packed = jax.ShapeDtypeStruct((PEER_HEADS, N_KEYS, t), BF16)
    out_spec = pl.BlockSpec((PEER_HEADS, N_KEYS, tq), lambda i: (0, 0, i))
    return pl.pallas_call(
        functools.partial(_retrieve_kernel, tq=tq),
        grid=(t // tq,),
        in_specs=[full(wq_t), pl.BlockSpec((d, tq), lambda i: (0, i)), full(k1), full(k2), full(flat), full(valid)],
        out_specs=[out_spec] * 4,
        out_shape=[out, out, out_packed, out_packed],
        scratch_shapes=[pltpu.VMEM((wq_t.shape[0], tq), F32)],
        compiler_params=_params("parallel"),
        name="peer_retrieve",
    )(wq_t, x1t, k1, k2, flat, valid)


def _gelu_exact(x):
    return 0.5 * x * (1.0 + lax.erf(x * (1.0 / math.sqrt(2.0))))


def _experts_kernel(xb_ref, u_ref, vt_ref, lim_ref, e1_ref, r2_in_ref, e2_in_ref, o_ref,
                    h_ref, a_ref, rows_ref, gate_ref, r2_ref, e2_ref, *, tm, te):
    e = pl.program_id(1)

    @pl.when(e == 0)
    def _():
        o_ref[...] = jnp.zeros(o_ref.shape, F32)
        r2_ref[:, 0:tm] = r2_in_ref[...]
        e2_ref[:, LANES:LANES + tm] = e2_in_ref[...]

    n_chunks = te // EXPERT_CHUNK

    def rows_of(p):
        return slice(p * EXPERT_CHUNK, (p + 1) * EXPERT_CHUNK)

    def gates(p):
        keys = range(p * EXPERT_CHUNK // N_KEYS, (p + 1) * EXPERT_CHUNK // N_KEYS)
        blocks = range(0, N_KEYS, GATE_ROWS)
        for i in keys:
            for hd in range(PEER_HEADS):
                for q, ref in enumerate((lim_ref, e1_ref)):
                    row = jnp.broadcast_to(ref[i, hd:hd + 1, :], (GATE_ROWS, tm)).astype(BF16)
                    rows_ref[q, i, hd, :, q * LANES:q * LANES + tm] = row
        for c in range(tm // LANES):
            lanes = slice(c * LANES, (c + 1) * LANES)
            lanes1 = slice((c + 1) * LANES, (c + 2) * LANES)
            gate = {i: {jb: jnp.zeros((GATE_ROWS, LANES), BF16) for jb in blocks} for i in keys}
            for hd in range(PEER_HEADS):
                lim = {i: rows_ref[0, i, hd, :, lanes] for i in keys}
                e1 = {i: rows_ref[1, i, hd, :, lanes1] for i in keys}
                for jb in blocks:
                    r2 = r2_ref[hd * N_KEYS + jb:hd * N_KEYS + jb + GATE_ROWS, lanes]
                    e2 = e2_ref[hd * N_KEYS + jb:hd * N_KEYS + jb + GATE_ROWS, lanes1]
                    for i in keys:
                        gate[i][jb] = gate[i][jb] + jnp.where(r2 < lim[i], e2 * e1[i], jnp.zeros_like(e2))
            for i in keys:
                for jb in blocks:
                    gate_ref[i * N_KEYS + jb:i * N_KEYS + jb + GATE_ROWS, lanes] = gate[i][jb]

    for p in range(n_chunks):
        gates(p)
    h_ref[...] = jnp.dot(u_ref[...], xb_ref[...], preferred_element_type=F32)
    for p in range(n_chunks):
        a_ref[rows_of(p), :] = gate_ref[rows_of(p), :] * _gelu_exact(h_ref[rows_of(p), :]).astype(BF16)
        o_ref[...] += jnp.dot(vt_ref[:, rows_of(p)], a_ref[rows_of(p), :], preferred_element_type=F32)


def _experts(xbt, u, vt, sel, tm, te):
    d, t = xbt.shape
    n_exp = u.shape[0]
    tm, te = min(tm, t), min(te, n_exp)
    tok = pl.BlockSpec((d, tm), lambda i, j: (0, i))
    sel_spec = pl.BlockSpec((PEER_HEADS * N_KEYS, tm), lambda i, j: (0, i))
    key_spec = pl.BlockSpec((te // N_KEYS, PEER_HEADS, tm), lambda i, j: (j, 0, i))
    lim, e1n, rank2, e2 = sel
    sel = (lim.transpose(1, 0, 2), e1n.transpose(1, 0, 2), rank2.reshape(-1, t), e2.reshape(-1, t))
    return pl.pallas_call(
        functools.partial(_experts_kernel, tm=tm, te=te),
        grid=(t // tm, n_exp // te),
        in_specs=[tok, pl.BlockSpec((te, d), lambda i, j: (j, 0)), pl.BlockSpec((d, te), lambda i, j: (0, j)),
                  key_spec, key_spec, sel_spec, sel_spec],
        out_specs=tok,
        out_shape=jax.ShapeDtypeStruct((d, t), F32),
        scratch_shapes=[pltpu.VMEM((te, tm), F32), pltpu.VMEM((te, tm), BF16),
                        pltpu.VMEM((2, te // N_KEYS, PEER_HEADS, GATE_ROWS, tm + LANES), BF16),
                        pltpu.VMEM((te, tm), BF16),
                        pltpu.VMEM((PEER_HEADS * N_KEYS, tm + LANES), BF16),
                        pltpu.VMEM((PEER_HEADS * N_KEYS, tm + LANES), BF16)],
        compiler_params=_params("parallel", "arbitrary"),
        name="peer_experts",
    )(xbt, u, vt, *sel)


def _ln_t_kernel(x_ref, f_ref, g_ref, b_ref, o_ref, *, alpha):
    z = (alpha * x_ref[...] + f_ref[...]).T
    mu = jnp.mean(z, axis=1, keepdims=True)
    zc = z - mu
    var = jnp.mean(zc * zc, axis=1, keepdims=True)
    o_ref[...] = zc * lax.rsqrt(var + LN_EPS) * g_ref[...] + b_ref[...]


def _residual_ln_t(x1t, fft, g, b, alpha, tm):
    d, t = x1t.shape
    tm = min(tm, t)
    tok = pl.BlockSpec((d, tm), lambda i: (0, i))
    vec = pl.BlockSpec((1, d), lambda i: (0, 0))
    return pl.pallas_call(
        functools.partial(_ln_t_kernel, alpha=alpha),
        grid=(t // tm,),
        in_specs=[tok, tok, vec, vec],
        out_specs=pl.BlockSpec((tm, d), lambda i: (i, 0)),
        out_shape=jax.ShapeDtypeStruct((t, d), F32),
        compiler_params=_params("parallel"),
        name="residual_ln",
    )(x1t, fft, g, b)


def kernel(x, mem, positions, w_in, w_mem_kv, w_pool, pool_scale, attn_sinks, w_out, ln1_g, ln1_b,
           w_peer_q, sub_keys_1, sub_keys_2, expert_u, expert_v, ln2_g, ln2_b):
    bsz, seq, d = x.shape
    depth = w_in.shape[0]
    t = bsz * seq
    alpha = (2.0 * depth) ** 0.25
    for l in range(depth):
        x2 = x.reshape(t, d)
        h = _matmul(x2.astype(BF16), w_in[l].astype(BF16), F32, 1024, 512).reshape(bsz, seq, -1)
        mem2 = mem.reshape(-1, d).astype(BF16)
        kvm = _matmul(mem2, w_mem_kv[l].astype(BF16), BF16, 512, 512).reshape(bsz, mem.shape[1], -1)
        pool_o = _pool(h, w_pool[l].astype(BF16), pool_scale[l].reshape(1, -1), 512)
        swa_o = _swa(h, positions, attn_sinks[l])
        mem_o = _mem_attention(h, kvm, 512)
        x1t, x1bt = _outproj_ln(pool_o.reshape(t, -1), swa_o.reshape(t, -1), mem_o.reshape(t, -1),
                          w_out[l].astype(BF16), x2, ln1_g[l].reshape(1, d), ln1_b[l].reshape(1, d), alpha, 512)
        sel = _retrieve(x1bt, w_peer_q[l].T.astype(BF16), sub_keys_1[l].astype(BF16),
                        sub_keys_2[l].astype(BF16), 256)
        fft = _experts(x1bt, expert_u[l].astype(BF16), expert_v[l].T.astype(BF16), sel, 1024, 512)
        x = _residual_ln_t(x1t, fft, ln2_g[l].reshape(1, d), ln2_b[l].reshape(1, d), alpha, 512).reshape(bsz, seq, d)
    return x
```

```python
import functools
import math

import numpy as np
import jax
import jax.numpy as jnp
from jax import lax
from jax.experimental import pallas as pl
from jax.experimental.pallas import tpu as pltpu

F32 = jnp.float32
BF16 = jnp.bfloat16

LANES = 128
SUBLANES = 8
VMEM_LIMIT_BYTES = 56 * 1024 * 1024

POOL_WINDOWS = (2, 4, 8, 16)
POOL_GROUP = 128
POOL_HALO = 16
SWA_HEAD_DIM = 64
SWA_HEADS = 16
SWA_KV_HEADS = 4
SWA_BLOCK = 128
ROPE_THETA = 500000.0
ROPE_DIM = 16
MEM_HEADS = 4
MEM_HEAD_DIM = 128
PEER_HEADS = 8
N_KEYS = 128
PEER_TOPK = 16
HALF_DIM = 128
LN_EPS = 1e-5
NEG = -1e30

EXPERT_CHUNK = 256
GATE_ROWS = 16


def _params(*semantics):
    return pltpu.CompilerParams(dimension_semantics=semantics, vmem_limit_bytes=VMEM_LIMIT_BYTES)


def _matmul_kernel(a_ref, b_ref, o_ref):
    o_ref[...] = jnp.dot(a_ref[...], b_ref[...], preferred_element_type=F32).astype(o_ref.dtype)


def _matmul(a, b, out_dtype, tm, tn):
    m, k = a.shape
    n = b.shape[1]
    tm, tn = min(tm, m), min(tn, n)
    return pl.pallas_call(
        _matmul_kernel,
        grid=(m // tm, n // tn),
        in_specs=[pl.BlockSpec((tm, k), lambda i, j: (i, 0)), pl.BlockSpec((k, tn), lambda i, j: (0, j))],
        out_specs=pl.BlockSpec((tm, tn), lambda i, j: (i, j)),
        out_shape=jax.ShapeDtypeStruct((m, n), out_dtype),
        compiler_params=_params("parallel", "arbitrary"),
        name="matmul",
    )(a, b)


def _pool_kernel(v_ref, w_ref, scale_ref, o_ref, ext_ref, *, ts):
    s = pl.program_id(1)

    @pl.when(s == 0)
    def _():
        ext_ref[0:POOL_HALO, :] = jnp.zeros((POOL_HALO, ext_ref.shape[1]), F32)

    ext_ref[POOL_HALO:POOL_HALO + ts, :] = v_ref[0]
    pos = s * ts + lax.broadcasted_iota(jnp.int32, (ts, 1), 0)
    for g, w in enumerate(POOL_WINDOWS):
        cols = slice(g * POOL_GROUP, (g + 1) * POOL_GROUP)
        acc = ext_ref[POOL_HALO:POOL_HALO + ts, cols]
        for k in range(1, w):
            acc = acc + ext_ref[POOL_HALO - k:POOL_HALO - k + ts, cols]
        count = jnp.minimum(pos + 1, w).astype(F32)
        pooled = acc / count - ext_ref[POOL_HALO:POOL_HALO + ts, cols]
        y = jnp.dot(pooled.astype(BF16), w_ref[g], preferred_element_type=F32)
        o_ref[0, :, cols] = (y * scale_ref[:, cols]).astype(o_ref.dtype)
    ext_ref[0:POOL_HALO, :] = ext_ref[ts:ts + POOL_HALO, :]


def _pool(h, w_pool, pool_scale, ts):
    b, s, _ = h.shape
    width = POOL_GROUP * len(POOL_WINDOWS)
    ts = min(ts, s)
    return pl.pallas_call(
        functools.partial(_pool_kernel, ts=ts),
        grid=(b, s // ts),
        in_specs=[
            pl.BlockSpec((1, ts, width), lambda i, j: (i, j, 0)),
            pl.BlockSpec(w_pool.shape, lambda i, j: (0, 0, 0)),
            pl.BlockSpec((1, width), lambda i, j: (0, 0)),
        ],
        out_specs=pl.BlockSpec((1, ts, width), lambda i, j: (i, j, 0)),
        out_shape=jax.ShapeDtypeStruct((b, s, width), BF16),
        scratch_shapes=[pltpu.VMEM((ts + POOL_HALO, width), F32)],
        compiler_params=_params("arbitrary", "arbitrary"),
        name="pool",
    )(h, w_pool, pool_scale)


def _rope_tables(pos_ref, freq_ref, sa_ref, sb_ref):
    ang = pos_ref[0].astype(F32) * freq_ref[...]
    c, s = jnp.cos(ang), jnp.sin(ang)
    return c, s * sa_ref[...], s * sb_ref[...]


def _rope(t, tables):
    c, sa, sb = tables
    half = ROPE_DIM // 2
    out = []
    for j in range(t.shape[1] // LANES):
        x = t[:, j * LANES:(j + 1) * LANES]
        out.append(x * c + pltpu.roll(x, LANES - half, 1) * sa + pltpu.roll(x, half, 1) * sb)
    return jnp.concatenate(out, axis=1) if len(out) > 1 else out[0]


def _swa_kernel(sink_ref, q0_ref, q1_ref, k_ref, v_ref, kp_ref, vp_ref, pos_ref, posp_ref,
                freq_ref, sa_ref, sb_ref, o_ref):
    n = pl.program_id(1)
    cur = _rope_tables(pos_ref, freq_ref, sa_ref, sb_ref)
    prev = _rope_tables(posp_ref, freq_ref, sa_ref, sb_ref)
    scale = SWA_HEAD_DIM ** -0.5
    q = jnp.concatenate([_rope(q0_ref[0], cur), _rope(q1_ref[0], cur)], axis=1)
    q = (q * scale).astype(BF16)
    k = jnp.concatenate([_rope(kp_ref[0], prev), _rope(k_ref[0], cur)], axis=0).astype(BF16)
    v = jnp.concatenate([vp_ref[0], v_ref[0]], axis=0).astype(BF16)
    row = lax.broadcasted_iota(jnp.int32, (SWA_BLOCK, 2 * SWA_BLOCK), 0)
    col = lax.broadcasted_iota(jnp.int32, (SWA_BLOCK, 2 * SWA_BLOCK), 1)
    rel = row + SWA_BLOCK - col
    valid = (rel >= 0) & (rel < SWA_BLOCK) & ((col >= SWA_BLOCK) | (n > 0))
    group = SWA_HEADS // SWA_KV_HEADS
    outs = []
    for hq in range(SWA_HEADS):
        kv = hq // group
        qh = q[:, hq * SWA_HEAD_DIM:(hq + 1) * SWA_HEAD_DIM]
        kh = k[:, kv * SWA_HEAD_DIM:(kv + 1) * SWA_HEAD_DIM]
        vh = v[:, kv * SWA_HEAD_DIM:(kv + 1) * SWA_HEAD_DIM]
        sc = lax.dot_general(qh, kh, (((1,), (1,)), ((), ())), preferred_element_type=F32)
        sc = jnp.where(valid, sc, NEG)
        sink = sink_ref[hq]
        m = jnp.maximum(jnp.max(sc, axis=1, keepdims=True), sink)
        p = jnp.exp(sc - m)
        denom = jnp.sum(p, axis=1, keepdims=True) + jnp.exp(sink - m)
        o = jnp.dot(p.astype(BF16), vh, preferred_element_type=F32)
        outs.append(o / denom)
    o_ref[0] = jnp.concatenate(outs, axis=1).astype(o_ref.dtype)


def _rope_constants():
    lane = np.arange(LANES)
    d = lane % SWA_HEAD_DIM
    half = ROPE_DIM // 2
    inv_freq = np.float32(ROPE_THETA) ** (-np.arange(0, ROPE_DIM, 2, dtype=np.float32) / np.float32(ROPE_DIM))
    freq = np.where(d < ROPE_DIM, inv_freq[d % half], 0.0).astype(np.float32)
    sa = np.where(d < half, -1.0, 0.0).astype(np.float32)
    sb = np.where((d >= half) & (d < ROPE_DIM), 1.0, 0.0).astype(np.float32)
    return [jnp.asarray(a.reshape(1, LANES)) for a in (freq, sa, sb)]


def _swa(h, positions, sinks):
    b, s, _ = h.shape
    nb = s // SWA_BLOCK
    pos3 = positions.reshape(b, s, 1)
    freq, sa, sb = _rope_constants()
    blk = lambda w, c: pl.BlockSpec((1, SWA_BLOCK, w), lambda i, j: (i, j, c))
    blk_prev = lambda w, c: pl.BlockSpec((1, SWA_BLOCK, w), lambda i, j: (i, jnp.maximum(j - 1, 0), c))
    const = pl.BlockSpec((1, LANES), lambda i, j: (0, 0))
    kvw = SWA_KV_HEADS * SWA_HEAD_DIM
    return pl.pallas_call(
        _swa_kernel,
        grid=(b, nb),
        in_specs=[
            pl.BlockSpec(memory_space=pltpu.SMEM),
            blk(512, 1), blk(512, 2), blk(kvw, 6), blk(kvw, 7), blk_prev(kvw, 6), blk_prev(kvw, 7),
            blk(1, 0), blk_prev(1, 0), const, const, const,
        ],
        out_specs=pl.BlockSpec((1, SWA_BLOCK, SWA_HEADS * SWA_HEAD_DIM), lambda i, j: (i, j, 0)),
        out_shape=jax.ShapeDtypeStruct((b, s, SWA_HEADS * SWA_HEAD_DIM), BF16),
        compiler_params=_params("parallel", "arbitrary"),
        name="swa",
    )(sinks, h, h, h, h, h, h, pos3, pos3, freq, sa, sb)


def _mem_kernel(q_ref, kv_ref, o_ref):
    scale = MEM_HEAD_DIM ** -0.5
    width = MEM_HEADS * MEM_HEAD_DIM
    for hm in range(MEM_HEADS):
        cols = slice(hm * MEM_HEAD_DIM, (hm + 1) * MEM_HEAD_DIM)
        q = (q_ref[0, :, cols] * scale).astype(BF16)
        km = kv_ref[0, :, cols]
        vm = kv_ref[0, :, width + hm * MEM_HEAD_DIM:width + (hm + 1) * MEM_HEAD_DIM]
        sc = lax.dot_general(q, km, (((1,), (1,)), ((), ())), preferred_element_type=F32)
        m = jnp.max(sc, axis=1, keepdims=True)
        p = jnp.exp(sc - m)
        denom = jnp.sum(p, axis=1, keepdims=True)
        o = jnp.dot(p.astype(BF16), vm, preferred_element_type=F32)
        o_ref[0, :, cols] = (o / denom).astype(o_ref.dtype)


def _mem_attention(h, kvm, tq):
    b, s, _ = h.shape
    width = MEM_HEADS * MEM_HEAD_DIM
    tq = min(tq, s)
    return pl.pallas_call(
        _mem_kernel,
        grid=(b, s // tq),
        in_specs=[
            pl.BlockSpec((1, tq, width), lambda i, j: (i, j, 4)),
            pl.BlockSpec((1,) + kvm.shape[1:], lambda i, j: (i, 0, 0)),
        ],
        out_specs=pl.BlockSpec((1, tq, width), lambda i, j: (i, j, 0)),
        out_shape=jax.ShapeDtypeStruct((b, s, width), BF16),
        compiler_params=_params("parallel", "arbitrary"),
        name="mem_attention",
    )(h, kvm)


def _outproj_kernel(pool_ref, swa_ref, mem_ref, wp_ref, ws_ref, wm_ref, x_ref, g_ref, b_ref, o_ref, ob_ref,
                    *, alpha):
    mix = jnp.dot(pool_ref[...], wp_ref[...], preferred_element_type=F32)
    mix += jnp.dot(swa_ref[...], ws_ref[...], preferred_element_type=F32)
    mix += jnp.dot(mem_ref[...], wm_ref[...], preferred_element_type=F32)
    z = alpha * x_ref[...] + mix
    mu = jnp.mean(z, axis=1, keepdims=True)
    zc = z - mu
    var = jnp.mean(zc * zc, axis=1, keepdims=True)
    y = zc * lax.rsqrt(var + LN_EPS) * g_ref[...] + b_ref[...]
    yt = y.T
    o_ref[...] = yt
    ob_ref[...] = yt.astype(BF16)


def _outproj_ln(pool_o, swa_o, mem_o, w_out, x2, g, b, alpha, tm):
    t, d = x2.shape
    tm = min(tm, t)
    wp, ws, wm = pool_o.shape[1], swa_o.shape[1], mem_o.shape[1]
    w_p, w_s, w_m = w_out[:wp], w_out[wp:wp + ws], w_out[wp + ws:]
    row = lambda w: pl.BlockSpec((tm, w), lambda i: (i, 0))
    full = lambda a: pl.BlockSpec(a.shape, lambda i: (0, 0))
    return pl.pallas_call(
        functools.partial(_outproj_kernel, alpha=alpha),
        grid=(t // tm,),
        in_specs=[row(wp), row(ws), row(wm), full(w_p), full(w_s), full(w_m), row(d),
                  pl.BlockSpec((1, d), lambda i: (0, 0)), pl.BlockSpec((1, d), lambda i: (0, 0))],
        out_specs=[pl.BlockSpec((d, tm), lambda i: (0, i))] * 2,
        out_shape=[jax.ShapeDtypeStruct((d, t), F32), jax.ShapeDtypeStruct((d, t), BF16)],
        compiler_params=_params("parallel"),
        name="outproj_ln",
    )(pool_o, swa_o, mem_o, w_p, w_s, w_m, x2, g, b)


def _top16_rows(s):
    n = s.shape[0]
    iota = lax.broadcasted_iota(jnp.int32, s.shape, 0).astype(F32)
    rank = jnp.full(s.shape, float(PEER_TOPK), F32)
    vals = []
    for r in range(PEER_TOPK):
        m = jnp.max(s, axis=0, keepdims=True)
        idx = jnp.min(jnp.where(s == m, iota, float(n)), axis=0, keepdims=True)
        hit = iota == idx
        rank = jnp.where(hit, float(r), rank)
        s = jnp.where(hit, -jnp.inf, s)
        vals.append(m)
    return rank, vals


_CAND_SMALL_A = PEER_TOPK // 2
_CAND_ROWS = PEER_TOPK + (_CAND_SMALL_A - 1) * SUBLANES + SUBLANES


def _cand_constants():
    flat = np.full((_CAND_ROWS, 1), 1e9, np.float32)
    valid = np.zeros((_CAND_ROWS, 1), np.float32)
    for b in range(PEER_TOPK):
        flat[b, 0], valid[b, 0] = b, 1.0
    for a in range(1, _CAND_SMALL_A):
        base = PEER_TOPK + (a - 1) * SUBLANES
        for b in range(PEER_TOPK // (a + 1)):
            flat[base + b, 0], valid[base + b, 0] = a * PEER_TOPK + b, 1.0
    base = PEER_TOPK + (_CAND_SMALL_A - 1) * SUBLANES
    for k in range(SUBLANES):
        flat[base + k, 0], valid[base + k, 0] = (_CAND_SMALL_A + k) * PEER_TOPK, 1.0
    return jnp.asarray(flat), jnp.asarray(valid)


def _select_experts(s1, s2, flat, valid):
    t = s1.shape[1]
    rank1, v1 = _top16_rows(s1)
    rank2, v2 = _top16_rows(s2)
    v2_lo = jnp.concatenate(v2[:SUBLANES], axis=0)
    v2_all = jnp.concatenate(v2, axis=0)
    v1_hi = jnp.concatenate(v1[_CAND_SMALL_A:], axis=0)
    groups = [v1[0] + v2_all]
    for a in range(1, _CAND_SMALL_A):
        groups.append(v1[a] + v2_lo)
    groups.append(v1_hi + v2[0])
    cand = jnp.concatenate(groups, axis=0)
    cand = jnp.where(valid > 0.5, cand, -jnp.inf)
    flat_b = jnp.broadcast_to(flat, cand.shape)
    hits = jnp.zeros(cand.shape, F32)
    top = []
    for r in range(PEER_TOPK):
        m = jnp.max(cand, axis=0, keepdims=True)
        pick = jnp.min(jnp.where(cand == m, flat_b, 2e9), axis=0, keepdims=True)
        hit = flat_b == pick
        hits = jnp.where(hit, 1.0, hits)
        cand = jnp.where(hit, -jnp.inf, cand)
        top.append(m)
    z = jnp.ones((1, t), F32)
    for r in range(1, PEER_TOPK):
        z = z + jnp.exp(top[r] - top[0])
    counts = [jnp.sum(hits[0:PEER_TOPK], axis=0, keepdims=True)]
    for a in range(1, _CAND_SMALL_A):
        base = PEER_TOPK + (a - 1) * SUBLANES
        counts.append(jnp.sum(hits[base:base + SUBLANES], axis=0, keepdims=True))
    base = PEER_TOPK + (_CAND_SMALL_A - 1) * SUBLANES
    for k in range(SUBLANES):
        counts.append(hits[base + k:base + k + 1])
    lim = jnp.zeros(s1.shape, F32)
    for a in range(PEER_TOPK):
        lim = jnp.where(rank1 == float(a), counts[a], lim)
    e1n = jnp.exp(s1 - v1[0]) / z
    e2 = jnp.exp(s2 - v2[0])
    return lim, e1n, rank2, e2


def _retrieve_kernel(wq_ref, x_ref, k1_ref, k2_ref, flat_ref, valid_ref,
                     lim_ref, e1_ref, r2_ref, e2_ref, q_ref, *, tq):
    q_ref[...] = jnp.dot(wq_ref[...], x_ref[...], preferred_element_type=F32)
    flat, valid = flat_ref[...], valid_ref[...]

    def head(h, carry):
        for c in range(tq // LANES):
            lanes = slice(c * LANES, (c + 1) * LANES)
            r0 = pl.multiple_of(h * 2 * HALF_DIM, 2 * HALF_DIM)
            q1 = q_ref[pl.ds(r0, HALF_DIM), lanes].astype(BF16)
            q2 = q_ref[pl.ds(r0 + HALF_DIM, HALF_DIM), lanes].astype(BF16)
            s1 = jnp.dot(k1_ref[...], q1, preferred_element_type=F32)
            s2 = jnp.dot(k2_ref[...], q2, preferred_element_type=F32)
            lim, e1n, rank2, e2 = _select_experts(s1, s2, flat, valid)
            lim_ref[h, :, lanes] = lim
            e1_ref[h, :, lanes] = e1n
            r2_ref[h, :, lanes] = rank2.astype(BF16)
            e2_ref[h, :, lanes] = e2.astype(BF16)
        return carry

    lax.fori_loop(0, PEER_HEADS, head, 0)


def _retrieve(x1t, wq_t, k1, k2, tq):
    d, t = x1t.shape
    tq = min(tq, t)
    flat, valid = _cand_constants()
    full = lambda a: pl.BlockSpec(a.shape, lambda i: (0,) * a.ndim)
    out = jax.ShapeDtypeStruct((PEER_HEADS, N_KEYS, t), F32)
    out_packed = jax.ShapeDtypeStruct((PEER_HEADS, N_KEYS, t), BF16)
    out_spec = pl.BlockSpec((PEER_HEADS, N_KEYS, tq), lambda i: (0, 0, i))
    return pl.pallas_call(
        functools.partial(_retrieve_kernel, tq=tq),
        grid=(t // tq,),
        in_specs=[full(wq_t), pl.BlockSpec((d, tq), lambda i: (0, i)), full(k1), full(k2), full(flat), full(valid)],
        out_specs=[out_spec] * 4,
        out_shape=[out, out, out_packed, out_packed],
        scratch_shapes=[pltpu.VMEM((wq_t.shape[0], tq), F32)],
        compiler_params=_params("parallel"),
        name="peer_retrieve",
    )(wq_t, x1t, k1, k2, flat, valid)


def _gelu_exact(x):
    return 0.5 * x * (1.0 + lax.erf(x * (1.0 / math.sqrt(2.0))))


def _experts_kernel(xb_ref, u_ref, vt_ref, lim_ref, e1_ref, r2_in_ref, e2_in_ref, o_ref,
                    h_ref, a_ref, rows_ref, gate_ref, r2_ref, e2_ref, *, tm, te):
    e = pl.program_id(1)

    @pl.when(e == 0)
    def _():
        o_ref[...] = jnp.zeros(o_ref.shape, F32)
        r2_ref[:, 0:tm] = r2_in_ref[...]
        e2_ref[:, LANES:LANES + tm] = e2_in_ref[...]

    n_chunks = te // EXPERT_CHUNK

    def rows_of(p):
        return slice(p * EXPERT_CHUNK, (p + 1) * EXPERT_CHUNK)

    def gates(p):
        keys = range(p * EXPERT_CHUNK // N_KEYS, (p + 1) * EXPERT_CHUNK // N_KEYS)
        blocks = range(0, N_KEYS, GATE_ROWS)
        for i in keys:
            for hd in range(PEER_HEADS):
                for q, ref in enumerate((lim_ref, e1_ref)):
                    row = jnp.broadcast_to(ref[i, hd:hd + 1, :], (GATE_ROWS, tm)).astype(BF16)
                    rows_ref[q, i, hd, :, q * LANES:q * LANES + tm] = row
        for c in range(tm // LANES):
            lanes = slice(c * LANES, (c + 1) * LANES)
            lanes1 = slice((c + 1) * LANES, (c + 2) * LANES)
            gate = {i: {jb: jnp.zeros((GATE_ROWS, LANES), BF16) for jb in blocks} for i in keys}
            for hd in range(PEER_HEADS):
                lim = {i: rows_ref[0, i, hd, :, lanes] for i in keys}
                e1 = {i: rows_ref[1, i, hd, :, lanes1] for i in keys}
                for jb in blocks:
                    r2 = r2_ref[hd * N_KEYS + jb:hd * N_KEYS + jb + GATE_ROWS, lanes]
                    e2 = e2_ref[hd * N_KEYS + jb:hd * N_KEYS + jb + GATE_ROWS, lanes1]
                    for i in keys:
                        gate[i][jb] = gate[i][jb] + jnp.where(r2 < lim[i], e2 * e1[i], jnp.zeros_like(e2))
            for i in keys:
                for jb in blocks:
                    gate_ref[i * N_KEYS + jb:i * N_KEYS + jb + GATE_ROWS, lanes] = gate[i][jb]

    for p in range(n_chunks):
        gates(p)
    h_ref[:, 0:tm] = jnp.dot(u_ref[...], xb_ref[...], preferred_element_type=F32)
    for p in range(n_chunks):
        act = _gelu_exact(h_ref[rows_of(p), 0:tm]).astype(BF16)
        a_ref[rows_of(p), 0:tm] = gate_ref[rows_of(p), 0:tm] * act
        o_ref[...] += jnp.dot(vt_ref[:, rows_of(p)], a_ref[rows_of(p), 0:tm], preferred_element_type=F32)


def _experts(xbt, u, vt, sel, tm, te):
    d, t = xbt.shape
    n_exp = u.shape[0]
    tm, te = min(tm, t), min(te, n_exp)
    tok = pl.BlockSpec((d, tm), lambda i, j: (0, i))
    sel_spec = pl.BlockSpec((PEER_HEADS * N_KEYS, tm), lambda i, j: (0, i))
    key_spec = pl.BlockSpec((te // N_KEYS, PEER_HEADS, tm), lambda i, j: (j, 0, i))
    lim, e1n, rank2, e2 = sel
    sel = (lim.transpose(1, 0, 2), e1n.transpose(1, 0, 2), rank2.reshape(-1, t), e2.reshape(-1, t))
    return pl.pallas_call(
        functools.partial(_experts_kernel, tm=tm, te=te),
        grid=(t // tm, n_exp // te),
        in_specs=[tok, pl.BlockSpec((te, d), lambda i, j: (j, 0)), pl.BlockSpec((d, te), lambda i, j: (0, j)),
                  key_spec, key_spec, sel_spec, sel_spec],
        out_specs=tok,
        out_shape=jax.ShapeDtypeStruct((d, t), F32),
        scratch_shapes=[pltpu.VMEM((te, tm + LANES), F32), pltpu.VMEM((te, tm + LANES), BF16),
                        pltpu.VMEM((2, te // N_KEYS, PEER_HEADS, GATE_ROWS, tm + LANES), BF16),
                        pltpu.VMEM((te, tm + LANES), BF16),
                        pltpu.VMEM((PEER_HEADS * N_KEYS, tm + LANES), BF16),
                        pltpu.VMEM((PEER_HEADS * N_KEYS, tm + LANES), BF16)],
        compiler_params=_params("parallel", "arbitrary"),
        name="peer_experts",
    )(xbt, u, vt, *sel)


def _ln_t_kernel(x_ref, f_ref, g_ref, b_ref, o_ref, *, alpha):
    z = (alpha * x_ref[...] + f_ref[...]).T
    mu = jnp.mean(z, axis=1, keepdims=True)
    zc = z - mu
    var = jnp.mean(zc * zc, axis=1, keepdims=True)
    o_ref[...] = zc * lax.rsqrt(var + LN_EPS) * g_ref[...] + b_ref[...]


def _residual_ln_t(x1t, fft, g, b, alpha, tm):
    d, t = x1t.shape
    tm = min(tm, t)
    tok = pl.BlockSpec((d, tm), lambda i: (0, i))
    vec = pl.BlockSpec((1, d), lambda i: (0, 0))
    return pl.pallas_call(
        functools.partial(_ln_t_kernel, alpha=alpha),
        grid=(t // tm,),
        in_specs=[tok, tok, vec, vec],
        out_specs=pl.BlockSpec((tm, d), lambda i: (i, 0)),
        out_shape=jax.ShapeDtypeStruct((t, d), F32),
        compiler_params=_params("parallel"),
        name="residual_ln",
    )(x1t, fft, g, b)


def kernel(x, mem, positions, w_in, w_mem_kv, w_pool, pool_scale, attn_sinks, w_out, ln1_g, ln1_b,
           w_peer_q, sub_keys_1, sub_keys_2, expert_u, expert_v, ln2_g, ln2_b):
    bsz, seq, d = x.shape
    depth = w_in.shape[0]
    t = bsz * seq
    alpha = (2.0 * depth) ** 0.25
    for l in range(depth):
        x2 = x.reshape(t, d)
        h = _matmul(x2.astype(BF16), w_in[l].astype(BF16), F32, 1024, 512).reshape(bsz, seq, -1)
        mem2 = mem.reshape(-1, d).astype(BF16)
        kvm = _matmul(mem2, w_mem_kv[l].astype(BF16), BF16, 512, 512).reshape(bsz, mem.shape[1], -1)
        pool_o = _pool(h, w_pool[l].astype(BF16), pool_scale[l].reshape(1, -1), 512)
        swa_o = _swa(h, positions, attn_sinks[l])
        mem_o = _mem_attention(h, kvm, 512)
        x1t, x1bt = _outproj_ln(pool_o.reshape(t, -1), swa_o.reshape(t, -1), mem_o.reshape(t, -1),
                          w_out[l].astype(BF16), x2, ln1_g[l].reshape(1, d), ln1_b[l].reshape(1, d), alpha, 512)
        sel = _retrieve(x1bt, w_peer_q[l].T.astype(BF16), sub_keys_1[l].astype(BF16),
                        sub_keys_2[l].astype(BF16), 256)
        fft = _experts(x1bt, expert_u[l].astype(BF16), expert_v[l].T.astype(BF16), sel, 1024, 512)
        x = _residual_ln_t(x1t, fft, ln2_g[l].reshape(1, d), ln2_b[l].reshape(1, d), alpha, 512).reshape(bsz, seq, d)
    return x
```

```python
import functools
import math

import numpy as np
import jax
import jax.numpy as jnp
from jax import lax
from jax.experimental import pallas as pl
from jax.experimental.pallas import tpu as pltpu

F32 = jnp.float32
BF16 = jnp.bfloat16

LANES = 128
SUBLANES = 8
VMEM_LIMIT_BYTES = 56 * 1024 * 1024

POOL_WINDOWS = (2, 4, 8, 16)
POOL_GROUP = 128
POOL_HALO = 16
SWA_HEAD_DIM = 64
SWA_HEADS = 16
SWA_KV_HEADS = 4
SWA_BLOCK = 128
ROPE_THETA = 500000.0
ROPE_DIM = 16
MEM_HEADS = 4
MEM_HEAD_DIM = 128
PEER_HEADS = 8
N_KEYS = 128
PEER_TOPK = 16
HALF_DIM = 128
LN_EPS = 1e-5
NEG = -1e30

EXPERT_CHUNK = 512
GATE_KEYS = 2
GATE_ROWS = 16


def _params(*semantics):
    return pltpu.CompilerParams(dimension_semantics=semantics, vmem_limit_bytes=VMEM_LIMIT_BYTES)


def _matmul_kernel(a_ref, b_ref, o_ref):
    o_ref[...] = jnp.dot(a_ref[...], b_ref[...], preferred_element_type=F32).astype(o_ref.dtype)


def _matmul(a, b, out_dtype, tm, tn):
    m, k = a.shape
    n = b.shape[1]
    tm, tn = min(tm, m), min(tn, n)
    return pl.pallas_call(
        _matmul_kernel,
        grid=(m // tm, n // tn),
        in_specs=[pl.BlockSpec((tm, k), lambda i, j: (i, 0)), pl.BlockSpec((k, tn), lambda i, j: (0, j))],
        out_specs=pl.BlockSpec((tm, tn), lambda i, j: (i, j)),
        out_shape=jax.ShapeDtypeStruct((m, n), out_dtype),
        compiler_params=_params("parallel", "arbitrary"),
        name="matmul",
    )(a, b)


def _pool_kernel(v_ref, w_ref, scale_ref, o_ref, ext_ref, *, ts):
    s = pl.program_id(1)

    @pl.when(s == 0)
    def _():
        ext_ref[0:POOL_HALO, :] = jnp.zeros((POOL_HALO, ext_ref.shape[1]), F32)

    ext_ref[POOL_HALO:POOL_HALO + ts, :] = v_ref[0]
    pos = s * ts + lax.broadcasted_iota(jnp.int32, (ts, 1), 0)
    for g, w in enumerate(POOL_WINDOWS):
        cols = slice(g * POOL_GROUP, (g + 1) * POOL_GROUP)
        acc = ext_ref[POOL_HALO:POOL_HALO + ts, cols]
        for k in range(1, w):
            acc = acc + ext_ref[POOL_HALO - k:POOL_HALO - k + ts, cols]
        count = jnp.minimum(pos + 1, w).astype(F32)
        pooled = acc / count - ext_ref[POOL_HALO:POOL_HALO + ts, cols]
        y = jnp.dot(pooled.astype(BF16), w_ref[g], preferred_element_type=F32)
        o_ref[0, :, cols] = (y * scale_ref[:, cols]).astype(o_ref.dtype)
    ext_ref[0:POOL_HALO, :] = ext_ref[ts:ts + POOL_HALO, :]


def _pool(h, w_pool, pool_scale, ts):
    b, s, _ = h.shape
    width = POOL_GROUP * len(POOL_WINDOWS)
    ts = min(ts, s)
    return pl.pallas_call(
        functools.partial(_pool_kernel, ts=ts),
        grid=(b, s // ts),
        in_specs=[
            pl.BlockSpec((1, ts, width), lambda i, j: (i, j, 0)),
            pl.BlockSpec(w_pool.shape, lambda i, j: (0, 0, 0)),
            pl.BlockSpec((1, width), lambda i, j: (0, 0)),
        ],
        out_specs=pl.BlockSpec((1, ts, width), lambda i, j: (i, j, 0)),
        out_shape=jax.ShapeDtypeStruct((b, s, width), BF16),
        scratch_shapes=[pltpu.VMEM((ts + POOL_HALO, width), F32)],
        compiler_params=_params("arbitrary", "arbitrary"),
        name="pool",
    )(h, w_pool, pool_scale)


def _rope_tables(pos_ref, freq_ref, sa_ref, sb_ref):
    ang = pos_ref[0].astype(F32) * freq_ref[...]
    c, s = jnp.cos(ang), jnp.sin(ang)
    return c, s * sa_ref[...], s * sb_ref[...]


def _rope(t, tables):
    c, sa, sb = tables
    half = ROPE_DIM // 2
    out = []
    for j in range(t.shape[1] // LANES):
        x = t[:, j * LANES:(j + 1) * LANES]
        out.append(x * c + pltpu.roll(x, LANES - half, 1) * sa + pltpu.roll(x, half, 1) * sb)
    return jnp.concatenate(out, axis=1) if len(out) > 1 else out[0]


def _swa_kernel(sink_ref, q0_ref, q1_ref, k_ref, v_ref, kp_ref, vp_ref, pos_ref, posp_ref,
                freq_ref, sa_ref, sb_ref, o_ref):
    n = pl.program_id(1)
    cur = _rope_tables(pos_ref, freq_ref, sa_ref, sb_ref)
    prev = _rope_tables(posp_ref, freq_ref, sa_ref, sb_ref)
    scale = SWA_HEAD_DIM ** -0.5
    q = jnp.concatenate([_rope(q0_ref[0], cur), _rope(q1_ref[0], cur)], axis=1)
    q = (q * scale).astype(BF16)
    k = jnp.concatenate([_rope(kp_ref[0], prev), _rope(k_ref[0], cur)], axis=0).astype(BF16)
    v = jnp.concatenate([vp_ref[0], v_ref[0]], axis=0).astype(BF16)
    row = lax.broadcasted_iota(jnp.int32, (SWA_BLOCK, 2 * SWA_BLOCK), 0)
    col = lax.broadcasted_iota(jnp.int32, (SWA_BLOCK, 2 * SWA_BLOCK), 1)
    rel = row + SWA_BLOCK - col
    valid = (rel >= 0) & (rel < SWA_BLOCK) & ((col >= SWA_BLOCK) | (n > 0))
    group = SWA_HEADS // SWA_KV_HEADS
    outs = []
    for hq in range(SWA_HEADS):
        kv = hq // group
        qh = q[:, hq * SWA_HEAD_DIM:(hq + 1) * SWA_HEAD_DIM]
        kh = k[:, kv * SWA_HEAD_DIM:(kv + 1) * SWA_HEAD_DIM]
        vh = v[:, kv * SWA_HEAD_DIM:(kv + 1) * SWA_HEAD_DIM]
        sc = lax.dot_general(qh, kh, (((1,), (1,)), ((), ())), preferred_element_type=F32)
        sc = jnp.where(valid, sc, NEG)
        sink = sink_ref[hq]
        m = jnp.maximum(jnp.max(sc, axis=1, keepdims=True), sink)
        p = jnp.exp(sc - m)
        denom = jnp.sum(p, axis=1, keepdims=True) + jnp.exp(sink - m)
        o = jnp.dot(p.astype(BF16), vh, preferred_element_type=F32)
        outs.append(o / denom)
    o_ref[0] = jnp.concatenate(outs, axis=1).astype(o_ref.dtype)


def _rope_constants():
    lane = np.arange(LANES)
    d = lane % SWA_HEAD_DIM
    half = ROPE_DIM // 2
    inv_freq = np.float32(ROPE_THETA) ** (-np.arange(0, ROPE_DIM, 2, dtype=np.float32) / np.float32(ROPE_DIM))
    freq = np.where(d < ROPE_DIM, inv_freq[d % half], 0.0).astype(np.float32)
    sa = np.where(d < half, -1.0, 0.0).astype(np.float32)
    sb = np.where((d >= half) & (d < ROPE_DIM), 1.0, 0.0).astype(np.float32)
    return [jnp.asarray(a.reshape(1, LANES)) for a in (freq, sa, sb)]


def _swa(h, positions, sinks):
    b, s, _ = h.shape
    nb = s // SWA_BLOCK
    pos3 = positions.reshape(b, s, 1)
    freq, sa, sb = _rope_constants()
    blk = lambda w, c: pl.BlockSpec((1, SWA_BLOCK, w), lambda i, j: (i, j, c))
    blk_prev = lambda w, c: pl.BlockSpec((1, SWA_BLOCK, w), lambda i, j: (i, jnp.maximum(j - 1, 0), c))
    const = pl.BlockSpec((1, LANES), lambda i, j: (0, 0))
    kvw = SWA_KV_HEADS * SWA_HEAD_DIM
    return pl.pallas_call(
        _swa_kernel,
        grid=(b, nb),
        in_specs=[
            pl.BlockSpec(memory_space=pltpu.SMEM),
            blk(512, 1), blk(512, 2), blk(kvw, 6), blk(kvw, 7), blk_prev(kvw, 6), blk_prev(kvw, 7),
            blk(1, 0), blk_prev(1, 0), const, const, const,
        ],
        out_specs=pl.BlockSpec((1, SWA_BLOCK, SWA_HEADS * SWA_HEAD_DIM), lambda i, j: (i, j, 0)),
        out_shape=jax.ShapeDtypeStruct((b, s, SWA_HEADS * SWA_HEAD_DIM), BF16),
        compiler_params=_params("parallel", "arbitrary"),
        name="swa",
    )(sinks, h, h, h, h, h, h, pos3, pos3, freq, sa, sb)


def _mem_kernel(q_ref, kv_ref, o_ref):
    scale = MEM_HEAD_DIM ** -0.5
    width = MEM_HEADS * MEM_HEAD_DIM
    for hm in range(MEM_HEADS):
        cols = slice(hm * MEM_HEAD_DIM, (hm + 1) * MEM_HEAD_DIM)
        q = (q_ref[0, :, cols] * scale).astype(BF16)
        km = kv_ref[0, :, cols]
        vm = kv_ref[0, :, width + hm * MEM_HEAD_DIM:width + (hm + 1) * MEM_HEAD_DIM]
        sc = lax.dot_general(q, km, (((1,), (1,)), ((), ())), preferred_element_type=F32)
        m = jnp.max(sc, axis=1, keepdims=True)
        p = jnp.exp(sc - m)
        denom = jnp.sum(p, axis=1, keepdims=True)
        o = jnp.dot(p.astype(BF16), vm, preferred_element_type=F32)
        o_ref[0, :, cols] = (o / denom).astype(o_ref.dtype)


def _mem_attention(h, kvm, tq):
    b, s, _ = h.shape
    width = MEM_HEADS * MEM_HEAD_DIM
    tq = min(tq, s)
    return pl.pallas_call(
        _mem_kernel,
        grid=(b, s // tq),
        in_specs=[
            pl.BlockSpec((1, tq, width), lambda i, j: (i, j, 4)),
            pl.BlockSpec((1,) + kvm.shape[1:], lambda i, j: (i, 0, 0)),
        ],
        out_specs=pl.BlockSpec((1, tq, width), lambda i, j: (i, j, 0)),
        out_shape=jax.ShapeDtypeStruct((b, s, width), BF16),
        compiler_params=_params("parallel", "arbitrary"),
        name="mem_attention",
    )(h, kvm)


def _outproj_kernel(pool_ref, swa_ref, mem_ref, wp_ref, ws_ref, wm_ref, x_ref, g_ref, b_ref, o_ref, ob_ref,
                    *, alpha):
    mix = jnp.dot(pool_ref[...], wp_ref[...], preferred_element_type=F32)
    mix += jnp.dot(swa_ref[...], ws_ref[...], preferred_element_type=F32)
    mix += jnp.dot(mem_ref[...], wm_ref[...], preferred_element_type=F32)
    z = alpha * x_ref[...] + mix
    mu = jnp.mean(z, axis=1, keepdims=True)
    zc = z - mu
    var = jnp.mean(zc * zc, axis=1, keepdims=True)
    y = zc * lax.rsqrt(var + LN_EPS) * g_ref[...] + b_ref[...]
    yt = y.T
    o_ref[...] = yt
    ob_ref[...] = yt.astype(BF16)


def _outproj_ln(pool_o, swa_o, mem_o, w_out, x2, g, b, alpha, tm):
    t, d = x2.shape
    tm = min(tm, t)
    wp, ws, wm = pool_o.shape[1], swa_o.shape[1], mem_o.shape[1]
    w_p, w_s, w_m = w_out[:wp], w_out[wp:wp + ws], w_out[wp + ws:]
    row = lambda w: pl.BlockSpec((tm, w), lambda i: (i, 0))
    full = lambda a: pl.BlockSpec(a.shape, lambda i: (0, 0))
    return pl.pallas_call(
        functools.partial(_outproj_kernel, alpha=alpha),
        grid=(t // tm,),
        in_specs=[row(wp), row(ws), row(wm), full(w_p), full(w_s), full(w_m), row(d),
                  pl.BlockSpec((1, d), lambda i: (0, 0)), pl.BlockSpec((1, d), lambda i: (0, 0))],
        out_specs=[pl.BlockSpec((d, tm), lambda i: (0, i))] * 2,
        out_shape=[jax.ShapeDtypeStruct((d, t), F32), jax.ShapeDtypeStruct((d, t), BF16)],
        compiler_params=_params("parallel"),
        name="outproj_ln",
    )(pool_o, swa_o, mem_o, w_p, w_s, w_m, x2, g, b)


def _top16_rows(s):
    n = s.shape[0]
    iota = lax.broadcasted_iota(jnp.int32, s.shape, 0).astype(F32)
    rank = jnp.full(s.shape, float(PEER_TOPK), F32)
    vals = []
    for r in range(PEER_TOPK):
        m = jnp.max(s, axis=0, keepdims=True)
        idx = jnp.min(jnp.where(s == m, iota, float(n)), axis=0, keepdims=True)
        hit = iota == idx
        rank = jnp.where(hit, float(r), rank)
        s = jnp.where(hit, -jnp.inf, s)
        vals.append(m)
    return rank, vals


_CAND_SMALL_A = PEER_TOPK // 2
_CAND_ROWS = PEER_TOPK + (_CAND_SMALL_A - 1) * SUBLANES + SUBLANES


def _cand_constants():
    flat = np.full((_CAND_ROWS, 1), 1e9, np.float32)
    valid = np.zeros((_CAND_ROWS, 1), np.float32)
    for b in range(PEER_TOPK):
        flat[b, 0], valid[b, 0] = b, 1.0
    for a in range(1, _CAND_SMALL_A):
        base = PEER_TOPK + (a - 1) * SUBLANES
        for b in range(PEER_TOPK // (a + 1)):
            flat[base + b, 0], valid[base + b, 0] = a * PEER_TOPK + b, 1.0
    base = PEER_TOPK + (_CAND_SMALL_A - 1) * SUBLANES
    for k in range(SUBLANES):
        flat[base + k, 0], valid[base + k, 0] = (_CAND_SMALL_A + k) * PEER_TOPK, 1.0
    return jnp.asarray(flat), jnp.asarray(valid)


def _select_experts(s1, s2, flat, valid):
    t = s1.shape[1]
    rank1, v1 = _top16_rows(s1)
    rank2, v2 = _top16_rows(s2)
    v2_lo = jnp.concatenate(v2[:SUBLANES], axis=0)
    v2_all = jnp.concatenate(v2, axis=0)
    v1_hi = jnp.concatenate(v1[_CAND_SMALL_A:], axis=0)
    groups = [v1[0] + v2_all]
    for a in range(1, _CAND_SMALL_A):
        groups.append(v1[a] + v2_lo)
    groups.append(v1_hi + v2[0])
    cand = jnp.concatenate(groups, axis=0)
    cand = jnp.where(valid > 0.5, cand, -jnp.inf)
    flat_b = jnp.broadcast_to(flat, cand.shape)
    hits = jnp.zeros(cand.shape, F32)
    top = []
    for r in range(PEER_TOPK):
        m = jnp.max(cand, axis=0, keepdims=True)
        pick = jnp.min(jnp.where(cand == m, flat_b, 2e9), axis=0, keepdims=True)
        hit = flat_b == pick
        hits = jnp.where(hit, 1.0, hits)
        cand = jnp.where(hit, -jnp.inf, cand)
        top.append(m)
    z = jnp.ones((1, t), F32)
    for r in range(1, PEER_TOPK):
        z = z + jnp.exp(top[r] - top[0])
    counts = [jnp.sum(hits[0:PEER_TOPK], axis=0, keepdims=True)]
    for a in range(1, _CAND_SMALL_A):
        base = PEER_TOPK + (a - 1) * SUBLANES
        counts.append(jnp.sum(hits[base:base + SUBLANES], axis=0, keepdims=True))
    base = PEER_TOPK + (_CAND_SMALL_A - 1) * SUBLANES
    for k in range(SUBLANES):
        counts.append(hits[base + k:base + k + 1])
    lim = jnp.zeros(s1.shape, F32)
    for a in range(PEER_TOPK):
        lim = jnp.where(rank1 == float(a), counts[a], lim)
    e1n = jnp.exp(s1 - v1[0]) / z
    e2 = jnp.exp(s2 - v2[0])
    return lim, e1n, rank2, e2


def _retrieve_kernel(wq_ref, x_ref, k1_ref, k2_ref, flat_ref, valid_ref,
                     lim_ref, e1_ref, r2_ref, e2_ref, q_ref, *, tq):
    q_ref[...] = jnp.dot(wq_ref[...], x_ref[...], preferred_element_type=F32)
    flat, valid = flat_ref[...], valid_ref[...]

    def head(h, carry):
        for c in range(tq // LANES):
            lanes = slice(c * LANES, (c + 1) * LANES)
            r0 = pl.multiple_of(h * 2 * HALF_DIM, 2 * HALF_DIM)
            q1 = q_ref[pl.ds(r0, HALF_DIM), lanes].astype(BF16)
            q2 = q_ref[pl.ds(r0 + HALF_DIM, HALF_DIM), lanes].astype(BF16)
            s1 = jnp.dot(k1_ref[...], q1, preferred_element_type=F32)
            s2 = jnp.dot(k2_ref[...], q2, preferred_element_type=F32)
            lim, e1n, rank2, e2 = _select_experts(s1, s2, flat, valid)
            lim_ref[h, :, lanes] = lim
            e1_ref[h, :, lanes] = e1n
            r2_ref[h, :, lanes] = rank2.astype(BF16)
            e2_ref[h, :, lanes] = e2.astype(BF16)
        return carry

    lax.fori_loop(0, PEER_HEADS, head, 0)


def _retrieve(x1t, wq_t, k1, k2, tq):
    d, t = x1t.shape
    tq = min(tq, t)
    flat, valid = _cand_constants()
    full = lambda a: pl.BlockSpec(a.shape, lambda i: (0,) * a.ndim)
    out = jax.ShapeDtypeStruct((PEER_HEADS, N_KEYS, t), F32)
    out_packed = jax.ShapeDtypeStruct((PEER_HEADS, N_KEYS, t), BF16)
    out_spec = pl.BlockSpec((PEER_HEADS, N_KEYS, tq), lambda i: (0, 0, i))
    return pl.pallas_call(
        functools.partial(_retrieve_kernel, tq=tq),
        grid=(t // tq,),
        in_specs=[full(wq_t), pl.BlockSpec((d, tq), lambda i: (0, i)), full(k1), full(k2), full(flat), full(valid)],
        out_specs=[out_spec] * 4,
        out_shape=[out, out, out_packed, out_packed],
        scratch_shapes=[pltpu.VMEM((wq_t.shape[0], tq), F32)],
        compiler_params=_params("parallel"),
        name="peer_retrieve",
    )(wq_t, x1t, k1, k2, flat, valid)


def _gelu_exact(x):
    return 0.5 * x * (1.0 + lax.erf(x * (1.0 / math.sqrt(2.0))))


def _experts_kernel(xb_ref, u_ref, vt_ref, lim_ref, e1_ref, r2_in_ref, e2_in_ref, o_ref,
                    h_ref, a_ref, rows_ref, gate_ref, r2_ref, e2_ref, *, tm, te):
    e = pl.program_id(1)

    @pl.when(e == 0)
    def _():
        o_ref[...] = jnp.zeros(o_ref.shape, F32)
        r2_ref[:, 0:tm] = r2_in_ref[...]
        e2_ref[:, LANES:LANES + tm] = e2_in_ref[...]

    n_chunks = te // EXPERT_CHUNK

    def rows_of(p):
        return slice(p * EXPERT_CHUNK, (p + 1) * EXPERT_CHUNK)

    def gates(first_key):
        keys = range(first_key, first_key + GATE_KEYS)
        blocks = range(0, N_KEYS, GATE_ROWS)
        for i in keys:
            for hd in range(PEER_HEADS):
                for q, ref in enumerate((lim_ref, e1_ref)):
                    row = jnp.broadcast_to(ref[i, hd:hd + 1, :], (GATE_ROWS, tm)).astype(BF16)
                    rows_ref[q, i, hd, :, q * LANES:q * LANES + tm] = row
        for c in range(tm // LANES):
            lanes = slice(c * LANES, (c + 1) * LANES)
            lanes1 = slice((c + 1) * LANES, (c + 2) * LANES)
            gate = {i: {jb: jnp.zeros((GATE_ROWS, LANES), BF16) for jb in blocks} for i in keys}
            for hd in range(PEER_HEADS):
                lim = {i: rows_ref[0, i, hd, :, lanes] for i in keys}
                e1 = {i: rows_ref[1, i, hd, :, lanes1] for i in keys}
                for jb in blocks:
                    r2 = r2_ref[hd * N_KEYS + jb:hd * N_KEYS + jb + GATE_ROWS, lanes]
                    e2 = e2_ref[hd * N_KEYS + jb:hd * N_KEYS + jb + GATE_ROWS, lanes1]
                    for i in keys:
                        gate[i][jb] = gate[i][jb] + jnp.where(r2 < lim[i], e2 * e1[i], jnp.zeros_like(e2))
            for i in keys:
                for jb in blocks:
                    gate_ref[i * N_KEYS + jb:i * N_KEYS + jb + GATE_ROWS, lanes] = gate[i][jb]

    for first_key in range(0, te // N_KEYS, GATE_KEYS):
        gates(first_key)
    h_ref[...] = jnp.dot(u_ref[...], xb_ref[...], preferred_element_type=F32)
    for p in range(n_chunks):
        a_ref[rows_of(p), :] = gate_ref[rows_of(p), :] * _gelu_exact(h_ref[rows_of(p), :]).astype(BF16)
        o_ref[...] += jnp.dot(vt_ref[:, rows_of(p)], a_ref[rows_of(p), :], preferred_element_type=F32)


def _experts(xbt, u, vt, sel, tm, te):
    d, t = xbt.shape
    n_exp = u.shape[0]
    tm, te = min(tm, t), min(te, n_exp)
    tok = pl.BlockSpec((d, tm), lambda i, j: (0, i))
    sel_spec = pl.BlockSpec((PEER_HEADS * N_KEYS, tm), lambda i, j: (0, i))
    key_spec = pl.BlockSpec((te // N_KEYS, PEER_HEADS, tm), lambda i, j: (j, 0, i))
    lim, e1n, rank2, e2 = sel
    sel = (lim.transpose(1, 0, 2), e1n.transpose(1, 0, 2), rank2.reshape(-1, t), e2.reshape(-1, t))
    return pl.pallas_call(
        functools.partial(_experts_kernel, tm=tm, te=te),
        grid=(t // tm, n_exp // te),
        in_specs=[tok, pl.BlockSpec((te, d), lambda i, j: (j, 0)), pl.BlockSpec((d, te), lambda i, j: (0, j)),
                  key_spec, key_spec, sel_spec, sel_spec],
        out_specs=tok,
        out_shape=jax.ShapeDtypeStruct((d, t), F32),
        scratch_shapes=[pltpu.VMEM((te, tm), F32), pltpu.VMEM((te, tm), BF16),
                        pltpu.VMEM((2, te // N_KEYS, PEER_HEADS, GATE_ROWS, tm + LANES), BF16),
                        pltpu.VMEM((te, tm), BF16),
                        pltpu.VMEM((PEER_HEADS * N_KEYS, tm + LANES), BF16),
                        pltpu.VMEM((PEER_HEADS * N_KEYS, tm + LANES), BF16)],
        compiler_params=_params("parallel", "arbitrary"),
        name="peer_experts",
    )(xbt, u, vt, *sel)


def _ln_t_kernel(x_ref, f_ref, g_ref, b_ref, o_ref, *, alpha):
    z = (alpha * x_ref[...] + f_ref[...]).T
    mu = jnp.mean(z, axis=1, keepdims=True)
    zc = z - mu
    var = jnp.mean(zc * zc, axis=1, keepdims=True)
    o_ref[...] = zc * lax.rsqrt(var + LN_EPS) * g_ref[...] + b_ref[...]


def _residual_ln_t(x1t, fft, g, b, alpha, tm):
    d, t = x1t.shape
    tm = min(tm, t)
    tok = pl.BlockSpec((d, tm), lambda i: (0, i))
    vec = pl.BlockSpec((1, d), lambda i: (0, 0))
    return pl.pallas_call(
        functools.partial(_ln_t_kernel, alpha=alpha),
        grid=(t // tm,),
        in_specs=[tok, tok, vec, vec],
        out_specs=pl.BlockSpec((tm, d), lambda i: (i, 0)),
        out_shape=jax.ShapeDtypeStruct((t, d), F32),
        compiler_params=_params("parallel"),
        name="residual_ln",
    )(x1t, fft, g, b)


def kernel(x, mem, positions, w_in, w_mem_kv, w_pool, pool_scale, attn_sinks, w_out, ln1_g, ln1_b,
           w_peer_q, sub_keys_1, sub_keys_2, expert_u, expert_v, ln2_g, ln2_b):
    bsz, seq, d = x.shape
    depth = w_in.shape[0]
    t = bsz * seq
    alpha = (2.0 * depth) ** 0.25
    for l in range(depth):
        x2 = x.reshape(t, d)
        h = _matmul(x2.astype(BF16), w_in[l].astype(BF16), F32, 1024, 512).reshape(bsz, seq, -1)
        mem2 = mem.reshape(-1, d).astype(BF16)
        kvm = _matmul(mem2, w_mem_kv[l].astype(BF16), BF16, 512, 512).reshape(bsz, mem.shape[1], -1)
        pool_o = _pool(h, w_pool[l].astype(BF16), pool_scale[l].reshape(1, -1), 512)
        swa_o = _swa(h, positions, attn_sinks[l])
        mem_o = _mem_attention(h, kvm, 512)
        x1t, x1bt = _outproj_ln(pool_o.reshape(t, -1), swa_o.reshape(t, -1), mem_o.reshape(t, -1),
                          w_out[l].astype(BF16), x2, ln1_g[l].reshape(1, d), ln1_b[l].reshape(1, d), alpha, 512)
        sel = _retrieve(x1bt, w_peer_q[l].T.astype(BF16), sub_keys_1[l].astype(BF16),
                        sub_keys_2[l].astype(BF16), 256)
        fft = _experts(x1bt, expert_u[l].astype(BF16), expert_v[l].T.astype(BF16), sel, 1024, 512)
        x = _residual_ln_t(x1t, fft, ln2_g[l].reshape(1, d), ln2_b[l].reshape(1, d), alpha, 512).reshape(bsz, seq, d)
    return x
```

```python
import functools
import math

import numpy as np
import jax
import jax.numpy as jnp
from jax import lax
from jax.experimental import pallas as pl
from jax.experimental.pallas import tpu as pltpu

F32 = jnp.float32
BF16 = jnp.bfloat16

LANES = 128
SUBLANES = 8
VMEM_LIMIT_BYTES = 56 * 1024 * 1024

POOL_WINDOWS = (2, 4, 8, 16)
POOL_GROUP = 128
POOL_HALO = 16
SWA_HEAD_DIM = 64
SWA_HEADS = 16
SWA_KV_HEADS = 4
SWA_BLOCK = 128
ROPE_THETA = 500000.0
ROPE_DIM = 16
MEM_HEADS = 4
MEM_HEAD_DIM = 128
PEER_HEADS = 8
N_KEYS = 128
PEER_TOPK = 16
HALF_DIM = 128
LN_EPS = 1e-5
NEG = -1e30

EXPERT_CHUNK = 512
GATE_KEYS = 2
GATE_ROWS = 16
SELECT_TOKENS = 512


def _params(*semantics):
    return pltpu.CompilerParams(dimension_semantics=semantics, vmem_limit_bytes=VMEM_LIMIT_BYTES)


def _matmul_kernel(a_ref, b_ref, o_ref):
    o_ref[...] = jnp.dot(a_ref[...], b_ref[...], preferred_element_type=F32).astype(o_ref.dtype)


def _matmul(a, b, out_dtype, tm, tn):
    m, k = a.shape
    n = b.shape[1]
    tm, tn = min(tm, m), min(tn, n)
    return pl.pallas_call(
        _matmul_kernel,
        grid=(m // tm, n // tn),
        in_specs=[pl.BlockSpec((tm, k), lambda i, j: (i, 0)), pl.BlockSpec((k, tn), lambda i, j: (0, j))],
        out_specs=pl.BlockSpec((tm, tn), lambda i, j: (i, j)),
        out_shape=jax.ShapeDtypeStruct((m, n), out_dtype),
        compiler_params=_params("parallel", "arbitrary"),
        name="matmul",
    )(a, b)


def _pool_kernel(v_ref, w_ref, scale_ref, o_ref, ext_ref, *, ts):
    s = pl.program_id(1)

    @pl.when(s == 0)
    def _():
        ext_ref[0:POOL_HALO, :] = jnp.zeros((POOL_HALO, ext_ref.shape[1]), F32)

    ext_ref[POOL_HALO:POOL_HALO + ts, :] = v_ref[0]
    pos = s * ts + lax.broadcasted_iota(jnp.int32, (ts, 1), 0)
    for g, w in enumerate(POOL_WINDOWS):
        cols = slice(g * POOL_GROUP, (g + 1) * POOL_GROUP)
        acc = ext_ref[POOL_HALO:POOL_HALO + ts, cols]
        for k in range(1, w):
            acc = acc + ext_ref[POOL_HALO - k:POOL_HALO - k + ts, cols]
        count = jnp.minimum(pos + 1, w).astype(F32)
        pooled = acc / count - ext_ref[POOL_HALO:POOL_HALO + ts, cols]
        y = jnp.dot(pooled.astype(BF16), w_ref[g], preferred_element_type=F32)
        o_ref[0, :, cols] = (y * scale_ref[:, cols]).astype(o_ref.dtype)
    ext_ref[0:POOL_HALO, :] = ext_ref[ts:ts + POOL_HALO, :]


def _pool(h, w_pool, pool_scale, ts):
    b, s, _ = h.shape
    width = POOL_GROUP * len(POOL_WINDOWS)
    ts = min(ts, s)
    return pl.pallas_call(
        functools.partial(_pool_kernel, ts=ts),
        grid=(b, s // ts),
        in_specs=[
            pl.BlockSpec((1, ts, width), lambda i, j: (i, j, 0)),
            pl.BlockSpec(w_pool.shape, lambda i, j: (0, 0, 0)),
            pl.BlockSpec((1, width), lambda i, j: (0, 0)),
        ],
        out_specs=pl.BlockSpec((1, ts, width), lambda i, j: (i, j, 0)),
        out_shape=jax.ShapeDtypeStruct((b, s, width), BF16),
        scratch_shapes=[pltpu.VMEM((ts + POOL_HALO, width), F32)],
        compiler_params=_params("arbitrary", "arbitrary"),
        name="pool",
    )(h, w_pool, pool_scale)


def _rope_tables(pos_ref, freq_ref, sa_ref, sb_ref):
    ang = pos_ref[0].astype(F32) * freq_ref[...]
    c, s = jnp.cos(ang), jnp.sin(ang)
    return c, s * sa_ref[...], s * sb_ref[...]


def _rope(t, tables):
    c, sa, sb = tables
    half = ROPE_DIM // 2
    out = []
    for j in range(t.shape[1] // LANES):
        x = t[:, j * LANES:(j + 1) * LANES]
        out.append(x * c + pltpu.roll(x, LANES - half, 1) * sa + pltpu.roll(x, half, 1) * sb)
    return jnp.concatenate(out, axis=1) if len(out) > 1 else out[0]


def _swa_kernel(sink_ref, q0_ref, q1_ref, k_ref, v_ref, kp_ref, vp_ref, pos_ref, posp_ref,
                freq_ref, sa_ref, sb_ref, o_ref):
    n = pl.program_id(1)
    cur = _rope_tables(pos_ref, freq_ref, sa_ref, sb_ref)
    prev = _rope_tables(posp_ref, freq_ref, sa_ref, sb_ref)
    scale = SWA_HEAD_DIM ** -0.5
    q = jnp.concatenate([_rope(q0_ref[0], cur), _rope(q1_ref[0], cur)], axis=1)
    q = (q * scale).astype(BF16)
    k = jnp.concatenate([_rope(kp_ref[0], prev), _rope(k_ref[0], cur)], axis=0).astype(BF16)
    v = jnp.concatenate([vp_ref[0], v_ref[0]], axis=0).astype(BF16)
    row = lax.broadcasted_iota(jnp.int32, (SWA_BLOCK, 2 * SWA_BLOCK), 0)
    col = lax.broadcasted_iota(jnp.int32, (SWA_BLOCK, 2 * SWA_BLOCK), 1)
    rel = row + SWA_BLOCK - col
    valid = (rel >= 0) & (rel < SWA_BLOCK) & ((col >= SWA_BLOCK) | (n > 0))
    group = SWA_HEADS // SWA_KV_HEADS
    outs = []
    for hq in range(SWA_HEADS):
        kv = hq // group
        qh = q[:, hq * SWA_HEAD_DIM:(hq + 1) * SWA_HEAD_DIM]
        kh = k[:, kv * SWA_HEAD_DIM:(kv + 1) * SWA_HEAD_DIM]
        vh = v[:, kv * SWA_HEAD_DIM:(kv + 1) * SWA_HEAD_DIM]
        sc = lax.dot_general(qh, kh, (((1,), (1,)), ((), ())), preferred_element_type=F32)
        sc = jnp.where(valid, sc, NEG)
        sink = sink_ref[hq]
        m = jnp.maximum(jnp.max(sc, axis=1, keepdims=True), sink)
        p = jnp.exp(sc - m)
        denom = jnp.sum(p, axis=1, keepdims=True) + jnp.exp(sink - m)
        o = jnp.dot(p.astype(BF16), vh, preferred_element_type=F32)
        outs.append(o / denom)
    o_ref[0] = jnp.concatenate(outs, axis=1).astype(o_ref.dtype)


def _rope_constants():
    lane = np.arange(LANES)
    d = lane % SWA_HEAD_DIM
    half = ROPE_DIM // 2
    inv_freq = np.float32(ROPE_THETA) ** (-np.arange(0, ROPE_DIM, 2, dtype=np.float32) / np.float32(ROPE_DIM))
    freq = np.where(d < ROPE_DIM, inv_freq[d % half], 0.0).astype(np.float32)
    sa = np.where(d < half, -1.0, 0.0).astype(np.float32)
    sb = np.where((d >= half) & (d < ROPE_DIM), 1.0, 0.0).astype(np.float32)
    return [jnp.asarray(a.reshape(1, LANES)) for a in (freq, sa, sb)]


def _swa(h, positions, sinks):
    b, s, _ = h.shape
    nb = s // SWA_BLOCK
    pos3 = positions.reshape(b, s, 1)
    freq, sa, sb = _rope_constants()
    blk = lambda w, c: pl.BlockSpec((1, SWA_BLOCK, w), lambda i, j: (i, j, c))
    blk_prev = lambda w, c: pl.BlockSpec((1, SWA_BLOCK, w), lambda i, j: (i, jnp.maximum(j - 1, 0), c))
    const = pl.BlockSpec((1, LANES), lambda i, j: (0, 0))
    kvw = SWA_KV_HEADS * SWA_HEAD_DIM
    return pl.pallas_call(
        _swa_kernel,
        grid=(b, nb),
        in_specs=[
            pl.BlockSpec(memory_space=pltpu.SMEM),
            blk(512, 1), blk(512, 2), blk(kvw, 6), blk(kvw, 7), blk_prev(kvw, 6), blk_prev(kvw, 7),
            blk(1, 0), blk_prev(1, 0), const, const, const,
        ],
        out_specs=pl.BlockSpec((1, SWA_BLOCK, SWA_HEADS * SWA_HEAD_DIM), lambda i, j: (i, j, 0)),
        out_shape=jax.ShapeDtypeStruct((b, s, SWA_HEADS * SWA_HEAD_DIM), BF16),
        compiler_params=_params("parallel", "arbitrary"),
        name="swa",
    )(sinks, h, h, h, h, h, h, pos3, pos3, freq, sa, sb)


def _mem_kernel(q_ref, kv_ref, o_ref):
    scale = MEM_HEAD_DIM ** -0.5
    width = MEM_HEADS * MEM_HEAD_DIM
    for hm in range(MEM_HEADS):
        cols = slice(hm * MEM_HEAD_DIM, (hm + 1) * MEM_HEAD_DIM)
        q = (q_ref[0, :, cols] * scale).astype(BF16)
        km = kv_ref[0, :, cols]
        vm = kv_ref[0, :, width + hm * MEM_HEAD_DIM:width + (hm + 1) * MEM_HEAD_DIM]
        sc = lax.dot_general(q, km, (((1,), (1,)), ((), ())), preferred_element_type=F32)
        m = jnp.max(sc, axis=1, keepdims=True)
        p = jnp.exp(sc - m)
        denom = jnp.sum(p, axis=1, keepdims=True)
        o = jnp.dot(p.astype(BF16), vm, preferred_element_type=F32)
        o_ref[0, :, cols] = (o / denom).astype(o_ref.dtype)


def _mem_attention(h, kvm, tq):
    b, s, _ = h.shape
    width = MEM_HEADS * MEM_HEAD_DIM
    tq = min(tq, s)
    return pl.pallas_call(
        _mem_kernel,
        grid=(b, s // tq),
        in_specs=[
            pl.BlockSpec((1, tq, width), lambda i, j: (i, j, 4)),
            pl.BlockSpec((1,) + kvm.shape[1:], lambda i, j: (i, 0, 0)),
        ],
        out_specs=pl.BlockSpec((1, tq, width), lambda i, j: (i, j, 0)),
        out_shape=jax.ShapeDtypeStruct((b, s, width), BF16),
        compiler_params=_params("parallel", "arbitrary"),
        name="mem_attention",
    )(h, kvm)


def _outproj_kernel(pool_ref, swa_ref, mem_ref, wp_ref, ws_ref, wm_ref, x_ref, g_ref, b_ref, o_ref, ob_ref,
                    *, alpha):
    mix = jnp.dot(pool_ref[...], wp_ref[...], preferred_element_type=F32)
    mix += jnp.dot(swa_ref[...], ws_ref[...], preferred_element_type=F32)
    mix += jnp.dot(mem_ref[...], wm_ref[...], preferred_element_type=F32)
    z = alpha * x_ref[...] + mix
    mu = jnp.mean(z, axis=1, keepdims=True)
    zc = z - mu
    var = jnp.mean(zc * zc, axis=1, keepdims=True)
    y = zc * lax.rsqrt(var + LN_EPS) * g_ref[...] + b_ref[...]
    yt = y.T
    o_ref[...] = yt
    ob_ref[...] = yt.astype(BF16)


def _outproj_ln(pool_o, swa_o, mem_o, w_out, x2, g, b, alpha, tm):
    t, d = x2.shape
    tm = min(tm, t)
    wp, ws, wm = pool_o.shape[1], swa_o.shape[1], mem_o.shape[1]
    w_p, w_s, w_m = w_out[:wp], w_out[wp:wp + ws], w_out[wp + ws:]
    row = lambda w: pl.BlockSpec((tm, w), lambda i: (i, 0))
    full = lambda a: pl.BlockSpec(a.shape, lambda i: (0, 0))
    return pl.pallas_call(
        functools.partial(_outproj_kernel, alpha=alpha),
        grid=(t // tm,),
        in_specs=[row(wp), row(ws), row(wm), full(w_p), full(w_s), full(w_m), row(d),
                  pl.BlockSpec((1, d), lambda i: (0, 0)), pl.BlockSpec((1, d), lambda i: (0, 0))],
        out_specs=[pl.BlockSpec((d, tm), lambda i: (0, i))] * 2,
        out_shape=[jax.ShapeDtypeStruct((d, t), F32), jax.ShapeDtypeStruct((d, t), BF16)],
        compiler_params=_params("parallel"),
        name="outproj_ln",
    )(pool_o, swa_o, mem_o, w_p, w_s, w_m, x2, g, b)


def _top16_rows(s, break_ties):
    n = s.shape[0]
    iota = lax.broadcasted_iota(jnp.int32, s.shape, 0).astype(F32) if break_ties else None
    rank = jnp.full(s.shape, float(PEER_TOPK), F32)
    vals = []
    for r in range(PEER_TOPK):
        m = jnp.max(s, axis=0, keepdims=True)
        hit = s == m
        if break_ties:
            idx = jnp.min(jnp.where(hit, iota, float(n)), axis=0, keepdims=True)
            hit = iota == idx
        rank = jnp.where(hit, float(r), rank)
        s = jnp.where(hit, -jnp.inf, s)
        vals.append(m)
    count = jnp.sum(jnp.where(rank < float(PEER_TOPK), 1.0, 0.0), axis=0, keepdims=True)
    return rank, vals, count


_CAND_SMALL_A = PEER_TOPK // 2
_CAND_ROWS = PEER_TOPK + (_CAND_SMALL_A - 1) * SUBLANES + SUBLANES


def _cand_constants():
    flat = np.full((_CAND_ROWS, 1), 1e9, np.float32)
    valid = np.zeros((_CAND_ROWS, 1), np.float32)
    for b in range(PEER_TOPK):
        flat[b, 0], valid[b, 0] = b, 1.0
    for a in range(1, _CAND_SMALL_A):
        base = PEER_TOPK + (a - 1) * SUBLANES
        for b in range(PEER_TOPK // (a + 1)):
            flat[base + b, 0], valid[base + b, 0] = a * PEER_TOPK + b, 1.0
    base = PEER_TOPK + (_CAND_SMALL_A - 1) * SUBLANES
    for k in range(SUBLANES):
        flat[base + k, 0], valid[base + k, 0] = (_CAND_SMALL_A + k) * PEER_TOPK, 1.0
    return jnp.asarray(flat), jnp.asarray(valid)


def _select_experts(s1, s2, flat, valid, break_ties):
    t = s1.shape[1]
    rank1, v1, count1 = _top16_rows(s1, break_ties)
    rank2, v2, count2 = _top16_rows(s2, break_ties)
    v2_lo = jnp.concatenate(v2[:SUBLANES], axis=0)
    v2_all = jnp.concatenate(v2, axis=0)
    v1_hi = jnp.concatenate(v1[_CAND_SMALL_A:], axis=0)
    groups = [v1[0] + v2_all]
    for a in range(1, _CAND_SMALL_A):
        groups.append(v1[a] + v2_lo)
    groups.append(v1_hi + v2[0])
    cand = jnp.concatenate(groups, axis=0)
    cand = jnp.where(valid > 0.5, cand, -jnp.inf)
    flat_b = jnp.broadcast_to(flat, cand.shape) if break_ties else None
    hits = jnp.zeros(cand.shape, F32)
    top = []
    for r in range(PEER_TOPK):
        m = jnp.max(cand, axis=0, keepdims=True)
        hit = cand == m
        if break_ties:
            pick = jnp.min(jnp.where(hit, flat_b, 2e9), axis=0, keepdims=True)
            hit = flat_b == pick
        hits = jnp.where(hit, 1.0, hits)
        cand = jnp.where(hit, -jnp.inf, cand)
        top.append(m)
    z = jnp.ones((1, t), F32)
    for r in range(1, PEER_TOPK):
        z = z + jnp.exp(top[r] - top[0])
    counts = [jnp.sum(hits[0:PEER_TOPK], axis=0, keepdims=True)]
    for a in range(1, _CAND_SMALL_A):
        base = PEER_TOPK + (a - 1) * SUBLANES
        counts.append(jnp.sum(hits[base:base + SUBLANES], axis=0, keepdims=True))
    base = PEER_TOPK + (_CAND_SMALL_A - 1) * SUBLANES
    for k in range(SUBLANES):
        counts.append(hits[base + k:base + k + 1])
    lim = jnp.zeros(s1.shape, F32)
    for a in range(PEER_TOPK):
        lim = jnp.where(rank1 == float(a), counts[a], lim)
    e1n = jnp.exp(s1 - v1[0]) / z
    e2 = jnp.exp(s2 - v2[0])
    count3 = jnp.sum(hits, axis=0, keepdims=True)
    k = float(PEER_TOPK)
    ok = jnp.where((count1 == k) & (count2 == k) & (count3 == k), 1.0, 0.0)
    return lim, e1n, rank2, e2, ok


def _retrieve_kernel(wq_ref, x_ref, k1_ref, k2_ref, flat_ref, valid_ref,
                     lim_ref, e1_ref, r2_ref, e2_ref, q_ref, *, tq):
    q_ref[...] = jnp.dot(wq_ref[...], x_ref[...], preferred_element_type=F32)
    flat, valid = flat_ref[...], valid_ref[...]

    def head(h, carry):
        for c in range(tq // SELECT_TOKENS):
            lanes = slice(c * SELECT_TOKENS, (c + 1) * SELECT_TOKENS)
            r0 = pl.multiple_of(h * 2 * HALF_DIM, 2 * HALF_DIM)
            q1 = q_ref[pl.ds(r0, HALF_DIM), lanes].astype(BF16)
            q2 = q_ref[pl.ds(r0 + HALF_DIM, HALF_DIM), lanes].astype(BF16)
            s1 = jnp.dot(k1_ref[...], q1, preferred_element_type=F32)
            s2 = jnp.dot(k2_ref[...], q2, preferred_element_type=F32)

            def emit(break_ties):
                lim, e1n, rank2, e2, ok = _select_experts(s1, s2, flat, valid, break_ties)
                lim_ref[h, :, lanes] = lim
                e1_ref[h, :, lanes] = e1n
                r2_ref[h, :, lanes] = rank2.astype(BF16)
                e2_ref[h, :, lanes] = e2.astype(BF16)
                return ok

            ok = emit(False)

            @pl.when(jnp.min(ok) < 0.5)
            def _():
                emit(True)
        return carry

    lax.fori_loop(0, PEER_HEADS, head, 0)


def _retrieve(x1t, wq_t, k1, k2, tq):
    d, t = x1t.shape
    tq = min(tq, t)
    flat, valid = _cand_constants()
    full = lambda a: pl.BlockSpec(a.shape, lambda i: (0,) * a.ndim)
    out = jax.ShapeDtypeStruct((PEER_HEADS, N_KEYS, t), F32)
    out_packed = jax.ShapeDtypeStruct((PEER_HEADS, N_KEYS, t), BF16)
    out_spec = pl.BlockSpec((PEER_HEADS, N_KEYS, tq), lambda i: (0, 0, i))
    return pl.pallas_call(
        functools.partial(_retrieve_kernel, tq=tq),
        grid=(t // tq,),
        in_specs=[full(wq_t), pl.BlockSpec((d, tq), lambda i: (0, i)), full(k1), full(k2), full(flat), full(valid)],
        out_specs=[out_spec] * 4,
        out_shape=[out, out, out_packed, out_packed],
        scratch_shapes=[pltpu.VMEM((wq_t.shape[0], tq), F32)],
        compiler_params=_params("parallel"),
        name="peer_retrieve",
    )(wq_t, x1t, k1, k2, flat, valid)


def _gelu_exact(x):
    return 0.5 * x * (1.0 + lax.erf(x * (1.0 / math.sqrt(2.0))))


def _experts_kernel(xb_ref, u_ref, vt_ref, lim_ref, e1_ref, r2_in_ref, e2_in_ref, o_ref,
                    h_ref, a_ref, rows_ref, gate_ref, r2_ref, e2_ref, *, tm, te):
    e = pl.program_id(1)

    @pl.when(e == 0)
    def _():
        o_ref[...] = jnp.zeros(o_ref.shape, F32)
        r2_ref[:, 0:tm] = r2_in_ref[...]
        e2_ref[:, LANES:LANES + tm] = e2_in_ref[...]

    n_chunks = te // EXPERT_CHUNK

    def rows_of(p):
        return slice(p * EXPERT_CHUNK, (p + 1) * EXPERT_CHUNK)

    def gates(first_key):
        keys = range(first_key, first_key + GATE_KEYS)
        blocks = range(0, N_KEYS, GATE_ROWS)
        for i in keys:
            for hd in range(PEER_HEADS):
                for q, ref in enumerate((lim_ref, e1_ref)):
                    row = jnp.broadcast_to(ref[i, hd:hd + 1, :], (GATE_ROWS, tm)).astype(BF16)
                    rows_ref[q, i, hd, :, q * LANES:q * LANES + tm] = row
        for c in range(tm // LANES):
            lanes = slice(c * LANES, (c + 1) * LANES)
            lanes1 = slice((c + 1) * LANES, (c + 2) * LANES)
            gate = {i: {jb: jnp.zeros((GATE_ROWS, LANES), BF16) for jb in blocks} for i in keys}
            for hd in range(PEER_HEADS):
                lim = {i: rows_ref[0, i, hd, :, lanes] for i in keys}
                e1 = {i: rows_ref[1, i, hd, :, lanes1] for i in keys}
                for jb in blocks:
                    r2 = r2_ref[hd * N_KEYS + jb:hd * N_KEYS + jb + GATE_ROWS, lanes]
                    e2 = e2_ref[hd * N_KEYS + jb:hd * N_KEYS + jb + GATE_ROWS, lanes1]
                    for i in keys:
                        gate[i][jb] = gate[i][jb] + jnp.where(r2 < lim[i], e2 * e1[i], jnp.zeros_like(e2))
            for i in keys:
                for jb in blocks:
                    gate_ref[i * N_KEYS + jb:i * N_KEYS + jb + GATE_ROWS, lanes] = gate[i][jb]

    for first_key in range(0, te // N_KEYS, GATE_KEYS):
        gates(first_key)
    h_ref[...] = jnp.dot(u_ref[...], xb_ref[...], preferred_element_type=F32)
    for p in range(n_chunks):
        a_ref[rows_of(p), :] = gate_ref[rows_of(p), :] * _gelu_exact(h_ref[rows_of(p), :]).astype(BF16)
        o_ref[...] += jnp.dot(vt_ref[:, rows_of(p)], a_ref[rows_of(p), :], preferred_element_type=F32)


def _experts(xbt, u, vt, sel, tm, te):
    d, t = xbt.shape
    n_exp = u.shape[0]
    tm, te = min(tm, t), min(te, n_exp)
    tok = pl.BlockSpec((d, tm), lambda i, j: (0, i))
    sel_spec = pl.BlockSpec((PEER_HEADS * N_KEYS, tm), lambda i, j: (0, i))
    key_spec = pl.BlockSpec((te // N_KEYS, PEER_HEADS, tm), lambda i, j: (j, 0, i))
    lim, e1n, rank2, e2 = sel
    sel = (lim.transpose(1, 0, 2), e1n.transpose(1, 0, 2), rank2.reshape(-1, t), e2.reshape(-1, t))
    return pl.pallas_call(
        functools.partial(_experts_kernel, tm=tm, te=te),
        grid=(t // tm, n_exp // te),
        in_specs=[tok, pl.BlockSpec((te, d), lambda i, j: (j, 0)), pl.BlockSpec((d, te), lambda i, j: (0, j)),
                  key_spec, key_spec, sel_spec, sel_spec],
        out_specs=tok,
        out_shape=jax.ShapeDtypeStruct((d, t), F32),
        scratch_shapes=[pltpu.VMEM((te, tm), F32), pltpu.VMEM((te, tm), BF16),
                        pltpu.VMEM((2, te // N_KEYS, PEER_HEADS, GATE_ROWS, tm + LANES), BF16),
                        pltpu.VMEM((te, tm), BF16),
                        pltpu.VMEM((PEER_HEADS * N_KEYS, tm + LANES), BF16),
                        pltpu.VMEM((PEER_HEADS * N_KEYS, tm + LANES), BF16)],
        compiler_params=_params("parallel", "arbitrary"),
        name="peer_experts",
    )(xbt, u, vt, *sel)


def _ln_t_kernel(x_ref, f_ref, g_ref, b_ref, o_ref, *, alpha):
    z = (alpha * x_ref[...] + f_ref[...]).T
    mu = jnp.mean(z, axis=1, keepdims=True)
    zc = z - mu
    var = jnp.mean(zc * zc, axis=1, keepdims=True)
    o_ref[...] = zc * lax.rsqrt(var + LN_EPS) * g_ref[...] + b_ref[...]


def _residual_ln_t(x1t, fft, g, b, alpha, tm):
    d, t = x1t.shape
    tm = min(tm, t)
    tok = pl.BlockSpec((d, tm), lambda i: (0, i))
    vec = pl.BlockSpec((1, d), lambda i: (0, 0))
    return pl.pallas_call(
        functools.partial(_ln_t_kernel, alpha=alpha),
        grid=(t // tm,),
        in_specs=[tok, tok, vec, vec],
        out_specs=pl.BlockSpec((tm, d), lambda i: (i, 0)),
        out_shape=jax.ShapeDtypeStruct((t, d), F32),
        compiler_params=_params("parallel"),
        name="residual_ln",
    )(x1t, fft, g, b)


def kernel(x, mem, positions, w_in, w_mem_kv, w_pool, pool_scale, attn_sinks, w_out, ln1_g, ln1_b,
           w_peer_q, sub_keys_1, sub_keys_2, expert_u, expert_v, ln2_g, ln2_b):
    bsz, seq, d = x.shape
    depth = w_in.shape[0]
    t = bsz * seq
    alpha = (2.0 * depth) ** 0.25
    for l in range(depth):
        x2 = x.reshape(t, d)
        h = _matmul(x2.astype(BF16), w_in[l].astype(BF16), F32, 1024, 512).reshape(bsz, seq, -1)
        mem2 = mem.reshape(-1, d).astype(BF16)
        kvm = _matmul(mem2, w_mem_kv[l].astype(BF16), BF16, 512, 512).reshape(bsz, mem.shape[1], -1)
        pool_o = _pool(h, w_pool[l].astype(BF16), pool_scale[l].reshape(1, -1), 512)
        swa_o = _swa(h, positions, attn_sinks[l])
        mem_o = _mem_attention(h, kvm, 512)
        x1t, x1bt = _outproj_ln(pool_o.reshape(t, -1), swa_o.reshape(t, -1), mem_o.reshape(t, -1),
                          w_out[l].astype(BF16), x2, ln1_g[l].reshape(1, d), ln1_b[l].reshape(1, d), alpha, 512)
        sel = _retrieve(x1bt, w_peer_q[l].T.astype(BF16), sub_keys_1[l].astype(BF16),
                        sub_keys_2[l].astype(BF16), 512)
        fft = _experts(x1bt, expert_u[l].astype(BF16), expert_v[l].T.astype(BF16), sel, 1024, 512)
        x = _residual_ln_t(x1t, fft, ln2_g[l].reshape(1, d), ln2_b[l].reshape(1, d), alpha, 512).reshape(bsz, seq, d)
    return x
```

```python
import functools
import math

import numpy as np
import jax
import jax.numpy as jnp
from jax import lax
from jax.experimental import pallas as pl
from jax.experimental.pallas import tpu as pltpu

F32 = jnp.float32
BF16 = jnp.bfloat16

LANES = 128
SUBLANES = 8
VMEM_LIMIT_BYTES = 56 * 1024 * 1024

POOL_WINDOWS = (2, 4, 8, 16)
POOL_GROUP = 128
POOL_HALO = 16
SWA_HEAD_DIM = 64
SWA_HEADS = 16
SWA_KV_HEADS = 4
SWA_BLOCK = 128
ROPE_THETA = 500000.0
ROPE_DIM = 16
MEM_HEADS = 4
MEM_HEAD_DIM = 128
PEER_HEADS = 8
N_KEYS = 128
PEER_TOPK = 16
HALF_DIM = 128
LN_EPS = 1e-5
NEG = -1e30

EXPERT_CHUNK = 512
GATE_KEYS = 4
GATE_ROWS = 16
SELECT_TOKENS = 512


def _params(*semantics):
    return pltpu.CompilerParams(dimension_semantics=semantics, vmem_limit_bytes=VMEM_LIMIT_BYTES)


def _matmul_kernel(a_ref, b_ref, o_ref):
    o_ref[...] = jnp.dot(a_ref[...], b_ref[...], preferred_element_type=F32).astype(o_ref.dtype)


def _matmul(a, b, out_dtype, tm, tn):
    m, k = a.shape
    n = b.shape[1]
    tm, tn = min(tm, m), min(tn, n)
    return pl.pallas_call(
        _matmul_kernel,
        grid=(m // tm, n // tn),
        in_specs=[pl.BlockSpec((tm, k), lambda i, j: (i, 0)), pl.BlockSpec((k, tn), lambda i, j: (0, j))],
        out_specs=pl.BlockSpec((tm, tn), lambda i, j: (i, j)),
        out_shape=jax.ShapeDtypeStruct((m, n), out_dtype),
        compiler_params=_params("parallel", "arbitrary"),
        name="matmul",
    )(a, b)


def _pool_kernel(v_ref, w_ref, scale_ref, o_ref, ext_ref, *, ts):
    s = pl.program_id(1)

    @pl.when(s == 0)
    def _():
        ext_ref[0:POOL_HALO, :] = jnp.zeros((POOL_HALO, ext_ref.shape[1]), F32)

    ext_ref[POOL_HALO:POOL_HALO + ts, :] = v_ref[0]
    pos = s * ts + lax.broadcasted_iota(jnp.int32, (ts, 1), 0)
    for g, w in enumerate(POOL_WINDOWS):
        cols = slice(g * POOL_GROUP, (g + 1) * POOL_GROUP)
        acc = ext_ref[POOL_HALO:POOL_HALO + ts, cols]
        for k in range(1, w):
            acc = acc + ext_ref[POOL_HALO - k:POOL_HALO - k + ts, cols]
        count = jnp.minimum(pos + 1, w).astype(F32)
        pooled = acc / count - ext_ref[POOL_HALO:POOL_HALO + ts, cols]
        y = jnp.dot(pooled.astype(BF16), w_ref[g], preferred_element_type=F32)
        o_ref[0, :, cols] = (y * scale_ref[:, cols]).astype(o_ref.dtype)
    ext_ref[0:POOL_HALO, :] = ext_ref[ts:ts + POOL_HALO, :]


def _pool(h, w_pool, pool_scale, ts):
    b, s, _ = h.shape
    width = POOL_GROUP * len(POOL_WINDOWS)
    ts = min(ts, s)
    return pl.pallas_call(
        functools.partial(_pool_kernel, ts=ts),
        grid=(b, s // ts),
        in_specs=[
            pl.BlockSpec((1, ts, width), lambda i, j: (i, j, 0)),
            pl.BlockSpec(w_pool.shape, lambda i, j: (0, 0, 0)),
            pl.BlockSpec((1, width), lambda i, j: (0, 0)),
        ],
        out_specs=pl.BlockSpec((1, ts, width), lambda i, j: (i, j, 0)),
        out_shape=jax.ShapeDtypeStruct((b, s, width), BF16),
        scratch_shapes=[pltpu.VMEM((ts + POOL_HALO, width), F32)],
        compiler_params=_params("arbitrary", "arbitrary"),
        name="pool",
    )(h, w_pool, pool_scale)


def _rope_tables(pos_ref, freq_ref, sa_ref, sb_ref):
    ang = pos_ref[0].astype(F32) * freq_ref[...]
    c, s = jnp.cos(ang), jnp.sin(ang)
    return c, s * sa_ref[...], s * sb_ref[...]


def _rope(t, tables):
    c, sa, sb = tables
    half = ROPE_DIM // 2
    out = []
    for j in range(t.shape[1] // LANES):
        x = t[:, j * LANES:(j + 1) * LANES]
        out.append(x * c + pltpu.roll(x, LANES - half, 1) * sa + pltpu.roll(x, half, 1) * sb)
    return jnp.concatenate(out, axis=1) if len(out) > 1 else out[0]


def _swa_kernel(sink_ref, q0_ref, q1_ref, k_ref, v_ref, kp_ref, vp_ref, pos_ref, posp_ref,
                freq_ref, sa_ref, sb_ref, o_ref):
    n = pl.program_id(1)
    cur = _rope_tables(pos_ref, freq_ref, sa_ref, sb_ref)
    prev = _rope_tables(posp_ref, freq_ref, sa_ref, sb_ref)
    scale = SWA_HEAD_DIM ** -0.5
    q = jnp.concatenate([_rope(q0_ref[0], cur), _rope(q1_ref[0], cur)], axis=1)
    q = (q * scale).astype(BF16)
    k = jnp.concatenate([_rope(kp_ref[0], prev), _rope(k_ref[0], cur)], axis=0).astype(BF16)
    v = jnp.concatenate([vp_ref[0], v_ref[0]], axis=0).astype(BF16)
    row = lax.broadcasted_iota(jnp.int32, (SWA_BLOCK, 2 * SWA_BLOCK), 0)
    col = lax.broadcasted_iota(jnp.int32, (SWA_BLOCK, 2 * SWA_BLOCK), 1)
    rel = row + SWA_BLOCK - col
    valid = (rel >= 0) & (rel < SWA_BLOCK) & ((col >= SWA_BLOCK) | (n > 0))
    group = SWA_HEADS // SWA_KV_HEADS
    outs = []
    for hq in range(SWA_HEADS):
        kv = hq // group
        qh = q[:, hq * SWA_HEAD_DIM:(hq + 1) * SWA_HEAD_DIM]
        kh = k[:, kv * SWA_HEAD_DIM:(kv + 1) * SWA_HEAD_DIM]
        vh = v[:, kv * SWA_HEAD_DIM:(kv + 1) * SWA_HEAD_DIM]
        sc = lax.dot_general(qh, kh, (((1,), (1,)), ((), ())), preferred_element_type=F32)
        sc = jnp.where(valid, sc, NEG)
        sink = sink_ref[hq]
        m = jnp.maximum(jnp.max(sc, axis=1, keepdims=True), sink)
        p = jnp.exp(sc - m)
        denom = jnp.sum(p, axis=1, keepdims=True) + jnp.exp(sink - m)
        o = jnp.dot(p.astype(BF16), vh, preferred_element_type=F32)
        outs.append(o / denom)
    o_ref[0] = jnp.concatenate(outs, axis=1).astype(o_ref.dtype)


def _rope_constants():
    lane = np.arange(LANES)
    d = lane % SWA_HEAD_DIM
    half = ROPE_DIM // 2
    inv_freq = np.float32(ROPE_THETA) ** (-np.arange(0, ROPE_DIM, 2, dtype=np.float32) / np.float32(ROPE_DIM))
    freq = np.where(d < ROPE_DIM, inv_freq[d % half], 0.0).astype(np.float32)
    sa = np.where(d < half, -1.0, 0.0).astype(np.float32)
    sb = np.where((d >= half) & (d < ROPE_DIM), 1.0, 0.0).astype(np.float32)
    return [jnp.asarray(a.reshape(1, LANES)) for a in (freq, sa, sb)]


def _swa(h, positions, sinks):
    b, s, _ = h.shape
    nb = s // SWA_BLOCK
    pos3 = positions.reshape(b, s, 1)
    freq, sa, sb = _rope_constants()
    blk = lambda w, c: pl.BlockSpec((1, SWA_BLOCK, w), lambda i, j: (i, j, c))
    blk_prev = lambda w, c: pl.BlockSpec((1, SWA_BLOCK, w), lambda i, j: (i, jnp.maximum(j - 1, 0), c))
    const = pl.BlockSpec((1, LANES), lambda i, j: (0, 0))
    kvw = SWA_KV_HEADS * SWA_HEAD_DIM
    return pl.pallas_call(
        _swa_kernel,
        grid=(b, nb),
        in_specs=[
            pl.BlockSpec(memory_space=pltpu.SMEM),
            blk(512, 1), blk(512, 2), blk(kvw, 6), blk(kvw, 7), blk_prev(kvw, 6), blk_prev(kvw, 7),
            blk(1, 0), blk_prev(1, 0), const, const, const,
        ],
        out_specs=pl.BlockSpec((1, SWA_BLOCK, SWA_HEADS * SWA_HEAD_DIM), lambda i, j: (i, j, 0)),
        out_shape=jax.ShapeDtypeStruct((b, s, SWA_HEADS * SWA_HEAD_DIM), BF16),
        compiler_params=_params("parallel", "arbitrary"),
        name="swa",
    )(sinks, h, h, h, h, h, h, pos3, pos3, freq, sa, sb)


def _mem_kernel(q_ref, kv_ref, o_ref):
    scale = MEM_HEAD_DIM ** -0.5
    width = MEM_HEADS * MEM_HEAD_DIM
    for hm in range(MEM_HEADS):
        cols = slice(hm * MEM_HEAD_DIM, (hm + 1) * MEM_HEAD_DIM)
        q = (q_ref[0, :, cols] * scale).astype(BF16)
        km = kv_ref[0, :, cols]
        vm = kv_ref[0, :, width + hm * MEM_HEAD_DIM:width + (hm + 1) * MEM_HEAD_DIM]
        sc = lax.dot_general(q, km, (((1,), (1,)), ((), ())), preferred_element_type=F32)
        m = jnp.max(sc, axis=1, keepdims=True)
        p = jnp.exp(sc - m)
        denom = jnp.sum(p, axis=1, keepdims=True)
        o = jnp.dot(p.astype(BF16), vm, preferred_element_type=F32)
        o_ref[0, :, cols] = (o / denom).astype(o_ref.dtype)


def _mem_attention(h, kvm, tq):
    b, s, _ = h.shape
    width = MEM_HEADS * MEM_HEAD_DIM
    tq = min(tq, s)
    return pl.pallas_call(
        _mem_kernel,
        grid=(b, s // tq),
        in_specs=[
            pl.BlockSpec((1, tq, width), lambda i, j: (i, j, 4)),
            pl.BlockSpec((1,) + kvm.shape[1:], lambda i, j: (i, 0, 0)),
        ],
        out_specs=pl.BlockSpec((1, tq, width), lambda i, j: (i, j, 0)),
        out_shape=jax.ShapeDtypeStruct((b, s, width), BF16),
        compiler_params=_params("parallel", "arbitrary"),
        name="mem_attention",
    )(h, kvm)


def _outproj_kernel(pool_ref, swa_ref, mem_ref, wp_ref, ws_ref, wm_ref, x_ref, g_ref, b_ref, o_ref, ob_ref,
                    *, alpha):
    mix = jnp.dot(pool_ref[...], wp_ref[...], preferred_element_type=F32)
    mix += jnp.dot(swa_ref[...], ws_ref[...], preferred_element_type=F32)
    mix += jnp.dot(mem_ref[...], wm_ref[...], preferred_element_type=F32)
    z = alpha * x_ref[...] + mix
    mu = jnp.mean(z, axis=1, keepdims=True)
    zc = z - mu
    var = jnp.mean(zc * zc, axis=1, keepdims=True)
    y = zc * lax.rsqrt(var + LN_EPS) * g_ref[...] + b_ref[...]
    yt = y.T
    o_ref[...] = yt
    ob_ref[...] = yt.astype(BF16)


def _outproj_ln(pool_o, swa_o, mem_o, w_out, x2, g, b, alpha, tm):
    t, d = x2.shape
    tm = min(tm, t)
    wp, ws, wm = pool_o.shape[1], swa_o.shape[1], mem_o.shape[1]
    w_p, w_s, w_m = w_out[:wp], w_out[wp:wp + ws], w_out[wp + ws:]
    row = lambda w: pl.BlockSpec((tm, w), lambda i: (i, 0))
    full = lambda a: pl.BlockSpec(a.shape, lambda i: (0, 0))
    return pl.pallas_call(
        functools.partial(_outproj_kernel, alpha=alpha),
        grid=(t // tm,),
        in_specs=[row(wp), row(ws), row(wm), full(w_p), full(w_s), full(w_m), row(d),
                  pl.BlockSpec((1, d), lambda i: (0, 0)), pl.BlockSpec((1, d), lambda i: (0, 0))],
        out_specs=[pl.BlockSpec((d, tm), lambda i: (0, i))] * 2,
        out_shape=[jax.ShapeDtypeStruct((d, t), F32), jax.ShapeDtypeStruct((d, t), BF16)],
        compiler_params=_params("parallel"),
        name="outproj_ln",
    )(pool_o, swa_o, mem_o, w_p, w_s, w_m, x2, g, b)


def _top16_rows(s, break_ties):
    n = s.shape[0]
    iota = lax.broadcasted_iota(jnp.int32, s.shape, 0).astype(F32) if break_ties else None
    rank = jnp.full(s.shape, float(PEER_TOPK), F32)
    vals = []
    for r in range(PEER_TOPK):
        m = jnp.max(s, axis=0, keepdims=True)
        hit = s == m
        if break_ties:
            idx = jnp.min(jnp.where(hit, iota, float(n)), axis=0, keepdims=True)
            hit = iota == idx
        rank = jnp.where(hit, float(r), rank)
        s = jnp.where(hit, -jnp.inf, s)
        vals.append(m)
    count = jnp.sum(jnp.where(rank < float(PEER_TOPK), 1.0, 0.0), axis=0, keepdims=True)
    return rank, vals, count


_CAND_SMALL_A = PEER_TOPK // 2
_CAND_ROWS = PEER_TOPK + (_CAND_SMALL_A - 1) * SUBLANES + SUBLANES


def _cand_constants():
    flat = np.full((_CAND_ROWS, 1), 1e9, np.float32)
    valid = np.zeros((_CAND_ROWS, 1), np.float32)
    for b in range(PEER_TOPK):
        flat[b, 0], valid[b, 0] = b, 1.0
    for a in range(1, _CAND_SMALL_A):
        base = PEER_TOPK + (a - 1) * SUBLANES
        for b in range(PEER_TOPK // (a + 1)):
            flat[base + b, 0], valid[base + b, 0] = a * PEER_TOPK + b, 1.0
    base = PEER_TOPK + (_CAND_SMALL_A - 1) * SUBLANES
    for k in range(SUBLANES):
        flat[base + k, 0], valid[base + k, 0] = (_CAND_SMALL_A + k) * PEER_TOPK, 1.0
    return jnp.asarray(flat), jnp.asarray(valid)


def _select_experts(s1, s2, flat, valid, break_ties):
    t = s1.shape[1]
    rank1, v1, count1 = _top16_rows(s1, break_ties)
    rank2, v2, count2 = _top16_rows(s2, break_ties)
    v2_lo = jnp.concatenate(v2[:SUBLANES], axis=0)
    v2_all = jnp.concatenate(v2, axis=0)
    v1_hi = jnp.concatenate(v1[_CAND_SMALL_A:], axis=0)
    groups = [v1[0] + v2_all]
    for a in range(1, _CAND_SMALL_A):
        groups.append(v1[a] + v2_lo)
    groups.append(v1_hi + v2[0])
    cand = jnp.concatenate(groups, axis=0)
    cand = jnp.where(valid > 0.5, cand, -jnp.inf)
    flat_b = jnp.broadcast_to(flat, cand.shape) if break_ties else None
    hits = jnp.zeros(cand.shape, F32)
    top = []
    for r in range(PEER_TOPK):
        m = jnp.max(cand, axis=0, keepdims=True)
        hit = cand == m
        if break_ties:
            pick = jnp.min(jnp.where(hit, flat_b, 2e9), axis=0, keepdims=True)
            hit = flat_b == pick
        hits = jnp.where(hit, 1.0, hits)
        cand = jnp.where(hit, -jnp.inf, cand)
        top.append(m)
    z = jnp.ones((1, t), F32)
    for r in range(1, PEER_TOPK):
        z = z + jnp.exp(top[r] - top[0])
    counts = [jnp.sum(hits[0:PEER_TOPK], axis=0, keepdims=True)]
    for a in range(1, _CAND_SMALL_A):
        base = PEER_TOPK + (a - 1) * SUBLANES
        counts.append(jnp.sum(hits[base:base + SUBLANES], axis=0, keepdims=True))
    base = PEER_TOPK + (_CAND_SMALL_A - 1) * SUBLANES
    for k in range(SUBLANES):
        counts.append(hits[base + k:base + k + 1])
    lim = jnp.zeros(s1.shape, F32)
    for a in range(PEER_TOPK):
        lim = jnp.where(rank1 == float(a), counts[a], lim)
    e1n = jnp.exp(s1 - v1[0]) / z
    e2 = jnp.exp(s2 - v2[0])
    count3 = jnp.sum(hits, axis=0, keepdims=True)
    k = float(PEER_TOPK)
    ok = jnp.where((count1 == k) & (count2 == k) & (count3 == k), 1.0, 0.0)
    return lim, e1n, rank2, e2, ok


def _retrieve_kernel(wq_ref, x_ref, k1_ref, k2_ref, flat_ref, valid_ref,
                     lim_ref, e1_ref, r2_ref, e2_ref, q_ref, *, tq):
    q_ref[...] = jnp.dot(wq_ref[...], x_ref[...], preferred_element_type=F32)
    flat, valid = flat_ref[...], valid_ref[...]

    def head(h, carry):
        for c in range(tq // SELECT_TOKENS):
            lanes = slice(c * SELECT_TOKENS, (c + 1) * SELECT_TOKENS)
            r0 = pl.multiple_of(h * 2 * HALF_DIM, 2 * HALF_DIM)
            q1 = q_ref[pl.ds(r0, HALF_DIM), lanes].astype(BF16)
            q2 = q_ref[pl.ds(r0 + HALF_DIM, HALF_DIM), lanes].astype(BF16)
            s1 = jnp.dot(k1_ref[...], q1, preferred_element_type=F32)
            s2 = jnp.dot(k2_ref[...], q2, preferred_element_type=F32)

            def emit(break_ties):
                lim, e1n, rank2, e2, ok = _select_experts(s1, s2, flat, valid, break_ties)
                lim_ref[h, :, lanes] = lim
                e1_ref[h, :, lanes] = e1n
                r2_ref[h, :, lanes] = rank2.astype(BF16)
                e2_ref[h, :, lanes] = e2.astype(BF16)
                return ok

            ok = emit(False)

            @pl.when(jnp.min(ok) < 0.5)
            def _():
                emit(True)
        return carry

    lax.fori_loop(0, PEER_HEADS, head, 0)


def _retrieve(x1t, wq_t, k1, k2, tq):
    d, t = x1t.shape
    tq = min(tq, t)
    flat, valid = _cand_constants()
    full = lambda a: pl.BlockSpec(a.shape, lambda i: (0,) * a.ndim)
    out = jax.ShapeDtypeStruct((PEER_HEADS, N_KEYS, t), F32)
    out_packed = jax.ShapeDtypeStruct((PEER_HEADS, N_KEYS, t), BF16)
    out_spec = pl.BlockSpec((PEER_HEADS, N_KEYS, tq), lambda i: (0, 0, i))
    return pl.pallas_call(
        functools.partial(_retrieve_kernel, tq=tq),
        grid=(t // tq,),
        in_specs=[full(wq_t), pl.BlockSpec((d, tq), lambda i: (0, i)), full(k1), full(k2), full(flat), full(valid)],
        out_specs=[out_spec] * 4,
        out_shape=[out, out, out_packed, out_packed],
        scratch_shapes=[pltpu.VMEM((wq_t.shape[0], tq), F32)],
        compiler_params=_params("parallel"),
        name="peer_retrieve",
    )(wq_t, x1t, k1, k2, flat, valid)


def _gelu_exact(x):
    return 0.5 * x * (1.0 + lax.erf(x * (1.0 / math.sqrt(2.0))))


def _experts_kernel(xb_ref, u_ref, vt_ref, lim_ref, e1_ref, r2_in_ref, e2_in_ref, o_ref,
                    h_ref, a_ref, rows_ref, gate_ref, r2_ref, e2_ref, *, tm, te):
    e = pl.program_id(1)

    @pl.when(e == 0)
    def _():
        o_ref[...] = jnp.zeros(o_ref.shape, F32)
        r2_ref[:, 0:tm] = r2_in_ref[...]
        e2_ref[:, LANES:LANES + tm] = e2_in_ref[...]

    n_chunks = te // EXPERT_CHUNK

    def rows_of(p):
        return slice(p * EXPERT_CHUNK, (p + 1) * EXPERT_CHUNK)

    def gates(first_key):
        keys = range(first_key, first_key + GATE_KEYS)
        blocks = range(0, N_KEYS, GATE_ROWS)
        for i in keys:
            for hd in range(PEER_HEADS):
                for q, ref in enumerate((lim_ref, e1_ref)):
                    row = jnp.broadcast_to(ref[i, hd:hd + 1, :], (GATE_ROWS, tm)).astype(BF16)
                    rows_ref[q, i, hd, :, q * LANES:q * LANES + tm] = row
        for c in range(tm // LANES):
            lanes = slice(c * LANES, (c + 1) * LANES)
            lanes1 = slice((c + 1) * LANES, (c + 2) * LANES)
            gate = {i: {jb: jnp.zeros((GATE_ROWS, LANES), BF16) for jb in blocks} for i in keys}
            for hd in range(PEER_HEADS):
                lim = {i: rows_ref[0, i, hd, :, lanes] for i in keys}
                e1 = {i: rows_ref[1, i, hd, :, lanes1] for i in keys}
                for jb in blocks:
                    r2 = r2_ref[hd * N_KEYS + jb:hd * N_KEYS + jb + GATE_ROWS, lanes]
                    e2 = e2_ref[hd * N_KEYS + jb:hd * N_KEYS + jb + GATE_ROWS, lanes1]
                    for i in keys:
                        gate[i][jb] = gate[i][jb] + jnp.where(r2 < lim[i], e2 * e1[i], jnp.zeros_like(e2))
            for i in keys:
                for jb in blocks:
                    gate_ref[i * N_KEYS + jb:i * N_KEYS + jb + GATE_ROWS, lanes] = gate[i][jb]

    for first_key in range(0, te // N_KEYS, GATE_KEYS):
        gates(first_key)
    h_ref[...] = jnp.dot(u_ref[...], xb_ref[...], preferred_element_type=F32)
    for p in range(n_chunks):
        a_ref[rows_of(p), :] = gate_ref[rows_of(p), :] * _gelu_exact(h_ref[rows_of(p), :]).astype(BF16)
        o_ref[...] += jnp.dot(vt_ref[:, rows_of(p)], a_ref[rows_of(p), :], preferred_element_type=F32)


def _experts(xbt, u, vt, sel, tm, te):
    d, t = xbt.shape
    n_exp = u.shape[0]
    tm, te = min(tm, t), min(te, n_exp)
    tok = pl.BlockSpec((d, tm), lambda i, j: (0, i))
    sel_spec = pl.BlockSpec((PEER_HEADS * N_KEYS, tm), lambda i, j: (0, i))
    key_spec = pl.BlockSpec((te // N_KEYS, PEER_HEADS, tm), lambda i, j: (j, 0, i))
    lim, e1n, rank2, e2 = sel
    sel = (lim.transpose(1, 0, 2), e1n.transpose(1, 0, 2), rank2.reshape(-1, t), e2.reshape(-1, t))
    return pl.pallas_call(
        functools.partial(_experts_kernel, tm=tm, te=te),
        grid=(t // tm, n_exp // te),
        in_specs=[tok, pl.BlockSpec((te, d), lambda i, j: (j, 0)), pl.BlockSpec((d, te), lambda i, j: (0, j)),
                  key_spec, key_spec, sel_spec, sel_spec],
        out_specs=tok,
        out_shape=jax.ShapeDtypeStruct((d, t), F32),
        scratch_shapes=[pltpu.VMEM((te, tm), F32), pltpu.VMEM((te, tm), BF16),
                        pltpu.VMEM((2, te // N_KEYS, PEER_HEADS, GATE_ROWS, tm + LANES), BF16),
                        pltpu.VMEM((te, tm), BF16),
                        pltpu.VMEM((PEER_HEADS * N_KEYS, tm + LANES), BF16),
                        pltpu.VMEM((PEER_HEADS * N_KEYS, tm + LANES), BF16)],
        compiler_params=_params("parallel", "arbitrary"),
        name="peer_experts",
    )(xbt, u, vt, *sel)


def _ln_t_kernel(x_ref, f_ref, g_ref, b_ref, o_ref, *, alpha):
    z = (alpha * x_ref[...] + f_ref[...]).T
    mu = jnp.mean(z, axis=1, keepdims=True)
    zc = z - mu
    var = jnp.mean(zc * zc, axis=1, keepdims=True)
    o_ref[...] = zc * lax.rsqrt(var + LN_EPS) * g_ref[...] + b_ref[...]


def _residual_ln_t(x1t, fft, g, b, alpha, tm):
    d, t = x1t.shape
    tm = min(tm, t)
    tok = pl.BlockSpec((d, tm), lambda i: (0, i))
    vec = pl.BlockSpec((1, d), lambda i: (0, 0))
    return pl.pallas_call(
        functools.partial(_ln_t_kernel, alpha=alpha),
        grid=(t // tm,),
        in_specs=[tok, tok, vec, vec],
        out_specs=pl.BlockSpec((tm, d), lambda i: (i, 0)),
        out_shape=jax.ShapeDtypeStruct((t, d), F32),
        compiler_params=_params("parallel"),
        name="residual_ln",
    )(x1t, fft, g, b)


def kernel(x, mem, positions, w_in, w_mem_kv, w_pool, pool_scale, attn_sinks, w_out, ln1_g, ln1_b,
           w_peer_q, sub_keys_1, sub_keys_2, expert_u, expert_v, ln2_g, ln2_b):
    bsz, seq, d = x.shape
    depth = w_in.shape[0]
    t = bsz * seq
    alpha = (2.0 * depth) ** 0.25
    for l in range(depth):
        x2 = x.reshape(t, d)
        h = _matmul(x2.astype(BF16), w_in[l].astype(BF16), F32, 1024, 512).reshape(bsz, seq, -1)
        mem2 = mem.reshape(-1, d).astype(BF16)
        kvm = _matmul(mem2, w_mem_kv[l].astype(BF16), BF16, 512, 512).reshape(bsz, mem.shape[1], -1)
        pool_o = _pool(h, w_pool[l].astype(BF16), pool_scale[l].reshape(1, -1), 512)
        swa_o = _swa(h, positions, attn_sinks[l])
        mem_o = _mem_attention(h, kvm, 512)
        x1t, x1bt = _outproj_ln(pool_o.reshape(t, -1), swa_o.reshape(t, -1), mem_o.reshape(t, -1),
                          w_out[l].astype(BF16), x2, ln1_g[l].reshape(1, d), ln1_b[l].reshape(1, d), alpha, 512)
        sel = _retrieve(x1bt, w_peer_q[l].T.astype(BF16), sub_keys_1[l].astype(BF16),
                        sub_keys_2[l].astype(BF16), 512)
        fft = _experts(x1bt, expert_u[l].astype(BF16), expert_v[l].T.astype(BF16), sel, 1024, 512)
        x = _residual_ln_t(x1t, fft, ln2_g[l].reshape(1, d), ln2_b[l].reshape(1, d), alpha, 512).reshape(bsz, seq, d)
    return x
```

```python
import functools
import math

import numpy as np
import jax
import jax.numpy as jnp
from jax import lax
from jax.experimental import pallas as pl
from jax.experimental.pallas import tpu as pltpu

F32 = jnp.float32
BF16 = jnp.bfloat16

LANES = 128
SUBLANES = 8
VMEM_LIMIT_BYTES = 56 * 1024 * 1024

POOL_WINDOWS = (2, 4, 8, 16)
POOL_GROUP = 128
POOL_HALO = 16
SWA_HEAD_DIM = 64
SWA_HEADS = 16
SWA_KV_HEADS = 4
SWA_BLOCK = 128
ROPE_THETA = 500000.0
ROPE_DIM = 16
MEM_HEADS = 4
MEM_HEAD_DIM = 128
PEER_HEADS = 8
N_KEYS = 128
PEER_TOPK = 16
HALF_DIM = 128
LN_EPS = 1e-5
NEG = -1e30

EXPERT_CHUNK = 512
GATE_KEYS = 4
GATE_ROWS = 16
SELECT_TOKENS = 512


def _params(*semantics):
    return pltpu.CompilerParams(dimension_semantics=semantics, vmem_limit_bytes=VMEM_LIMIT_BYTES)


def _matmul_kernel(a_ref, b_ref, o_ref):
    o_ref[...] = jnp.dot(a_ref[...], b_ref[...], preferred_element_type=F32).astype(o_ref.dtype)


def _matmul(a, b, out_dtype, tm, tn):
    m, k = a.shape
    n = b.shape[1]
    tm, tn = min(tm, m), min(tn, n)
    return pl.pallas_call(
        _matmul_kernel,
        grid=(m // tm, n // tn),
        in_specs=[pl.BlockSpec((tm, k), lambda i, j: (i, 0)), pl.BlockSpec((k, tn), lambda i, j: (0, j))],
        out_specs=pl.BlockSpec((tm, tn), lambda i, j: (i, j)),
        out_shape=jax.ShapeDtypeStruct((m, n), out_dtype),
        compiler_params=_params("parallel", "arbitrary"),
        name="matmul",
    )(a, b)


IN_BLOCK = 512
_Q_BLOCKS = (1, 2)
_KV_BLOCK = 3


def _rope(x, c, sa, sb):
    half = ROPE_DIM // 2
    return x * c + pltpu.roll(x, LANES - half, 1) * sa + pltpu.roll(x, half, 1) * sb


def _inproj_kernel(x_ref, w_ref, pos_ref, freq_ref, sa_ref, sb_ref, o_ref, xb_ref, cos_ref, sina_ref, sinb_ref):
    j = pl.program_id(1)

    @pl.when(j == 0)
    def _():
        xb_ref[...] = x_ref[...].astype(BF16)
        ang = pos_ref[...].astype(F32) * freq_ref[...]
        s = jnp.sin(ang)
        cos_ref[...] = jnp.cos(ang)
        sina_ref[...] = s * sa_ref[...]
        sinb_ref[...] = s * sb_ref[...]

    y = jnp.dot(xb_ref[...], w_ref[...], preferred_element_type=F32)
    tiles = [slice(k * LANES, (k + 1) * LANES) for k in range(IN_BLOCK // LANES)]

    def rope(cols):
        return _rope(y[:, cols], cos_ref[...], sina_ref[...], sinb_ref[...])

    @pl.when((j == _Q_BLOCKS[0]) | (j == _Q_BLOCKS[1]))
    def _():
        for cols in tiles:
            o_ref[:, cols] = (rope(cols) * SWA_HEAD_DIM ** -0.5).astype(o_ref.dtype)

    @pl.when(j == _KV_BLOCK)
    def _():
        for cols in tiles[:2]:
            o_ref[:, cols] = rope(cols).astype(o_ref.dtype)
        for cols in tiles[2:]:
            o_ref[:, cols] = y[:, cols].astype(o_ref.dtype)

    @pl.when((j != _Q_BLOCKS[0]) & (j != _Q_BLOCKS[1]) & (j != _KV_BLOCK))
    def _():
        o_ref[...] = y.astype(o_ref.dtype)


def _rope_constants():
    lane = np.arange(LANES)
    d = lane % SWA_HEAD_DIM
    half = ROPE_DIM // 2
    inv_freq = np.float32(ROPE_THETA) ** (-np.arange(0, ROPE_DIM, 2, dtype=np.float32) / np.float32(ROPE_DIM))
    freq = np.where(d < ROPE_DIM, inv_freq[d % half], 0.0).astype(np.float32)
    sa = np.where(d < half, -1.0, 0.0).astype(np.float32)
    sb = np.where((d >= half) & (d < ROPE_DIM), 1.0, 0.0).astype(np.float32)
    return [jnp.asarray(a.reshape(1, LANES)) for a in (freq, sa, sb)]


def _inproj(x2, w, positions, tm):
    t, d = x2.shape
    n = w.shape[1]
    tm = min(tm, t)
    freq, sa, sb = _rope_constants()
    const = pl.BlockSpec((1, LANES), lambda i, j: (0, 0))
    return pl.pallas_call(
        _inproj_kernel,
        grid=(t // tm, n // IN_BLOCK),
        in_specs=[pl.BlockSpec((tm, d), lambda i, j: (i, 0)), pl.BlockSpec((d, IN_BLOCK), lambda i, j: (0, j)),
                  pl.BlockSpec((tm, 1), lambda i, j: (i, 0)), const, const, const],
        out_specs=pl.BlockSpec((tm, IN_BLOCK), lambda i, j: (i, j)),
        out_shape=jax.ShapeDtypeStruct((t, n), BF16),
        scratch_shapes=[pltpu.VMEM((tm, d), BF16)] + [pltpu.VMEM((tm, LANES), F32)] * 3,
        compiler_params=_params("parallel", "arbitrary"),
        name="inproj_rope",
    )(x2, w, positions.reshape(t, 1), freq, sa, sb)


def _pool_kernel(v_ref, w_ref, scale_ref, o_ref, ext_ref, *, ts):
    s = pl.program_id(1)

    @pl.when(s == 0)
    def _():
        ext_ref[0:POOL_HALO, :] = jnp.zeros((POOL_HALO, ext_ref.shape[1]), F32)

    ext_ref[POOL_HALO:POOL_HALO + ts, :] = v_ref[0].astype(F32)
    pos = s * ts + lax.broadcasted_iota(jnp.int32, (ts, 1), 0)
    for g, w in enumerate(POOL_WINDOWS):
        cols = slice(g * POOL_GROUP, (g + 1) * POOL_GROUP)
        acc = ext_ref[POOL_HALO:POOL_HALO + ts, cols]
        for k in range(1, w):
            acc = acc + ext_ref[POOL_HALO - k:POOL_HALO - k + ts, cols]
        count = jnp.minimum(pos + 1, w).astype(F32)
        pooled = acc / count - ext_ref[POOL_HALO:POOL_HALO + ts, cols]
        y = jnp.dot(pooled.astype(BF16), w_ref[g], preferred_element_type=F32)
        o_ref[0, :, cols] = (y * scale_ref[:, cols]).astype(o_ref.dtype)
    ext_ref[0:POOL_HALO, :] = ext_ref[ts:ts + POOL_HALO, :]


def _pool(h, w_pool, pool_scale, ts):
    b, s, _ = h.shape
    width = POOL_GROUP * len(POOL_WINDOWS)
    ts = min(ts, s)
    return pl.pallas_call(
        functools.partial(_pool_kernel, ts=ts),
        grid=(b, s // ts),
        in_specs=[
            pl.BlockSpec((1, ts, width), lambda i, j: (i, j, 0)),
            pl.BlockSpec(w_pool.shape, lambda i, j: (0, 0, 0)),
            pl.BlockSpec((1, width), lambda i, j: (0, 0)),
        ],
        out_specs=pl.BlockSpec((1, ts, width), lambda i, j: (i, j, 0)),
        out_shape=jax.ShapeDtypeStruct((b, s, width), BF16),
        scratch_shapes=[pltpu.VMEM((ts + POOL_HALO, width), F32)],
        compiler_params=_params("arbitrary", "arbitrary"),
        name="pool",
    )(h, w_pool, pool_scale)


def _swa_kernel(sink_ref, q0_ref, q1_ref, kv_ref, kvp_ref, o_ref):
    n = pl.program_id(1)
    kvw = SWA_KV_HEADS * SWA_HEAD_DIM
    q = jnp.concatenate([q0_ref[0], q1_ref[0]], axis=1)
    k = jnp.concatenate([kvp_ref[0, :, 0:kvw], kv_ref[0, :, 0:kvw]], axis=0)
    v = jnp.concatenate([kvp_ref[0, :, kvw:2 * kvw], kv_ref[0, :, kvw:2 * kvw]], axis=0)
    row = lax.broadcasted_iota(jnp.int32, (SWA_BLOCK, 2 * SWA_BLOCK), 0)
    col = lax.broadcasted_iota(jnp.int32, (SWA_BLOCK, 2 * SWA_BLOCK), 1)
    rel = row + SWA_BLOCK - col
    valid = (rel >= 0) & (rel < SWA_BLOCK) & ((col >= SWA_BLOCK) | (n > 0))
    group = SWA_HEADS // SWA_KV_HEADS
    outs = []
    for hq in range(SWA_HEADS):
        kv = hq // group
        qh = q[:, hq * SWA_HEAD_DIM:(hq + 1) * SWA_HEAD_DIM]
        kh = k[:, kv * SWA_HEAD_DIM:(kv + 1) * SWA_HEAD_DIM]
        vh = v[:, kv * SWA_HEAD_DIM:(kv + 1) * SWA_HEAD_DIM]
        sc = lax.dot_general(qh, kh, (((1,), (1,)), ((), ())), preferred_element_type=F32)
        sc = jnp.where(valid, sc, NEG)
        sink = sink_ref[hq]
        m = jnp.maximum(jnp.max(sc, axis=1, keepdims=True), sink)
        p = jnp.exp(sc - m)
        denom = jnp.sum(p, axis=1, keepdims=True) + jnp.exp(sink - m)
        o = jnp.dot(p.astype(BF16), vh, preferred_element_type=F32)
        outs.append(o / denom)
    o_ref[0] = jnp.concatenate(outs, axis=1).astype(o_ref.dtype)


def _swa(h, sinks):
    b, s, _ = h.shape
    nb = s // SWA_BLOCK
    blk = lambda c: pl.BlockSpec((1, SWA_BLOCK, IN_BLOCK), lambda i, j: (i, j, c))
    blk_prev = lambda c: pl.BlockSpec((1, SWA_BLOCK, IN_BLOCK), lambda i, j: (i, jnp.maximum(j - 1, 0), c))
    return pl.pallas_call(
        _swa_kernel,
        grid=(b, nb),
        in_specs=[pl.BlockSpec(memory_space=pltpu.SMEM),
                  blk(_Q_BLOCKS[0]), blk(_Q_BLOCKS[1]), blk(_KV_BLOCK), blk_prev(_KV_BLOCK)],
        out_specs=pl.BlockSpec((1, SWA_BLOCK, SWA_HEADS * SWA_HEAD_DIM), lambda i, j: (i, j, 0)),
        out_shape=jax.ShapeDtypeStruct((b, s, SWA_HEADS * SWA_HEAD_DIM), BF16),
        compiler_params=_params("parallel", "arbitrary"),
        name="swa",
    )(sinks, h, h, h, h)


def _mem_kernel(q_ref, kv_ref, o_ref):
    scale = MEM_HEAD_DIM ** -0.5
    width = MEM_HEADS * MEM_HEAD_DIM
    for hm in range(MEM_HEADS):
        cols = slice(hm * MEM_HEAD_DIM, (hm + 1) * MEM_HEAD_DIM)
        km = kv_ref[0, :, cols]
        vm = kv_ref[0, :, width + hm * MEM_HEAD_DIM:width + (hm + 1) * MEM_HEAD_DIM]
        sc = lax.dot_general(q_ref[0, :, cols], km, (((1,), (1,)), ((), ())), preferred_element_type=F32) * scale
        m = jnp.max(sc, axis=1, keepdims=True)
        p = jnp.exp(sc - m)
        denom = jnp.sum(p, axis=1, keepdims=True)
        o = jnp.dot(p.astype(BF16), vm, preferred_element_type=F32)
        o_ref[0, :, cols] = (o / denom).astype(o_ref.dtype)


def _mem_attention(h, kvm, tq):
    b, s, _ = h.shape
    width = MEM_HEADS * MEM_HEAD_DIM
    tq = min(tq, s)
    return pl.pallas_call(
        _mem_kernel,
        grid=(b, s // tq),
        in_specs=[
            pl.BlockSpec((1, tq, width), lambda i, j: (i, j, 4)),
            pl.BlockSpec((1,) + kvm.shape[1:], lambda i, j: (i, 0, 0)),
        ],
        out_specs=pl.BlockSpec((1, tq, width), lambda i, j: (i, j, 0)),
        out_shape=jax.ShapeDtypeStruct((b, s, width), BF16),
        compiler_params=_params("parallel", "arbitrary"),
        name="mem_attention",
    )(h, kvm)


def _outproj_kernel(pool_ref, swa_ref, mem_ref, wp_ref, ws_ref, wm_ref, x_ref, g_ref, b_ref, o_ref, ob_ref,
                    *, alpha):
    mix = jnp.dot(pool_ref[...], wp_ref[...], preferred_element_type=F32)
    mix += jnp.dot(swa_ref[...], ws_ref[...], preferred_element_type=F32)
    mix += jnp.dot(mem_ref[...], wm_ref[...], preferred_element_type=F32)
    z = alpha * x_ref[...] + mix
    mu = jnp.mean(z, axis=1, keepdims=True)
    zc = z - mu
    var = jnp.mean(zc * zc, axis=1, keepdims=True)
    y = zc * lax.rsqrt(var + LN_EPS) * g_ref[...] + b_ref[...]
    yt = y.T
    o_ref[...] = yt
    ob_ref[...] = yt.astype(BF16)


def _outproj_ln(pool_o, swa_o, mem_o, w_out, x2, g, b, alpha, tm):
    t, d = x2.shape
    tm = min(tm, t)
    wp, ws, wm = pool_o.shape[1], swa_o.shape[1], mem_o.shape[1]
    w_p, w_s, w_m = w_out[:wp], w_out[wp:wp + ws], w_out[wp + ws:]
    row = lambda w: pl.BlockSpec((tm, w), lambda i: (i, 0))
    full = lambda a: pl.BlockSpec(a.shape, lambda i: (0, 0))
    return pl.pallas_call(
        functools.partial(_outproj_kernel, alpha=alpha),
        grid=(t // tm,),
        in_specs=[row(wp), row(ws), row(wm), full(w_p), full(w_s), full(w_m), row(d),
                  pl.BlockSpec((1, d), lambda i: (0, 0)), pl.BlockSpec((1, d), lambda i: (0, 0))],
        out_specs=[pl.BlockSpec((d, tm), lambda i: (0, i))] * 2,
        out_shape=[jax.ShapeDtypeStruct((d, t), F32), jax.ShapeDtypeStruct((d, t), BF16)],
        compiler_params=_params("parallel"),
        name="outproj_ln",
    )(pool_o, swa_o, mem_o, w_p, w_s, w_m, x2, g, b)


def _top16_rows(s, break_ties):
    n = s.shape[0]
    iota = lax.broadcasted_iota(jnp.int32, s.shape, 0).astype(F32) if break_ties else None
    rank = jnp.full(s.shape, float(PEER_TOPK), F32)
    vals = []
    for r in range(PEER_TOPK):
        m = jnp.max(s, axis=0, keepdims=True)
        hit = s == m
        if break_ties:
            idx = jnp.min(jnp.where(hit, iota, float(n)), axis=0, keepdims=True)
            hit = iota == idx
        rank = jnp.where(hit, float(r), rank)
        s = jnp.where(hit, -jnp.inf, s)
        vals.append(m)
    count = jnp.sum(jnp.where(rank < float(PEER_TOPK), 1.0, 0.0), axis=0, keepdims=True)
    return rank, vals, count


_CAND_SMALL_A = PEER_TOPK // 2
_CAND_ROWS = PEER_TOPK + (_CAND_SMALL_A - 1) * SUBLANES + SUBLANES


def _cand_constants():
    flat = np.full((_CAND_ROWS, 1), 1e9, np.float32)
    valid = np.zeros((_CAND_ROWS, 1), np.float32)
    for b in range(PEER_TOPK):
        flat[b, 0], valid[b, 0] = b, 1.0
    for a in range(1, _CAND_SMALL_A):
        base = PEER_TOPK + (a - 1) * SUBLANES
        for b in range(PEER_TOPK // (a + 1)):
            flat[base + b, 0], valid[base + b, 0] = a * PEER_TOPK + b, 1.0
    base = PEER_TOPK + (_CAND_SMALL_A - 1) * SUBLANES
    for k in range(SUBLANES):
        flat[base + k, 0], valid[base + k, 0] = (_CAND_SMALL_A + k) * PEER_TOPK, 1.0
    return jnp.asarray(flat), jnp.asarray(valid)


def _select_experts(s1, s2, flat, valid, break_ties):
    t = s1.shape[1]
    rank1, v1, count1 = _top16_rows(s1, break_ties)
    rank2, v2, count2 = _top16_rows(s2, break_ties)
    v2_lo = jnp.concatenate(v2[:SUBLANES], axis=0)
    v2_all = jnp.concatenate(v2, axis=0)
    v1_hi = jnp.concatenate(v1[_CAND_SMALL_A:], axis=0)
    groups = [v1[0] + v2_all]
    for a in range(1, _CAND_SMALL_A):
        groups.append(v1[a] + v2_lo)
    groups.append(v1_hi + v2[0])
    cand = jnp.concatenate(groups, axis=0)
    cand = jnp.where(valid > 0.5, cand, -jnp.inf)
    flat_b = jnp.broadcast_to(flat, cand.shape) if break_ties else None
    hits = jnp.zeros(cand.shape, F32)
    top = []
    for r in range(PEER_TOPK):
        m = jnp.max(cand, axis=0, keepdims=True)
        hit = cand == m
        if break_ties:
            pick = jnp.min(jnp.where(hit, flat_b, 2e9), axis=0, keepdims=True)
            hit = flat_b == pick
        hits = jnp.where(hit, 1.0, hits)
        cand = jnp.where(hit, -jnp.inf, cand)
        top.append(m)
    z = jnp.ones((1, t), F32)
    for r in range(1, PEER_TOPK):
        z = z + jnp.exp(top[r] - top[0])
    counts = [jnp.sum(hits[0:PEER_TOPK], axis=0, keepdims=True)]
    for a in range(1, _CAND_SMALL_A):
        base = PEER_TOPK + (a - 1) * SUBLANES
        counts.append(jnp.sum(hits[base:base + SUBLANES], axis=0, keepdims=True))
    base = PEER_TOPK + (_CAND_SMALL_A - 1) * SUBLANES
    for k in range(SUBLANES):
        counts.append(hits[base + k:base + k + 1])
    lim = jnp.zeros(s1.shape, F32)
    for a in range(PEER_TOPK):
        lim = jnp.where(rank1 == float(a), counts[a], lim)
    e1n = jnp.exp(s1 - v1[0]) / z
    e2 = jnp.exp(s2 - v2[0])
    count3 = jnp.sum(hits, axis=0, keepdims=True)
    k = float(PEER_TOPK)
    ok = jnp.where((count1 == k) & (count2 == k) & (count3 == k), 1.0, 0.0)
    return lim, e1n, rank2, e2, ok


def _retrieve_kernel(wq_ref, x_ref, k1_ref, k2_ref, flat_ref, valid_ref,
                     lim_ref, e1_ref, r2_ref, e2_ref, q_ref, *, tq):
    q_ref[...] = jnp.dot(wq_ref[...], x_ref[...], preferred_element_type=F32)
    flat, valid = flat_ref[...], valid_ref[...]

    def head(h, carry):
        for c in range(tq // SELECT_TOKENS):
            lanes = slice(c * SELECT_TOKENS, (c + 1) * SELECT_TOKENS)
            r0 = pl.multiple_of(h * 2 * HALF_DIM, 2 * HALF_DIM)
            q1 = q_ref[pl.ds(r0, HALF_DIM), lanes].astype(BF16)
            q2 = q_ref[pl.ds(r0 + HALF_DIM, HALF_DIM), lanes].astype(BF16)
            s1 = jnp.dot(k1_ref[...], q1, preferred_element_type=F32)
            s2 = jnp.dot(k2_ref[...], q2, preferred_element_type=F32)

            def emit(break_ties):
                lim, e1n, rank2, e2, ok = _select_experts(s1, s2, flat, valid, break_ties)
                lim_ref[h, :, lanes] = lim
                e1_ref[h, :, lanes] = e1n
                r2_ref[h, :, lanes] = rank2.astype(BF16)
                e2_ref[h, :, lanes] = e2.astype(BF16)
                return ok

            ok = emit(False)

            @pl.when(jnp.min(ok) < 0.5)
            def _():
                emit(True)
        return carry

    lax.fori_loop(0, PEER_HEADS, head, 0)


def _retrieve(x1t, wq_t, k1, k2, tq):
    d, t = x1t.shape
    tq = min(tq, t)
    flat, valid = _cand_constants()
    full = lambda a: pl.BlockSpec(a.shape, lambda i: (0,) * a.ndim)
    out = jax.ShapeDtypeStruct((PEER_HEADS, N_KEYS, t), F32)
    out_packed = jax.ShapeDtypeStruct((PEER_HEADS, N_KEYS, t), BF16)
    out_spec = pl.BlockSpec((PEER_HEADS, N_KEYS, tq), lambda i: (0, 0, i))
    return pl.pallas_call(
        functools.partial(_retrieve_kernel, tq=tq),
        grid=(t // tq,),
        in_specs=[full(wq_t), pl.BlockSpec((d, tq), lambda i: (0, i)), full(k1), full(k2), full(flat), full(valid)],
        out_specs=[out_spec] * 4,
        out_shape=[out, out, out_packed, out_packed],
        scratch_shapes=[pltpu.VMEM((wq_t.shape[0], tq), F32)],
        compiler_params=_params("parallel"),
        name="peer_retrieve",
    )(wq_t, x1t, k1, k2, flat, valid)


def _gelu_exact(x):
    return 0.5 * x * (1.0 + lax.erf(x * (1.0 / math.sqrt(2.0))))


def _experts_kernel(xb_ref, u_ref, vt_ref, lim_ref, e1_ref, r2_in_ref, e2_in_ref, o_ref,
                    h_ref, a_ref, rows_ref, gate_ref, r2_ref, e2_ref, *, tm, te):
    e = pl.program_id(1)

    @pl.when(e == 0)
    def _():
        o_ref[...] = jnp.zeros(o_ref.shape, F32)
        r2_ref[:, 0:tm] = r2_in_ref[...]
        e2_ref[:, LANES:LANES + tm] = e2_in_ref[...]

    n_chunks = te // EXPERT_CHUNK

    def rows_of(p):
        return slice(p * EXPERT_CHUNK, (p + 1) * EXPERT_CHUNK)

    def gates(first_key):
        keys = range(first_key, first_key + GATE_KEYS)
        blocks = range(0, N_KEYS, GATE_ROWS)
        for i in keys:
            for hd in range(PEER_HEADS):
                for q, ref in enumerate((lim_ref, e1_ref)):
                    row = jnp.broadcast_to(ref[i, hd:hd + 1, :], (GATE_ROWS, tm)).astype(BF16)
                    rows_ref[q, i, hd, :, q * LANES:q * LANES + tm] = row
        for c in range(tm // LANES):
            lanes = slice(c * LANES, (c + 1) * LANES)
            lanes1 = slice((c + 1) * LANES, (c + 2) * LANES)
            gate = {i: {jb: jnp.zeros((GATE_ROWS, LANES), BF16) for jb in blocks} for i in keys}
            for hd in range(PEER_HEADS):
                lim = {i: rows_ref[0, i, hd, :, lanes] for i in keys}
                e1 = {i: rows_ref[1, i, hd, :, lanes1] for i in keys}
                for jb in blocks:
                    r2 = r2_ref[hd * N_KEYS + jb:hd * N_KEYS + jb + GATE_ROWS, lanes]
                    e2 = e2_ref[hd * N_KEYS + jb:hd * N_KEYS + jb + GATE_ROWS, lanes1]
                    for i in keys:
                        gate[i][jb] = gate[i][jb] + jnp.where(r2 < lim[i], e2 * e1[i], jnp.zeros_like(e2))
            for i in keys:
                for jb in blocks:
                    gate_ref[i * N_KEYS + jb:i * N_KEYS + jb + GATE_ROWS, lanes] = gate[i][jb]

    for first_key in range(0, te // N_KEYS, GATE_KEYS):
        gates(first_key)
    h_ref[...] = jnp.dot(u_ref[...], xb_ref[...], preferred_element_type=F32)
    for p in range(n_chunks):
        a_ref[rows_of(p), :] = gate_ref[rows_of(p), :] * _gelu_exact(h_ref[rows_of(p), :]).astype(BF16)
        o_ref[...] += jnp.dot(vt_ref[:, rows_of(p)], a_ref[rows_of(p), :], preferred_element_type=F32)


def _experts(xbt, u, vt, sel, tm, te):
    d, t = xbt.shape
    n_exp = u.shape[0]
    tm, te = min(tm, t), min(te, n_exp)
    tok = pl.BlockSpec((d, tm), lambda i, j: (0, i))
    sel_spec = pl.BlockSpec((PEER_HEADS * N_KEYS, tm), lambda i, j: (0, i))
    key_spec = pl.BlockSpec((te // N_KEYS, PEER_HEADS, tm), lambda i, j: (j, 0, i))
    lim, e1n, rank2, e2 = sel
    sel = (lim.transpose(1, 0, 2), e1n.transpose(1, 0, 2), rank2.reshape(-1, t), e2.reshape(-1, t))
    return pl.pallas_call(
        functools.partial(_experts_kernel, tm=tm, te=te),
        grid=(t // tm, n_exp // te),
        in_specs=[tok, pl.BlockSpec((te, d), lambda i, j: (j, 0)), pl.BlockSpec((d, te), lambda i, j: (0, j)),
                  key_spec, key_spec, sel_spec, sel_spec],
        out_specs=tok,
        out_shape=jax.ShapeDtypeStruct((d, t), F32),
        scratch_shapes=[pltpu.VMEM((te, tm), F32), pltpu.VMEM((te, tm), BF16),
                        pltpu.VMEM((2, te // N_KEYS, PEER_HEADS, GATE_ROWS, tm + LANES), BF16),
                        pltpu.VMEM((te, tm), BF16),
                        pltpu.VMEM((PEER_HEADS * N_KEYS, tm + LANES), BF16),
                        pltpu.VMEM((PEER_HEADS * N_KEYS, tm + LANES), BF16)],
        compiler_params=_params("parallel", "arbitrary"),
        name="peer_experts",
    )(xbt, u, vt, *sel)


def _ln_t_kernel(x_ref, f_ref, g_ref, b_ref, o_ref, *, alpha):
    z = (alpha * x_ref[...] + f_ref[...]).T
    mu = jnp.mean(z, axis=1, keepdims=True)
    zc = z - mu
    var = jnp.mean(zc * zc, axis=1, keepdims=True)
    o_ref[...] = zc * lax.rsqrt(var + LN_EPS) * g_ref[...] + b_ref[...]


def _residual_ln_t(x1t, fft, g, b, alpha, tm):
    d, t = x1t.shape
    tm = min(tm, t)
    tok = pl.BlockSpec((d, tm), lambda i: (0, i))
    vec = pl.BlockSpec((1, d), lambda i: (0, 0))
    return pl.pallas_call(
        functools.partial(_ln_t_kernel, alpha=alpha),
        grid=(t // tm,),
        in_specs=[tok, tok, vec, vec],
        out_specs=pl.BlockSpec((tm, d), lambda i: (i, 0)),
        out_shape=jax.ShapeDtypeStruct((t, d), F32),
        compiler_params=_params("parallel"),
        name="residual_ln",
    )(x1t, fft, g, b)


def kernel(x, mem, positions, w_in, w_mem_kv, w_pool, pool_scale, attn_sinks, w_out, ln1_g, ln1_b,
           w_peer_q, sub_keys_1, sub_keys_2, expert_u, expert_v, ln2_g, ln2_b):
    bsz, seq, d = x.shape
    depth = w_in.shape[0]
    t = bsz * seq
    alpha = (2.0 * depth) ** 0.25
    for l in range(depth):
        x2 = x.reshape(t, d)
        h = _inproj(x2, w_in[l].astype(BF16), positions, 1024).reshape(bsz, seq, -1)
        mem2 = mem.reshape(-1, d).astype(BF16)
        kvm = _matmul(mem2, w_mem_kv[l].astype(BF16), BF16, 512, 512).reshape(bsz, mem.shape[1], -1)
        pool_o = _pool(h, w_pool[l].astype(BF16), pool_scale[l].reshape(1, -1), 512)
        swa_o = _swa(h, attn_sinks[l])
        mem_o = _mem_attention(h, kvm, 512)
        x1t, x1bt = _outproj_ln(pool_o.reshape(t, -1), swa_o.reshape(t, -1), mem_o.reshape(t, -1),
                          w_out[l].astype(BF16), x2, ln1_g[l].reshape(1, d), ln1_b[l].reshape(1, d), alpha, 512)
        sel = _retrieve(x1bt, w_peer_q[l].T.astype(BF16), sub_keys_1[l].astype(BF16),
                        sub_keys_2[l].astype(BF16), 512)
        fft = _experts(x1bt, expert_u[l].astype(BF16), expert_v[l].T.astype(BF16), sel, 1024, 512)
        x = _residual_ln_t(x1t, fft, ln2_g[l].reshape(1, d), ln2_b[l].reshape(1, d), alpha, 512).reshape(bsz, seq, d)
    return x
```

```python
import functools
import math

import numpy as np
import jax
import jax.numpy as jnp
from jax import lax
from jax.experimental import pallas as pl
from jax.experimental.pallas import tpu as pltpu

F32 = jnp.float32
BF16 = jnp.bfloat16

LANES = 128
SUBLANES = 8
VMEM_LIMIT_BYTES = 56 * 1024 * 1024

POOL_WINDOWS = (2, 4, 8, 16)
POOL_GROUP = 128
POOL_HALO = 16
SWA_HEAD_DIM = 64
SWA_HEADS = 16
SWA_KV_HEADS = 4
SWA_BLOCK = 128
ROPE_THETA = 500000.0
ROPE_DIM = 16
MEM_HEADS = 4
MEM_HEAD_DIM = 128
PEER_HEADS = 8
N_KEYS = 128
PEER_TOPK = 16
HALF_DIM = 128
LN_EPS = 1e-5
NEG = -1e30

EXPERT_CHUNK = 512
GATE_KEYS = 4
GATE_ROWS = 16
SELECT_TOKENS = 512


def _params(*semantics):
    return pltpu.CompilerParams(dimension_semantics=semantics, vmem_limit_bytes=VMEM_LIMIT_BYTES)


def _matmul_kernel(a_ref, b_ref, o_ref):
    o_ref[...] = jnp.dot(a_ref[...], b_ref[...], preferred_element_type=F32).astype(o_ref.dtype)


def _matmul(a, b, out_dtype, tm, tn):
    m, k = a.shape
    n = b.shape[1]
    tm, tn = min(tm, m), min(tn, n)
    return pl.pallas_call(
        _matmul_kernel,
        grid=(m // tm, n // tn),
        in_specs=[pl.BlockSpec((tm, k), lambda i, j: (i, 0)), pl.BlockSpec((k, tn), lambda i, j: (0, j))],
        out_specs=pl.BlockSpec((tm, tn), lambda i, j: (i, j)),
        out_shape=jax.ShapeDtypeStruct((m, n), out_dtype),
        compiler_params=_params("parallel", "arbitrary"),
        name="matmul",
    )(a, b)


IN_BLOCK = 512
_Q_BLOCKS = (1, 2)
_KV_BLOCK = 3


def _rope(x, c, sa, sb):
    half = ROPE_DIM // 2
    return x * c + pltpu.roll(x, LANES - half, 1) * sa + pltpu.roll(x, half, 1) * sb


def _inproj_kernel(x_ref, w_ref, pos_ref, freq_ref, sa_ref, sb_ref, o_ref, xb_ref, cos_ref, sina_ref, sinb_ref):
    j = pl.program_id(1)

    @pl.when(j == 0)
    def _():
        xb_ref[...] = x_ref[...].astype(BF16)
        ang = pos_ref[...].astype(F32) * freq_ref[...]
        s = jnp.sin(ang)
        cos_ref[...] = jnp.cos(ang)
        sina_ref[...] = s * sa_ref[...]
        sinb_ref[...] = s * sb_ref[...]

    y = jnp.dot(xb_ref[...], w_ref[...], preferred_element_type=F32)
    tiles = [slice(k * LANES, (k + 1) * LANES) for k in range(IN_BLOCK // LANES)]

    def rope(cols):
        return _rope(y[:, cols], cos_ref[...], sina_ref[...], sinb_ref[...])

    @pl.when((j == _Q_BLOCKS[0]) | (j == _Q_BLOCKS[1]))
    def _():
        for cols in tiles:
            o_ref[:, cols] = (rope(cols) * SWA_HEAD_DIM ** -0.5).astype(o_ref.dtype)

    @pl.when(j == _KV_BLOCK)
    def _():
        for cols in tiles[:2]:
            o_ref[:, cols] = rope(cols).astype(o_ref.dtype)
        for cols in tiles[2:]:
            o_ref[:, cols] = y[:, cols].astype(o_ref.dtype)

    @pl.when((j != _Q_BLOCKS[0]) & (j != _Q_BLOCKS[1]) & (j != _KV_BLOCK))
    def _():
        o_ref[...] = y.astype(o_ref.dtype)


def _rope_constants():
    lane = np.arange(LANES)
    d = lane % SWA_HEAD_DIM
    half = ROPE_DIM // 2
    inv_freq = np.float32(ROPE_THETA) ** (-np.arange(0, ROPE_DIM, 2, dtype=np.float32) / np.float32(ROPE_DIM))
    freq = np.where(d < ROPE_DIM, inv_freq[d % half], 0.0).astype(np.float32)
    sa = np.where(d < half, -1.0, 0.0).astype(np.float32)
    sb = np.where((d >= half) & (d < ROPE_DIM), 1.0, 0.0).astype(np.float32)
    return [jnp.asarray(a.reshape(1, LANES)) for a in (freq, sa, sb)]


def _inproj(x2, w, positions, tm):
    t, d = x2.shape
    n = w.shape[1]
    tm = min(tm, t)
    freq, sa, sb = _rope_constants()
    const = pl.BlockSpec((1, LANES), lambda i, j: (0, 0))
    return pl.pallas_call(
        _inproj_kernel,
        grid=(t // tm, n // IN_BLOCK),
        in_specs=[pl.BlockSpec((tm, d), lambda i, j: (i, 0)), pl.BlockSpec((d, IN_BLOCK), lambda i, j: (0, j)),
                  pl.BlockSpec((tm, 1), lambda i, j: (i, 0)), const, const, const],
        out_specs=pl.BlockSpec((tm, IN_BLOCK), lambda i, j: (i, j)),
        out_shape=jax.ShapeDtypeStruct((t, n), BF16),
        scratch_shapes=[pltpu.VMEM((tm, d), BF16)] + [pltpu.VMEM((tm, LANES), F32)] * 3,
        compiler_params=_params("parallel", "arbitrary"),
        name="inproj_rope",
    )(x2, w, positions.reshape(t, 1), freq, sa, sb)


def _pool_kernel(v_ref, w_ref, scale_ref, o_ref, ext_ref, *, ts):
    s = pl.program_id(1)

    @pl.when(s == 0)
    def _():
        ext_ref[0:POOL_HALO, :] = jnp.zeros((POOL_HALO, ext_ref.shape[1]), F32)

    ext_ref[POOL_HALO:POOL_HALO + ts, :] = v_ref[0].astype(F32)
    pos = s * ts + lax.broadcasted_iota(jnp.int32, (ts, 1), 0)
    for g, w in enumerate(POOL_WINDOWS):
        cols = slice(g * POOL_GROUP, (g + 1) * POOL_GROUP)
        acc = ext_ref[POOL_HALO:POOL_HALO + ts, cols]
        for k in range(1, w):
            acc = acc + ext_ref[POOL_HALO - k:POOL_HALO - k + ts, cols]
        count = jnp.minimum(pos + 1, w).astype(F32)
        pooled = acc / count - ext_ref[POOL_HALO:POOL_HALO + ts, cols]
        y = jnp.dot(pooled.astype(BF16), w_ref[g], preferred_element_type=F32)
        o_ref[0, :, cols] = (y * scale_ref[:, cols]).astype(o_ref.dtype)
    ext_ref[0:POOL_HALO, :] = ext_ref[ts:ts + POOL_HALO, :]


def _pool(h, w_pool, pool_scale, ts):
    b, s, _ = h.shape
    width = POOL_GROUP * len(POOL_WINDOWS)
    ts = min(ts, s)
    return pl.pallas_call(
        functools.partial(_pool_kernel, ts=ts),
        grid=(b, s // ts),
        in_specs=[
            pl.BlockSpec((1, ts, width), lambda i, j: (i, j, 0)),
            pl.BlockSpec(w_pool.shape, lambda i, j: (0, 0, 0)),
            pl.BlockSpec((1, width), lambda i, j: (0, 0)),
        ],
        out_specs=pl.BlockSpec((1, ts, width), lambda i, j: (i, j, 0)),
        out_shape=jax.ShapeDtypeStruct((b, s, width), BF16),
        scratch_shapes=[pltpu.VMEM((ts + POOL_HALO, width), F32)],
        compiler_params=_params("arbitrary", "arbitrary"),
        name="pool",
    )(h, w_pool, pool_scale)


def _swa_kernel(sink_ref, q0_ref, q1_ref, kv_ref, kvp_ref, o_ref):
    n = pl.program_id(1)
    kvw = SWA_KV_HEADS * SWA_HEAD_DIM
    q = jnp.concatenate([q0_ref[0], q1_ref[0]], axis=1)
    k = jnp.concatenate([kvp_ref[0, :, 0:kvw], kv_ref[0, :, 0:kvw]], axis=0)
    v = jnp.concatenate([kvp_ref[0, :, kvw:2 * kvw], kv_ref[0, :, kvw:2 * kvw]], axis=0)
    row = lax.broadcasted_iota(jnp.int32, (SWA_BLOCK, 2 * SWA_BLOCK), 0)
    col = lax.broadcasted_iota(jnp.int32, (SWA_BLOCK, 2 * SWA_BLOCK), 1)
    rel = row + SWA_BLOCK - col
    valid = (rel >= 0) & (rel < SWA_BLOCK) & ((col >= SWA_BLOCK) | (n > 0))
    group = SWA_HEADS // SWA_KV_HEADS
    outs = []
    for hq in range(SWA_HEADS):
        kv = hq // group
        qh = q[:, hq * SWA_HEAD_DIM:(hq + 1) * SWA_HEAD_DIM]
        kh = k[:, kv * SWA_HEAD_DIM:(kv + 1) * SWA_HEAD_DIM]
        vh = v[:, kv * SWA_HEAD_DIM:(kv + 1) * SWA_HEAD_DIM]
        sc = lax.dot_general(qh, kh, (((1,), (1,)), ((), ())), preferred_element_type=F32)
        sc = jnp.where(valid, sc, NEG)
        sink = sink_ref[hq]
        m = jnp.maximum(jnp.max(sc, axis=1, keepdims=True), sink)
        p = jnp.exp(sc - m)
        denom = jnp.sum(p, axis=1, keepdims=True) + jnp.exp(sink - m)
        o = jnp.dot(p.astype(BF16), vh, preferred_element_type=F32)
        outs.append(o / denom)
    o_ref[0] = jnp.concatenate(outs, axis=1).astype(o_ref.dtype)


def _swa(h, sinks):
    b, s, _ = h.shape
    nb = s // SWA_BLOCK
    blk = lambda c: pl.BlockSpec((1, SWA_BLOCK, IN_BLOCK), lambda i, j: (i, j, c))
    blk_prev = lambda c: pl.BlockSpec((1, SWA_BLOCK, IN_BLOCK), lambda i, j: (i, jnp.maximum(j - 1, 0), c))
    return pl.pallas_call(
        _swa_kernel,
        grid=(b, nb),
        in_specs=[pl.BlockSpec(memory_space=pltpu.SMEM),
                  blk(_Q_BLOCKS[0]), blk(_Q_BLOCKS[1]), blk(_KV_BLOCK), blk_prev(_KV_BLOCK)],
        out_specs=pl.BlockSpec((1, SWA_BLOCK, SWA_HEADS * SWA_HEAD_DIM), lambda i, j: (i, j, 0)),
        out_shape=jax.ShapeDtypeStruct((b, s, SWA_HEADS * SWA_HEAD_DIM), BF16),
        compiler_params=_params("parallel", "arbitrary"),
        name="swa",
    )(sinks, h, h, h, h)


def _mem_kernel(q_ref, kv_ref, o_ref):
    scale = MEM_HEAD_DIM ** -0.5
    width = MEM_HEADS * MEM_HEAD_DIM
    for hm in range(MEM_HEADS):
        cols = slice(hm * MEM_HEAD_DIM, (hm + 1) * MEM_HEAD_DIM)
        km = kv_ref[0, :, cols]
        vm = kv_ref[0, :, width + hm * MEM_HEAD_DIM:width + (hm + 1) * MEM_HEAD_DIM]
        sc = lax.dot_general(q_ref[0, :, cols], km, (((1,), (1,)), ((), ())), preferred_element_type=F32) * scale
        m = jnp.max(sc, axis=1, keepdims=True)
        p = jnp.exp(sc - m)
        denom = jnp.sum(p, axis=1, keepdims=True)
        o = jnp.dot(p.astype(BF16), vm, preferred_element_type=F32)
        o_ref[0, :, cols] = (o / denom).astype(o_ref.dtype)


def _mem_attention(h, kvm, tq):
    b, s, _ = h.shape
    width = MEM_HEADS * MEM_HEAD_DIM
    tq = min(tq, s)
    return pl.pallas_call(
        _mem_kernel,
        grid=(b, s // tq),
        in_specs=[
            pl.BlockSpec((1, tq, width), lambda i, j: (i, j, 4)),
            pl.BlockSpec((1,) + kvm.shape[1:], lambda i, j: (i, 0, 0)),
        ],
        out_specs=pl.BlockSpec((1, tq, width), lambda i, j: (i, j, 0)),
        out_shape=jax.ShapeDtypeStruct((b, s, width), BF16),
        compiler_params=_params("parallel", "arbitrary"),
        name="mem_attention",
    )(h, kvm)


def _outproj_kernel(pool_ref, swa_ref, mem_ref, wp_ref, ws_ref, wm_ref, x_ref, g_ref, b_ref, o_ref, ob_ref,
                    *, alpha):
    mix = jnp.dot(pool_ref[...], wp_ref[...], preferred_element_type=F32)
    mix += jnp.dot(swa_ref[...], ws_ref[...], preferred_element_type=F32)
    mix += jnp.dot(mem_ref[...], wm_ref[...], preferred_element_type=F32)
    z = alpha * x_ref[...] + mix
    mu = jnp.mean(z, axis=1, keepdims=True)
    zc = z - mu
    var = jnp.mean(zc * zc, axis=1, keepdims=True)
    y = zc * lax.rsqrt(var + LN_EPS) * g_ref[...] + b_ref[...]
    yt = y.T
    o_ref[...] = yt
    ob_ref[...] = yt.astype(BF16)


def _outproj_ln(pool_o, swa_o, mem_o, w_out, x2, g, b, alpha, tm):
    t, d = x2.shape
    tm = min(tm, t)
    wp, ws, wm = pool_o.shape[1], swa_o.shape[1], mem_o.shape[1]
    w_p, w_s, w_m = w_out[:wp], w_out[wp:wp + ws], w_out[wp + ws:]
    row = lambda w: pl.BlockSpec((tm, w), lambda i: (i, 0))
    full = lambda a: pl.BlockSpec(a.shape, lambda i: (0, 0))
    return pl.pallas_call(
        functools.partial(_outproj_kernel, alpha=alpha),
        grid=(t // tm,),
        in_specs=[row(wp), row(ws), row(wm), full(w_p), full(w_s), full(w_m), row(d),
                  pl.BlockSpec((1, d), lambda i: (0, 0)), pl.BlockSpec((1, d), lambda i: (0, 0))],
        out_specs=[pl.BlockSpec((d, tm), lambda i: (0, i))] * 2,
        out_shape=[jax.ShapeDtypeStruct((d, t), F32), jax.ShapeDtypeStruct((d, t), BF16)],
        compiler_params=_params("parallel"),
        name="outproj_ln",
    )(pool_o, swa_o, mem_o, w_p, w_s, w_m, x2, g, b)


def _top16_rows(s, break_ties):
    n = s.shape[0]
    iota = lax.broadcasted_iota(jnp.int32, s.shape, 0).astype(F32) if break_ties else None
    rank = jnp.full(s.shape, float(PEER_TOPK), F32)
    vals = []
    for r in range(PEER_TOPK):
        m = jnp.max(s, axis=0, keepdims=True)
        hit = s == m
        if break_ties:
            idx = jnp.min(jnp.where(hit, iota, float(n)), axis=0, keepdims=True)
            hit = iota == idx
        rank = jnp.where(hit, float(r), rank)
        s = jnp.where(hit, -jnp.inf, s)
        vals.append(m)
    count = jnp.sum(jnp.where(rank < float(PEER_TOPK), 1.0, 0.0), axis=0, keepdims=True)
    return rank, vals, count


_CAND_SMALL_A = PEER_TOPK // 2
_CAND_ROWS = PEER_TOPK + (_CAND_SMALL_A - 1) * SUBLANES + SUBLANES


def _cand_constants():
    flat = np.full((_CAND_ROWS, 1), 1e9, np.float32)
    valid = np.zeros((_CAND_ROWS, 1), np.float32)
    for b in range(PEER_TOPK):
        flat[b, 0], valid[b, 0] = b, 1.0
    for a in range(1, _CAND_SMALL_A):
        base = PEER_TOPK + (a - 1) * SUBLANES
        for b in range(PEER_TOPK // (a + 1)):
            flat[base + b, 0], valid[base + b, 0] = a * PEER_TOPK + b, 1.0
    base = PEER_TOPK + (_CAND_SMALL_A - 1) * SUBLANES
    for k in range(SUBLANES):
        flat[base + k, 0], valid[base + k, 0] = (_CAND_SMALL_A + k) * PEER_TOPK, 1.0
    return jnp.asarray(flat), jnp.asarray(valid)


def _select_experts(s1, s2, flat, valid, break_ties):
    t = s1.shape[1]
    rank1, v1, count1 = _top16_rows(s1, break_ties)
    rank2, v2, count2 = _top16_rows(s2, break_ties)
    v2_lo = jnp.concatenate(v2[:SUBLANES], axis=0)
    v2_all = jnp.concatenate(v2, axis=0)
    v1_hi = jnp.concatenate(v1[_CAND_SMALL_A:], axis=0)
    groups = [v1[0] + v2_all]
    for a in range(1, _CAND_SMALL_A):
        groups.append(v1[a] + v2_lo)
    groups.append(v1_hi + v2[0])
    cand = jnp.concatenate(groups, axis=0)
    cand = jnp.where(valid > 0.5, cand, -jnp.inf)
    flat_b = jnp.broadcast_to(flat, cand.shape) if break_ties else None
    hits = jnp.zeros(cand.shape, F32)
    top = []
    for r in range(PEER_TOPK):
        m = jnp.max(cand, axis=0, keepdims=True)
        hit = cand == m
        if break_ties:
            pick = jnp.min(jnp.where(hit, flat_b, 2e9), axis=0, keepdims=True)
            hit = flat_b == pick
        hits = jnp.where(hit, 1.0, hits)
        cand = jnp.where(hit, -jnp.inf, cand)
        top.append(m)
    z = jnp.ones((1, t), F32)
    for r in range(1, PEER_TOPK):
        z = z + jnp.exp(top[r] - top[0])
    counts = [jnp.sum(hits[0:PEER_TOPK], axis=0, keepdims=True)]
    for a in range(1, _CAND_SMALL_A):
        base = PEER_TOPK + (a - 1) * SUBLANES
        counts.append(jnp.sum(hits[base:base + SUBLANES], axis=0, keepdims=True))
    base = PEER_TOPK + (_CAND_SMALL_A - 1) * SUBLANES
    for k in range(SUBLANES):
        counts.append(hits[base + k:base + k + 1])
    lim = jnp.zeros(s1.shape, F32)
    for a in range(PEER_TOPK):
        lim = jnp.where(rank1 == float(a), counts[a], lim)
    e1n = jnp.exp(s1 - v1[0]) / z
    e2 = jnp.exp(s2 - v2[0])
    count3 = jnp.sum(hits, axis=0, keepdims=True)
    k = float(PEER_TOPK)
    ok = jnp.where((count1 == k) & (count2 == k) & (count3 == k), 1.0, 0.0)
    return lim, e1n, rank2, e2, ok


def _retrieve_kernel(wq_ref, x_ref, k1_ref, k2_ref, flat_ref, valid_ref,
                     lim_ref, e1_ref, r2_ref, e2_ref, q_ref, *, tq):
    q_ref[...] = jnp.dot(wq_ref[...], x_ref[...], preferred_element_type=F32)
    flat, valid = flat_ref[...], valid_ref[...]

    def head(h, carry):
        for c in range(tq // SELECT_TOKENS):
            lanes = slice(c * SELECT_TOKENS, (c + 1) * SELECT_TOKENS)
            r0 = pl.multiple_of(h * 2 * HALF_DIM, 2 * HALF_DIM)
            q1 = q_ref[pl.ds(r0, HALF_DIM), lanes].astype(BF16)
            q2 = q_ref[pl.ds(r0 + HALF_DIM, HALF_DIM), lanes].astype(BF16)
            s1 = jnp.dot(k1_ref[...], q1, preferred_element_type=F32)
            s2 = jnp.dot(k2_ref[...], q2, preferred_element_type=F32)

            def emit(break_ties):
                lim, e1n, rank2, e2, ok = _select_experts(s1, s2, flat, valid, break_ties)
                lim_ref[h, :, lanes] = lim
                e1_ref[h, :, lanes] = e1n
                r2_ref[h, :, lanes] = rank2.astype(BF16)
                e2_ref[h, :, lanes] = e2.astype(BF16)
                return ok

            ok = emit(False)

            @pl.when(jnp.min(ok) < 0.5)
            def _():
                emit(True)
        return carry

    lax.fori_loop(0, PEER_HEADS, head, 0)


def _retrieve(x1t, wq_t, k1, k2, tq):
    d, t = x1t.shape
    tq = min(tq, t)
    flat, valid = _cand_constants()
    full = lambda a: pl.BlockSpec(a.shape, lambda i: (0,) * a.ndim)
    out = jax.ShapeDtypeStruct((PEER_HEADS, N_KEYS, t), F32)
    out_packed = jax.ShapeDtypeStruct((PEER_HEADS, N_KEYS, t), BF16)
    out_spec = pl.BlockSpec((PEER_HEADS, N_KEYS, tq), lambda i: (0, 0, i))
    return pl.pallas_call(
        functools.partial(_retrieve_kernel, tq=tq),
        grid=(t // tq,),
        in_specs=[full(wq_t), pl.BlockSpec((d, tq), lambda i: (0, i)), full(k1), full(k2), full(flat), full(valid)],
        out_specs=[out_spec] * 4,
        out_shape=[out, out, out_packed, out_packed],
        scratch_shapes=[pltpu.VMEM((wq_t.shape[0], tq), F32)],
        compiler_params=_params("parallel"),
        name="peer_retrieve",
    )(wq_t, x1t, k1, k2, flat, valid)


def _gelu_exact(x):
    return 0.5 * x * (1.0 + lax.erf(x * (1.0 / math.sqrt(2.0))))


def _experts_kernel(xb_ref, u_ref, vt_ref, lim_ref, e1_ref, r2_in_ref, e2_in_ref, o_ref,
                    h_ref, a_ref, rows_ref, gate_ref, r2_ref, e2_ref, *, tm, te):
    e = pl.program_id(1)

    @pl.when(e == 0)
    def _():
        o_ref[...] = jnp.zeros(o_ref.shape, F32)
        r2_ref[:, 0:tm] = r2_in_ref[...]
        e2_ref[:, LANES:LANES + tm] = e2_in_ref[...]

    n_chunks = te // EXPERT_CHUNK

    def rows_of(p):
        return slice(p * EXPERT_CHUNK, (p + 1) * EXPERT_CHUNK)

    def gates(first_key):
        keys = range(first_key, first_key + GATE_KEYS)
        blocks = range(0, N_KEYS, GATE_ROWS)
        for i in keys:
            for hd in range(PEER_HEADS):
                for q, ref in enumerate((lim_ref, e1_ref)):
                    row = jnp.broadcast_to(ref[i, hd:hd + 1, :], (GATE_ROWS, tm)).astype(BF16)
                    rows_ref[q, i, hd, :, q * LANES:q * LANES + tm] = row
        for c in range(tm // LANES):
            lanes = slice(c * LANES, (c + 1) * LANES)
            lanes1 = slice((c + 1) * LANES, (c + 2) * LANES)
            gate = {i: {jb: jnp.zeros((GATE_ROWS, LANES), BF16) for jb in blocks} for i in keys}
            for hd in range(PEER_HEADS):
                lim = {i: rows_ref[0, i, hd, :, lanes] for i in keys}
                e1 = {i: rows_ref[1, i, hd, :, lanes1] for i in keys}
                for jb in blocks:
                    r2 = r2_ref[hd * N_KEYS + jb:hd * N_KEYS + jb + GATE_ROWS, lanes]
                    e2 = e2_ref[hd * N_KEYS + jb:hd * N_KEYS + jb + GATE_ROWS, lanes1]
                    for i in keys:
                        gate[i][jb] = gate[i][jb] + jnp.where(r2 < lim[i], e2 * e1[i], jnp.zeros_like(e2))
            for i in keys:
                for jb in blocks:
                    gate_ref[i * N_KEYS + jb:i * N_KEYS + jb + GATE_ROWS, lanes] = gate[i][jb]

    for first_key in range(0, te // N_KEYS, GATE_KEYS):
        gates(first_key)
    h_ref[...] = jnp.dot(u_ref[...], xb_ref[...], preferred_element_type=F32)
    for p in range(n_chunks):
        a_ref[rows_of(p), :] = gate_ref[rows_of(p), :] * _gelu_exact(h_ref[rows_of(p), :]).astype(BF16)
        o_ref[...] += jnp.dot(vt_ref[:, rows_of(p)], a_ref[rows_of(p), :], preferred_element_type=F32)


def _experts(xbt, u, vt, sel, tm, te):
    d, t = xbt.shape
    n_exp = u.shape[0]
    tm, te = min(tm, t), min(te, n_exp)
    tok = pl.BlockSpec((d, tm), lambda i, j: (0, i))
    sel_spec = pl.BlockSpec((PEER_HEADS * N_KEYS, tm), lambda i, j: (0, i))
    key_spec = pl.BlockSpec((te // N_KEYS, PEER_HEADS, tm), lambda i, j: (j, 0, i))
    lim, e1n, rank2, e2 = sel
    sel = (lim.transpose(1, 0, 2), e1n.transpose(1, 0, 2), rank2.reshape(-1, t), e2.reshape(-1, t))
    return pl.pallas_call(
        functools.partial(_experts_kernel, tm=tm, te=te),
        grid=(t // tm, n_exp // te),
        in_specs=[tok, pl.BlockSpec((te, d), lambda i, j: (j, 0)), pl.BlockSpec((d, te), lambda i, j: (0, j)),
                  key_spec, key_spec, sel_spec, sel_spec],
        out_specs=tok,
        out_shape=jax.ShapeDtypeStruct((d, t), F32),
        scratch_shapes=[pltpu.VMEM((te, tm), F32), pltpu.VMEM((te, tm), BF16),
                        pltpu.VMEM((2, te // N_KEYS, PEER_HEADS, GATE_ROWS, tm + LANES), BF16),
                        pltpu.VMEM((te, tm), BF16),
                        pltpu.VMEM((PEER_HEADS * N_KEYS, tm + LANES), BF16),
                        pltpu.VMEM((PEER_HEADS * N_KEYS, tm + LANES), BF16)],
        compiler_params=_params("parallel", "arbitrary"),
        name="peer_experts",
    )(xbt, u, vt, *sel)


def _ln_t_kernel(x_ref, f_ref, g_ref, b_ref, o_ref, *, alpha):
    z = (alpha * x_ref[...] + f_ref[...]).T
    mu = jnp.mean(z, axis=1, keepdims=True)
    zc = z - mu
    var = jnp.mean(zc * zc, axis=1, keepdims=True)
    o_ref[...] = zc * lax.rsqrt(var + LN_EPS) * g_ref[...] + b_ref[...]


def _residual_ln_t(x1t, fft, g, b, alpha, tm):
    d, t = x1t.shape
    tm = min(tm, t)
    tok = pl.BlockSpec((d, tm), lambda i: (0, i))
    vec = pl.BlockSpec((1, d), lambda i: (0, 0))
    return pl.pallas_call(
        functools.partial(_ln_t_kernel, alpha=alpha),
        grid=(t // tm,),
        in_specs=[tok, tok, vec, vec],
        out_specs=pl.BlockSpec((tm, d), lambda i: (i, 0)),
        out_shape=jax.ShapeDtypeStruct((t, d), F32),
        compiler_params=_params("parallel"),
        name="residual_ln",
    )(x1t, fft, g, b)


def kernel(x, mem, positions, w_in, w_mem_kv, w_pool, pool_scale, attn_sinks, w_out, ln1_g, ln1_b,
           w_peer_q, sub_keys_1, sub_keys_2, expert_u, expert_v, ln2_g, ln2_b):
    bsz, seq, d = x.shape
    depth = w_in.shape[0]
    t = bsz * seq
    alpha = (2.0 * depth) ** 0.25
    for l in range(depth):
        x2 = x.reshape(t, d)
        h = _inproj(x2, w_in[l].astype(BF16), positions, 1024).reshape(bsz, seq, -1)
        mem2 = mem.reshape(-1, d).astype(BF16)
        kvm = _matmul(mem2, w_mem_kv[l].astype(BF16), BF16, 512, 512).reshape(bsz, mem.shape[1], -1)
        pool_o = _pool(h, w_pool[l].astype(BF16), pool_scale[l].reshape(1, -1), 512)
        swa_o = _swa(h, attn_sinks[l])
        mem_o = _mem_attention(h, kvm, 512)
        x1t, x1bt = _outproj_ln(pool_o.reshape(t, -1), swa_o.reshape(t, -1), mem_o.reshape(t, -1),
                          w_out[l].astype(BF16), x2, ln1_g[l].reshape(1, d), ln1_b[l].reshape(1, d), alpha, 512)
        sel = _retrieve(x1bt, w_peer_q[l].T.astype(BF16), sub_keys_1[l].astype(BF16),
                        sub_keys_2[l].astype(BF16), 512)
        fft = _experts(x1bt, expert_u[l].astype(BF16), expert_v[l].T.astype(BF16), sel, 512, 512)
        x = _residual_ln_t(x1t, fft, ln2_g[l].reshape(1, d), ln2_b[l].reshape(1, d), alpha, 512).reshape(bsz, seq, d)
    return x
```

```python
import functools
import math

import numpy as np
import jax
import jax.numpy as jnp
from jax import lax
from jax.experimental import pallas as pl
from jax.experimental.pallas import tpu as pltpu

F32 = jnp.float32
BF16 = jnp.bfloat16

LANES = 128
SUBLANES = 8
VMEM_LIMIT_BYTES = 56 * 1024 * 1024

POOL_WINDOWS = (2, 4, 8, 16)
POOL_GROUP = 128
POOL_HALO = 16
SWA_HEAD_DIM = 64
SWA_HEADS = 16
SWA_KV_HEADS = 4
SWA_BLOCK = 128
ROPE_THETA = 500000.0
ROPE_DIM = 16
MEM_HEADS = 4
MEM_HEAD_DIM = 128
PEER_HEADS = 8
N_KEYS = 128
PEER_TOPK = 16
HALF_DIM = 128
LN_EPS = 1e-5
NEG = -1e30

EXPERT_CHUNK = 512
GATE_KEYS = 4
GATE_ROWS = 16
SELECT_TOKENS = 512


def _params(*semantics):
    return pltpu.CompilerParams(dimension_semantics=semantics, vmem_limit_bytes=VMEM_LIMIT_BYTES)


def _matmul_kernel(a_ref, b_ref, o_ref):
    o_ref[...] = jnp.dot(a_ref[...], b_ref[...], preferred_element_type=F32).astype(o_ref.dtype)


def _matmul(a, b, out_dtype, tm, tn):
    m, k = a.shape
    n = b.shape[1]
    tm, tn = min(tm, m), min(tn, n)
    return pl.pallas_call(
        _matmul_kernel,
        grid=(m // tm, n // tn),
        in_specs=[pl.BlockSpec((tm, k), lambda i, j: (i, 0)), pl.BlockSpec((k, tn), lambda i, j: (0, j))],
        out_specs=pl.BlockSpec((tm, tn), lambda i, j: (i, j)),
        out_shape=jax.ShapeDtypeStruct((m, n), out_dtype),
        compiler_params=_params("parallel", "arbitrary"),
        name="matmul",
    )(a, b)


IN_BLOCK = 512
_Q_BLOCKS = (1, 2)
_KV_BLOCK = 3


def _rope(x, c, sa, sb):
    half = ROPE_DIM // 2
    return x * c + pltpu.roll(x, LANES - half, 1) * sa + pltpu.roll(x, half, 1) * sb


def _inproj_kernel(x_ref, w_ref, pos_ref, freq_ref, sa_ref, sb_ref, o_ref, xb_ref, cos_ref, sina_ref, sinb_ref):
    j = pl.program_id(1)

    @pl.when(j == 0)
    def _():
        xb_ref[...] = x_ref[...].astype(BF16)
        ang = pos_ref[...].astype(F32) * freq_ref[...]
        s = jnp.sin(ang)
        cos_ref[...] = jnp.cos(ang)
        sina_ref[...] = s * sa_ref[...]
        sinb_ref[...] = s * sb_ref[...]

    y = jnp.dot(xb_ref[...], w_ref[...], preferred_element_type=F32)
    tiles = [slice(k * LANES, (k + 1) * LANES) for k in range(IN_BLOCK // LANES)]

    def rope(cols):
        return _rope(y[:, cols], cos_ref[...], sina_ref[...], sinb_ref[...])

    @pl.when((j == _Q_BLOCKS[0]) | (j == _Q_BLOCKS[1]))
    def _():
        for cols in tiles:
            o_ref[:, cols] = (rope(cols) * SWA_HEAD_DIM ** -0.5).astype(o_ref.dtype)

    @pl.when(j == _KV_BLOCK)
    def _():
        for cols in tiles[:2]:
            o_ref[:, cols] = rope(cols).astype(o_ref.dtype)
        for cols in tiles[2:]:
            o_ref[:, cols] = y[:, cols].astype(o_ref.dtype)

    @pl.when((j != _Q_BLOCKS[0]) & (j != _Q_BLOCKS[1]) & (j != _KV_BLOCK))
    def _():
        o_ref[...] = y.astype(o_ref.dtype)


def _rope_constants():
    lane = np.arange(LANES)
    d = lane % SWA_HEAD_DIM
    half = ROPE_DIM // 2
    inv_freq = np.float32(ROPE_THETA) ** (-np.arange(0, ROPE_DIM, 2, dtype=np.float32) / np.float32(ROPE_DIM))
    freq = np.where(d < ROPE_DIM, inv_freq[d % half], 0.0).astype(np.float32)
    sa = np.where(d < half, -1.0, 0.0).astype(np.float32)
    sb = np.where((d >= half) & (d < ROPE_DIM), 1.0, 0.0).astype(np.float32)
    return [jnp.asarray(a.reshape(1, LANES)) for a in (freq, sa, sb)]


def _inproj(x2, w, positions, tm):
    t, d = x2.shape
    n = w.shape[1]
    tm = min(tm, t)
    freq, sa, sb = _rope_constants()
    const = pl.BlockSpec((1, LANES), lambda i, j: (0, 0))
    return pl.pallas_call(
        _inproj_kernel,
        grid=(t // tm, n // IN_BLOCK),
        in_specs=[pl.BlockSpec((tm, d), lambda i, j: (i, 0)), pl.BlockSpec((d, IN_BLOCK), lambda i, j: (0, j)),
                  pl.BlockSpec((tm, 1), lambda i, j: (i, 0)), const, const, const],
        out_specs=pl.BlockSpec((tm, IN_BLOCK), lambda i, j: (i, j)),
        out_shape=jax.ShapeDtypeStruct((t, n), BF16),
        scratch_shapes=[pltpu.VMEM((tm, d), BF16)] + [pltpu.VMEM((tm, LANES), F32)] * 3,
        compiler_params=_params("parallel", "arbitrary"),
        name="inproj_rope",
    )(x2, w, positions.reshape(t, 1), freq, sa, sb)


def _pool_kernel(v_ref, w_ref, scale_ref, o_ref, ext_ref, *, ts):
    s = pl.program_id(1)

    @pl.when(s == 0)
    def _():
        ext_ref[0:POOL_HALO, :] = jnp.zeros((POOL_HALO, ext_ref.shape[1]), F32)

    ext_ref[POOL_HALO:POOL_HALO + ts, :] = v_ref[0].astype(F32)
    pos = s * ts + lax.broadcasted_iota(jnp.int32, (ts, 1), 0)
    for g, w in enumerate(POOL_WINDOWS):
        cols = slice(g * POOL_GROUP, (g + 1) * POOL_GROUP)
        acc = ext_ref[POOL_HALO:POOL_HALO + ts, cols]
        for k in range(1, w):
            acc = acc + ext_ref[POOL_HALO - k:POOL_HALO - k + ts, cols]
        count = jnp.minimum(pos + 1, w).astype(F32)
        pooled = acc / count - ext_ref[POOL_HALO:POOL_HALO + ts, cols]
        y = jnp.dot(pooled.astype(BF16), w_ref[g], preferred_element_type=F32)
        o_ref[0, :, cols] = (y * scale_ref[:, cols]).astype(o_ref.dtype)
    ext_ref[0:POOL_HALO, :] = ext_ref[ts:ts + POOL_HALO, :]


def _pool(h, w_pool, pool_scale, ts):
    b, s, _ = h.shape
    width = POOL_GROUP * len(POOL_WINDOWS)
    ts = min(ts, s)
    return pl.pallas_call(
        functools.partial(_pool_kernel, ts=ts),
        grid=(b, s // ts),
        in_specs=[
            pl.BlockSpec((1, ts, width), lambda i, j: (i, j, 0)),
            pl.BlockSpec(w_pool.shape, lambda i, j: (0, 0, 0)),
            pl.BlockSpec((1, width), lambda i, j: (0, 0)),
        ],
        out_specs=pl.BlockSpec((1, ts, width), lambda i, j: (i, j, 0)),
        out_shape=jax.ShapeDtypeStruct((b, s, width), BF16),
        scratch_shapes=[pltpu.VMEM((ts + POOL_HALO, width), F32)],
        compiler_params=_params("arbitrary", "arbitrary"),
        name="pool",
    )(h, w_pool, pool_scale)


def _swa_kernel(sink_ref, q0_ref, q1_ref, kv_ref, kvp_ref, o_ref):
    n = pl.program_id(1)
    kvw = SWA_KV_HEADS * SWA_HEAD_DIM
    q = jnp.concatenate([q0_ref[0], q1_ref[0]], axis=1)
    k = jnp.concatenate([kvp_ref[0, :, 0:kvw], kv_ref[0, :, 0:kvw]], axis=0)
    v = jnp.concatenate([kvp_ref[0, :, kvw:2 * kvw], kv_ref[0, :, kvw:2 * kvw]], axis=0)
    row = lax.broadcasted_iota(jnp.int32, (SWA_BLOCK, 2 * SWA_BLOCK), 0)
    col = lax.broadcasted_iota(jnp.int32, (SWA_BLOCK, 2 * SWA_BLOCK), 1)
    rel = row + SWA_BLOCK - col
    valid = (rel >= 0) & (rel < SWA_BLOCK) & ((col >= SWA_BLOCK) | (n > 0))
    group = SWA_HEADS // SWA_KV_HEADS
    outs = []
    for hq in range(SWA_HEADS):
        kv = hq // group
        qh = q[:, hq * SWA_HEAD_DIM:(hq + 1) * SWA_HEAD_DIM]
        kh = k[:, kv * SWA_HEAD_DIM:(kv + 1) * SWA_HEAD_DIM]
        vh = v[:, kv * SWA_HEAD_DIM:(kv + 1) * SWA_HEAD_DIM]
        sc = lax.dot_general(qh, kh, (((1,), (1,)), ((), ())), preferred_element_type=F32)
        sc = jnp.where(valid, sc, NEG)
        sink = sink_ref[hq]
        m = jnp.maximum(jnp.max(sc, axis=1, keepdims=True), sink)
        p = jnp.exp(sc - m)
        denom = jnp.sum(p, axis=1, keepdims=True) + jnp.exp(sink - m)
        o = jnp.dot(p.astype(BF16), vh, preferred_element_type=F32)
        outs.append(o / denom)
    o_ref[0] = jnp.concatenate(outs, axis=1).astype(o_ref.dtype)


def _swa(h, sinks):
    b, s, _ = h.shape
    nb = s // SWA_BLOCK
    blk = lambda c: pl.BlockSpec((1, SWA_BLOCK, IN_BLOCK), lambda i, j: (i, j, c))
    blk_prev = lambda c: pl.BlockSpec((1, SWA_BLOCK, IN_BLOCK), lambda i, j: (i, jnp.maximum(j - 1, 0), c))
    return pl.pallas_call(
        _swa_kernel,
        grid=(b, nb),
        in_specs=[pl.BlockSpec(memory_space=pltpu.SMEM),
                  blk(_Q_BLOCKS[0]), blk(_Q_BLOCKS[1]), blk(_KV_BLOCK), blk_prev(_KV_BLOCK)],
        out_specs=pl.BlockSpec((1, SWA_BLOCK, SWA_HEADS * SWA_HEAD_DIM), lambda i, j: (i, j, 0)),
        out_shape=jax.ShapeDtypeStruct((b, s, SWA_HEADS * SWA_HEAD_DIM), BF16),
        compiler_params=_params("parallel", "arbitrary"),
        name="swa",
    )(sinks, h, h, h, h)


def _mem_kernel(q_ref, kv_ref, o_ref):
    scale = MEM_HEAD_DIM ** -0.5
    width = MEM_HEADS * MEM_HEAD_DIM
    for hm in range(MEM_HEADS):
        cols = slice(hm * MEM_HEAD_DIM, (hm + 1) * MEM_HEAD_DIM)
        km = kv_ref[0, :, cols]
        vm = kv_ref[0, :, width + hm * MEM_HEAD_DIM:width + (hm + 1) * MEM_HEAD_DIM]
        sc = lax.dot_general(q_ref[0, :, cols], km, (((1,), (1,)), ((), ())), preferred_element_type=F32) * scale
        m = jnp.max(sc, axis=1, keepdims=True)
        p = jnp.exp(sc - m)
        denom = jnp.sum(p, axis=1, keepdims=True)
        o = jnp.dot(p.astype(BF16), vm, preferred_element_type=F32)
        o_ref[0, :, cols] = (o / denom).astype(o_ref.dtype)


def _mem_attention(h, kvm, tq):
    b, s, _ = h.shape
    width = MEM_HEADS * MEM_HEAD_DIM
    tq = min(tq, s)
    return pl.pallas_call(
        _mem_kernel,
        grid=(b, s // tq),
        in_specs=[
            pl.BlockSpec((1, tq, width), lambda i, j: (i, j, 4)),
            pl.BlockSpec((1,) + kvm.shape[1:], lambda i, j: (i, 0, 0)),
        ],
        out_specs=pl.BlockSpec((1, tq, width), lambda i, j: (i, j, 0)),
        out_shape=jax.ShapeDtypeStruct((b, s, width), BF16),
        compiler_params=_params("parallel", "arbitrary"),
        name="mem_attention",
    )(h, kvm)


def _outproj_kernel(pool_ref, swa_ref, mem_ref, wp_ref, ws_ref, wm_ref, x_ref, g_ref, b_ref, o_ref, ob_ref,
                    *, alpha):
    mix = jnp.dot(pool_ref[...], wp_ref[...], preferred_element_type=F32)
    mix += jnp.dot(swa_ref[...], ws_ref[...], preferred_element_type=F32)
    mix += jnp.dot(mem_ref[...], wm_ref[...], preferred_element_type=F32)
    z = alpha * x_ref[...] + mix
    mu = jnp.mean(z, axis=1, keepdims=True)
    zc = z - mu
    var = jnp.mean(zc * zc, axis=1, keepdims=True)
    y = zc * lax.rsqrt(var + LN_EPS) * g_ref[...] + b_ref[...]
    yt = y.T
    o_ref[...] = yt
    ob_ref[...] = yt.astype(BF16)


def _outproj_ln(pool_o, swa_o, mem_o, w_out, x2, g, b, alpha, tm):
    t, d = x2.shape
    tm = min(tm, t)
    wp, ws, wm = pool_o.shape[1], swa_o.shape[1], mem_o.shape[1]
    w_p, w_s, w_m = w_out[:wp], w_out[wp:wp + ws], w_out[wp + ws:]
    row = lambda w: pl.BlockSpec((tm, w), lambda i: (i, 0))
    full = lambda a: pl.BlockSpec(a.shape, lambda i: (0, 0))
    return pl.pallas_call(
        functools.partial(_outproj_kernel, alpha=alpha),
        grid=(t // tm,),
        in_specs=[row(wp), row(ws), row(wm), full(w_p), full(w_s), full(w_m), row(d),
                  pl.BlockSpec((1, d), lambda i: (0, 0)), pl.BlockSpec((1, d), lambda i: (0, 0))],
        out_specs=[pl.BlockSpec((d, tm), lambda i: (0, i))] * 2,
        out_shape=[jax.ShapeDtypeStruct((d, t), F32), jax.ShapeDtypeStruct((d, t), BF16)],
        compiler_params=_params("parallel"),
        name="outproj_ln",
    )(pool_o, swa_o, mem_o, w_p, w_s, w_m, x2, g, b)


def _top16_rows(s, break_ties):
    n = s.shape[0]
    iota = lax.broadcasted_iota(jnp.int32, s.shape, 0).astype(F32) if break_ties else None
    rank = jnp.full(s.shape, float(PEER_TOPK), F32)
    vals = []
    for r in range(PEER_TOPK):
        m = jnp.max(s, axis=0, keepdims=True)
        hit = s == m
        if break_ties:
            idx = jnp.min(jnp.where(hit, iota, float(n)), axis=0, keepdims=True)
            hit = iota == idx
        rank = jnp.where(hit, float(r), rank)
        s = jnp.where(hit, -jnp.inf, s)
        vals.append(m)
    count = jnp.sum(jnp.where(rank < float(PEER_TOPK), 1.0, 0.0), axis=0, keepdims=True)
    return rank, vals, count


_CAND_SMALL_A = PEER_TOPK // 2
_CAND_ROWS = PEER_TOPK + (_CAND_SMALL_A - 1) * SUBLANES + SUBLANES


def _cand_constants():
    flat = np.full((_CAND_ROWS, 1), 1e9, np.float32)
    valid = np.zeros((_CAND_ROWS, 1), np.float32)
    for b in range(PEER_TOPK):
        flat[b, 0], valid[b, 0] = b, 1.0
    for a in range(1, _CAND_SMALL_A):
        base = PEER_TOPK + (a - 1) * SUBLANES
        for b in range(PEER_TOPK // (a + 1)):
            flat[base + b, 0], valid[base + b, 0] = a * PEER_TOPK + b, 1.0
    base = PEER_TOPK + (_CAND_SMALL_A - 1) * SUBLANES
    for k in range(SUBLANES):
        flat[base + k, 0], valid[base + k, 0] = (_CAND_SMALL_A + k) * PEER_TOPK, 1.0
    return jnp.asarray(flat), jnp.asarray(valid)


def _select_experts(s1, s2, flat, valid, break_ties):
    t = s1.shape[1]
    rank1, v1, count1 = _top16_rows(s1, break_ties)
    rank2, v2, count2 = _top16_rows(s2, break_ties)
    v2_lo = jnp.concatenate(v2[:SUBLANES], axis=0)
    v2_all = jnp.concatenate(v2, axis=0)
    v1_hi = jnp.concatenate(v1[_CAND_SMALL_A:], axis=0)
    groups = [v1[0] + v2_all]
    for a in range(1, _CAND_SMALL_A):
        groups.append(v1[a] + v2_lo)
    groups.append(v1_hi + v2[0])
    cand = jnp.concatenate(groups, axis=0)
    cand = jnp.where(valid > 0.5, cand, -jnp.inf)
    flat_b = jnp.broadcast_to(flat, cand.shape) if break_ties else None
    hits = jnp.zeros(cand.shape, F32)
    top = []
    for r in range(PEER_TOPK):
        m = jnp.max(cand, axis=0, keepdims=True)
        hit = cand == m
        if break_ties:
            pick = jnp.min(jnp.where(hit, flat_b, 2e9), axis=0, keepdims=True)
            hit = flat_b == pick
        hits = jnp.where(hit, 1.0, hits)
        cand = jnp.where(hit, -jnp.inf, cand)
        top.append(m)
    z = jnp.ones((1, t), F32)
    for r in range(1, PEER_TOPK):
        z = z + jnp.exp(top[r] - top[0])
    counts = [jnp.sum(hits[0:PEER_TOPK], axis=0, keepdims=True)]
    for a in range(1, _CAND_SMALL_A):
        base = PEER_TOPK + (a - 1) * SUBLANES
        counts.append(jnp.sum(hits[base:base + SUBLANES], axis=0, keepdims=True))
    base = PEER_TOPK + (_CAND_SMALL_A - 1) * SUBLANES
    for k in range(SUBLANES):
        counts.append(hits[base + k:base + k + 1])
    lim = jnp.zeros(s1.shape, F32)
    for a in range(PEER_TOPK):
        lim = jnp.where(rank1 == float(a), counts[a], lim)
    e1n = jnp.exp(s1 - v1[0]) / z
    e2 = jnp.exp(s2 - v2[0])
    count3 = jnp.sum(hits, axis=0, keepdims=True)
    k = float(PEER_TOPK)
    ok = jnp.where((count1 == k) & (count2 == k) & (count3 == k), 1.0, 0.0)
    return lim, e1n, rank2, e2, ok


def _retrieve_kernel(wq_ref, x_ref, k1_ref, k2_ref, flat_ref, valid_ref,
                     lim_ref, e1_ref, r2_ref, e2_ref, q_ref, *, tq):
    q_ref[...] = jnp.dot(wq_ref[...], x_ref[...], preferred_element_type=F32)
    flat, valid = flat_ref[...], valid_ref[...]

    def head(h, carry):
        for c in range(tq // SELECT_TOKENS):
            lanes = slice(c * SELECT_TOKENS, (c + 1) * SELECT_TOKENS)
            r0 = pl.multiple_of(h * 2 * HALF_DIM, 2 * HALF_DIM)
            q1 = q_ref[pl.ds(r0, HALF_DIM), lanes].astype(BF16)
            q2 = q_ref[pl.ds(r0 + HALF_DIM, HALF_DIM), lanes].astype(BF16)
            s1 = jnp.dot(k1_ref[...], q1, preferred_element_type=F32)
            s2 = jnp.dot(k2_ref[...], q2, preferred_element_type=F32)

            def emit(break_ties):
                lim, e1n, rank2, e2, ok = _select_experts(s1, s2, flat, valid, break_ties)
                lim_ref[h, :, lanes] = lim
                e1_ref[h, :, lanes] = e1n
                r2_ref[h, :, lanes] = rank2.astype(BF16)
                e2_ref[h, :, lanes] = e2.astype(BF16)
                return ok

            ok = emit(False)

            @pl.when(jnp.min(ok) < 0.5)
            def _():
                emit(True)
        return carry

    lax.fori_loop(0, PEER_HEADS, head, 0)


def _retrieve(x1t, wq_t, k1, k2, tq):
    d, t = x1t.shape
    tq = min(tq, t)
    flat, valid = _cand_constants()
    full = lambda a: pl.BlockSpec(a.shape, lambda i: (0,) * a.ndim)
    out = jax.ShapeDtypeStruct((PEER_HEADS, N_KEYS, t), F32)
    out_packed = jax.ShapeDtypeStruct((PEER_HEADS, N_KEYS, t), BF16)
    out_spec = pl.BlockSpec((PEER_HEADS, N_KEYS, tq), lambda i: (0, 0, i))
    return pl.pallas_call(
        functools.partial(_retrieve_kernel, tq=tq),
        grid=(t // tq,),
        in_specs=[full(wq_t), pl.BlockSpec((d, tq), lambda i: (0, i)), full(k1), full(k2), full(flat), full(valid)],
        out_specs=[out_spec] * 4,
        out_shape=[out, out, out_packed, out_packed],
        scratch_shapes=[pltpu.VMEM((wq_t.shape[0], tq), F32)],
        compiler_params=_params("parallel"),
        name="peer_retrieve",
    )(wq_t, x1t, k1, k2, flat, valid)


def _gelu_exact(x):
    return 0.5 * x * (1.0 + lax.erf(x * (1.0 / math.sqrt(2.0))))


def _experts_kernel(xb_ref, u_ref, vt_ref, lim_ref, e1_ref, r2_in_ref, e2_in_ref, o_ref,
                    h_ref, a_ref, rows_ref, gate_ref, r2_ref, e2_ref, *, tm, te):
    e = pl.program_id(1)

    @pl.when(e == 0)
    def _():
        o_ref[...] = jnp.zeros(o_ref.shape, F32)
        r2_ref[:, 0:tm] = r2_in_ref[...]
        e2_ref[:, LANES:LANES + tm] = e2_in_ref[...]

    n_chunks = te // EXPERT_CHUNK

    def rows_of(p):
        return slice(p * EXPERT_CHUNK, (p + 1) * EXPERT_CHUNK)

    def gates(first_key):
        keys = range(first_key, first_key + GATE_KEYS)
        blocks = range(0, N_KEYS, GATE_ROWS)
        for i in keys:
            for hd in range(PEER_HEADS):
                for q, ref in enumerate((lim_ref, e1_ref)):
                    row = jnp.broadcast_to(ref[i, hd:hd + 1, :], (GATE_ROWS, tm)).astype(BF16)
                    rows_ref[q, i, hd, :, q * LANES:q * LANES + tm] = row
        for c in range(tm // LANES):
            lanes = slice(c * LANES, (c + 1) * LANES)
            lanes1 = slice((c + 1) * LANES, (c + 2) * LANES)
            gate = {i: {jb: jnp.zeros((GATE_ROWS, LANES), BF16) for jb in blocks} for i in keys}
            for hd in range(PEER_HEADS):
                lim = {i: rows_ref[0, i, hd, :, lanes] for i in keys}
                e1 = {i: rows_ref[1, i, hd, :, lanes1] for i in keys}
                for jb in blocks:
                    r2 = r2_ref[hd * N_KEYS + jb:hd * N_KEYS + jb + GATE_ROWS, lanes]
                    e2 = e2_ref[hd * N_KEYS + jb:hd * N_KEYS + jb + GATE_ROWS, lanes1]
                    for i in keys:
                        gate[i][jb] = gate[i][jb] + jnp.where(r2 < lim[i], e2 * e1[i], jnp.zeros_like(e2))
            for i in keys:
                for jb in blocks:
                    gate_ref[i * N_KEYS + jb:i * N_KEYS + jb + GATE_ROWS, lanes] = gate[i][jb]

    h_ref[...] = jnp.dot(u_ref[...], xb_ref[...], preferred_element_type=F32)

    @pl.when(e >= 0)
    def _():
        for first_key in range(0, te // N_KEYS, GATE_KEYS):
            gates(first_key)
        a_ref[...] = gate_ref[...] * _gelu_exact(h_ref[...]).astype(BF16)

    for p in range(n_chunks):
        o_ref[...] += jnp.dot(vt_ref[:, rows_of(p)], a_ref[rows_of(p), :], preferred_element_type=F32)


def _experts(xbt, u, vt, sel, tm, te):
    d, t = xbt.shape
    n_exp = u.shape[0]
    tm, te = min(tm, t), min(te, n_exp)
    tok = pl.BlockSpec((d, tm), lambda i, j: (0, i))
    sel_spec = pl.BlockSpec((PEER_HEADS * N_KEYS, tm), lambda i, j: (0, i))
    key_spec = pl.BlockSpec((te // N_KEYS, PEER_HEADS, tm), lambda i, j: (j, 0, i))
    lim, e1n, rank2, e2 = sel
    sel = (lim.transpose(1, 0, 2), e1n.transpose(1, 0, 2), rank2.reshape(-1, t), e2.reshape(-1, t))
    return pl.pallas_call(
        functools.partial(_experts_kernel, tm=tm, te=te),
        grid=(t // tm, n_exp // te),
        in_specs=[tok, pl.BlockSpec((te, d), lambda i, j: (j, 0)), pl.BlockSpec((d, te), lambda i, j: (0, j)),
                  key_spec, key_spec, sel_spec, sel_spec],
        out_specs=tok,
        out_shape=jax.ShapeDtypeStruct((d, t), F32),
        scratch_shapes=[pltpu.VMEM((te, tm), F32), pltpu.VMEM((te, tm), BF16),
                        pltpu.VMEM((2, te // N_KEYS, PEER_HEADS, GATE_ROWS, tm + LANES), BF16),
                        pltpu.VMEM((te, tm), BF16),
                        pltpu.VMEM((PEER_HEADS * N_KEYS, tm + LANES), BF16),
                        pltpu.VMEM((PEER_HEADS * N_KEYS, tm + LANES), BF16)],
        compiler_params=_params("parallel", "arbitrary"),
        name="peer_experts",
    )(xbt, u, vt, *sel)


def _ln_t_kernel(x_ref, f_ref, g_ref, b_ref, o_ref, *, alpha):
    z = (alpha * x_ref[...] + f_ref[...]).T
    mu = jnp.mean(z, axis=1, keepdims=True)
    zc = z - mu
    var = jnp.mean(zc * zc, axis=1, keepdims=True)
    o_ref[...] = zc * lax.rsqrt(var + LN_EPS) * g_ref[...] + b_ref[...]


def _residual_ln_t(x1t, fft, g, b, alpha, tm):
    d, t = x1t.shape
    tm = min(tm, t)
    tok = pl.BlockSpec((d, tm), lambda i: (0, i))
    vec = pl.BlockSpec((1, d), lambda i: (0, 0))
    return pl.pallas_call(
        functools.partial(_ln_t_kernel, alpha=alpha),
        grid=(t // tm,),
        in_specs=[tok, tok, vec, vec],
        out_specs=pl.BlockSpec((tm, d), lambda i: (i, 0)),
        out_shape=jax.ShapeDtypeStruct((t, d), F32),
        compiler_params=_params("parallel"),
        name="residual_ln",
    )(x1t, fft, g, b)


def kernel(x, mem, positions, w_in, w_mem_kv, w_pool, pool_scale, attn_sinks, w_out, ln1_g, ln1_b,
           w_peer_q, sub_keys_1, sub_keys_2, expert_u, expert_v, ln2_g, ln2_b):
    bsz, seq, d = x.shape
    depth = w_in.shape[0]
    t = bsz * seq
    alpha = (2.0 * depth) ** 0.25
    for l in range(depth):
        x2 = x.reshape(t, d)
        h = _inproj(x2, w_in[l].astype(BF16), positions, 1024).reshape(bsz, seq, -1)
        mem2 = mem.reshape(-1, d).astype(BF16)
        kvm = _matmul(mem2, w_mem_kv[l].astype(BF16), BF16, 512, 512).reshape(bsz, mem.shape[1], -1)
        pool_o = _pool(h, w_pool[l].astype(BF16), pool_scale[l].reshape(1, -1), 512)
        swa_o = _swa(h, attn_sinks[l])
        mem_o = _mem_attention(h, kvm, 512)
        x1t, x1bt = _outproj_ln(pool_o.reshape(t, -1), swa_o.reshape(t, -1), mem_o.reshape(t, -1),
                          w_out[l].astype(BF16), x2, ln1_g[l].reshape(1, d), ln1_b[l].reshape(1, d), alpha, 512)
        sel = _retrieve(x1bt, w_peer_q[l].T.astype(BF16), sub_keys_1[l].astype(BF16),
                        sub_keys_2[l].astype(BF16), 512)
        fft = _experts(x1bt, expert_u[l].astype(BF16), expert_v[l].T.astype(BF16), sel, 1024, 512)
        x = _residual_ln_t(x1t, fft, ln2_g[l].reshape(1, d), ln2_b[l].reshape(1, d), alpha, 512).reshape(bsz, seq, d)
    return x
```

```python
import functools
import math

import numpy as np
import jax
import jax.numpy as jnp
from jax import lax
from jax.experimental import pallas as pl
from jax.experimental.pallas import tpu as pltpu

F32 = jnp.float32
BF16 = jnp.bfloat16

LANES = 128
SUBLANES = 8
VMEM_LIMIT_BYTES = 56 * 1024 * 1024

POOL_WINDOWS = (2, 4, 8, 16)
POOL_GROUP = 128
POOL_HALO = 16
SWA_HEAD_DIM = 64
SWA_HEADS = 16
SWA_KV_HEADS = 4
SWA_BLOCK = 128
ROPE_THETA = 500000.0
ROPE_DIM = 16
MEM_HEADS = 4
MEM_HEAD_DIM = 128
PEER_HEADS = 8
N_KEYS = 128
PEER_TOPK = 16
HALF_DIM = 128
LN_EPS = 1e-5
NEG = -1e30

EXPERT_CHUNK = 512
GATE_KEYS = 4
GATE_ROWS = 16
SELECT_TOKENS = 512


def _params(*semantics):
    return pltpu.CompilerParams(dimension_semantics=semantics, vmem_limit_bytes=VMEM_LIMIT_BYTES)


def _matmul_kernel(a_ref, b_ref, o_ref):
    o_ref[...] = jnp.dot(a_ref[...], b_ref[...], preferred_element_type=F32).astype(o_ref.dtype)


def _matmul(a, b, out_dtype, tm, tn):
    m, k = a.shape
    n = b.shape[1]
    tm, tn = min(tm, m), min(tn, n)
    return pl.pallas_call(
        _matmul_kernel,
        grid=(m // tm, n // tn),
        in_specs=[pl.BlockSpec((tm, k), lambda i, j: (i, 0)), pl.BlockSpec((k, tn), lambda i, j: (0, j))],
        out_specs=pl.BlockSpec((tm, tn), lambda i, j: (i, j)),
        out_shape=jax.ShapeDtypeStruct((m, n), out_dtype),
        compiler_params=_params("parallel", "arbitrary"),
        name="matmul",
    )(a, b)


IN_BLOCK = 512
_Q_BLOCKS = (1, 2)
_KV_BLOCK = 3


def _rope(x, c, sa, sb):
    half = ROPE_DIM // 2
    return x * c + pltpu.roll(x, LANES - half, 1) * sa + pltpu.roll(x, half, 1) * sb


def _inproj_kernel(x_ref, w_ref, pos_ref, freq_ref, sa_ref, sb_ref, o_ref):
    xb = x_ref[...].astype(BF16)
    ang = pos_ref[...].astype(F32) * freq_ref[...]
    s = jnp.sin(ang)
    c, sa, sb = jnp.cos(ang), s * sa_ref[...], s * sb_ref[...]
    for j in range(w_ref.shape[1] // IN_BLOCK):
        y = jnp.dot(xb, w_ref[:, j * IN_BLOCK:(j + 1) * IN_BLOCK], preferred_element_type=F32)
        for k in range(IN_BLOCK // LANES):
            piece = y[:, k * LANES:(k + 1) * LANES]
            if j in _Q_BLOCKS:
                piece = _rope(piece, c, sa, sb) * SWA_HEAD_DIM ** -0.5
            elif j == _KV_BLOCK and k < IN_BLOCK // LANES // 2:
                piece = _rope(piece, c, sa, sb)
            o_ref[:, j * IN_BLOCK + k * LANES:j * IN_BLOCK + (k + 1) * LANES] = piece.astype(o_ref.dtype)


def _rope_constants():
    lane = np.arange(LANES)
    d = lane % SWA_HEAD_DIM
    half = ROPE_DIM // 2
    inv_freq = np.float32(ROPE_THETA) ** (-np.arange(0, ROPE_DIM, 2, dtype=np.float32) / np.float32(ROPE_DIM))
    freq = np.where(d < ROPE_DIM, inv_freq[d % half], 0.0).astype(np.float32)
    sa = np.where(d < half, -1.0, 0.0).astype(np.float32)
    sb = np.where((d >= half) & (d < ROPE_DIM), 1.0, 0.0).astype(np.float32)
    return [jnp.asarray(a.reshape(1, LANES)) for a in (freq, sa, sb)]


def _inproj(x2, w, positions, tm):
    t, d = x2.shape
    n = w.shape[1]
    tm = min(tm, t)
    freq, sa, sb = _rope_constants()
    const = pl.BlockSpec((1, LANES), lambda i: (0, 0))
    return pl.pallas_call(
        _inproj_kernel,
        grid=(t // tm,),
        in_specs=[pl.BlockSpec((tm, d), lambda i: (i, 0)), pl.BlockSpec((d, n), lambda i: (0, 0)),
                  pl.BlockSpec((tm, 1), lambda i: (i, 0)), const, const, const],
        out_specs=pl.BlockSpec((tm, n), lambda i: (i, 0)),
        out_shape=jax.ShapeDtypeStruct((t, n), BF16),
        compiler_params=_params("parallel"),
        name="inproj_rope",
    )(x2, w, positions.reshape(t, 1), freq, sa, sb)


def _pool_kernel(v_ref, w_ref, scale_ref, o_ref, ext_ref, *, ts):
    s = pl.program_id(1)

    @pl.when(s == 0)
    def _():
        ext_ref[0:POOL_HALO, :] = jnp.zeros((POOL_HALO, ext_ref.shape[1]), F32)

    ext_ref[POOL_HALO:POOL_HALO + ts, :] = v_ref[0].astype(F32)
    pos = s * ts + lax.broadcasted_iota(jnp.int32, (ts, 1), 0)
    for g, w in enumerate(POOL_WINDOWS):
        cols = slice(g * POOL_GROUP, (g + 1) * POOL_GROUP)
        acc = ext_ref[POOL_HALO:POOL_HALO + ts, cols]
        for k in range(1, w):
            acc = acc + ext_ref[POOL_HALO - k:POOL_HALO - k + ts, cols]
        count = jnp.minimum(pos + 1, w).astype(F32)
        pooled = acc / count - ext_ref[POOL_HALO:POOL_HALO + ts, cols]
        y = jnp.dot(pooled.astype(BF16), w_ref[g], preferred_element_type=F32)
        o_ref[0, :, cols] = (y * scale_ref[:, cols]).astype(o_ref.dtype)
    ext_ref[0:POOL_HALO, :] = ext_ref[ts:ts + POOL_HALO, :]


def _pool(h, w_pool, pool_scale, ts):
    b, s, _ = h.shape
    width = POOL_GROUP * len(POOL_WINDOWS)
    ts = min(ts, s)
    return pl.pallas_call(
        functools.partial(_pool_kernel, ts=ts),
        grid=(b, s // ts),
        in_specs=[
            pl.BlockSpec((1, ts, width), lambda i, j: (i, j, 0)),
            pl.BlockSpec(w_pool.shape, lambda i, j: (0, 0, 0)),
            pl.BlockSpec((1, width), lambda i, j: (0, 0)),
        ],
        out_specs=pl.BlockSpec((1, ts, width), lambda i, j: (i, j, 0)),
        out_shape=jax.ShapeDtypeStruct((b, s, width), BF16),
        scratch_shapes=[pltpu.VMEM((ts + POOL_HALO, width), F32)],
        compiler_params=_params("arbitrary", "arbitrary"),
        name="pool",
    )(h, w_pool, pool_scale)


def _swa_kernel(sink_ref, q0_ref, q1_ref, kv_ref, kvp_ref, o_ref):
    n = pl.program_id(1)
    kvw = SWA_KV_HEADS * SWA_HEAD_DIM
    q = jnp.concatenate([q0_ref[0], q1_ref[0]], axis=1)
    k = jnp.concatenate([kvp_ref[0, :, 0:kvw], kv_ref[0, :, 0:kvw]], axis=0)
    v = jnp.concatenate([kvp_ref[0, :, kvw:2 * kvw], kv_ref[0, :, kvw:2 * kvw]], axis=0)
    row = lax.broadcasted_iota(jnp.int32, (SWA_BLOCK, 2 * SWA_BLOCK), 0)
    col = lax.broadcasted_iota(jnp.int32, (SWA_BLOCK, 2 * SWA_BLOCK), 1)
    rel = row + SWA_BLOCK - col
    valid = (rel >= 0) & (rel < SWA_BLOCK) & ((col >= SWA_BLOCK) | (n > 0))
    group = SWA_HEADS // SWA_KV_HEADS
    outs = []
    for hq in range(SWA_HEADS):
        kv = hq // group
        qh = q[:, hq * SWA_HEAD_DIM:(hq + 1) * SWA_HEAD_DIM]
        kh = k[:, kv * SWA_HEAD_DIM:(kv + 1) * SWA_HEAD_DIM]
        vh = v[:, kv * SWA_HEAD_DIM:(kv + 1) * SWA_HEAD_DIM]
        sc = lax.dot_general(qh, kh, (((1,), (1,)), ((), ())), preferred_element_type=F32)
        sc = jnp.where(valid, sc, NEG)
        sink = sink_ref[hq]
        m = jnp.maximum(jnp.max(sc, axis=1, keepdims=True), sink)
        p = jnp.exp(sc - m)
        denom = jnp.sum(p, axis=1, keepdims=True) + jnp.exp(sink - m)
        o = jnp.dot(p.astype(BF16), vh, preferred_element_type=F32)
        outs.append(o / denom)
    o_ref[0] = jnp.concatenate(outs, axis=1).astype(o_ref.dtype)


def _swa(h, sinks):
    b, s, _ = h.shape
    nb = s // SWA_BLOCK
    blk = lambda c: pl.BlockSpec((1, SWA_BLOCK, IN_BLOCK), lambda i, j: (i, j, c))
    blk_prev = lambda c: pl.BlockSpec((1, SWA_BLOCK, IN_BLOCK), lambda i, j: (i, jnp.maximum(j - 1, 0), c))
    return pl.pallas_call(
        _swa_kernel,
        grid=(b, nb),
        in_specs=[pl.BlockSpec(memory_space=pltpu.SMEM),
                  blk(_Q_BLOCKS[0]), blk(_Q_BLOCKS[1]), blk(_KV_BLOCK), blk_prev(_KV_BLOCK)],
        out_specs=pl.BlockSpec((1, SWA_BLOCK, SWA_HEADS * SWA_HEAD_DIM), lambda i, j: (i, j, 0)),
        out_shape=jax.ShapeDtypeStruct((b, s, SWA_HEADS * SWA_HEAD_DIM), BF16),
        compiler_params=_params("parallel", "arbitrary"),
        name="swa",
    )(sinks, h, h, h, h)


def _mem_kernel(q_ref, kv_ref, o_ref):
    scale = MEM_HEAD_DIM ** -0.5
    width = MEM_HEADS * MEM_HEAD_DIM
    for hm in range(MEM_HEADS):
        cols = slice(hm * MEM_HEAD_DIM, (hm + 1) * MEM_HEAD_DIM)
        km = kv_ref[0, :, cols]
        vm = kv_ref[0, :, width + hm * MEM_HEAD_DIM:width + (hm + 1) * MEM_HEAD_DIM]
        sc = lax.dot_general(q_ref[0, :, cols], km, (((1,), (1,)), ((), ())), preferred_element_type=F32) * scale
        m = jnp.max(sc, axis=1, keepdims=True)
        p = jnp.exp(sc - m)
        denom = jnp.sum(p, axis=1, keepdims=True)
        o = jnp.dot(p.astype(BF16), vm, preferred_element_type=F32)
        o_ref[0, :, cols] = (o / denom).astype(o_ref.dtype)


def _mem_attention(h, kvm, tq):
    b, s, _ = h.shape
    width = MEM_HEADS * MEM_HEAD_DIM
    tq = min(tq, s)
    return pl.pallas_call(
        _mem_kernel,
        grid=(b, s // tq),
        in_specs=[
            pl.BlockSpec((1, tq, width), lambda i, j: (i, j, 4)),
            pl.BlockSpec((1,) + kvm.shape[1:], lambda i, j: (i, 0, 0)),
        ],
        out_specs=pl.BlockSpec((1, tq, width), lambda i, j: (i, j, 0)),
        out_shape=jax.ShapeDtypeStruct((b, s, width), BF16),
        compiler_params=_params("parallel", "arbitrary"),
        name="mem_attention",
    )(h, kvm)


def _outproj_kernel(pool_ref, swa_ref, mem_ref, wp_ref, ws_ref, wm_ref, x_ref, g_ref, b_ref, o_ref, ob_ref,
                    *, alpha):
    mix = jnp.dot(pool_ref[...], wp_ref[...], preferred_element_type=F32)
    mix += jnp.dot(swa_ref[...], ws_ref[...], preferred_element_type=F32)
    mix += jnp.dot(mem_ref[...], wm_ref[...], preferred_element_type=F32)
    z = alpha * x_ref[...] + mix
    mu = jnp.mean(z, axis=1, keepdims=True)
    zc = z - mu
    var = jnp.mean(zc * zc, axis=1, keepdims=True)
    y = zc * lax.rsqrt(var + LN_EPS) * g_ref[...] + b_ref[...]
    yt = y.T
    o_ref[...] = yt
    ob_ref[...] = yt.astype(BF16)


def _outproj_ln(pool_o, swa_o, mem_o, w_out, x2, g, b, alpha, tm):
    t, d = x2.shape
    tm = min(tm, t)
    wp, ws, wm = pool_o.shape[1], swa_o.shape[1], mem_o.shape[1]
    w_p, w_s, w_m = w_out[:wp], w_out[wp:wp + ws], w_out[wp + ws:]
    row = lambda w: pl.BlockSpec((tm, w), lambda i: (i, 0))
    full = lambda a: pl.BlockSpec(a.shape, lambda i: (0, 0))
    return pl.pallas_call(
        functools.partial(_outproj_kernel, alpha=alpha),
        grid=(t // tm,),
        in_specs=[row(wp), row(ws), row(wm), full(w_p), full(w_s), full(w_m), row(d),
                  pl.BlockSpec((1, d), lambda i: (0, 0)), pl.BlockSpec((1, d), lambda i: (0, 0))],
        out_specs=[pl.BlockSpec((d, tm), lambda i: (0, i))] * 2,
        out_shape=[jax.ShapeDtypeStruct((d, t), F32), jax.ShapeDtypeStruct((d, t), BF16)],
        compiler_params=_params("parallel"),
        name="outproj_ln",
    )(pool_o, swa_o, mem_o, w_p, w_s, w_m, x2, g, b)


def _top16_rows(s, break_ties):
    n = s.shape[0]
    iota = lax.broadcasted_iota(jnp.int32, s.shape, 0).astype(F32) if break_ties else None
    rank = jnp.full(s.shape, float(PEER_TOPK), F32)
    vals = []
    for r in range(PEER_TOPK):
        m = jnp.max(s, axis=0, keepdims=True)
        hit = s == m
        if break_ties:
            idx = jnp.min(jnp.where(hit, iota, float(n)), axis=0, keepdims=True)
            hit = iota == idx
        rank = jnp.where(hit, float(r), rank)
        s = jnp.where(hit, -jnp.inf, s)
        vals.append(m)
    count = jnp.sum(jnp.where(rank < float(PEER_TOPK), 1.0, 0.0), axis=0, keepdims=True)
    return rank, vals, count


_CAND_SMALL_A = PEER_TOPK // 2
_CAND_ROWS = PEER_TOPK + (_CAND_SMALL_A - 1) * SUBLANES + SUBLANES


def _cand_constants():
    flat = np.full((_CAND_ROWS, 1), 1e9, np.float32)
    valid = np.zeros((_CAND_ROWS, 1), np.float32)
    for b in range(PEER_TOPK):
        flat[b, 0], valid[b, 0] = b, 1.0
    for a in range(1, _CAND_SMALL_A):
        base = PEER_TOPK + (a - 1) * SUBLANES
        for b in range(PEER_TOPK // (a + 1)):
            flat[base + b, 0], valid[base + b, 0] = a * PEER_TOPK + b, 1.0
    base = PEER_TOPK + (_CAND_SMALL_A - 1) * SUBLANES
    for k in range(SUBLANES):
        flat[base + k, 0], valid[base + k, 0] = (_CAND_SMALL_A + k) * PEER_TOPK, 1.0
    return jnp.asarray(flat), jnp.asarray(valid)


def _select_experts(s1, s2, flat, valid, break_ties):
    t = s1.shape[1]
    rank1, v1, count1 = _top16_rows(s1, break_ties)
    rank2, v2, count2 = _top16_rows(s2, break_ties)
    v2_lo = jnp.concatenate(v2[:SUBLANES], axis=0)
    v2_all = jnp.concatenate(v2, axis=0)
    v1_hi = jnp.concatenate(v1[_CAND_SMALL_A:], axis=0)
    groups = [v1[0] + v2_all]
    for a in range(1, _CAND_SMALL_A):
        groups.append(v1[a] + v2_lo)
    groups.append(v1_hi + v2[0])
    cand = jnp.concatenate(groups, axis=0)
    cand = jnp.where(valid > 0.5, cand, -jnp.inf)
    flat_b = jnp.broadcast_to(flat, cand.shape) if break_ties else None
    hits = jnp.zeros(cand.shape, F32)
    top = []
    for r in range(PEER_TOPK):
        m = jnp.max(cand, axis=0, keepdims=True)
        hit = cand == m
        if break_ties:
            pick = jnp.min(jnp.where(hit, flat_b, 2e9), axis=0, keepdims=True)
            hit = flat_b == pick
        hits = jnp.where(hit, 1.0, hits)
        cand = jnp.where(hit, -jnp.inf, cand)
        top.append(m)
    z = jnp.ones((1, t), F32)
    for r in range(1, PEER_TOPK):
        z = z + jnp.exp(top[r] - top[0])
    counts = [jnp.sum(hits[0:PEER_TOPK], axis=0, keepdims=True)]
    for a in range(1, _CAND_SMALL_A):
        base = PEER_TOPK + (a - 1) * SUBLANES
        counts.append(jnp.sum(hits[base:base + SUBLANES], axis=0, keepdims=True))
    base = PEER_TOPK + (_CAND_SMALL_A - 1) * SUBLANES
    for k in range(SUBLANES):
        counts.append(hits[base + k:base + k + 1])
    lim = jnp.zeros(s1.shape, F32)
    for a in range(PEER_TOPK):
        lim = jnp.where(rank1 == float(a), counts[a], lim)
    e1n = jnp.exp(s1 - v1[0]) / z
    e2 = jnp.exp(s2 - v2[0])
    count3 = jnp.sum(hits, axis=0, keepdims=True)
    k = float(PEER_TOPK)
    ok = jnp.where((count1 == k) & (count2 == k) & (count3 == k), 1.0, 0.0)
    return lim, e1n, rank2, e2, ok


def _retrieve_kernel(wq_ref, x_ref, k1_ref, k2_ref, flat_ref, valid_ref,
                     lim_ref, e1_ref, r2_ref, e2_ref, q_ref, *, tq):
    q_ref[...] = jnp.dot(wq_ref[...], x_ref[...], preferred_element_type=F32)
    flat, valid = flat_ref[...], valid_ref[...]

    def head(h, carry):
        for c in range(tq // SELECT_TOKENS):
            lanes = slice(c * SELECT_TOKENS, (c + 1) * SELECT_TOKENS)
            r0 = pl.multiple_of(h * 2 * HALF_DIM, 2 * HALF_DIM)
            q1 = q_ref[pl.ds(r0, HALF_DIM), lanes].astype(BF16)
            q2 = q_ref[pl.ds(r0 + HALF_DIM, HALF_DIM), lanes].astype(BF16)
            s1 = jnp.dot(k1_ref[...], q1, preferred_element_type=F32)
            s2 = jnp.dot(k2_ref[...], q2, preferred_element_type=F32)

            def emit(break_ties):
                lim, e1n, rank2, e2, ok = _select_experts(s1, s2, flat, valid, break_ties)
                lim_ref[h, :, lanes] = lim
                e1_ref[h, :, lanes] = e1n
                r2_ref[h, :, lanes] = rank2.astype(BF16)
                e2_ref[h, :, lanes] = e2.astype(BF16)
                return ok

            ok = emit(False)

            @pl.when(jnp.min(ok) < 0.5)
            def _():
                emit(True)
        return carry

    lax.fori_loop(0, PEER_HEADS, head, 0)


def _retrieve(x1t, wq_t, k1, k2, tq):
    d, t = x1t.shape
    tq = min(tq, t)
    flat, valid = _cand_constants()
    full = lambda a: pl.BlockSpec(a.shape, lambda i: (0,) * a.ndim)
    out = jax.ShapeDtypeStruct((PEER_HEADS, N_KEYS, t), F32)
    out_packed = jax.ShapeDtypeStruct((PEER_HEADS, N_KEYS, t), BF16)
    out_spec = pl.BlockSpec((PEER_HEADS, N_KEYS, tq), lambda i: (0, 0, i))
    return pl.pallas_call(
        functools.partial(_retrieve_kernel, tq=tq),
        grid=(t // tq,),
        in_specs=[full(wq_t), pl.BlockSpec((d, tq), lambda i: (0, i)), full(k1), full(k2), full(flat), full(valid)],
        out_specs=[out_spec] * 4,
        out_shape=[out, out, out_packed, out_packed],
        scratch_shapes=[pltpu.VMEM((wq_t.shape[0], tq), F32)],
        compiler_params=_params("parallel"),
        name="peer_retrieve",
    )(wq_t, x1t, k1, k2, flat, valid)


def _gelu_exact(x):
    return 0.5 * x * (1.0 + lax.erf(x * (1.0 / math.sqrt(2.0))))


def _experts_kernel(xb_ref, u_ref, vt_ref, lim_ref, e1_ref, r2_in_ref, e2_in_ref, o_ref,
                    h_ref, a_ref, rows_ref, gate_ref, r2_ref, e2_ref, *, tm, te):
    e = pl.program_id(1)

    @pl.when(e == 0)
    def _():
        o_ref[...] = jnp.zeros(o_ref.shape, F32)
        r2_ref[:, 0:tm] = r2_in_ref[...]
        e2_ref[:, LANES:LANES + tm] = e2_in_ref[...]

    n_chunks = te // EXPERT_CHUNK

    def rows_of(p):
        return slice(p * EXPERT_CHUNK, (p + 1) * EXPERT_CHUNK)

    def gates(first_key):
        keys = range(first_key, first_key + GATE_KEYS)
        blocks = range(0, N_KEYS, GATE_ROWS)
        for i in keys:
            for hd in range(PEER_HEADS):
                for q, ref in enumerate((lim_ref, e1_ref)):
                    row = jnp.broadcast_to(ref[i, hd:hd + 1, :], (GATE_ROWS, tm)).astype(BF16)
                    rows_ref[q, i, hd, :, q * LANES:q * LANES + tm] = row
        for c in range(tm // LANES):
            lanes = slice(c * LANES, (c + 1) * LANES)
            lanes1 = slice((c + 1) * LANES, (c + 2) * LANES)
            gate = {i: {jb: jnp.zeros((GATE_ROWS, LANES), BF16) for jb in blocks} for i in keys}
            for hd in range(PEER_HEADS):
                lim = {i: rows_ref[0, i, hd, :, lanes] for i in keys}
                e1 = {i: rows_ref[1, i, hd, :, lanes1] for i in keys}
                for jb in blocks:
                    r2 = r2_ref[hd * N_KEYS + jb:hd * N_KEYS + jb + GATE_ROWS, lanes]
                    e2 = e2_ref[hd * N_KEYS + jb:hd * N_KEYS + jb + GATE_ROWS, lanes1]
                    for i in keys:
                        gate[i][jb] = gate[i][jb] + jnp.where(r2 < lim[i], e2 * e1[i], jnp.zeros_like(e2))
            for i in keys:
                for jb in blocks:
                    gate_ref[i * N_KEYS + jb:i * N_KEYS + jb + GATE_ROWS, lanes] = gate[i][jb]

    for first_key in range(0, te // N_KEYS, GATE_KEYS):
        gates(first_key)
    h_ref[...] = jnp.dot(u_ref[...], xb_ref[...], preferred_element_type=F32)
    for p in range(n_chunks):
        a_ref[rows_of(p), :] = gate_ref[rows_of(p), :] * _gelu_exact(h_ref[rows_of(p), :]).astype(BF16)
        o_ref[...] += jnp.dot(vt_ref[:, rows_of(p)], a_ref[rows_of(p), :], preferred_element_type=F32)


def _experts(xbt, u, vt, sel, tm, te):
    d, t = xbt.shape
    n_exp = u.shape[0]
    tm, te = min(tm, t), min(te, n_exp)
    tok = pl.BlockSpec((d, tm), lambda i, j: (0, i))
    sel_spec = pl.BlockSpec((PEER_HEADS * N_KEYS, tm), lambda i, j: (0, i))
    key_spec = pl.BlockSpec((te // N_KEYS, PEER_HEADS, tm), lambda i, j: (j, 0, i))
    lim, e1n, rank2, e2 = sel
    sel = (lim.transpose(1, 0, 2), e1n.transpose(1, 0, 2), rank2.reshape(-1, t), e2.reshape(-1, t))
    return pl.pallas_call(
        functools.partial(_experts_kernel, tm=tm, te=te),
        grid=(t // tm, n_exp // te),
        in_specs=[tok, pl.BlockSpec((te, d), lambda i, j: (j, 0)), pl.BlockSpec((d, te), lambda i, j: (0, j)),
                  key_spec, key_spec, sel_spec, sel_spec],
        out_specs=tok,
        out_shape=jax.ShapeDtypeStruct((d, t), F32),
        scratch_shapes=[pltpu.VMEM((te, tm), F32), pltpu.VMEM((te, tm), BF16),
                        pltpu.VMEM((2, te // N_KEYS, PEER_HEADS, GATE_ROWS, tm + LANES), BF16),
                        pltpu.VMEM((te, tm), BF16),
                        pltpu.VMEM((PEER_HEADS * N_KEYS, tm + LANES), BF16),
                        pltpu.VMEM((PEER_HEADS * N_KEYS, tm + LANES), BF16)],
        compiler_params=_params("parallel", "arbitrary"),
        name="peer_experts",
    )(xbt, u, vt, *sel)


def _ln_t_kernel(x_ref, f_ref, g_ref, b_ref, o_ref, *, alpha):
    z = (alpha * x_ref[...] + f_ref[...]).T
    mu = jnp.mean(z, axis=1, keepdims=True)
    zc = z - mu
    var = jnp.mean(zc * zc, axis=1, keepdims=True)
    o_ref[...] = zc * lax.rsqrt(var + LN_EPS) * g_ref[...] + b_ref[...]


def _residual_ln_t(x1t, fft, g, b, alpha, tm):
    d, t = x1t.shape
    tm = min(tm, t)
    tok = pl.BlockSpec((d, tm), lambda i: (0, i))
    vec = pl.BlockSpec((1, d), lambda i: (0, 0))
    return pl.pallas_call(
        functools.partial(_ln_t_kernel, alpha=alpha),
        grid=(t // tm,),
        in_specs=[tok, tok, vec, vec],
        out_specs=pl.BlockSpec((tm, d), lambda i: (i, 0)),
        out_shape=jax.ShapeDtypeStruct((t, d), F32),
        compiler_params=_params("parallel"),
        name="residual_ln",
    )(x1t, fft, g, b)


def kernel(x, mem, positions, w_in, w_mem_kv, w_pool, pool_scale, attn_sinks, w_out, ln1_g, ln1_b,
           w_peer_q, sub_keys_1, sub_keys_2, expert_u, expert_v, ln2_g, ln2_b):
    bsz, seq, d = x.shape
    depth = w_in.shape[0]
    t = bsz * seq
    alpha = (2.0 * depth) ** 0.25
    for l in range(depth):
        x2 = x.reshape(t, d)
        h = _inproj(x2, w_in[l].astype(BF16), positions, 512).reshape(bsz, seq, -1)
        mem2 = mem.reshape(-1, d).astype(BF16)
        kvm = _matmul(mem2, w_mem_kv[l].astype(BF16), BF16, 512, 512).reshape(bsz, mem.shape[1], -1)
        pool_o = _pool(h, w_pool[l].astype(BF16), pool_scale[l].reshape(1, -1), 512)
        swa_o = _swa(h, attn_sinks[l])
        mem_o = _mem_attention(h, kvm, 512)
        x1t, x1bt = _outproj_ln(pool_o.reshape(t, -1), swa_o.reshape(t, -1), mem_o.reshape(t, -1),
                          w_out[l].astype(BF16), x2, ln1_g[l].reshape(1, d), ln1_b[l].reshape(1, d), alpha, 512)
        sel = _retrieve(x1bt, w_peer_q[l].T.astype(BF16), sub_keys_1[l].astype(BF16),
                        sub_keys_2[l].astype(BF16), 512)
        fft = _experts(x1bt, expert_u[l].astype(BF16), expert_v[l].T.astype(BF16), sel, 1024, 512)
        x = _residual_ln_t(x1t, fft, ln2_g[l].reshape(1, d), ln2_b[l].reshape(1, d), alpha, 512).reshape(bsz, seq, d)
    return x
```

```python
import functools
import math

import numpy as np
import jax
import jax.numpy as jnp
from jax import lax
from jax.experimental import pallas as pl
from jax.experimental.pallas import tpu as pltpu

F32 = jnp.float32
BF16 = jnp.bfloat16

LANES = 128
SUBLANES = 8
VMEM_LIMIT_BYTES = 56 * 1024 * 1024

POOL_WINDOWS = (2, 4, 8, 16)
POOL_GROUP = 128
POOL_HALO = 16
SWA_HEAD_DIM = 64
SWA_HEADS = 16
SWA_KV_HEADS = 4
SWA_BLOCK = 128
ROPE_THETA = 500000.0
ROPE_DIM = 16
MEM_HEADS = 4
MEM_HEAD_DIM = 128
PEER_HEADS = 8
N_KEYS = 128
PEER_TOPK = 16
HALF_DIM = 128
LN_EPS = 1e-5
NEG = -1e30

EXPERT_CHUNK = 512
GATE_KEYS = 4
GATE_ROWS = 16
SELECT_TOKENS = 512


def _params(*semantics):
    return pltpu.CompilerParams(dimension_semantics=semantics, vmem_limit_bytes=VMEM_LIMIT_BYTES)


def _matmul_kernel(a_ref, b_ref, o_ref):
    o_ref[...] = jnp.dot(a_ref[...], b_ref[...], preferred_element_type=F32).astype(o_ref.dtype)


def _matmul(a, b, out_dtype, tm, tn):
    m, k = a.shape
    n = b.shape[1]
    tm, tn = min(tm, m), min(tn, n)
    return pl.pallas_call(
        _matmul_kernel,
        grid=(m // tm, n // tn),
        in_specs=[pl.BlockSpec((tm, k), lambda i, j: (i, 0)), pl.BlockSpec((k, tn), lambda i, j: (0, j))],
        out_specs=pl.BlockSpec((tm, tn), lambda i, j: (i, j)),
        out_shape=jax.ShapeDtypeStruct((m, n), out_dtype),
        compiler_params=_params("parallel", "arbitrary"),
        name="matmul",
    )(a, b)


IN_BLOCK = 512
_Q_BLOCKS = (1, 2)
_KV_BLOCK = 3


def _rope(x, c, sa, sb):
    half = ROPE_DIM // 2
    return x * c + pltpu.roll(x, LANES - half, 1) * sa + pltpu.roll(x, half, 1) * sb


def _inproj_kernel(x_ref, w_ref, pos_ref, freq_ref, sa_ref, sb_ref, o_ref):
    xb = x_ref[...].astype(BF16)
    ang = pos_ref[...].astype(F32) * freq_ref[...]
    s = jnp.sin(ang)
    c, sa, sb = jnp.cos(ang), s * sa_ref[...], s * sb_ref[...]
    for j in range(w_ref.shape[1] // IN_BLOCK):
        y = jnp.dot(xb, w_ref[:, j * IN_BLOCK:(j + 1) * IN_BLOCK], preferred_element_type=F32)
        for k in range(IN_BLOCK // LANES):
            piece = y[:, k * LANES:(k + 1) * LANES]
            if j in _Q_BLOCKS:
                piece = _rope(piece, c, sa, sb) * SWA_HEAD_DIM ** -0.5
            elif j == _KV_BLOCK and k < IN_BLOCK // LANES // 2:
                piece = _rope(piece, c, sa, sb)
            o_ref[:, j * IN_BLOCK + k * LANES:j * IN_BLOCK + (k + 1) * LANES] = piece.astype(o_ref.dtype)


def _rope_constants():
    lane = np.arange(LANES)
    d = lane % SWA_HEAD_DIM
    half = ROPE_DIM // 2
    inv_freq = np.float32(ROPE_THETA) ** (-np.arange(0, ROPE_DIM, 2, dtype=np.float32) / np.float32(ROPE_DIM))
    freq = np.where(d < ROPE_DIM, inv_freq[d % half], 0.0).astype(np.float32)
    sa = np.where(d < half, -1.0, 0.0).astype(np.float32)
    sb = np.where((d >= half) & (d < ROPE_DIM), 1.0, 0.0).astype(np.float32)
    return [jnp.asarray(a.reshape(1, LANES)) for a in (freq, sa, sb)]


def _inproj(x2, w, positions, tm):
    t, d = x2.shape
    n = w.shape[1]
    tm = min(tm, t)
    freq, sa, sb = _rope_constants()
    const = pl.BlockSpec((1, LANES), lambda i: (0, 0))
    return pl.pallas_call(
        _inproj_kernel,
        grid=(t // tm,),
        in_specs=[pl.BlockSpec((tm, d), lambda i: (i, 0)), pl.BlockSpec((d, n), lambda i: (0, 0)),
                  pl.BlockSpec((tm, 1), lambda i: (i, 0)), const, const, const],
        out_specs=pl.BlockSpec((tm, n), lambda i: (i, 0)),
        out_shape=jax.ShapeDtypeStruct((t, n), BF16),
        compiler_params=_params("parallel"),
        name="inproj_rope",
    )(x2, w, positions.reshape(t, 1), freq, sa, sb)


def _pool_kernel(v_ref, w_ref, scale_ref, o_ref, ext_ref, *, ts):
    s = pl.program_id(1)

    @pl.when(s == 0)
    def _():
        ext_ref[0:POOL_HALO, :] = jnp.zeros((POOL_HALO, ext_ref.shape[1]), F32)

    ext_ref[POOL_HALO:POOL_HALO + ts, :] = v_ref[0].astype(F32)
    pos = s * ts + lax.broadcasted_iota(jnp.int32, (ts, 1), 0)
    for g, w in enumerate(POOL_WINDOWS):
        cols = slice(g * POOL_GROUP, (g + 1) * POOL_GROUP)
        acc = ext_ref[POOL_HALO:POOL_HALO + ts, cols]
        for k in range(1, w):
            acc = acc + ext_ref[POOL_HALO - k:POOL_HALO - k + ts, cols]
        count = jnp.minimum(pos + 1, w).astype(F32)
        pooled = acc / count - ext_ref[POOL_HALO:POOL_HALO + ts, cols]
        y = jnp.dot(pooled.astype(BF16), w_ref[g], preferred_element_type=F32)
        o_ref[0, :, cols] = (y * scale_ref[:, cols]).astype(o_ref.dtype)
    ext_ref[0:POOL_HALO, :] = ext_ref[ts:ts + POOL_HALO, :]


def _pool(h, w_pool, pool_scale, ts):
    b, s, _ = h.shape
    width = POOL_GROUP * len(POOL_WINDOWS)
    ts = min(ts, s)
    return pl.pallas_call(
        functools.partial(_pool_kernel, ts=ts),
        grid=(b, s // ts),
        in_specs=[
            pl.BlockSpec((1, ts, width), lambda i, j: (i, j, 0)),
            pl.BlockSpec(w_pool.shape, lambda i, j: (0, 0, 0)),
            pl.BlockSpec((1, width), lambda i, j: (0, 0)),
        ],
        out_specs=pl.BlockSpec((1, ts, width), lambda i, j: (i, j, 0)),
        out_shape=jax.ShapeDtypeStruct((b, s, width), BF16),
        scratch_shapes=[pltpu.VMEM((ts + POOL_HALO, width), F32)],
        compiler_params=_params("arbitrary", "arbitrary"),
        name="pool",
    )(h, w_pool, pool_scale)


def _swa_kernel(sink_ref, q0_ref, q1_ref, kv_ref, kvp_ref, o_ref):
    n = pl.program_id(1)
    kvw = SWA_KV_HEADS * SWA_HEAD_DIM
    q = jnp.concatenate([q0_ref[0], q1_ref[0]], axis=1)
    k = jnp.concatenate([kvp_ref[0, :, 0:kvw], kv_ref[0, :, 0:kvw]], axis=0)
    v = jnp.concatenate([kvp_ref[0, :, kvw:2 * kvw], kv_ref[0, :, kvw:2 * kvw]], axis=0)
    row = lax.broadcasted_iota(jnp.int32, (SWA_BLOCK, 2 * SWA_BLOCK), 0)
    col = lax.broadcasted_iota(jnp.int32, (SWA_BLOCK, 2 * SWA_BLOCK), 1)
    rel = row + SWA_BLOCK - col
    valid = (rel >= 0) & (rel < SWA_BLOCK) & ((col >= SWA_BLOCK) | (n > 0))
    group = SWA_HEADS // SWA_KV_HEADS
    outs = []
    for hq in range(SWA_HEADS):
        kv = hq // group
        qh = q[:, hq * SWA_HEAD_DIM:(hq + 1) * SWA_HEAD_DIM]
        kh = k[:, kv * SWA_HEAD_DIM:(kv + 1) * SWA_HEAD_DIM]
        vh = v[:, kv * SWA_HEAD_DIM:(kv + 1) * SWA_HEAD_DIM]
        sc = lax.dot_general(qh, kh, (((1,), (1,)), ((), ())), preferred_element_type=F32)
        sc = jnp.where(valid, sc, NEG)
        sink = sink_ref[hq]
        m = jnp.maximum(jnp.max(sc, axis=1, keepdims=True), sink)
        p = jnp.exp(sc - m)
        denom = jnp.sum(p, axis=1, keepdims=True) + jnp.exp(sink - m)
        o = jnp.dot(p.astype(BF16), vh, preferred_element_type=F32)
        outs.append(o / denom)
    o_ref[0] = jnp.concatenate(outs, axis=1).astype(o_ref.dtype)


def _swa(h, sinks):
    b, s, _ = h.shape
    nb = s // SWA_BLOCK
    blk = lambda c: pl.BlockSpec((1, SWA_BLOCK, IN_BLOCK), lambda i, j: (i, j, c))
    blk_prev = lambda c: pl.BlockSpec((1, SWA_BLOCK, IN_BLOCK), lambda i, j: (i, jnp.maximum(j - 1, 0), c))
    return pl.pallas_call(
        _swa_kernel,
        grid=(b, nb),
        in_specs=[pl.BlockSpec(memory_space=pltpu.SMEM),
                  blk(_Q_BLOCKS[0]), blk(_Q_BLOCKS[1]), blk(_KV_BLOCK), blk_prev(_KV_BLOCK)],
        out_specs=pl.BlockSpec((1, SWA_BLOCK, SWA_HEADS * SWA_HEAD_DIM), lambda i, j: (i, j, 0)),
        out_shape=jax.ShapeDtypeStruct((b, s, SWA_HEADS * SWA_HEAD_DIM), BF16),
        compiler_params=_params("parallel", "arbitrary"),
        name="swa",
    )(sinks, h, h, h, h)


def _mem_kernel(q_ref, kv_ref, o_ref):
    scale = MEM_HEAD_DIM ** -0.5
    width = MEM_HEADS * MEM_HEAD_DIM
    for hm in range(MEM_HEADS):
        cols = slice(hm * MEM_HEAD_DIM, (hm + 1) * MEM_HEAD_DIM)
        km = kv_ref[0, :, cols]
        vm = kv_ref[0, :, width + hm * MEM_HEAD_DIM:width + (hm + 1) * MEM_HEAD_DIM]
        sc = lax.dot_general(q_ref[0, :, cols], km, (((1,), (1,)), ((), ())), preferred_element_type=F32) * scale
        m = jnp.max(sc, axis=1, keepdims=True)
        p = jnp.exp(sc - m)
        denom = jnp.sum(p, axis=1, keepdims=True)
        o = jnp.dot(p.astype(BF16), vm, preferred_element_type=F32)
        o_ref[0, :, cols] = (o / denom).astype(o_ref.dtype)


def _mem_attention(h, kvm, tq):
    b, s, _ = h.shape
    width = MEM_HEADS * MEM_HEAD_DIM
    tq = min(tq, s)
    return pl.pallas_call(
        _mem_kernel,
        grid=(b, s // tq),
        in_specs=[
            pl.BlockSpec((1, tq, width), lambda i, j: (i, j, 4)),
            pl.BlockSpec((1,) + kvm.shape[1:], lambda i, j: (i, 0, 0)),
        ],
        out_specs=pl.BlockSpec((1, tq, width), lambda i, j: (i, j, 0)),
        out_shape=jax.ShapeDtypeStruct((b, s, width), BF16),
        compiler_params=_params("parallel", "arbitrary"),
        name="mem_attention",
    )(h, kvm)


def _outproj_kernel(pool_ref, swa_ref, mem_ref, wp_ref, ws_ref, wm_ref, x_ref, g_ref, b_ref, o_ref, ob_ref,
                    *, alpha):
    mix = jnp.dot(pool_ref[...], wp_ref[...], preferred_element_type=F32)
    mix += jnp.dot(swa_ref[...], ws_ref[...], preferred_element_type=F32)
    mix += jnp.dot(mem_ref[...], wm_ref[...], preferred_element_type=F32)
    z = alpha * x_ref[...] + mix
    mu = jnp.mean(z, axis=1, keepdims=True)
    zc = z - mu
    var = jnp.mean(zc * zc, axis=1, keepdims=True)
    y = zc * lax.rsqrt(var + LN_EPS) * g_ref[...] + b_ref[...]
    yt = y.T
    o_ref[...] = yt
    ob_ref[...] = yt.astype(BF16)


def _outproj_ln(pool_o, swa_o, mem_o, w_out, x2, g, b, alpha, tm):
    t, d = x2.shape
    tm = min(tm, t)
    wp, ws, wm = pool_o.shape[1], swa_o.shape[1], mem_o.shape[1]
    w_p, w_s, w_m = w_out[:wp], w_out[wp:wp + ws], w_out[wp + ws:]
    row = lambda w: pl.BlockSpec((tm, w), lambda i: (i, 0))
    full = lambda a: pl.BlockSpec(a.shape, lambda i: (0, 0))
    return pl.pallas_call(
        functools.partial(_outproj_kernel, alpha=alpha),
        grid=(t // tm,),
        in_specs=[row(wp), row(ws), row(wm), full(w_p), full(w_s), full(w_m), row(d),
                  pl.BlockSpec((1, d), lambda i: (0, 0)), pl.BlockSpec((1, d), lambda i: (0, 0))],
        out_specs=[pl.BlockSpec((d, tm), lambda i: (0, i))] * 2,
        out_shape=[jax.ShapeDtypeStruct((d, t), F32), jax.ShapeDtypeStruct((d, t), BF16)],
        compiler_params=_params("parallel"),
        name="outproj_ln",
    )(pool_o, swa_o, mem_o, w_p, w_s, w_m, x2, g, b)


def _top16_rows(s, break_ties):
    n = s.shape[0]
    iota = lax.broadcasted_iota(jnp.int32, s.shape, 0).astype(F32) if break_ties else None
    rank = jnp.full(s.shape, float(PEER_TOPK), F32)
    vals = []
    for r in range(PEER_TOPK):
        m = jnp.max(s, axis=0, keepdims=True)
        hit = s == m
        if break_ties:
            idx = jnp.min(jnp.where(hit, iota, float(n)), axis=0, keepdims=True)
            hit = iota == idx
        rank = jnp.where(hit, float(r), rank)
        s = jnp.where(hit, -jnp.inf, s)
        vals.append(m)
    count = jnp.sum(jnp.where(rank < float(PEER_TOPK), 1.0, 0.0), axis=0, keepdims=True)
    return rank, vals, count


_CAND_SMALL_A = PEER_TOPK // 2
_CAND_ROWS = PEER_TOPK + (_CAND_SMALL_A - 1) * SUBLANES + SUBLANES


def _cand_constants():
    flat = np.full((_CAND_ROWS, 1), 1e9, np.float32)
    valid = np.zeros((_CAND_ROWS, 1), np.float32)
    for b in range(PEER_TOPK):
        flat[b, 0], valid[b, 0] = b, 1.0
    for a in range(1, _CAND_SMALL_A):
        base = PEER_TOPK + (a - 1) * SUBLANES
        for b in range(PEER_TOPK // (a + 1)):
            flat[base + b, 0], valid[base + b, 0] = a * PEER_TOPK + b, 1.0
    base = PEER_TOPK + (_CAND_SMALL_A - 1) * SUBLANES
    for k in range(SUBLANES):
        flat[base + k, 0], valid[base + k, 0] = (_CAND_SMALL_A + k) * PEER_TOPK, 1.0
    return jnp.asarray(flat), jnp.asarray(valid)


def _select_experts(s1, s2, flat, valid, break_ties):
    t = s1.shape[1]
    rank1, v1, count1 = _top16_rows(s1, break_ties)
    rank2, v2, count2 = _top16_rows(s2, break_ties)
    v2_lo = jnp.concatenate(v2[:SUBLANES], axis=0)
    v2_all = jnp.concatenate(v2, axis=0)
    v1_hi = jnp.concatenate(v1[_CAND_SMALL_A:], axis=0)
    groups = [v1[0] + v2_all]
    for a in range(1, _CAND_SMALL_A):
        groups.append(v1[a] + v2_lo)
    groups.append(v1_hi + v2[0])
    cand = jnp.concatenate(groups, axis=0)
    cand = jnp.where(valid > 0.5, cand, -jnp.inf)
    flat_b = jnp.broadcast_to(flat, cand.shape) if break_ties else None
    hits = jnp.zeros(cand.shape, F32)
    top = []
    for r in range(PEER_TOPK):
        m = jnp.max(cand, axis=0, keepdims=True)
        hit = cand == m
        if break_ties:
            pick = jnp.min(jnp.where(hit, flat_b, 2e9), axis=0, keepdims=True)
            hit = flat_b == pick
        hits = jnp.where(hit, 1.0, hits)
        cand = jnp.where(hit, -jnp.inf, cand)
        top.append(m)
    z = jnp.ones((1, t), F32)
    for r in range(1, PEER_TOPK):
        z = z + jnp.exp(top[r] - top[0])
    counts = [jnp.sum(hits[0:PEER_TOPK], axis=0, keepdims=True)]
    for a in range(1, _CAND_SMALL_A):
        base = PEER_TOPK + (a - 1) * SUBLANES
        counts.append(jnp.sum(hits[base:base + SUBLANES], axis=0, keepdims=True))
    base = PEER_TOPK + (_CAND_SMALL_A - 1) * SUBLANES
    for k in range(SUBLANES):
        counts.append(hits[base + k:base + k + 1])
    lim = jnp.zeros(s1.shape, F32)
    for a in range(PEER_TOPK):
        lim = jnp.where(rank1 == float(a), counts[a], lim)
    e1n = jnp.exp(s1 - v1[0]) / z
    e2 = jnp.exp(s2 - v2[0])
    count3 = jnp.sum(hits, axis=0, keepdims=True)
    k = float(PEER_TOPK)
    ok = jnp.where((count1 == k) & (count2 == k) & (count3 == k), 1.0, 0.0)
    return lim, e1n, rank2, e2, ok


def _retrieve_kernel(wq_ref, x_ref, k1_ref, k2_ref, flat_ref, valid_ref,
                     lim_out_ref, e1_out_ref, r2_ref, e2_ref, q_ref, lim_ref, e1_ref, *, tq):
    q_ref[...] = jnp.dot(wq_ref[...], x_ref[...], preferred_element_type=F32)
    flat, valid = flat_ref[...], valid_ref[...]

    def head(h, carry):
        for c in range(tq // SELECT_TOKENS):
            lanes = slice(c * SELECT_TOKENS, (c + 1) * SELECT_TOKENS)
            r0 = pl.multiple_of(h * 2 * HALF_DIM, 2 * HALF_DIM)
            q1 = q_ref[pl.ds(r0, HALF_DIM), lanes].astype(BF16)
            q2 = q_ref[pl.ds(r0 + HALF_DIM, HALF_DIM), lanes].astype(BF16)
            s1 = jnp.dot(k1_ref[...], q1, preferred_element_type=F32)
            s2 = jnp.dot(k2_ref[...], q2, preferred_element_type=F32)

            def emit(break_ties):
                lim, e1n, rank2, e2, ok = _select_experts(s1, s2, flat, valid, break_ties)
                lim_ref[h, :, lanes] = lim
                e1_ref[h, :, lanes] = e1n
                r2_ref[h, :, lanes] = rank2.astype(BF16)
                e2_ref[h, :, lanes] = e2.astype(BF16)
                return ok

            ok = emit(False)

            @pl.when(jnp.min(ok) < 0.5)
            def _():
                emit(True)
        return carry

    lax.fori_loop(0, PEER_HEADS, head, 0)
    lim_out_ref[...] = pltpu.einshape("hkt->kht", lim_ref[...])
    e1_out_ref[...] = pltpu.einshape("hkt->kht", e1_ref[...])


def _retrieve(x1t, wq_t, k1, k2, tq):
    d, t = x1t.shape
    tq = min(tq, t)
    flat, valid = _cand_constants()
    full = lambda a: pl.BlockSpec(a.shape, lambda i: (0,) * a.ndim)
    out_keys = jax.ShapeDtypeStruct((N_KEYS, PEER_HEADS, t), F32)
    key_spec = pl.BlockSpec((N_KEYS, PEER_HEADS, tq), lambda i: (0, 0, i))
    out_packed = jax.ShapeDtypeStruct((PEER_HEADS, N_KEYS, t), BF16)
    out_spec = pl.BlockSpec((PEER_HEADS, N_KEYS, tq), lambda i: (0, 0, i))
    return pl.pallas_call(
        functools.partial(_retrieve_kernel, tq=tq),
        grid=(t // tq,),
        in_specs=[full(wq_t), pl.BlockSpec((d, tq), lambda i: (0, i)), full(k1), full(k2), full(flat), full(valid)],
        out_specs=[key_spec, key_spec, out_spec, out_spec],
        out_shape=[out_keys, out_keys, out_packed, out_packed],
        scratch_shapes=[pltpu.VMEM((wq_t.shape[0], tq), F32)] + [pltpu.VMEM((PEER_HEADS, N_KEYS, tq), F32)] * 2,
        compiler_params=_params("parallel"),
        name="peer_retrieve",
    )(wq_t, x1t, k1, k2, flat, valid)


def _gelu_exact(x):
    return 0.5 * x * (1.0 + lax.erf(x * (1.0 / math.sqrt(2.0))))


def _experts_kernel(xb_ref, u_ref, vt_ref, lim_ref, e1_ref, r2_in_ref, e2_in_ref, o_ref,
                    h_ref, a_ref, rows_ref, gate_ref, r2_ref, e2_ref, *, tm, te):
    e = pl.program_id(1)

    @pl.when(e == 0)
    def _():
        o_ref[...] = jnp.zeros(o_ref.shape, F32)
        r2_ref[:, 0:tm] = r2_in_ref[...]
        e2_ref[:, LANES:LANES + tm] = e2_in_ref[...]

    n_chunks = te // EXPERT_CHUNK

    def rows_of(p):
        return slice(p * EXPERT_CHUNK, (p + 1) * EXPERT_CHUNK)

    def gates(first_key):
        keys = range(first_key, first_key + GATE_KEYS)
        blocks = range(0, N_KEYS, GATE_ROWS)
        for i in keys:
            for hd in range(PEER_HEADS):
                for q, ref in enumerate((lim_ref, e1_ref)):
                    row = jnp.broadcast_to(ref[i, hd:hd + 1, :], (GATE_ROWS, tm)).astype(BF16)
                    rows_ref[q, i, hd, :, q * LANES:q * LANES + tm] = row
        for c in range(tm // LANES):
            lanes = slice(c * LANES, (c + 1) * LANES)
            lanes1 = slice((c + 1) * LANES, (c + 2) * LANES)
            gate = {i: {jb: jnp.zeros((GATE_ROWS, LANES), BF16) for jb in blocks} for i in keys}
            for hd in range(PEER_HEADS):
                lim = {i: rows_ref[0, i, hd, :, lanes] for i in keys}
                e1 = {i: rows_ref[1, i, hd, :, lanes1] for i in keys}
                for jb in blocks:
                    r2 = r2_ref[hd * N_KEYS + jb:hd * N_KEYS + jb + GATE_ROWS, lanes]
                    e2 = e2_ref[hd * N_KEYS + jb:hd * N_KEYS + jb + GATE_ROWS, lanes1]
                    for i in keys:
                        gate[i][jb] = gate[i][jb] + jnp.where(r2 < lim[i], e2 * e1[i], jnp.zeros_like(e2))
            for i in keys:
                for jb in blocks:
                    gate_ref[i * N_KEYS + jb:i * N_KEYS + jb + GATE_ROWS, lanes] = gate[i][jb]

    for first_key in range(0, te // N_KEYS, GATE_KEYS):
        gates(first_key)
    h_ref[...] = jnp.dot(u_ref[...], xb_ref[...], preferred_element_type=F32)
    for p in range(n_chunks):
        a_ref[rows_of(p), :] = gate_ref[rows_of(p), :] * _gelu_exact(h_ref[rows_of(p), :]).astype(BF16)
        o_ref[...] += jnp.dot(vt_ref[:, rows_of(p)], a_ref[rows_of(p), :], preferred_element_type=F32)


def _experts(xbt, u, vt, sel, tm, te):
    d, t = xbt.shape
    n_exp = u.shape[0]
    tm, te = min(tm, t), min(te, n_exp)
    tok = pl.BlockSpec((d, tm), lambda i, j: (0, i))
    sel_spec = pl.BlockSpec((PEER_HEADS * N_KEYS, tm), lambda i, j: (0, i))
    key_spec = pl.BlockSpec((te // N_KEYS, PEER_HEADS, tm), lambda i, j: (j, 0, i))
    lim, e1n, rank2, e2 = sel
    sel = (lim, e1n, rank2.reshape(-1, t), e2.reshape(-1, t))
    return pl.pallas_call(
        functools.partial(_experts_kernel, tm=tm, te=te),
        grid=(t // tm, n_exp // te),
        in_specs=[tok, pl.BlockSpec((te, d), lambda i, j: (j, 0)), pl.BlockSpec((d, te), lambda i, j: (0, j)),
                  key_spec, key_spec, sel_spec, sel_spec],
        out_specs=tok,
        out_shape=jax.ShapeDtypeStruct((d, t), F32),
        scratch_shapes=[pltpu.VMEM((te, tm), F32), pltpu.VMEM((te, tm), BF16),
                        pltpu.VMEM((2, te // N_KEYS, PEER_HEADS, GATE_ROWS, tm + LANES), BF16),
                        pltpu.VMEM((te, tm), BF16),
                        pltpu.VMEM((PEER_HEADS * N_KEYS, tm + LANES), BF16),
                        pltpu.VMEM((PEER_HEADS * N_KEYS, tm + LANES), BF16)],
        compiler_params=_params("parallel", "arbitrary"),
        name="peer_experts",
    )(xbt, u, vt, *sel)


def _ln_t_kernel(x_ref, f_ref, g_ref, b_ref, o_ref, *, alpha):
    z = (alpha * x_ref[...] + f_ref[...]).T
    mu = jnp.mean(z, axis=1, keepdims=True)
    zc = z - mu
    var = jnp.mean(zc * zc, axis=1, keepdims=True)
    o_ref[...] = zc * lax.rsqrt(var + LN_EPS) * g_ref[...] + b_ref[...]


def _residual_ln_t(x1t, fft, g, b, alpha, tm):
    d, t = x1t.shape
    tm = min(tm, t)
    tok = pl.BlockSpec((d, tm), lambda i: (0, i))
    vec = pl.BlockSpec((1, d), lambda i: (0, 0))
    return pl.pallas_call(
        functools.partial(_ln_t_kernel, alpha=alpha),
        grid=(t // tm,),
        in_specs=[tok, tok, vec, vec],
        out_specs=pl.BlockSpec((tm, d), lambda i: (i, 0)),
        out_shape=jax.ShapeDtypeStruct((t, d), F32),
        compiler_params=_params("parallel"),
        name="residual_ln",
    )(x1t, fft, g, b)


def kernel(x, mem, positions, w_in, w_mem_kv, w_pool, pool_scale, attn_sinks, w_out, ln1_g, ln1_b,
           w_peer_q, sub_keys_1, sub_keys_2, expert_u, expert_v, ln2_g, ln2_b):
    bsz, seq, d = x.shape
    depth = w_in.shape[0]
    t = bsz * seq
    alpha = (2.0 * depth) ** 0.25
    for l in range(depth):
        x2 = x.reshape(t, d)
        h = _inproj(x2, w_in[l].astype(BF16), positions, 512).reshape(bsz, seq, -1)
        mem2 = mem.reshape(-1, d).astype(BF16)
        kvm = _matmul(mem2, w_mem_kv[l].astype(BF16), BF16, 512, 512).reshape(bsz, mem.shape[1], -1)
        pool_o = _pool(h, w_pool[l].astype(BF16), pool_scale[l].reshape(1, -1), 512)
        swa_o = _swa(h, attn_sinks[l])
        mem_o = _mem_attention(h, kvm, 512)
        x1t, x1bt = _outproj_ln(pool_o.reshape(t, -1), swa_o.reshape(t, -1), mem_o.reshape(t, -1),
                          w_out[l].astype(BF16), x2, ln1_g[l].reshape(1, d), ln1_b[l].reshape(1, d), alpha, 512)
        sel = _retrieve(x1bt, w_peer_q[l].T.astype(BF16), sub_keys_1[l].astype(BF16),
                        sub_keys_2[l].astype(BF16), 512)
        fft = _experts(x1bt, expert_u[l].astype(BF16), expert_v[l].T.astype(BF16), sel, 1024, 512)
        x = _residual_ln_t(x1t, fft, ln2_g[l].reshape(1, d), ln2_b[l].reshape(1, d), alpha, 512).reshape(bsz, seq, d)
    return x
```

```python
import functools
import math

import numpy as np
import jax
import jax.numpy as jnp
from jax import lax
from jax.experimental import pallas as pl
from jax.experimental.pallas import tpu as pltpu

F32 = jnp.float32
BF16 = jnp.bfloat16

LANES = 128
SUBLANES = 8
VMEM_LIMIT_BYTES = 56 * 1024 * 1024

POOL_WINDOWS = (2, 4, 8, 16)
POOL_GROUP = 128
POOL_HALO = 16
SWA_HEAD_DIM = 64
SWA_HEADS = 16
SWA_KV_HEADS = 4
SWA_BLOCK = 128
ROPE_THETA = 500000.0
ROPE_DIM = 16
MEM_HEADS = 4
MEM_HEAD_DIM = 128
PEER_HEADS = 8
N_KEYS = 128
PEER_TOPK = 16
HALF_DIM = 128
LN_EPS = 1e-5
NEG = -1e30

EXPERT_CHUNK = 512
GATE_KEYS = 4
GATE_ROWS = 16
SELECT_TOKENS = 512


def _params(*semantics):
    return pltpu.CompilerParams(dimension_semantics=semantics, vmem_limit_bytes=VMEM_LIMIT_BYTES)


def _matmul_kernel(a_ref, b_ref, o_ref):
    o_ref[...] = jnp.dot(a_ref[...], b_ref[...], preferred_element_type=F32).astype(o_ref.dtype)


def _matmul(a, b, out_dtype, tm, tn):
    m, k = a.shape
    n = b.shape[1]
    tm, tn = min(tm, m), min(tn, n)
    return pl.pallas_call(
        _matmul_kernel,
        grid=(m // tm, n // tn),
        in_specs=[pl.BlockSpec((tm, k), lambda i, j: (i, 0)), pl.BlockSpec((k, tn), lambda i, j: (0, j))],
        out_specs=pl.BlockSpec((tm, tn), lambda i, j: (i, j)),
        out_shape=jax.ShapeDtypeStruct((m, n), out_dtype),
        compiler_params=_params("parallel", "arbitrary"),
        name="matmul",
    )(a, b)


IN_BLOCK = 512
_Q_BLOCKS = (1, 2)
_KV_BLOCK = 3


def _rope(x, c, sa, sb):
    half = ROPE_DIM // 2
    return x * c + pltpu.roll(x, LANES - half, 1) * sa + pltpu.roll(x, half, 1) * sb


def _inproj_kernel(x_ref, w_ref, pos_ref, freq_ref, sa_ref, sb_ref, o_ref):
    xb = x_ref[...].astype(BF16)
    ang = pos_ref[...].astype(F32) * freq_ref[...]
    s = jnp.sin(ang)
    c, sa, sb = jnp.cos(ang), s * sa_ref[...], s * sb_ref[...]
    for j in range(w_ref.shape[1] // IN_BLOCK):
        y = jnp.dot(xb, w_ref[:, j * IN_BLOCK:(j + 1) * IN_BLOCK], preferred_element_type=F32)
        for k in range(IN_BLOCK // LANES):
            piece = y[:, k * LANES:(k + 1) * LANES]
            if j in _Q_BLOCKS:
                piece = _rope(piece, c, sa, sb) * SWA_HEAD_DIM ** -0.5
            elif j == _KV_BLOCK and k < IN_BLOCK // LANES // 2:
                piece = _rope(piece, c, sa, sb)
            o_ref[:, j * IN_BLOCK + k * LANES:j * IN_BLOCK + (k + 1) * LANES] = piece.astype(o_ref.dtype)


def _rope_constants():
    lane = np.arange(LANES)
    d = lane % SWA_HEAD_DIM
    half = ROPE_DIM // 2
    inv_freq = np.float32(ROPE_THETA) ** (-np.arange(0, ROPE_DIM, 2, dtype=np.float32) / np.float32(ROPE_DIM))
    freq = np.where(d < ROPE_DIM, inv_freq[d % half], 0.0).astype(np.float32)
    sa = np.where(d < half, -1.0, 0.0).astype(np.float32)
    sb = np.where((d >= half) & (d < ROPE_DIM), 1.0, 0.0).astype(np.float32)
    return [jnp.asarray(a.reshape(1, LANES)) for a in (freq, sa, sb)]


def _inproj(x2, w, positions, tm):
    t, d = x2.shape
    n = w.shape[1]
    tm = min(tm, t)
    freq, sa, sb = _rope_constants()
    const = pl.BlockSpec((1, LANES), lambda i: (0, 0))
    return pl.pallas_call(
        _inproj_kernel,
        grid=(t // tm,),
        in_specs=[pl.BlockSpec((tm, d), lambda i: (i, 0)), pl.BlockSpec((d, n), lambda i: (0, 0)),
                  pl.BlockSpec((tm, 1), lambda i: (i, 0)), const, const, const],
        out_specs=pl.BlockSpec((tm, n), lambda i: (i, 0)),
        out_shape=jax.ShapeDtypeStruct((t, n), BF16),
        compiler_params=_params("parallel"),
        name="inproj_rope",
    )(x2, w, positions.reshape(t, 1), freq, sa, sb)


def _pool_kernel(v_ref, w_ref, scale_ref, o_ref, ext_ref, *, ts):
    s = pl.program_id(1)

    @pl.when(s == 0)
    def _():
        ext_ref[0:POOL_HALO, :] = jnp.zeros((POOL_HALO, ext_ref.shape[1]), F32)

    ext_ref[POOL_HALO:POOL_HALO + ts, :] = v_ref[0].astype(F32)
    pos = s * ts + lax.broadcasted_iota(jnp.int32, (ts, 1), 0)
    for g, w in enumerate(POOL_WINDOWS):
        cols = slice(g * POOL_GROUP, (g + 1) * POOL_GROUP)
        acc = ext_ref[POOL_HALO:POOL_HALO + ts, cols]
        for k in range(1, w):
            acc = acc + ext_ref[POOL_HALO - k:POOL_HALO - k + ts, cols]
        count = jnp.minimum(pos + 1, w).astype(F32)
        pooled = acc / count - ext_ref[POOL_HALO:POOL_HALO + ts, cols]
        y = jnp.dot(pooled.astype(BF16), w_ref[g], preferred_element_type=F32)
        o_ref[0, :, cols] = (y * scale_ref[:, cols]).astype(o_ref.dtype)
    ext_ref[0:POOL_HALO, :] = ext_ref[ts:ts + POOL_HALO, :]


def _pool(h, w_pool, pool_scale, ts):
    b, s, _ = h.shape
    width = POOL_GROUP * len(POOL_WINDOWS)
    ts = min(ts, s)
    return pl.pallas_call(
        functools.partial(_pool_kernel, ts=ts),
        grid=(b, s // ts),
        in_specs=[
            pl.BlockSpec((1, ts, width), lambda i, j: (i, j, 0)),
            pl.BlockSpec(w_pool.shape, lambda i, j: (0, 0, 0)),
            pl.BlockSpec((1, width), lambda i, j: (0, 0)),
        ],
        out_specs=pl.BlockSpec((1, ts, width), lambda i, j: (i, j, 0)),
        out_shape=jax.ShapeDtypeStruct((b, s, width), BF16),
        scratch_shapes=[pltpu.VMEM((ts + POOL_HALO, width), F32)],
        compiler_params=_params("arbitrary", "arbitrary"),
        name="pool",
    )(h, w_pool, pool_scale)


def _swa_kernel(sink_ref, q0_ref, q1_ref, kv_ref, kvp_ref, o_ref):
    n = pl.program_id(1)
    kvw = SWA_KV_HEADS * SWA_HEAD_DIM
    q = jnp.concatenate([q0_ref[0], q1_ref[0]], axis=1)
    k = jnp.concatenate([kvp_ref[0, :, 0:kvw], kv_ref[0, :, 0:kvw]], axis=0)
    v = jnp.concatenate([kvp_ref[0, :, kvw:2 * kvw], kv_ref[0, :, kvw:2 * kvw]], axis=0)
    row = lax.broadcasted_iota(jnp.int32, (SWA_BLOCK, 2 * SWA_BLOCK), 0)
    col = lax.broadcasted_iota(jnp.int32, (SWA_BLOCK, 2 * SWA_BLOCK), 1)
    rel = row + SWA_BLOCK - col
    valid = (rel >= 0) & (rel < SWA_BLOCK) & ((col >= SWA_BLOCK) | (n > 0))
    group = SWA_HEADS // SWA_KV_HEADS
    outs = []
    for hq in range(SWA_HEADS):
        kv = hq // group
        qh = q[:, hq * SWA_HEAD_DIM:(hq + 1) * SWA_HEAD_DIM]
        kh = k[:, kv * SWA_HEAD_DIM:(kv + 1) * SWA_HEAD_DIM]
        vh = v[:, kv * SWA_HEAD_DIM:(kv + 1) * SWA_HEAD_DIM]
        sc = lax.dot_general(qh, kh, (((1,), (1,)), ((), ())), preferred_element_type=F32)
        sc = jnp.where(valid, sc, NEG)
        sink = sink_ref[hq]
        m = jnp.maximum(jnp.max(sc, axis=1, keepdims=True), sink)
        p = jnp.exp(sc - m)
        denom = jnp.sum(p, axis=1, keepdims=True) + jnp.exp(sink - m)
        o = jnp.dot(p.astype(BF16), vh, preferred_element_type=F32)
        outs.append(o / denom)
    o_ref[0] = jnp.concatenate(outs, axis=1).astype(o_ref.dtype)


def _swa(h, sinks):
    b, s, _ = h.shape
    nb = s // SWA_BLOCK
    blk = lambda c: pl.BlockSpec((1, SWA_BLOCK, IN_BLOCK), lambda i, j: (i, j, c))
    blk_prev = lambda c: pl.BlockSpec((1, SWA_BLOCK, IN_BLOCK), lambda i, j: (i, jnp.maximum(j - 1, 0), c))
    return pl.pallas_call(
        _swa_kernel,
        grid=(b, nb),
        in_specs=[pl.BlockSpec(memory_space=pltpu.SMEM),
                  blk(_Q_BLOCKS[0]), blk(_Q_BLOCKS[1]), blk(_KV_BLOCK), blk_prev(_KV_BLOCK)],
        out_specs=pl.BlockSpec((1, SWA_BLOCK, SWA_HEADS * SWA_HEAD_DIM), lambda i, j: (i, j, 0)),
        out_shape=jax.ShapeDtypeStruct((b, s, SWA_HEADS * SWA_HEAD_DIM), BF16),
        compiler_params=_params("parallel", "arbitrary"),
        name="swa",
    )(sinks, h, h, h, h)


def _mem_kernel(q_ref, kv_ref, o_ref):
    scale = MEM_HEAD_DIM ** -0.5
    width = MEM_HEADS * MEM_HEAD_DIM
    for hm in range(MEM_HEADS):
        cols = slice(hm * MEM_HEAD_DIM, (hm + 1) * MEM_HEAD_DIM)
        km = kv_ref[0, :, cols]
        vm = kv_ref[0, :, width + hm * MEM_HEAD_DIM:width + (hm + 1) * MEM_HEAD_DIM]
        sc = lax.dot_general(q_ref[0, :, cols], km, (((1,), (1,)), ((), ())), preferred_element_type=F32) * scale
        m = jnp.max(sc, axis=1, keepdims=True)
        p = jnp.exp(sc - m)
        denom = jnp.sum(p, axis=1, keepdims=True)
        o = jnp.dot(p.astype(BF16), vm, preferred_element_type=F32)
        o_ref[0, :, cols] = (o / denom).astype(o_ref.dtype)


def _mem_attention(h, kvm, tq):
    b, s, _ = h.shape
    width = MEM_HEADS * MEM_HEAD_DIM
    tq = min(tq, s)
    return pl.pallas_call(
        _mem_kernel,
        grid=(b, s // tq),
        in_specs=[
            pl.BlockSpec((1, tq, width), lambda i, j: (i, j, 4)),
            pl.BlockSpec((1,) + kvm.shape[1:], lambda i, j: (i, 0, 0)),
        ],
        out_specs=pl.BlockSpec((1, tq, width), lambda i, j: (i, j, 0)),
        out_shape=jax.ShapeDtypeStruct((b, s, width), BF16),
        compiler_params=_params("parallel", "arbitrary"),
        name="mem_attention",
    )(h, kvm)


def _outproj_kernel(pool_ref, swa_ref, mem_ref, wp_ref, ws_ref, wm_ref, x_ref, g_ref, b_ref, o_ref, ob_ref,
                    *, alpha):
    mix = jnp.dot(pool_ref[...], wp_ref[...], preferred_element_type=F32)
    mix += jnp.dot(swa_ref[...], ws_ref[...], preferred_element_type=F32)
    mix += jnp.dot(mem_ref[...], wm_ref[...], preferred_element_type=F32)
    z = alpha * x_ref[...] + mix
    mu = jnp.mean(z, axis=1, keepdims=True)
    zc = z - mu
    var = jnp.mean(zc * zc, axis=1, keepdims=True)
    y = zc * lax.rsqrt(var + LN_EPS) * g_ref[...] + b_ref[...]
    yt = y.T
    o_ref[...] = yt
    ob_ref[...] = yt.astype(BF16)


def _outproj_ln(pool_o, swa_o, mem_o, w_out, x2, g, b, alpha, tm):
    t, d = x2.shape
    tm = min(tm, t)
    wp, ws, wm = pool_o.shape[1], swa_o.shape[1], mem_o.shape[1]
    w_p, w_s, w_m = w_out[:wp], w_out[wp:wp + ws], w_out[wp + ws:]
    row = lambda w: pl.BlockSpec((tm, w), lambda i: (i, 0))
    full = lambda a: pl.BlockSpec(a.shape, lambda i: (0, 0))
    return pl.pallas_call(
        functools.partial(_outproj_kernel, alpha=alpha),
        grid=(t // tm,),
        in_specs=[row(wp), row(ws), row(wm), full(w_p), full(w_s), full(w_m), row(d),
                  pl.BlockSpec((1, d), lambda i: (0, 0)), pl.BlockSpec((1, d), lambda i: (0, 0))],
        out_specs=[pl.BlockSpec((d, tm), lambda i: (0, i))] * 2,
        out_shape=[jax.ShapeDtypeStruct((d, t), F32), jax.ShapeDtypeStruct((d, t), BF16)],
        compiler_params=_params("parallel"),
        name="outproj_ln",
    )(pool_o, swa_o, mem_o, w_p, w_s, w_m, x2, g, b)


def _top16_rows(s, break_ties):
    n = s.shape[0]
    iota = lax.broadcasted_iota(jnp.int32, s.shape, 0).astype(F32) if break_ties else None
    rank = jnp.full(s.shape, float(PEER_TOPK), F32)
    vals = []
    for r in range(PEER_TOPK):
        m = jnp.max(s, axis=0, keepdims=True)
        hit = s == m
        if break_ties:
            idx = jnp.min(jnp.where(hit, iota, float(n)), axis=0, keepdims=True)
            hit = iota == idx
        rank = jnp.where(hit, float(r), rank)
        s = jnp.where(hit, -jnp.inf, s)
        vals.append(m)
    count = jnp.sum(jnp.where(rank < float(PEER_TOPK), 1.0, 0.0), axis=0, keepdims=True)
    return rank, vals, count


_CAND_SMALL_A = PEER_TOPK // 2
_CAND_ROWS = PEER_TOPK + (_CAND_SMALL_A - 1) * SUBLANES + SUBLANES


def _cand_constants():
    flat = np.full((_CAND_ROWS, 1), 1e9, np.float32)
    valid = np.zeros((_CAND_ROWS, 1), np.float32)
    for b in range(PEER_TOPK):
        flat[b, 0], valid[b, 0] = b, 1.0
    for a in range(1, _CAND_SMALL_A):
        base = PEER_TOPK + (a - 1) * SUBLANES
        for b in range(PEER_TOPK // (a + 1)):
            flat[base + b, 0], valid[base + b, 0] = a * PEER_TOPK + b, 1.0
    base = PEER_TOPK + (_CAND_SMALL_A - 1) * SUBLANES
    for k in range(SUBLANES):
        flat[base + k, 0], valid[base + k, 0] = (_CAND_SMALL_A + k) * PEER_TOPK, 1.0
    return jnp.asarray(flat), jnp.asarray(valid)


def _select_experts(s1, s2, flat, valid, break_ties):
    t = s1.shape[1]
    rank1, v1, count1 = _top16_rows(s1, break_ties)
    rank2, v2, count2 = _top16_rows(s2, break_ties)
    v2_lo = jnp.concatenate(v2[:SUBLANES], axis=0)
    v2_all = jnp.concatenate(v2, axis=0)
    v1_hi = jnp.concatenate(v1[_CAND_SMALL_A:], axis=0)
    groups = [v1[0] + v2_all]
    for a in range(1, _CAND_SMALL_A):
        groups.append(v1[a] + v2_lo)
    groups.append(v1_hi + v2[0])
    cand = jnp.concatenate(groups, axis=0)
    cand = jnp.where(valid > 0.5, cand, -jnp.inf)
    flat_b = jnp.broadcast_to(flat, cand.shape) if break_ties else None
    hits = jnp.zeros(cand.shape, F32)
    top = []
    for r in range(PEER_TOPK):
        m = jnp.max(cand, axis=0, keepdims=True)
        hit = cand == m
        if break_ties:
            pick = jnp.min(jnp.where(hit, flat_b, 2e9), axis=0, keepdims=True)
            hit = flat_b == pick
        hits = jnp.where(hit, 1.0, hits)
        cand = jnp.where(hit, -jnp.inf, cand)
        top.append(m)
    z = jnp.ones((1, t), F32)
    for r in range(1, PEER_TOPK):
        z = z + jnp.exp(top[r] - top[0])
    counts = [jnp.sum(hits[0:PEER_TOPK], axis=0, keepdims=True)]
    for a in range(1, _CAND_SMALL_A):
        base = PEER_TOPK + (a - 1) * SUBLANES
        counts.append(jnp.sum(hits[base:base + SUBLANES], axis=0, keepdims=True))
    base = PEER_TOPK + (_CAND_SMALL_A - 1) * SUBLANES
    for k in range(SUBLANES):
        counts.append(hits[base + k:base + k + 1])
    lim = jnp.zeros(s1.shape, F32)
    for a in range(PEER_TOPK):
        lim = jnp.where(rank1 == float(a), counts[a], lim)
    e1n = jnp.exp(s1 - v1[0]) / z
    e2 = jnp.exp(s2 - v2[0])
    count3 = jnp.sum(hits, axis=0, keepdims=True)
    k = float(PEER_TOPK)
    ok = jnp.where((count1 == k) & (count2 == k) & (count3 == k), 1.0, 0.0)
    return lim, e1n, rank2, e2, ok


def _retrieve_kernel(wq_ref, x_ref, k1_ref, k2_ref, flat_ref, valid_ref,
                     lim_out_ref, e1_out_ref, r2_ref, e2_ref, q_ref, lim_ref, e1_ref, *, tq):
    q_ref[...] = jnp.dot(wq_ref[...], x_ref[...], preferred_element_type=F32)
    flat, valid = flat_ref[...], valid_ref[...]

    def head(h, carry):
        for c in range(tq // SELECT_TOKENS):
            lanes = slice(c * SELECT_TOKENS, (c + 1) * SELECT_TOKENS)
            r0 = pl.multiple_of(h * 2 * HALF_DIM, 2 * HALF_DIM)
            q1 = q_ref[pl.ds(r0, HALF_DIM), lanes].astype(BF16)
            q2 = q_ref[pl.ds(r0 + HALF_DIM, HALF_DIM), lanes].astype(BF16)
            s1 = jnp.dot(k1_ref[...], q1, preferred_element_type=F32)
            s2 = jnp.dot(k2_ref[...], q2, preferred_element_type=F32)

            def emit(break_ties):
                lim, e1n, rank2, e2, ok = _select_experts(s1, s2, flat, valid, break_ties)
                lim_ref[h, :, lanes] = lim
                e1_ref[h, :, lanes] = e1n
                r2_ref[h, :, lanes] = rank2.astype(BF16)
                e2_ref[h, :, lanes] = e2.astype(BF16)
                return ok

            ok = emit(False)

            @pl.when(jnp.min(ok) < 0.5)
            def _():
                emit(True)
        return carry

    lax.fori_loop(0, PEER_HEADS, head, 0)
    lim_out_ref[...] = jnp.swapaxes(lim_ref[...], 0, 1)
    e1_out_ref[...] = jnp.swapaxes(e1_ref[...], 0, 1)


def _retrieve(x1t, wq_t, k1, k2, tq):
    d, t = x1t.shape
    tq = min(tq, t)
    flat, valid = _cand_constants()
    full = lambda a: pl.BlockSpec(a.shape, lambda i: (0,) * a.ndim)
    out_keys = jax.ShapeDtypeStruct((N_KEYS, PEER_HEADS, t), F32)
    key_spec = pl.BlockSpec((N_KEYS, PEER_HEADS, tq), lambda i: (0, 0, i))
    out_packed = jax.ShapeDtypeStruct((PEER_HEADS, N_KEYS, t), BF16)
    out_spec = pl.BlockSpec((PEER_HEADS, N_KEYS, tq), lambda i: (0, 0, i))
    return pl.pallas_call(
        functools.partial(_retrieve_kernel, tq=tq),
        grid=(t // tq,),
        in_specs=[full(wq_t), pl.BlockSpec((d, tq), lambda i: (0, i)), full(k1), full(k2), full(flat), full(valid)],
        out_specs=[key_spec, key_spec, out_spec, out_spec],
        out_shape=[out_keys, out_keys, out_packed, out_packed],
        scratch_shapes=[pltpu.VMEM((wq_t.shape[0], tq), F32)] + [pltpu.VMEM((PEER_HEADS, N_KEYS, tq), F32)] * 2,
        compiler_params=_params("parallel"),
        name="peer_retrieve",
    )(wq_t, x1t, k1, k2, flat, valid)


def _gelu_exact(x):
    return 0.5 * x * (1.0 + lax.erf(x * (1.0 / math.sqrt(2.0))))


def _experts_kernel(xb_ref, u_ref, vt_ref, lim_ref, e1_ref, r2_in_ref, e2_in_ref, o_ref,
                    h_ref, a_ref, rows_ref, gate_ref, r2_ref, e2_ref, *, tm, te):
    e = pl.program_id(1)

    @pl.when(e == 0)
    def _():
        o_ref[...] = jnp.zeros(o_ref.shape, F32)
        r2_ref[:, 0:tm] = r2_in_ref[...]
        e2_ref[:, LANES:LANES + tm] = e2_in_ref[...]

    n_chunks = te // EXPERT_CHUNK

    def rows_of(p):
        return slice(p * EXPERT_CHUNK, (p + 1) * EXPERT_CHUNK)

    def gates(first_key):
        keys = range(first_key, first_key + GATE_KEYS)
        blocks = range(0, N_KEYS, GATE_ROWS)
        for i in keys:
            for hd in range(PEER_HEADS):
                for q, ref in enumerate((lim_ref, e1_ref)):
                    row = jnp.broadcast_to(ref[i, hd:hd + 1, :], (GATE_ROWS, tm)).astype(BF16)
                    rows_ref[q, i, hd, :, q * LANES:q * LANES + tm] = row
        for c in range(tm // LANES):
            lanes = slice(c * LANES, (c + 1) * LANES)
            lanes1 = slice((c + 1) * LANES, (c + 2) * LANES)
            gate = {i: {jb: jnp.zeros((GATE_ROWS, LANES), BF16) for jb in blocks} for i in keys}
            for hd in range(PEER_HEADS):
                lim = {i: rows_ref[0, i, hd, :, lanes] for i in keys}
                e1 = {i: rows_ref[1, i, hd, :, lanes1] for i in keys}
                for jb in blocks:
                    r2 = r2_ref[hd * N_KEYS + jb:hd * N_KEYS + jb + GATE_ROWS, lanes]
                    e2 = e2_ref[hd * N_KEYS + jb:hd * N_KEYS + jb + GATE_ROWS, lanes1]
                    for i in keys:
                        gate[i][jb] = gate[i][jb] + jnp.where(r2 < lim[i], e2 * e1[i], jnp.zeros_like(e2))
            for i in keys:
                for jb in blocks:
                    gate_ref[i * N_KEYS + jb:i * N_KEYS + jb + GATE_ROWS, lanes] = gate[i][jb]

    for first_key in range(0, te // N_KEYS, GATE_KEYS):
        gates(first_key)
    h_ref[...] = jnp.dot(u_ref[...], xb_ref[...], preferred_element_type=F32)
    for p in range(n_chunks):
        a_ref[rows_of(p), :] = gate_ref[rows_of(p), :] * _gelu_exact(h_ref[rows_of(p), :]).astype(BF16)
        o_ref[...] += jnp.dot(vt_ref[:, rows_of(p)], a_ref[rows_of(p), :], preferred_element_type=F32)


def _experts(xbt, u, vt, sel, tm, te):
    d, t = xbt.shape
    n_exp = u.shape[0]
    tm, te = min(tm, t), min(te, n_exp)
    tok = pl.BlockSpec((d, tm), lambda i, j: (0, i))
    sel_spec = pl.BlockSpec((PEER_HEADS * N_KEYS, tm), lambda i, j: (0, i))
    key_spec = pl.BlockSpec((te // N_KEYS, PEER_HEADS, tm), lambda i, j: (j, 0, i))
    lim, e1n, rank2, e2 = sel
    sel = (lim, e1n, rank2.reshape(-1, t), e2.reshape(-1, t))
    return pl.pallas_call(
        functools.partial(_experts_kernel, tm=tm, te=te),
        grid=(t // tm, n_exp // te),
        in_specs=[tok, pl.BlockSpec((te, d), lambda i, j: (j, 0)), pl.BlockSpec((d, te), lambda i, j: (0, j)),
                  key_spec, key_spec, sel_spec, sel_spec],
        out_specs=tok,
        out_shape=jax.ShapeDtypeStruct((d, t), F32),
        scratch_shapes=[pltpu.VMEM((te, tm), F32), pltpu.VMEM((te, tm), BF16),
                        pltpu.VMEM((2, te // N_KEYS, PEER_HEADS, GATE_ROWS, tm + LANES), BF16),
                        pltpu.VMEM((te, tm), BF16),
                        pltpu.VMEM((PEER_HEADS * N_KEYS, tm + LANES), BF16),
                        pltpu.VMEM((PEER_HEADS * N_KEYS, tm + LANES), BF16)],
        compiler_params=_params("parallel", "arbitrary"),
        name="peer_experts",
    )(xbt, u, vt, *sel)


def _ln_t_kernel(x_ref, f_ref, g_ref, b_ref, o_ref, *, alpha):
    z = (alpha * x_ref[...] + f_ref[...]).T
    mu = jnp.mean(z, axis=1, keepdims=True)
    zc = z - mu
    var = jnp.mean(zc * zc, axis=1, keepdims=True)
    o_ref[...] = zc * lax.rsqrt(var + LN_EPS) * g_ref[...] + b_ref[...]


def _residual_ln_t(x1t, fft, g, b, alpha, tm):
    d, t = x1t.shape
    tm = min(tm, t)
    tok = pl.BlockSpec((d, tm), lambda i: (0, i))
    vec = pl.BlockSpec((1, d), lambda i: (0, 0))
    return pl.pallas_call(
        functools.partial(_ln_t_kernel, alpha=alpha),
        grid=(t // tm,),
        in_specs=[tok, tok, vec, vec],
        out_specs=pl.BlockSpec((tm, d), lambda i: (i, 0)),
        out_shape=jax.ShapeDtypeStruct((t, d), F32),
        compiler_params=_params("parallel"),
        name="residual_ln",
    )(x1t, fft, g, b)


def kernel(x, mem, positions, w_in, w_mem_kv, w_pool, pool_scale, attn_sinks, w_out, ln1_g, ln1_b,
           w_peer_q, sub_keys_1, sub_keys_2, expert_u, expert_v, ln2_g, ln2_b):
    bsz, seq, d = x.shape
    depth = w_in.shape[0]
    t = bsz * seq
    alpha = (2.0 * depth) ** 0.25
    for l in range(depth):
        x2 = x.reshape(t, d)
        h = _inproj(x2, w_in[l].astype(BF16), positions, 512).reshape(bsz, seq, -1)
        mem2 = mem.reshape(-1, d).astype(BF16)
        kvm = _matmul(mem2, w_mem_kv[l].astype(BF16), BF16, 512, 512).reshape(bsz, mem.shape[1], -1)
        pool_o = _pool(h, w_pool[l].astype(BF16), pool_scale[l].reshape(1, -1), 512)
        swa_o = _swa(h, attn_sinks[l])
        mem_o = _mem_attention(h, kvm, 512)
        x1t, x1bt = _outproj_ln(pool_o.reshape(t, -1), swa_o.reshape(t, -1), mem_o.reshape(t, -1),
                          w_out[l].astype(BF16), x2, ln1_g[l].reshape(1, d), ln1_b[l].reshape(1, d), alpha, 512)
        sel = _retrieve(x1bt, w_peer_q[l].T.astype(BF16), sub_keys_1[l].astype(BF16),
                        sub_keys_2[l].astype(BF16), 512)
        fft = _experts(x1bt, expert_u[l].astype(BF16), expert_v[l].T.astype(BF16), sel, 1024, 512)
        x = _residual_ln_t(x1t, fft, ln2_g[l].reshape(1, d), ln2_b[l].reshape(1, d), alpha, 512).reshape(bsz, seq, d)
    return x
```

```python
import functools
import math

import numpy as np
import jax
import jax.numpy as jnp
from jax import lax
from jax.experimental import pallas as pl
from jax.experimental.pallas import tpu as pltpu

F32 = jnp.float32
BF16 = jnp.bfloat16

LANES = 128
SUBLANES = 8
VMEM_LIMIT_BYTES = 56 * 1024 * 1024

POOL_WINDOWS = (2, 4, 8, 16)
POOL_GROUP = 128
POOL_HALO = 16
SWA_HEAD_DIM = 64
SWA_HEADS = 16
SWA_KV_HEADS = 4
SWA_BLOCK = 128
ROPE_THETA = 500000.0
ROPE_DIM = 16
MEM_HEADS = 4
MEM_HEAD_DIM = 128
PEER_HEADS = 8
N_KEYS = 128
PEER_TOPK = 16
HALF_DIM = 128
LN_EPS = 1e-5
NEG = -1e30

EXPERT_CHUNK = 512
GATE_KEYS = 4
GATE_ROWS = 16
SELECT_TOKENS = 512


def _params(*semantics):
    return pltpu.CompilerParams(dimension_semantics=semantics, vmem_limit_bytes=VMEM_LIMIT_BYTES)


def _matmul_kernel(a_ref, b_ref, o_ref):
    o_ref[...] = jnp.dot(a_ref[...], b_ref[...], preferred_element_type=F32).astype(o_ref.dtype)


def _matmul(a, b, out_dtype, tm, tn):
    m, k = a.shape
    n = b.shape[1]
    tm, tn = min(tm, m), min(tn, n)
    return pl.pallas_call(
        _matmul_kernel,
        grid=(m // tm, n // tn),
        in_specs=[pl.BlockSpec((tm, k), lambda i, j: (i, 0)), pl.BlockSpec((k, tn), lambda i, j: (0, j))],
        out_specs=pl.BlockSpec((tm, tn), lambda i, j: (i, j)),
        out_shape=jax.ShapeDtypeStruct((m, n), out_dtype),
        compiler_params=_params("parallel", "arbitrary"),
        name="matmul",
    )(a, b)


IN_BLOCK = 512
_Q_BLOCKS = (1, 2)
_KV_BLOCK = 3


def _rope(x, c, sa, sb):
    half = ROPE_DIM // 2
    return x * c + pltpu.roll(x, LANES - half, 1) * sa + pltpu.roll(x, half, 1) * sb


def _inproj_kernel(x_ref, w_ref, pos_ref, freq_ref, sa_ref, sb_ref, o_ref):
    xb = x_ref[...].astype(BF16)
    ang = pos_ref[...].astype(F32) * freq_ref[...]
    s = jnp.sin(ang)
    c, sa, sb = jnp.cos(ang), s * sa_ref[...], s * sb_ref[...]
    for j in range(w_ref.shape[1] // IN_BLOCK):
        y = jnp.dot(xb, w_ref[:, j * IN_BLOCK:(j + 1) * IN_BLOCK], preferred_element_type=F32)
        for k in range(IN_BLOCK // LANES):
            piece = y[:, k * LANES:(k + 1) * LANES]
            if j in _Q_BLOCKS:
                piece = _rope(piece, c, sa, sb) * SWA_HEAD_DIM ** -0.5
            elif j == _KV_BLOCK and k < IN_BLOCK // LANES // 2:
                piece = _rope(piece, c, sa, sb)
            o_ref[:, j * IN_BLOCK + k * LANES:j * IN_BLOCK + (k + 1) * LANES] = piece.astype(o_ref.dtype)


def _rope_constants():
    lane = np.arange(LANES)
    d = lane % SWA_HEAD_DIM
    half = ROPE_DIM // 2
    inv_freq = np.float32(ROPE_THETA) ** (-np.arange(0, ROPE_DIM, 2, dtype=np.float32) / np.float32(ROPE_DIM))
    freq = np.where(d < ROPE_DIM, inv_freq[d % half], 0.0).astype(np.float32)
    sa = np.where(d < half, -1.0, 0.0).astype(np.float32)
    sb = np.where((d >= half) & (d < ROPE_DIM), 1.0, 0.0).astype(np.float32)
    return [jnp.asarray(a.reshape(1, LANES)) for a in (freq, sa, sb)]


def _inproj(x2, w, positions, tm):
    t, d = x2.shape
    n = w.shape[1]
    tm = min(tm, t)
    freq, sa, sb = _rope_constants()
    const = pl.BlockSpec((1, LANES), lambda i: (0, 0))
    return pl.pallas_call(
        _inproj_kernel,
        grid=(t // tm,),
        in_specs=[pl.BlockSpec((tm, d), lambda i: (i, 0)), pl.BlockSpec((d, n), lambda i: (0, 0)),
                  pl.BlockSpec((tm, 1), lambda i: (i, 0)), const, const, const],
        out_specs=pl.BlockSpec((tm, n), lambda i: (i, 0)),
        out_shape=jax.ShapeDtypeStruct((t, n), BF16),
        compiler_params=_params("parallel"),
        name="inproj_rope",
    )(x2, w, positions.reshape(t, 1), freq, sa, sb)


def _pool_kernel(v_ref, w_ref, scale_ref, o_ref, ext_ref, *, ts):
    s = pl.program_id(1)

    @pl.when(s == 0)
    def _():
        ext_ref[0:POOL_HALO, :] = jnp.zeros((POOL_HALO, ext_ref.shape[1]), F32)

    ext_ref[POOL_HALO:POOL_HALO + ts, :] = v_ref[0].astype(F32)
    pos = s * ts + lax.broadcasted_iota(jnp.int32, (ts, 1), 0)
    for g, w in enumerate(POOL_WINDOWS):
        cols = slice(g * POOL_GROUP, (g + 1) * POOL_GROUP)
        acc = ext_ref[POOL_HALO:POOL_HALO + ts, cols]
        for k in range(1, w):
            acc = acc + ext_ref[POOL_HALO - k:POOL_HALO - k + ts, cols]
        count = jnp.minimum(pos + 1, w).astype(F32)
        pooled = acc / count - ext_ref[POOL_HALO:POOL_HALO + ts, cols]
        y = jnp.dot(pooled.astype(BF16), w_ref[g], preferred_element_type=F32)
        o_ref[0, :, cols] = (y * scale_ref[:, cols]).astype(o_ref.dtype)
    ext_ref[0:POOL_HALO, :] = ext_ref[ts:ts + POOL_HALO, :]


def _pool(h, w_pool, pool_scale, ts):
    b, s, _ = h.shape
    width = POOL_GROUP * len(POOL_WINDOWS)
    ts = min(ts, s)
    return pl.pallas_call(
        functools.partial(_pool_kernel, ts=ts),
        grid=(b, s // ts),
        in_specs=[
            pl.BlockSpec((1, ts, width), lambda i, j: (i, j, 0)),
            pl.BlockSpec(w_pool.shape, lambda i, j: (0, 0, 0)),
            pl.BlockSpec((1, width), lambda i, j: (0, 0)),
        ],
        out_specs=pl.BlockSpec((1, ts, width), lambda i, j: (i, j, 0)),
        out_shape=jax.ShapeDtypeStruct((b, s, width), BF16),
        scratch_shapes=[pltpu.VMEM((ts + POOL_HALO, width), F32)],
        compiler_params=_params("arbitrary", "arbitrary"),
        name="pool",
    )(h, w_pool, pool_scale)


def _swa_kernel(sink_ref, q0_ref, q1_ref, kv_ref, kvp_ref, o_ref):
    n = pl.program_id(1)
    kvw = SWA_KV_HEADS * SWA_HEAD_DIM
    q = jnp.concatenate([q0_ref[0], q1_ref[0]], axis=1)
    k = jnp.concatenate([kvp_ref[0, :, 0:kvw], kv_ref[0, :, 0:kvw]], axis=0)
    v = jnp.concatenate([kvp_ref[0, :, kvw:2 * kvw], kv_ref[0, :, kvw:2 * kvw]], axis=0)
    row = lax.broadcasted_iota(jnp.int32, (SWA_BLOCK, 2 * SWA_BLOCK), 0)
    col = lax.broadcasted_iota(jnp.int32, (SWA_BLOCK, 2 * SWA_BLOCK), 1)
    rel = row + SWA_BLOCK - col
    valid = (rel >= 0) & (rel < SWA_BLOCK) & ((col >= SWA_BLOCK) | (n > 0))
    group = SWA_HEADS // SWA_KV_HEADS
    outs = []
    for hq in range(SWA_HEADS):
        kv = hq // group
        qh = q[:, hq * SWA_HEAD_DIM:(hq + 1) * SWA_HEAD_DIM]
        kh = k[:, kv * SWA_HEAD_DIM:(kv + 1) * SWA_HEAD_DIM]
        vh = v[:, kv * SWA_HEAD_DIM:(kv + 1) * SWA_HEAD_DIM]
        sc = lax.dot_general(qh, kh, (((1,), (1,)), ((), ())), preferred_element_type=F32)
        sc = jnp.where(valid, sc, NEG)
        sink = sink_ref[hq]
        m = jnp.maximum(jnp.max(sc, axis=1, keepdims=True), sink)
        p = jnp.exp(sc - m)
        denom = jnp.sum(p, axis=1, keepdims=True) + jnp.exp(sink - m)
        o = jnp.dot(p.astype(BF16), vh, preferred_element_type=F32)
        outs.append(o / denom)
    o_ref[0] = jnp.concatenate(outs, axis=1).astype(o_ref.dtype)


def _swa(h, sinks):
    b, s, _ = h.shape
    nb = s // SWA_BLOCK
    blk = lambda c: pl.BlockSpec((1, SWA_BLOCK, IN_BLOCK), lambda i, j: (i, j, c))
    blk_prev = lambda c: pl.BlockSpec((1, SWA_BLOCK, IN_BLOCK), lambda i, j: (i, jnp.maximum(j - 1, 0), c))
    return pl.pallas_call(
        _swa_kernel,
        grid=(b, nb),
        in_specs=[pl.BlockSpec(memory_space=pltpu.SMEM),
                  blk(_Q_BLOCKS[0]), blk(_Q_BLOCKS[1]), blk(_KV_BLOCK), blk_prev(_KV_BLOCK)],
        out_specs=pl.BlockSpec((1, SWA_BLOCK, SWA_HEADS * SWA_HEAD_DIM), lambda i, j: (i, j, 0)),
        out_shape=jax.ShapeDtypeStruct((b, s, SWA_HEADS * SWA_HEAD_DIM), BF16),
        compiler_params=_params("parallel", "arbitrary"),
        name="swa",
    )(sinks, h, h, h, h)


def _mem_kernel(q_ref, kv_ref, o_ref):
    scale = MEM_HEAD_DIM ** -0.5
    width = MEM_HEADS * MEM_HEAD_DIM
    for hm in range(MEM_HEADS):
        cols = slice(hm * MEM_HEAD_DIM, (hm + 1) * MEM_HEAD_DIM)
        km = kv_ref[0, :, cols]
        vm = kv_ref[0, :, width + hm * MEM_HEAD_DIM:width + (hm + 1) * MEM_HEAD_DIM]
        sc = lax.dot_general(q_ref[0, :, cols], km, (((1,), (1,)), ((), ())), preferred_element_type=F32) * scale
        m = jnp.max(sc, axis=1, keepdims=True)
        p = jnp.exp(sc - m)
        denom = jnp.sum(p, axis=1, keepdims=True)
        o = jnp.dot(p.astype(BF16), vm, preferred_element_type=F32)
        o_ref[0, :, cols] = (o / denom).astype(o_ref.dtype)


def _mem_attention(h, kvm, tq):
    b, s, _ = h.shape
    width = MEM_HEADS * MEM_HEAD_DIM
    tq = min(tq, s)
    return pl.pallas_call(
        _mem_kernel,
        grid=(b, s // tq),
        in_specs=[
            pl.BlockSpec((1, tq, width), lambda i, j: (i, j, 4)),
            pl.BlockSpec((1,) + kvm.shape[1:], lambda i, j: (i, 0, 0)),
        ],
        out_specs=pl.BlockSpec((1, tq, width), lambda i, j: (i, j, 0)),
        out_shape=jax.ShapeDtypeStruct((b, s, width), BF16),
        compiler_params=_params("parallel", "arbitrary"),
        name="mem_attention",
    )(h, kvm)


def _outproj_kernel(pool_ref, swa_ref, mem_ref, wp_ref, ws_ref, wm_ref, x_ref, g_ref, b_ref, o_ref, ob_ref,
                    *, alpha):
    mix = jnp.dot(pool_ref[...], wp_ref[...], preferred_element_type=F32)
    mix += jnp.dot(swa_ref[...], ws_ref[...], preferred_element_type=F32)
    mix += jnp.dot(mem_ref[...], wm_ref[...], preferred_element_type=F32)
    z = alpha * x_ref[...] + mix
    mu = jnp.mean(z, axis=1, keepdims=True)
    zc = z - mu
    var = jnp.mean(zc * zc, axis=1, keepdims=True)
    y = zc * lax.rsqrt(var + LN_EPS) * g_ref[...] + b_ref[...]
    yt = y.T
    o_ref[...] = yt
    ob_ref[...] = yt.astype(BF16)


def _outproj_ln(pool_o, swa_o, mem_o, w_out, x2, g, b, alpha, tm):
    t, d = x2.shape
    tm = min(tm, t)
    wp, ws, wm = pool_o.shape[1], swa_o.shape[1], mem_o.shape[1]
    w_p, w_s, w_m = w_out[:wp], w_out[wp:wp + ws], w_out[wp + ws:]
    row = lambda w: pl.BlockSpec((tm, w), lambda i: (i, 0))
    full = lambda a: pl.BlockSpec(a.shape, lambda i: (0, 0))
    return pl.pallas_call(
        functools.partial(_outproj_kernel, alpha=alpha),
        grid=(t // tm,),
        in_specs=[row(wp), row(ws), row(wm), full(w_p), full(w_s), full(w_m), row(d),
                  pl.BlockSpec((1, d), lambda i: (0, 0)), pl.BlockSpec((1, d), lambda i: (0, 0))],
        out_specs=[pl.BlockSpec((d, tm), lambda i: (0, i))] * 2,
        out_shape=[jax.ShapeDtypeStruct((d, t), F32), jax.ShapeDtypeStruct((d, t), BF16)],
        compiler_params=_params("parallel"),
        name="outproj_ln",
    )(pool_o, swa_o, mem_o, w_p, w_s, w_m, x2, g, b)


def _top16_rows(s, break_ties):
    n = s.shape[0]
    iota = lax.broadcasted_iota(jnp.int32, s.shape, 0).astype(F32) if break_ties else None
    rank = jnp.full(s.shape, float(PEER_TOPK), F32)
    vals = []
    for r in range(PEER_TOPK):
        m = jnp.max(s, axis=0, keepdims=True)
        hit = s == m
        if break_ties:
            idx = jnp.min(jnp.where(hit, iota, float(n)), axis=0, keepdims=True)
            hit = iota == idx
        rank = jnp.where(hit, float(r), rank)
        s = jnp.where(hit, -jnp.inf, s)
        vals.append(m)
    count = jnp.sum(jnp.where(rank < float(PEER_TOPK), 1.0, 0.0), axis=0, keepdims=True)
    return rank, vals, count


_CAND_SMALL_A = PEER_TOPK // 2
_CAND_ROWS = PEER_TOPK + (_CAND_SMALL_A - 1) * SUBLANES + SUBLANES


def _cand_constants():
    flat = np.full((_CAND_ROWS, 1), 1e9, np.float32)
    valid = np.zeros((_CAND_ROWS, 1), np.float32)
    for b in range(PEER_TOPK):
        flat[b, 0], valid[b, 0] = b, 1.0
    for a in range(1, _CAND_SMALL_A):
        base = PEER_TOPK + (a - 1) * SUBLANES
        for b in range(PEER_TOPK // (a + 1)):
            flat[base + b, 0], valid[base + b, 0] = a * PEER_TOPK + b, 1.0
    base = PEER_TOPK + (_CAND_SMALL_A - 1) * SUBLANES
    for k in range(SUBLANES):
        flat[base + k, 0], valid[base + k, 0] = (_CAND_SMALL_A + k) * PEER_TOPK, 1.0
    return jnp.asarray(flat), jnp.asarray(valid)


def _select_experts(s1, s2, flat, valid, break_ties):
    t = s1.shape[1]
    rank1, v1, count1 = _top16_rows(s1, break_ties)
    rank2, v2, count2 = _top16_rows(s2, break_ties)
    v2_lo = jnp.concatenate(v2[:SUBLANES], axis=0)
    v2_all = jnp.concatenate(v2, axis=0)
    v1_hi = jnp.concatenate(v1[_CAND_SMALL_A:], axis=0)
    groups = [v1[0] + v2_all]
    for a in range(1, _CAND_SMALL_A):
        groups.append(v1[a] + v2_lo)
    groups.append(v1_hi + v2[0])
    cand = jnp.concatenate(groups, axis=0)
    cand = jnp.where(valid > 0.5, cand, -jnp.inf)
    flat_b = jnp.broadcast_to(flat, cand.shape) if break_ties else None
    hits = jnp.zeros(cand.shape, F32)
    top = []
    for r in range(PEER_TOPK):
        m = jnp.max(cand, axis=0, keepdims=True)
        hit = cand == m
        if break_ties:
            pick = jnp.min(jnp.where(hit, flat_b, 2e9), axis=0, keepdims=True)
            hit = flat_b == pick
        hits = jnp.where(hit, 1.0, hits)
        cand = jnp.where(hit, -jnp.inf, cand)
        top.append(m)
    z = jnp.ones((1, t), F32)
    for r in range(1, PEER_TOPK):
        z = z + jnp.exp(top[r] - top[0])
    counts = [jnp.sum(hits[0:PEER_TOPK], axis=0, keepdims=True)]
    for a in range(1, _CAND_SMALL_A):
        base = PEER_TOPK + (a - 1) * SUBLANES
        counts.append(jnp.sum(hits[base:base + SUBLANES], axis=0, keepdims=True))
    base = PEER_TOPK + (_CAND_SMALL_A - 1) * SUBLANES
    for k in range(SUBLANES):
        counts.append(hits[base + k:base + k + 1])
    lim = jnp.zeros(s1.shape, F32)
    for a in range(PEER_TOPK):
        lim = jnp.where(rank1 == float(a), counts[a], lim)
    e1n = jnp.exp(s1 - v1[0]) / z
    e2 = jnp.exp(s2 - v2[0])
    count3 = jnp.sum(hits, axis=0, keepdims=True)
    k = float(PEER_TOPK)
    ok = jnp.where((count1 == k) & (count2 == k) & (count3 == k), 1.0, 0.0)
    return lim, e1n, rank2, e2, ok


def _retrieve_kernel(wq_ref, x_ref, k1_ref, k2_ref, flat_ref, valid_ref,
                     lim_out_ref, e1_out_ref, r2_ref, e2_ref, q_ref, lim_ref, e1_ref, *, tq):
    q_ref[...] = jnp.dot(wq_ref[...], x_ref[...], preferred_element_type=F32)
    flat, valid = flat_ref[...], valid_ref[...]

    def head(h, carry):
        for c in range(tq // SELECT_TOKENS):
            lanes = slice(c * SELECT_TOKENS, (c + 1) * SELECT_TOKENS)
            r0 = pl.multiple_of(h * 2 * HALF_DIM, 2 * HALF_DIM)
            q1 = q_ref[pl.ds(r0, HALF_DIM), lanes].astype(BF16)
            q2 = q_ref[pl.ds(r0 + HALF_DIM, HALF_DIM), lanes].astype(BF16)
            s1 = jnp.dot(k1_ref[...], q1, preferred_element_type=F32)
            s2 = jnp.dot(k2_ref[...], q2, preferred_element_type=F32)

            def emit(break_ties):
                lim, e1n, rank2, e2, ok = _select_experts(s1, s2, flat, valid, break_ties)
                lim_ref[h, :, lanes] = lim
                e1_ref[h, :, lanes] = e1n
                r2_ref[h, :, lanes] = rank2.astype(BF16)
                e2_ref[h, :, lanes] = e2.astype(BF16)
                return ok

            ok = emit(False)

            @pl.when(jnp.min(ok) < 0.5)
            def _():
                emit(True)
        return carry

    lax.fori_loop(0, PEER_HEADS, head, 0)
    lim_out_ref[...] = jnp.swapaxes(lim_ref[...], 0, 1)
    e1_out_ref[...] = jnp.swapaxes(e1_ref[...], 0, 1)


def _retrieve(x1t, wq_t, k1, k2, tq):
    d, t = x1t.shape
    tq = min(tq, t)
    flat, valid = _cand_constants()
    full = lambda a: pl.BlockSpec(a.shape, lambda i: (0,) * a.ndim)
    out_keys = jax.ShapeDtypeStruct((N_KEYS, PEER_HEADS, t), F32)
    key_spec = pl.BlockSpec((N_KEYS, PEER_HEADS, tq), lambda i: (0, 0, i))
    out_packed = jax.ShapeDtypeStruct((PEER_HEADS, N_KEYS, t), BF16)
    out_spec = pl.BlockSpec((PEER_HEADS, N_KEYS, tq), lambda i: (0, 0, i))
    return pl.pallas_call(
        functools.partial(_retrieve_kernel, tq=tq),
        grid=(t // tq,),
        in_specs=[full(wq_t), pl.BlockSpec((d, tq), lambda i: (0, i)), full(k1), full(k2), full(flat), full(valid)],
        out_specs=[key_spec, key_spec, out_spec, out_spec],
        out_shape=[out_keys, out_keys, out_packed, out_packed],
        scratch_shapes=[pltpu.VMEM((wq_t.shape[0], tq), F32)] + [pltpu.VMEM((PEER_HEADS, N_KEYS, tq), F32)] * 2,
        compiler_params=_params("parallel"),
        name="peer_retrieve",
    )(wq_t, x1t, k1, k2, flat, valid)


def _gelu_exact(x):
    return 0.5 * x * (1.0 + lax.erf(x * (1.0 / math.sqrt(2.0))))


def _experts_kernel(xb_ref, u_ref, vt_ref, lim_ref, e1_ref, r2_in_ref, e2_in_ref, o_ref,
                    h_ref, a_ref, rows_ref, gate_ref, r2_ref, e2_ref, *, tm, te):
    e = pl.program_id(1)

    @pl.when(e == 0)
    def _():
        o_ref[...] = jnp.zeros(o_ref.shape, F32)
        r2_ref[:, 0:tm] = r2_in_ref[...]
        e2_ref[:, LANES:LANES + tm] = e2_in_ref[...]

    n_chunks = te // EXPERT_CHUNK

    def rows_of(p):
        return slice(p * EXPERT_CHUNK, (p + 1) * EXPERT_CHUNK)

    def gates(first_key):
        keys = range(first_key, first_key + GATE_KEYS)
        blocks = range(0, N_KEYS, GATE_ROWS)
        for i in keys:
            for hd in range(PEER_HEADS):
                for q, ref in enumerate((lim_ref, e1_ref)):
                    row = jnp.broadcast_to(ref[i, hd:hd + 1, :], (GATE_ROWS, tm)).astype(BF16)
                    rows_ref[q, i, hd, :, q * LANES:q * LANES + tm] = row
        for c in range(tm // LANES):
            lanes = slice(c * LANES, (c + 1) * LANES)
            lanes1 = slice((c + 1) * LANES, (c + 2) * LANES)
            gate = {i: {jb: jnp.zeros((GATE_ROWS, LANES), BF16) for jb in blocks} for i in keys}
            for hd in range(PEER_HEADS):
                lim = {i: rows_ref[0, i, hd, :, lanes] for i in keys}
                e1 = {i: rows_ref[1, i, hd, :, lanes1] for i in keys}
                for jb in blocks:
                    r2 = r2_ref[hd * N_KEYS + jb:hd * N_KEYS + jb + GATE_ROWS, lanes]
                    e2 = e2_ref[hd * N_KEYS + jb:hd * N_KEYS + jb + GATE_ROWS, lanes1]
                    for i in keys:
                        gate[i][jb] = gate[i][jb] + jnp.where(r2 < lim[i], e2 * e1[i], jnp.zeros_like(e2))
            for i in keys:
                for jb in blocks:
                    gate_ref[i * N_KEYS + jb:i * N_KEYS + jb + GATE_ROWS, lanes] = gate[i][jb]

    for first_key in range(0, te // N_KEYS, GATE_KEYS):
        gates(first_key)
    h_ref[...] = jnp.dot(u_ref[...], xb_ref[...], preferred_element_type=F32)
    for p in range(n_chunks):
        a_ref[rows_of(p), :] = gate_ref[rows_of(p), :] * _gelu_exact(h_ref[rows_of(p), :]).astype(BF16)
        o_ref[...] += jnp.dot(vt_ref[:, rows_of(p)], a_ref[rows_of(p), :], preferred_element_type=F32)


def _experts(xbt, u, vt, sel, tm, te):
    d, t = xbt.shape
    n_exp = u.shape[0]
    tm, te = min(tm, t), min(te, n_exp)
    tok = pl.BlockSpec((d, tm), lambda i, j: (0, i))
    sel_spec = pl.BlockSpec((PEER_HEADS * N_KEYS, tm), lambda i, j: (0, i))
    key_spec = pl.BlockSpec((te // N_KEYS, PEER_HEADS, tm), lambda i, j: (j, 0, i))
    lim, e1n, rank2, e2 = sel
    sel = (lim, e1n, rank2.reshape(-1, t), e2.reshape(-1, t))
    return pl.pallas_call(
        functools.partial(_experts_kernel, tm=tm, te=te),
        grid=(t // tm, n_exp // te),
        in_specs=[tok, pl.BlockSpec((te, d), lambda i, j: (j, 0)), pl.BlockSpec((d, te), lambda i, j: (0, j)),
                  key_spec, key_spec, sel_spec, sel_spec],
        out_specs=tok,
        out_shape=jax.ShapeDtypeStruct((d, t), F32),
        scratch_shapes=[pltpu.VMEM((te, tm), F32), pltpu.VMEM((te, tm), BF16),
                        pltpu.VMEM((2, te // N_KEYS, PEER_HEADS, GATE_ROWS, tm + LANES), BF16),
                        pltpu.VMEM((te, tm), BF16),
                        pltpu.VMEM((PEER_HEADS * N_KEYS, tm + LANES), BF16),
                        pltpu.VMEM((PEER_HEADS * N_KEYS, tm + LANES), BF16)],
        compiler_params=_params("parallel", "arbitrary"),
        name="peer_experts",
    )(xbt, u, vt, *sel)


def _ln_t_kernel(x_ref, f_ref, g_ref, b_ref, o_ref, *, alpha):
    z = (alpha * x_ref[...] + f_ref[...]).T
    mu = jnp.mean(z, axis=1, keepdims=True)
    zc = z - mu
    var = jnp.mean(zc * zc, axis=1, keepdims=True)
    o_ref[...] = zc * lax.rsqrt(var + LN_EPS) * g_ref[...] + b_ref[...]


def _residual_ln_t(x1t, fft, g, b, alpha, tm):
    d, t = x1t.shape
    tm = min(tm, t)
    tok = pl.BlockSpec((d, tm), lambda i: (0, i))
    vec = pl.BlockSpec((1, d), lambda i: (0, 0))
    return pl.pallas_call(
        functools.partial(_ln_t_kernel, alpha=alpha),
        grid=(t // tm,),
        in_specs=[tok, tok, vec, vec],
        out_specs=pl.BlockSpec((tm, d), lambda i: (i, 0)),
        out_shape=jax.ShapeDtypeStruct((t, d), F32),
        compiler_params=_params("parallel"),
        name="residual_ln",
    )(x1t, fft, g, b)


def kernel(x, mem, positions, w_in, w_mem_kv, w_pool, pool_scale, attn_sinks, w_out, ln1_g, ln1_b,
           w_peer_q, sub_keys_1, sub_keys_2, expert_u, expert_v, ln2_g, ln2_b):
    bsz, seq, d = x.shape
    depth = w_in.shape[0]
    t = bsz * seq
    alpha = (2.0 * depth) ** 0.25
    for l in range(depth):
        x2 = x.reshape(t, d)
        h = _inproj(x2, w_in[l].astype(BF16), positions, 512).reshape(bsz, seq, -1)
        mem2 = mem.reshape(-1, d).astype(BF16)
        kvm = _matmul(mem2, w_mem_kv[l].astype(BF16), BF16, 512, 512).reshape(bsz, mem.shape[1], -1)
        pool_o = _pool(h, w_pool[l].astype(BF16), pool_scale[l].reshape(1, -1), 512)
        swa_o = _swa(h, attn_sinks[l])
        mem_o = _mem_attention(h, kvm, 512)
        x1t, x1bt = _outproj_ln(pool_o.reshape(t, -1), swa_o.reshape(t, -1), mem_o.reshape(t, -1),
                          w_out[l].astype(BF16), x2, ln1_g[l].reshape(1, d), ln1_b[l].reshape(1, d), alpha, 512)
        sel = _retrieve(x1bt, w_peer_q[l].T.astype(BF16), sub_keys_1[l].astype(BF16),
                        sub_keys_2[l].astype(BF16), 512)
        fft = _experts(x1bt, expert_u[l].astype(BF16), expert_v[l].T.astype(BF16), sel, 512, 1024)
        x = _residual_ln_t(x1t, fft, ln2_g[l].reshape(1, d), ln2_b[l].reshape(1, d), alpha, 512).reshape(bsz, seq, d)
    return x
```

```python
import functools
import math

import numpy as np
import jax
import jax.numpy as jnp
from jax import lax
from jax.experimental import pallas as pl
from jax.experimental.pallas import tpu as pltpu

F32 = jnp.float32
BF16 = jnp.bfloat16

LANES = 128
SUBLANES = 8
VMEM_LIMIT_BYTES = 56 * 1024 * 1024

POOL_WINDOWS = (2, 4, 8, 16)
POOL_GROUP = 128
POOL_HALO = 16
SWA_HEAD_DIM = 64
SWA_HEADS = 16
SWA_KV_HEADS = 4
SWA_BLOCK = 128
ROPE_THETA = 500000.0
ROPE_DIM = 16
MEM_HEADS = 4
MEM_HEAD_DIM = 128
PEER_HEADS = 8
N_KEYS = 128
PEER_TOPK = 16
HALF_DIM = 128
LN_EPS = 1e-5
NEG = -1e30

EXPERT_CHUNK = 512
GATE_KEYS = 4
OUT_ROWS = 512
GATE_ROWS = 16
SELECT_TOKENS = 512


def _params(*semantics):
    return pltpu.CompilerParams(dimension_semantics=semantics, vmem_limit_bytes=VMEM_LIMIT_BYTES)


def _matmul_kernel(a_ref, b_ref, o_ref):
    o_ref[...] = jnp.dot(a_ref[...], b_ref[...], preferred_element_type=F32).astype(o_ref.dtype)


def _matmul(a, b, out_dtype, tm, tn):
    m, k = a.shape
    n = b.shape[1]
    tm, tn = min(tm, m), min(tn, n)
    return pl.pallas_call(
        _matmul_kernel,
        grid=(m // tm, n // tn),
        in_specs=[pl.BlockSpec((tm, k), lambda i, j: (i, 0)), pl.BlockSpec((k, tn), lambda i, j: (0, j))],
        out_specs=pl.BlockSpec((tm, tn), lambda i, j: (i, j)),
        out_shape=jax.ShapeDtypeStruct((m, n), out_dtype),
        compiler_params=_params("parallel", "arbitrary"),
        name="matmul",
    )(a, b)


IN_BLOCK = 512
_Q_BLOCKS = (1, 2)
_KV_BLOCK = 3


def _rope(x, c, sa, sb):
    half = ROPE_DIM // 2
    return x * c + pltpu.roll(x, LANES - half, 1) * sa + pltpu.roll(x, half, 1) * sb


def _inproj_kernel(x_ref, w_ref, pos_ref, freq_ref, sa_ref, sb_ref, o_ref):
    xb = x_ref[...].astype(BF16)
    ang = pos_ref[...].astype(F32) * freq_ref[...]
    s = jnp.sin(ang)
    c, sa, sb = jnp.cos(ang), s * sa_ref[...], s * sb_ref[...]
    for j in range(w_ref.shape[1] // IN_BLOCK):
        y = jnp.dot(xb, w_ref[:, j * IN_BLOCK:(j + 1) * IN_BLOCK], preferred_element_type=F32)
        for k in range(IN_BLOCK // LANES):
            piece = y[:, k * LANES:(k + 1) * LANES]
            if j in _Q_BLOCKS:
                piece = _rope(piece, c, sa, sb) * SWA_HEAD_DIM ** -0.5
            elif j == _KV_BLOCK and k < IN_BLOCK // LANES // 2:
                piece = _rope(piece, c, sa, sb)
            o_ref[:, j * IN_BLOCK + k * LANES:j * IN_BLOCK + (k + 1) * LANES] = piece.astype(o_ref.dtype)


def _rope_constants():
    lane = np.arange(LANES)
    d = lane % SWA_HEAD_DIM
    half = ROPE_DIM // 2
    inv_freq = np.float32(ROPE_THETA) ** (-np.arange(0, ROPE_DIM, 2, dtype=np.float32) / np.float32(ROPE_DIM))
    freq = np.where(d < ROPE_DIM, inv_freq[d % half], 0.0).astype(np.float32)
    sa = np.where(d < half, -1.0, 0.0).astype(np.float32)
    sb = np.where((d >= half) & (d < ROPE_DIM), 1.0, 0.0).astype(np.float32)
    return [jnp.asarray(a.reshape(1, LANES)) for a in (freq, sa, sb)]


def _inproj(x2, w, positions, tm):
    t, d = x2.shape
    n = w.shape[1]
    tm = min(tm, t)
    freq, sa, sb = _rope_constants()
    const = pl.BlockSpec((1, LANES), lambda i: (0, 0))
    return pl.pallas_call(
        _inproj_kernel,
        grid=(t // tm,),
        in_specs=[pl.BlockSpec((tm, d), lambda i: (i, 0)), pl.BlockSpec((d, n), lambda i: (0, 0)),
                  pl.BlockSpec((tm, 1), lambda i: (i, 0)), const, const, const],
        out_specs=pl.BlockSpec((tm, n), lambda i: (i, 0)),
        out_shape=jax.ShapeDtypeStruct((t, n), BF16),
        compiler_params=_params("parallel"),
        name="inproj_rope",
    )(x2, w, positions.reshape(t, 1), freq, sa, sb)


def _pool_kernel(v_ref, w_ref, scale_ref, o_ref, ext_ref, *, ts):
    s = pl.program_id(1)

    @pl.when(s == 0)
    def _():
        ext_ref[0:POOL_HALO, :] = jnp.zeros((POOL_HALO, ext_ref.shape[1]), F32)

    ext_ref[POOL_HALO:POOL_HALO + ts, :] = v_ref[0].astype(F32)
    pos = s * ts + lax.broadcasted_iota(jnp.int32, (ts, 1), 0)
    for g, w in enumerate(POOL_WINDOWS):
        cols = slice(g * POOL_GROUP, (g + 1) * POOL_GROUP)
        acc = ext_ref[POOL_HALO:POOL_HALO + ts, cols]
        for k in range(1, w):
            acc = acc + ext_ref[POOL_HALO - k:POOL_HALO - k + ts, cols]
        count = jnp.minimum(pos + 1, w).astype(F32)
        pooled = acc / count - ext_ref[POOL_HALO:POOL_HALO + ts, cols]
        y = jnp.dot(pooled.astype(BF16), w_ref[g], preferred_element_type=F32)
        o_ref[0, :, cols] = (y * scale_ref[:, cols]).astype(o_ref.dtype)
    ext_ref[0:POOL_HALO, :] = ext_ref[ts:ts + POOL_HALO, :]


def _pool(h, w_pool, pool_scale, ts):
    b, s, _ = h.shape
    width = POOL_GROUP * len(POOL_WINDOWS)
    ts = min(ts, s)
    return pl.pallas_call(
        functools.partial(_pool_kernel, ts=ts),
        grid=(b, s // ts),
        in_specs=[
            pl.BlockSpec((1, ts, width), lambda i, j: (i, j, 0)),
            pl.BlockSpec(w_pool.shape, lambda i, j: (0, 0, 0)),
            pl.BlockSpec((1, width), lambda i, j: (0, 0)),
        ],
        out_specs=pl.BlockSpec((1, ts, width), lambda i, j: (i, j, 0)),
        out_shape=jax.ShapeDtypeStruct((b, s, width), BF16),
        scratch_shapes=[pltpu.VMEM((ts + POOL_HALO, width), F32)],
        compiler_params=_params("arbitrary", "arbitrary"),
        name="pool",
    )(h, w_pool, pool_scale)


def _swa_kernel(sink_ref, q0_ref, q1_ref, kv_ref, kvp_ref, o_ref):
    n = pl.program_id(1)
    kvw = SWA_KV_HEADS * SWA_HEAD_DIM
    q = jnp.concatenate([q0_ref[0], q1_ref[0]], axis=1)
    k = jnp.concatenate([kvp_ref[0, :, 0:kvw], kv_ref[0, :, 0:kvw]], axis=0)
    v = jnp.concatenate([kvp_ref[0, :, kvw:2 * kvw], kv_ref[0, :, kvw:2 * kvw]], axis=0)
    row = lax.broadcasted_iota(jnp.int32, (SWA_BLOCK, 2 * SWA_BLOCK), 0)
    col = lax.broadcasted_iota(jnp.int32, (SWA_BLOCK, 2 * SWA_BLOCK), 1)
    rel = row + SWA_BLOCK - col
    valid = (rel >= 0) & (rel < SWA_BLOCK) & ((col >= SWA_BLOCK) | (n > 0))
    group = SWA_HEADS // SWA_KV_HEADS
    outs = []
    for hq in range(SWA_HEADS):
        kv = hq // group
        qh = q[:, hq * SWA_HEAD_DIM:(hq + 1) * SWA_HEAD_DIM]
        kh = k[:, kv * SWA_HEAD_DIM:(kv + 1) * SWA_HEAD_DIM]
        vh = v[:, kv * SWA_HEAD_DIM:(kv + 1) * SWA_HEAD_DIM]
        sc = lax.dot_general(qh, kh, (((1,), (1,)), ((), ())), preferred_element_type=F32)
        sc = jnp.where(valid, sc, NEG)
        sink = sink_ref[hq]
        m = jnp.maximum(jnp.max(sc, axis=1, keepdims=True), sink)
        p = jnp.exp(sc - m)
        denom = jnp.sum(p, axis=1, keepdims=True) + jnp.exp(sink - m)
        o = jnp.dot(p.astype(BF16), vh, preferred_element_type=F32)
        outs.append(o / denom)
    o_ref[0] = jnp.concatenate(outs, axis=1).astype(o_ref.dtype)


def _swa(h, sinks):
    b, s, _ = h.shape
    nb = s // SWA_BLOCK
    blk = lambda c: pl.BlockSpec((1, SWA_BLOCK, IN_BLOCK), lambda i, j: (i, j, c))
    blk_prev = lambda c: pl.BlockSpec((1, SWA_BLOCK, IN_BLOCK), lambda i, j: (i, jnp.maximum(j - 1, 0), c))
    return pl.pallas_call(
        _swa_kernel,
        grid=(b, nb),
        in_specs=[pl.BlockSpec(memory_space=pltpu.SMEM),
                  blk(_Q_BLOCKS[0]), blk(_Q_BLOCKS[1]), blk(_KV_BLOCK), blk_prev(_KV_BLOCK)],
        out_specs=pl.BlockSpec((1, SWA_BLOCK, SWA_HEADS * SWA_HEAD_DIM), lambda i, j: (i, j, 0)),
        out_shape=jax.ShapeDtypeStruct((b, s, SWA_HEADS * SWA_HEAD_DIM), BF16),
        compiler_params=_params("parallel", "arbitrary"),
        name="swa",
    )(sinks, h, h, h, h)


def _mem_kernel(q_ref, kv_ref, o_ref):
    scale = MEM_HEAD_DIM ** -0.5
    width = MEM_HEADS * MEM_HEAD_DIM
    for hm in range(MEM_HEADS):
        cols = slice(hm * MEM_HEAD_DIM, (hm + 1) * MEM_HEAD_DIM)
        km = kv_ref[0, :, cols]
        vm = kv_ref[0, :, width + hm * MEM_HEAD_DIM:width + (hm + 1) * MEM_HEAD_DIM]
        sc = lax.dot_general(q_ref[0, :, cols], km, (((1,), (1,)), ((), ())), preferred_element_type=F32) * scale
        m = jnp.max(sc, axis=1, keepdims=True)
        p = jnp.exp(sc - m)
        denom = jnp.sum(p, axis=1, keepdims=True)
        o = jnp.dot(p.astype(BF16), vm, preferred_element_type=F32)
        o_ref[0, :, cols] = (o / denom).astype(o_ref.dtype)


def _mem_attention(h, kvm, tq):
    b, s, _ = h.shape
    width = MEM_HEADS * MEM_HEAD_DIM
    tq = min(tq, s)
    return pl.pallas_call(
        _mem_kernel,
        grid=(b, s // tq),
        in_specs=[
            pl.BlockSpec((1, tq, width), lambda i, j: (i, j, 4)),
            pl.BlockSpec((1,) + kvm.shape[1:], lambda i, j: (i, 0, 0)),
        ],
        out_specs=pl.BlockSpec((1, tq, width), lambda i, j: (i, j, 0)),
        out_shape=jax.ShapeDtypeStruct((b, s, width), BF16),
        compiler_params=_params("parallel", "arbitrary"),
        name="mem_attention",
    )(h, kvm)


def _outproj_kernel(pool_ref, swa_ref, mem_ref, wp_ref, ws_ref, wm_ref, x_ref, g_ref, b_ref, o_ref, ob_ref,
                    *, alpha):
    mix = jnp.dot(pool_ref[...], wp_ref[...], preferred_element_type=F32)
    mix += jnp.dot(swa_ref[...], ws_ref[...], preferred_element_type=F32)
    mix += jnp.dot(mem_ref[...], wm_ref[...], preferred_element_type=F32)
    z = alpha * x_ref[...] + mix
    mu = jnp.mean(z, axis=1, keepdims=True)
    zc = z - mu
    var = jnp.mean(zc * zc, axis=1, keepdims=True)
    y = zc * lax.rsqrt(var + LN_EPS) * g_ref[...] + b_ref[...]
    yt = y.T
    o_ref[...] = yt
    ob_ref[...] = yt.astype(BF16)


def _outproj_ln(pool_o, swa_o, mem_o, w_out, x2, g, b, alpha, tm):
    t, d = x2.shape
    tm = min(tm, t)
    wp, ws, wm = pool_o.shape[1], swa_o.shape[1], mem_o.shape[1]
    w_p, w_s, w_m = w_out[:wp], w_out[wp:wp + ws], w_out[wp + ws:]
    row = lambda w: pl.BlockSpec((tm, w), lambda i: (i, 0))
    full = lambda a: pl.BlockSpec(a.shape, lambda i: (0, 0))
    return pl.pallas_call(
        functools.partial(_outproj_kernel, alpha=alpha),
        grid=(t // tm,),
        in_specs=[row(wp), row(ws), row(wm), full(w_p), full(w_s), full(w_m), row(d),
                  pl.BlockSpec((1, d), lambda i: (0, 0)), pl.BlockSpec((1, d), lambda i: (0, 0))],
        out_specs=[pl.BlockSpec((d, tm), lambda i: (0, i))] * 2,
        out_shape=[jax.ShapeDtypeStruct((d, t), F32), jax.ShapeDtypeStruct((d, t), BF16)],
        compiler_params=_params("parallel"),
        name="outproj_ln",
    )(pool_o, swa_o, mem_o, w_p, w_s, w_m, x2, g, b)


def _top16_rows(s, break_ties):
    n = s.shape[0]
    iota = lax.broadcasted_iota(jnp.int32, s.shape, 0).astype(F32) if break_ties else None
    rank = jnp.full(s.shape, float(PEER_TOPK), F32)
    vals = []
    for r in range(PEER_TOPK):
        m = jnp.max(s, axis=0, keepdims=True)
        hit = s == m
        if break_ties:
            idx = jnp.min(jnp.where(hit, iota, float(n)), axis=0, keepdims=True)
            hit = iota == idx
        rank = jnp.where(hit, float(r), rank)
        s = jnp.where(hit, -jnp.inf, s)
        vals.append(m)
    count = jnp.sum(jnp.where(rank < float(PEER_TOPK), 1.0, 0.0), axis=0, keepdims=True)
    return rank, vals, count


_CAND_SMALL_A = PEER_TOPK // 2
_CAND_ROWS = PEER_TOPK + (_CAND_SMALL_A - 1) * SUBLANES + SUBLANES


def _cand_constants():
    flat = np.full((_CAND_ROWS, 1), 1e9, np.float32)
    valid = np.zeros((_CAND_ROWS, 1), np.float32)
    for b in range(PEER_TOPK):
        flat[b, 0], valid[b, 0] = b, 1.0
    for a in range(1, _CAND_SMALL_A):
        base = PEER_TOPK + (a - 1) * SUBLANES
        for b in range(PEER_TOPK // (a + 1)):
            flat[base + b, 0], valid[base + b, 0] = a * PEER_TOPK + b, 1.0
    base = PEER_TOPK + (_CAND_SMALL_A - 1) * SUBLANES
    for k in range(SUBLANES):
        flat[base + k, 0], valid[base + k, 0] = (_CAND_SMALL_A + k) * PEER_TOPK, 1.0
    return jnp.asarray(flat), jnp.asarray(valid)


def _select_experts(s1, s2, flat, valid, break_ties):
    t = s1.shape[1]
    rank1, v1, count1 = _top16_rows(s1, break_ties)
    rank2, v2, count2 = _top16_rows(s2, break_ties)
    v2_lo = jnp.concatenate(v2[:SUBLANES], axis=0)
    v2_all = jnp.concatenate(v2, axis=0)
    v1_hi = jnp.concatenate(v1[_CAND_SMALL_A:], axis=0)
    groups = [v1[0] + v2_all]
    for a in range(1, _CAND_SMALL_A):
        groups.append(v1[a] + v2_lo)
    groups.append(v1_hi + v2[0])
    cand = jnp.concatenate(groups, axis=0)
    cand = jnp.where(valid > 0.5, cand, -jnp.inf)
    flat_b = jnp.broadcast_to(flat, cand.shape) if break_ties else None
    hits = jnp.zeros(cand.shape, F32)
    top = []
    for r in range(PEER_TOPK):
        m = jnp.max(cand, axis=0, keepdims=True)
        hit = cand == m
        if break_ties:
            pick = jnp.min(jnp.where(hit, flat_b, 2e9), axis=0, keepdims=True)
            hit = flat_b == pick
        hits = jnp.where(hit, 1.0, hits)
        cand = jnp.where(hit, -jnp.inf, cand)
        top.append(m)
    z = jnp.ones((1, t), F32)
    for r in range(1, PEER_TOPK):
        z = z + jnp.exp(top[r] - top[0])
    counts = [jnp.sum(hits[0:PEER_TOPK], axis=0, keepdims=True)]
    for a in range(1, _CAND_SMALL_A):
        base = PEER_TOPK + (a - 1) * SUBLANES
        counts.append(jnp.sum(hits[base:base + SUBLANES], axis=0, keepdims=True))
    base = PEER_TOPK + (_CAND_SMALL_A - 1) * SUBLANES
    for k in range(SUBLANES):
        counts.append(hits[base + k:base + k + 1])
    lim = jnp.zeros(s1.shape, F32)
    for a in range(PEER_TOPK):
        lim = jnp.where(rank1 == float(a), counts[a], lim)
    e1n = jnp.exp(s1 - v1[0]) / z
    e2 = jnp.exp(s2 - v2[0])
    count3 = jnp.sum(hits, axis=0, keepdims=True)
    k = float(PEER_TOPK)
    ok = jnp.where((count1 == k) & (count2 == k) & (count3 == k), 1.0, 0.0)
    return lim, e1n, rank2, e2, ok


def _retrieve_kernel(wq_ref, x_ref, k1_ref, k2_ref, flat_ref, valid_ref,
                     lim_out_ref, e1_out_ref, r2_ref, e2_ref, q_ref, lim_ref, e1_ref, *, tq):
    q_ref[...] = jnp.dot(wq_ref[...], x_ref[...], preferred_element_type=F32)
    flat, valid = flat_ref[...], valid_ref[...]

    def head(h, carry):
        for c in range(tq // SELECT_TOKENS):
            lanes = slice(c * SELECT_TOKENS, (c + 1) * SELECT_TOKENS)
            r0 = pl.multiple_of(h * 2 * HALF_DIM, 2 * HALF_DIM)
            q1 = q_ref[pl.ds(r0, HALF_DIM), lanes].astype(BF16)
            q2 = q_ref[pl.ds(r0 + HALF_DIM, HALF_DIM), lanes].astype(BF16)
            s1 = jnp.dot(k1_ref[...], q1, preferred_element_type=F32)
            s2 = jnp.dot(k2_ref[...], q2, preferred_element_type=F32)

            def emit(break_ties):
                lim, e1n, rank2, e2, ok = _select_experts(s1, s2, flat, valid, break_ties)
                lim_ref[h, :, lanes] = lim
                e1_ref[h, :, lanes] = e1n
                r2_ref[h, :, lanes] = rank2.astype(BF16)
                e2_ref[h, :, lanes] = e2.astype(BF16)
                return ok

            ok = emit(False)

            @pl.when(jnp.min(ok) < 0.5)
            def _():
                emit(True)
        return carry

    lax.fori_loop(0, PEER_HEADS, head, 0)
    lim_out_ref[...] = jnp.swapaxes(lim_ref[...], 0, 1)
    e1_out_ref[...] = jnp.swapaxes(e1_ref[...], 0, 1)


def _retrieve(x1t, wq_t, k1, k2, tq):
    d, t = x1t.shape
    tq = min(tq, t)
    flat, valid = _cand_constants()
    full = lambda a: pl.BlockSpec(a.shape, lambda i: (0,) * a.ndim)
    out_keys = jax.ShapeDtypeStruct((N_KEYS, PEER_HEADS, t), F32)
    key_spec = pl.BlockSpec((N_KEYS, PEER_HEADS, tq), lambda i: (0, 0, i))
    out_packed = jax.ShapeDtypeStruct((PEER_HEADS, N_KEYS, t), BF16)
    out_spec = pl.BlockSpec((PEER_HEADS, N_KEYS, tq), lambda i: (0, 0, i))
    return pl.pallas_call(
        functools.partial(_retrieve_kernel, tq=tq),
        grid=(t // tq,),
        in_specs=[full(wq_t), pl.BlockSpec((d, tq), lambda i: (0, i)), full(k1), full(k2), full(flat), full(valid)],
        out_specs=[key_spec, key_spec, out_spec, out_spec],
        out_shape=[out_keys, out_keys, out_packed, out_packed],
        scratch_shapes=[pltpu.VMEM((wq_t.shape[0], tq), F32)] + [pltpu.VMEM((PEER_HEADS, N_KEYS, tq), F32)] * 2,
        compiler_params=_params("parallel"),
        name="peer_retrieve",
    )(wq_t, x1t, k1, k2, flat, valid)


def _gelu_exact(x):
    return 0.5 * x * (1.0 + lax.erf(x * (1.0 / math.sqrt(2.0))))


def _experts_kernel(xb_ref, u_ref, vt_ref, lim_ref, e1_ref, r2_in_ref, e2_in_ref, o_ref,
                    h_ref, a_ref, rows_ref, gate_ref, r2_ref, e2_ref, *, tm, te):
    e = pl.program_id(1)

    @pl.when(e == 0)
    def _():
        o_ref[...] = jnp.zeros(o_ref.shape, F32)
        r2_ref[:, 0:tm] = r2_in_ref[...]
        e2_ref[:, LANES:LANES + tm] = e2_in_ref[...]

    n_chunks = te // EXPERT_CHUNK

    def rows_of(p):
        return slice(p * EXPERT_CHUNK, (p + 1) * EXPERT_CHUNK)

    def gates(first_key):
        keys = range(first_key, first_key + GATE_KEYS)
        blocks = range(0, N_KEYS, GATE_ROWS)
        for i in keys:
            for hd in range(PEER_HEADS):
                for q, ref in enumerate((lim_ref, e1_ref)):
                    row = jnp.broadcast_to(ref[i, hd:hd + 1, :], (GATE_ROWS, tm)).astype(BF16)
                    rows_ref[q, i, hd, :, q * LANES:q * LANES + tm] = row
        for c in range(tm // LANES):
            lanes = slice(c * LANES, (c + 1) * LANES)
            lanes1 = slice((c + 1) * LANES, (c + 2) * LANES)
            gate = {i: {jb: jnp.zeros((GATE_ROWS, LANES), BF16) for jb in blocks} for i in keys}
            for hd in range(PEER_HEADS):
                lim = {i: rows_ref[0, i, hd, :, lanes] for i in keys}
                e1 = {i: rows_ref[1, i, hd, :, lanes1] for i in keys}
                for jb in blocks:
                    r2 = r2_ref[hd * N_KEYS + jb:hd * N_KEYS + jb + GATE_ROWS, lanes]
                    e2 = e2_ref[hd * N_KEYS + jb:hd * N_KEYS + jb + GATE_ROWS, lanes1]
                    for i in keys:
                        gate[i][jb] = gate[i][jb] + jnp.where(r2 < lim[i], e2 * e1[i], jnp.zeros_like(e2))
            for i in keys:
                for jb in blocks:
                    gate_ref[i * N_KEYS + jb:i * N_KEYS + jb + GATE_ROWS, lanes] = gate[i][jb]

    for first_key in range(0, te // N_KEYS, GATE_KEYS):
        gates(first_key)
    h_ref[...] = jnp.dot(u_ref[...], xb_ref[...], preferred_element_type=F32)
    for p in range(n_chunks):
        a_ref[rows_of(p), :] = gate_ref[rows_of(p), :] * _gelu_exact(h_ref[rows_of(p), :]).astype(BF16)
        for r in range(0, o_ref.shape[0], OUT_ROWS):
            piece = slice(r, r + OUT_ROWS)
            o_ref[piece, :] += jnp.dot(vt_ref[piece, rows_of(p)], a_ref[rows_of(p), :], preferred_element_type=F32)


def _experts(xbt, u, vt, sel, tm, te):
    d, t = xbt.shape
    n_exp = u.shape[0]
    tm, te = min(tm, t), min(te, n_exp)
    tok = pl.BlockSpec((d, tm), lambda i, j: (0, i))
    sel_spec = pl.BlockSpec((PEER_HEADS * N_KEYS, tm), lambda i, j: (0, i))
    key_spec = pl.BlockSpec((te // N_KEYS, PEER_HEADS, tm), lambda i, j: (j, 0, i))
    lim, e1n, rank2, e2 = sel
    sel = (lim, e1n, rank2.reshape(-1, t), e2.reshape(-1, t))
    return pl.pallas_call(
        functools.partial(_experts_kernel, tm=tm, te=te),
        grid=(t // tm, n_exp // te),
        in_specs=[tok, pl.BlockSpec((te, d), lambda i, j: (j, 0)), pl.BlockSpec((d, te), lambda i, j: (0, j)),
                  key_spec, key_spec, sel_spec, sel_spec],
        out_specs=tok,
        out_shape=jax.ShapeDtypeStruct((d, t), F32),
        scratch_shapes=[pltpu.VMEM((te, tm), F32), pltpu.VMEM((te, tm), BF16),
                        pltpu.VMEM((2, te // N_KEYS, PEER_HEADS, GATE_ROWS, tm + LANES), BF16),
                        pltpu.VMEM((te, tm), BF16),
                        pltpu.VMEM((PEER_HEADS * N_KEYS, tm + LANES), BF16),
                        pltpu.VMEM((PEER_HEADS * N_KEYS, tm + LANES), BF16)],
        compiler_params=_params("parallel", "arbitrary"),
        name="peer_experts",
    )(xbt, u, vt, *sel)


def _ln_t_kernel(x_ref, f_ref, g_ref, b_ref, o_ref, *, alpha):
    z = (alpha * x_ref[...] + f_ref[...]).T
    mu = jnp.mean(z, axis=1, keepdims=True)
    zc = z - mu
    var = jnp.mean(zc * zc, axis=1, keepdims=True)
    o_ref[...] = zc * lax.rsqrt(var + LN_EPS) * g_ref[...] + b_ref[...]


def _residual_ln_t(x1t, fft, g, b, alpha, tm):
    d, t = x1t.shape
    tm = min(tm, t)
    tok = pl.BlockSpec((d, tm), lambda i: (0, i))
    vec = pl.BlockSpec((1, d), lambda i: (0, 0))
    return pl.pallas_call(
        functools.partial(_ln_t_kernel, alpha=alpha),
        grid=(t // tm,),
        in_specs=[tok, tok, vec, vec],
        out_specs=pl.BlockSpec((tm, d), lambda i: (i, 0)),
        out_shape=jax.ShapeDtypeStruct((t, d), F32),
        compiler_params=_params("parallel"),
        name="residual_ln",
    )(x1t, fft, g, b)


def kernel(x, mem, positions, w_in, w_mem_kv, w_pool, pool_scale, attn_sinks, w_out, ln1_g, ln1_b,
           w_peer_q, sub_keys_1, sub_keys_2, expert_u, expert_v, ln2_g, ln2_b):
    bsz, seq, d = x.shape
    depth = w_in.shape[0]
    t = bsz * seq
    alpha = (2.0 * depth) ** 0.25
    for l in range(depth):
        x2 = x.reshape(t, d)
        h = _inproj(x2, w_in[l].astype(BF16), positions, 512).reshape(bsz, seq, -1)
        mem2 = mem.reshape(-1, d).astype(BF16)
        kvm = _matmul(mem2, w_mem_kv[l].astype(BF16), BF16, 512, 512).reshape(bsz, mem.shape[1], -1)
        pool_o = _pool(h, w_pool[l].astype(BF16), pool_scale[l].reshape(1, -1), 512)
        swa_o = _swa(h, attn_sinks[l])
        mem_o = _mem_attention(h, kvm, 512)
        x1t, x1bt = _outproj_ln(pool_o.reshape(t, -1), swa_o.reshape(t, -1), mem_o.reshape(t, -1),
                          w_out[l].astype(BF16), x2, ln1_g[l].reshape(1, d), ln1_b[l].reshape(1, d), alpha, 512)
        sel = _retrieve(x1bt, w_peer_q[l].T.astype(BF16), sub_keys_1[l].astype(BF16),
                        sub_keys_2[l].astype(BF16), 512)
        fft = _experts(x1bt, expert_u[l].astype(BF16), expert_v[l].T.astype(BF16), sel, 1024, 512)
        x = _residual_ln_t(x1t, fft, ln2_g[l].reshape(1, d), ln2_b[l].reshape(1, d), alpha, 512).reshape(bsz, seq, d)
    return x
```

```python
import functools
import math

import numpy as np
import jax
import jax.numpy as jnp
from jax import lax
from jax.experimental import pallas as pl
from jax.experimental.pallas import tpu as pltpu

F32 = jnp.float32
BF16 = jnp.bfloat16

LANES = 128
SUBLANES = 8
VMEM_LIMIT_BYTES = 56 * 1024 * 1024

POOL_WINDOWS = (2, 4, 8, 16)
POOL_GROUP = 128
POOL_HALO = 16
SWA_HEAD_DIM = 64
SWA_HEADS = 16
SWA_KV_HEADS = 4
SWA_BLOCK = 128
ROPE_THETA = 500000.0
ROPE_DIM = 16
MEM_HEADS = 4
MEM_HEAD_DIM = 128
PEER_HEADS = 8
N_KEYS = 128
PEER_TOPK = 16
HALF_DIM = 128
LN_EPS = 1e-5
NEG = -1e30

EXPERT_CHUNK = 512
GATE_KEYS = 4
GATE_ROWS = 16
SELECT_TOKENS = 512


def _params(*semantics):
    return pltpu.CompilerParams(dimension_semantics=semantics, vmem_limit_bytes=VMEM_LIMIT_BYTES)


def _matmul_kernel(a_ref, b_ref, o_ref):
    o_ref[...] = jnp.dot(a_ref[...], b_ref[...], preferred_element_type=F32).astype(o_ref.dtype)


def _matmul(a, b, out_dtype, tm, tn):
    m, k = a.shape
    n = b.shape[1]
    tm, tn = min(tm, m), min(tn, n)
    return pl.pallas_call(
        _matmul_kernel,
        grid=(m // tm, n // tn),
        in_specs=[pl.BlockSpec((tm, k), lambda i, j: (i, 0)), pl.BlockSpec((k, tn), lambda i, j: (0, j))],
        out_specs=pl.BlockSpec((tm, tn), lambda i, j: (i, j)),
        out_shape=jax.ShapeDtypeStruct((m, n), out_dtype),
        compiler_params=_params("parallel", "arbitrary"),
        name="matmul",
    )(a, b)


IN_BLOCK = 512
_Q_BLOCKS = (1, 2)
_KV_BLOCK = 3


def _rope(x, c, sa, sb):
    half = ROPE_DIM // 2
    return x * c + pltpu.roll(x, LANES - half, 1) * sa + pltpu.roll(x, half, 1) * sb


def _inproj_kernel(x_ref, w_ref, pos_ref, freq_ref, sa_ref, sb_ref, o_ref):
    xb = x_ref[...].astype(BF16)
    ang = pos_ref[...].astype(F32) * freq_ref[...]
    s = jnp.sin(ang)
    c, sa, sb = jnp.cos(ang), s * sa_ref[...], s * sb_ref[...]
    for j in range(w_ref.shape[1] // IN_BLOCK):
        y = jnp.dot(xb, w_ref[:, j * IN_BLOCK:(j + 1) * IN_BLOCK], preferred_element_type=F32)
        for k in range(IN_BLOCK // LANES):
            piece = y[:, k * LANES:(k + 1) * LANES]
            if j in _Q_BLOCKS:
                piece = _rope(piece, c, sa, sb) * SWA_HEAD_DIM ** -0.5
            elif j == _KV_BLOCK and k < IN_BLOCK // LANES // 2:
                piece = _rope(piece, c, sa, sb)
            o_ref[:, j * IN_BLOCK + k * LANES:j * IN_BLOCK + (k + 1) * LANES] = piece.astype(o_ref.dtype)


def _rope_constants():
    lane = np.arange(LANES)
    d = lane % SWA_HEAD_DIM
    half = ROPE_DIM // 2
    inv_freq = np.float32(ROPE_THETA) ** (-np.arange(0, ROPE_DIM, 2, dtype=np.float32) / np.float32(ROPE_DIM))
    freq = np.where(d < ROPE_DIM, inv_freq[d % half], 0.0).astype(np.float32)
    sa = np.where(d < half, -1.0, 0.0).astype(np.float32)
    sb = np.where((d >= half) & (d < ROPE_DIM), 1.0, 0.0).astype(np.float32)
    return [jnp.asarray(a.reshape(1, LANES)) for a in (freq, sa, sb)]


def _inproj(x2, w, positions, tm):
    t, d = x2.shape
    n = w.shape[1]
    tm = min(tm, t)
    freq, sa, sb = _rope_constants()
    const = pl.BlockSpec((1, LANES), lambda i: (0, 0))
    return pl.pallas_call(
        _inproj_kernel,
        grid=(t // tm,),
        in_specs=[pl.BlockSpec((tm, d), lambda i: (i, 0)), pl.BlockSpec((d, n), lambda i: (0, 0)),
                  pl.BlockSpec((tm, 1), lambda i: (i, 0)), const, const, const],
        out_specs=pl.BlockSpec((tm, n), lambda i: (i, 0)),
        out_shape=jax.ShapeDtypeStruct((t, n), BF16),
        compiler_params=_params("parallel"),
        name="inproj_rope",
    )(x2, w, positions.reshape(t, 1), freq, sa, sb)


def _pool_kernel(v_ref, w_ref, scale_ref, o_ref, ext_ref, *, ts):
    s = pl.program_id(1)

    @pl.when(s == 0)
    def _():
        ext_ref[0:POOL_HALO, :] = jnp.zeros((POOL_HALO, ext_ref.shape[1]), F32)

    ext_ref[POOL_HALO:POOL_HALO + ts, :] = v_ref[0].astype(F32)
    pos = s * ts + lax.broadcasted_iota(jnp.int32, (ts, 1), 0)
    for g, w in enumerate(POOL_WINDOWS):
        cols = slice(g * POOL_GROUP, (g + 1) * POOL_GROUP)
        acc = ext_ref[POOL_HALO:POOL_HALO + ts, cols]
        for k in range(1, w):
            acc = acc + ext_ref[POOL_HALO - k:POOL_HALO - k + ts, cols]
        count = jnp.minimum(pos + 1, w).astype(F32)
        pooled = acc / count - ext_ref[POOL_HALO:POOL_HALO + ts, cols]
        y = jnp.dot(pooled.astype(BF16), w_ref[g], preferred_element_type=F32)
        o_ref[0, :, cols] = (y * scale_ref[:, cols]).astype(o_ref.dtype)
    ext_ref[0:POOL_HALO, :] = ext_ref[ts:ts + POOL_HALO, :]


def _pool(h, w_pool, pool_scale, ts):
    b, s, _ = h.shape
    width = POOL_GROUP * len(POOL_WINDOWS)
    ts = min(ts, s)
    return pl.pallas_call(
        functools.partial(_pool_kernel, ts=ts),
        grid=(b, s // ts),
        in_specs=[
            pl.BlockSpec((1, ts, width), lambda i, j: (i, j, 0)),
            pl.BlockSpec(w_pool.shape, lambda i, j: (0, 0, 0)),
            pl.BlockSpec((1, width), lambda i, j: (0, 0)),
        ],
        out_specs=pl.BlockSpec((1, ts, width), lambda i, j: (i, j, 0)),
        out_shape=jax.ShapeDtypeStruct((b, s, width), BF16),
        scratch_shapes=[pltpu.VMEM((ts + POOL_HALO, width), F32)],
        compiler_params=_params("arbitrary", "arbitrary"),
        name="pool",
    )(h, w_pool, pool_scale)


def _swa_kernel(sink_ref, q0_ref, q1_ref, kv_ref, kvp_ref, o_ref):
    n = pl.program_id(1)
    kvw = SWA_KV_HEADS * SWA_HEAD_DIM
    q = jnp.concatenate([q0_ref[0], q1_ref[0]], axis=1)
    k = jnp.concatenate([kvp_ref[0, :, 0:kvw], kv_ref[0, :, 0:kvw]], axis=0)
    v = jnp.concatenate([kvp_ref[0, :, kvw:2 * kvw], kv_ref[0, :, kvw:2 * kvw]], axis=0)
    row = lax.broadcasted_iota(jnp.int32, (SWA_BLOCK, 2 * SWA_BLOCK), 0)
    col = lax.broadcasted_iota(jnp.int32, (SWA_BLOCK, 2 * SWA_BLOCK), 1)
    rel = row + SWA_BLOCK - col
    valid = (rel >= 0) & (rel < SWA_BLOCK) & ((col >= SWA_BLOCK) | (n > 0))
    group = SWA_HEADS // SWA_KV_HEADS
    outs = []
    for hq in range(SWA_HEADS):
        kv = hq // group
        qh = q[:, hq * SWA_HEAD_DIM:(hq + 1) * SWA_HEAD_DIM]
        kh = k[:, kv * SWA_HEAD_DIM:(kv + 1) * SWA_HEAD_DIM]
        vh = v[:, kv * SWA_HEAD_DIM:(kv + 1) * SWA_HEAD_DIM]
        sc = lax.dot_general(qh, kh, (((1,), (1,)), ((), ())), preferred_element_type=F32)
        sc = jnp.where(valid, sc, NEG)
        sink = sink_ref[hq]
        m = jnp.maximum(jnp.max(sc, axis=1, keepdims=True), sink)
        p = jnp.exp(sc - m)
        denom = jnp.sum(p, axis=1, keepdims=True) + jnp.exp(sink - m)
        o = jnp.dot(p.astype(BF16), vh, preferred_element_type=F32)
        outs.append(o / denom)
    o_ref[0] = jnp.concatenate(outs, axis=1).astype(o_ref.dtype)


def _swa(h, sinks):
    b, s, _ = h.shape
    nb = s // SWA_BLOCK
    blk = lambda c: pl.BlockSpec((1, SWA_BLOCK, IN_BLOCK), lambda i, j: (i, j, c))
    blk_prev = lambda c: pl.BlockSpec((1, SWA_BLOCK, IN_BLOCK), lambda i, j: (i, jnp.maximum(j - 1, 0), c))
    return pl.pallas_call(
        _swa_kernel,
        grid=(b, nb),
        in_specs=[pl.BlockSpec(memory_space=pltpu.SMEM),
                  blk(_Q_BLOCKS[0]), blk(_Q_BLOCKS[1]), blk(_KV_BLOCK), blk_prev(_KV_BLOCK)],
        out_specs=pl.BlockSpec((1, SWA_BLOCK, SWA_HEADS * SWA_HEAD_DIM), lambda i, j: (i, j, 0)),
        out_shape=jax.ShapeDtypeStruct((b, s, SWA_HEADS * SWA_HEAD_DIM), BF16),
        compiler_params=_params("parallel", "arbitrary"),
        name="swa",
    )(sinks, h, h, h, h)


def _mem_kernel(q_ref, kv_ref, o_ref):
    scale = MEM_HEAD_DIM ** -0.5
    width = MEM_HEADS * MEM_HEAD_DIM
    for hm in range(MEM_HEADS):
        cols = slice(hm * MEM_HEAD_DIM, (hm + 1) * MEM_HEAD_DIM)
        km = kv_ref[0, :, cols]
        vm = kv_ref[0, :, width + hm * MEM_HEAD_DIM:width + (hm + 1) * MEM_HEAD_DIM]
        sc = lax.dot_general(q_ref[0, :, cols], km, (((1,), (1,)), ((), ())), preferred_element_type=F32) * scale
        m = jnp.max(sc, axis=1, keepdims=True)
        p = jnp.exp(sc - m)
        denom = jnp.sum(p, axis=1, keepdims=True)
        o = jnp.dot(p.astype(BF16), vm, preferred_element_type=F32)
        o_ref[0, :, cols] = (o / denom).astype(o_ref.dtype)


def _mem_attention(h, kvm, tq):
    b, s, _ = h.shape
    width = MEM_HEADS * MEM_HEAD_DIM
    tq = min(tq, s)
    return pl.pallas_call(
        _mem_kernel,
        grid=(b, s // tq),
        in_specs=[
            pl.BlockSpec((1, tq, width), lambda i, j: (i, j, 4)),
            pl.BlockSpec((1,) + kvm.shape[1:], lambda i, j: (i, 0, 0)),
        ],
        out_specs=pl.BlockSpec((1, tq, width), lambda i, j: (i, j, 0)),
        out_shape=jax.ShapeDtypeStruct((b, s, width), BF16),
        compiler_params=_params("parallel", "arbitrary"),
        name="mem_attention",
    )(h, kvm)


def _outproj_kernel(pool_ref, swa_ref, mem_ref, wp_ref, ws_ref, wm_ref, x_ref, g_ref, b_ref, o_ref, ob_ref,
                    *, alpha):
    mix = jnp.dot(pool_ref[...], wp_ref[...], preferred_element_type=F32)
    mix += jnp.dot(swa_ref[...], ws_ref[...], preferred_element_type=F32)
    mix += jnp.dot(mem_ref[...], wm_ref[...], preferred_element_type=F32)
    z = alpha * x_ref[...] + mix
    mu = jnp.mean(z, axis=1, keepdims=True)
    zc = z - mu
    var = jnp.mean(zc * zc, axis=1, keepdims=True)
    y = zc * lax.rsqrt(var + LN_EPS) * g_ref[...] + b_ref[...]
    yt = y.T
    o_ref[...] = yt
    ob_ref[...] = yt.astype(BF16)


def _outproj_ln(pool_o, swa_o, mem_o, w_out, x2, g, b, alpha, tm):
    t, d = x2.shape
    tm = min(tm, t)
    wp, ws, wm = pool_o.shape[1], swa_o.shape[1], mem_o.shape[1]
    w_p, w_s, w_m = w_out[:wp], w_out[wp:wp + ws], w_out[wp + ws:]
    row = lambda w: pl.BlockSpec((tm, w), lambda i: (i, 0))
    full = lambda a: pl.BlockSpec(a.shape, lambda i: (0, 0))
    return pl.pallas_call(
        functools.partial(_outproj_kernel, alpha=alpha),
        grid=(t // tm,),
        in_specs=[row(wp), row(ws), row(wm), full(w_p), full(w_s), full(w_m), row(d),
                  pl.BlockSpec((1, d), lambda i: (0, 0)), pl.BlockSpec((1, d), lambda i: (0, 0))],
        out_specs=[pl.BlockSpec((d, tm), lambda i: (0, i))] * 2,
        out_shape=[jax.ShapeDtypeStruct((d, t), F32), jax.ShapeDtypeStruct((d, t), BF16)],
        compiler_params=_params("parallel"),
        name="outproj_ln",
    )(pool_o, swa_o, mem_o, w_p, w_s, w_m, x2, g, b)


def _merge_exchange_pairs(n):
    t = (n - 1).bit_length()
    pairs = []
    p = 1 << (t - 1)
    while p >= 1:
        q, r, d = 1 << (t - 1), 0, p
        while True:
            pairs += [(i, i + d) for i in range(n - d) if (i & p) == r]
            if q == p:
                break
            d, q, r = q - p, q // 2, p
        p //= 2
    return pairs


def _top16_rows_distinct(s):
    k = PEER_TOPK
    lists = [s[SUBLANES * g:SUBLANES * (g + 1), :] for g in range(s.shape[0] // SUBLANES)]
    for i, j in _merge_exchange_pairs(len(lists)):
        lists[i], lists[j] = jnp.maximum(lists[i], lists[j]), jnp.minimum(lists[i], lists[j])
    vals = []
    for r in range(k):
        m = jnp.max(lists[0], axis=0, keepdims=True)
        vals.append(m)
        pop = lists[0] == m
        for d in range(k - 1 - r):
            lists[d] = jnp.where(pop, lists[d + 1], lists[d])
    rank = jnp.zeros(s.shape, F32)
    for r in range(k):
        rank = rank + jnp.where(vals[r] > s, 1.0, 0.0)
    count = jnp.sum(jnp.where(rank < float(k), 1.0, 0.0), axis=0, keepdims=True)
    strict = jnp.zeros_like(count)
    for r in range(k - 1):
        strict = strict + jnp.where(vals[r] > vals[r + 1], 1.0, 0.0)
    distinct = (count == float(k)) & (strict == float(k - 1))
    return rank, vals, jnp.where(distinct, float(k), 0.0)


def _top16_rows(s, break_ties):
    if not break_ties:
        return _top16_rows_distinct(s)
    n = s.shape[0]
    iota = lax.broadcasted_iota(jnp.int32, s.shape, 0).astype(F32)
    rank = jnp.full(s.shape, float(PEER_TOPK), F32)
    vals = []
    for r in range(PEER_TOPK):
        m = jnp.max(s, axis=0, keepdims=True)
        idx = jnp.min(jnp.where(s == m, iota, float(n)), axis=0, keepdims=True)
        hit = iota == idx
        rank = jnp.where(hit, float(r), rank)
        s = jnp.where(hit, -jnp.inf, s)
        vals.append(m)
    count = jnp.sum(jnp.where(rank < float(PEER_TOPK), 1.0, 0.0), axis=0, keepdims=True)
    return rank, vals, count


_CAND_SMALL_A = PEER_TOPK // 2
_CAND_ROWS = PEER_TOPK + (_CAND_SMALL_A - 1) * SUBLANES + SUBLANES


def _cand_constants():
    flat = np.full((_CAND_ROWS, 1), 1e9, np.float32)
    valid = np.zeros((_CAND_ROWS, 1), np.float32)
    for b in range(PEER_TOPK):
        flat[b, 0], valid[b, 0] = b, 1.0
    for a in range(1, _CAND_SMALL_A):
        base = PEER_TOPK + (a - 1) * SUBLANES
        for b in range(PEER_TOPK // (a + 1)):
            flat[base + b, 0], valid[base + b, 0] = a * PEER_TOPK + b, 1.0
    base = PEER_TOPK + (_CAND_SMALL_A - 1) * SUBLANES
    for k in range(SUBLANES):
        flat[base + k, 0], valid[base + k, 0] = (_CAND_SMALL_A + k) * PEER_TOPK, 1.0
    return jnp.asarray(flat), jnp.asarray(valid)


def _select_experts(s1, s2, flat, valid, break_ties):
    t = s1.shape[1]
    rank1, v1, count1 = _top16_rows(s1, break_ties)
    rank2, v2, count2 = _top16_rows(s2, break_ties)
    v2_lo = jnp.concatenate(v2[:SUBLANES], axis=0)
    v2_all = jnp.concatenate(v2, axis=0)
    v1_hi = jnp.concatenate(v1[_CAND_SMALL_A:], axis=0)
    groups = [v1[0] + v2_all]
    for a in range(1, _CAND_SMALL_A):
        groups.append(v1[a] + v2_lo)
    groups.append(v1_hi + v2[0])
    cand = jnp.concatenate(groups, axis=0)
    cand = jnp.where(valid > 0.5, cand, -jnp.inf)
    flat_b = jnp.broadcast_to(flat, cand.shape) if break_ties else None
    hits = jnp.zeros(cand.shape, F32)
    top = []
    for r in range(PEER_TOPK):
        m = jnp.max(cand, axis=0, keepdims=True)
        hit = cand == m
        if break_ties:
            pick = jnp.min(jnp.where(hit, flat_b, 2e9), axis=0, keepdims=True)
            hit = flat_b == pick
        hits = jnp.where(hit, 1.0, hits)
        cand = jnp.where(hit, -jnp.inf, cand)
        top.append(m)
    z = jnp.ones((1, t), F32)
    for r in range(1, PEER_TOPK):
        z = z + jnp.exp(top[r] - top[0])
    counts = [jnp.sum(hits[0:PEER_TOPK], axis=0, keepdims=True)]
    for a in range(1, _CAND_SMALL_A):
        base = PEER_TOPK + (a - 1) * SUBLANES
        counts.append(jnp.sum(hits[base:base + SUBLANES], axis=0, keepdims=True))
    base = PEER_TOPK + (_CAND_SMALL_A - 1) * SUBLANES
    for k in range(SUBLANES):
        counts.append(hits[base + k:base + k + 1])
    lim = jnp.zeros(s1.shape, F32)
    for a in range(PEER_TOPK):
        lim = jnp.where(rank1 == float(a), counts[a], lim)
    e1n = jnp.exp(s1 - v1[0]) / z
    e2 = jnp.exp(s2 - v2[0])
    count3 = jnp.sum(hits, axis=0, keepdims=True)
    k = float(PEER_TOPK)
    ok = jnp.where((count1 == k) & (count2 == k) & (count3 == k), 1.0, 0.0)
    return lim, e1n, rank2, e2, ok


def _retrieve_kernel(wq_ref, x_ref, k1_ref, k2_ref, flat_ref, valid_ref,
                     lim_out_ref, e1_out_ref, r2_ref, e2_ref, q_ref, lim_ref, e1_ref, *, tq):
    q_ref[...] = jnp.dot(wq_ref[...], x_ref[...], preferred_element_type=F32)
    flat, valid = flat_ref[...], valid_ref[...]

    def head(h, carry):
        for c in range(tq // SELECT_TOKENS):
            lanes = slice(c * SELECT_TOKENS, (c + 1) * SELECT_TOKENS)
            r0 = pl.multiple_of(h * 2 * HALF_DIM, 2 * HALF_DIM)
            q1 = q_ref[pl.ds(r0, HALF_DIM), lanes].astype(BF16)
            q2 = q_ref[pl.ds(r0 + HALF_DIM, HALF_DIM), lanes].astype(BF16)
            s1 = jnp.dot(k1_ref[...], q1, preferred_element_type=F32)
            s2 = jnp.dot(k2_ref[...], q2, preferred_element_type=F32)

            def emit(break_ties):
                lim, e1n, rank2, e2, ok = _select_experts(s1, s2, flat, valid, break_ties)
                lim_ref[h, :, lanes] = lim
                e1_ref[h, :, lanes] = e1n
                r2_ref[h, :, lanes] = rank2.astype(BF16)
                e2_ref[h, :, lanes] = e2.astype(BF16)
                return ok

            ok = emit(False)

            @pl.when(jnp.min(ok) < 0.5)
            def _():
                emit(True)
        return carry

    lax.fori_loop(0, PEER_HEADS, head, 0)
    lim_out_ref[...] = jnp.swapaxes(lim_ref[...], 0, 1)
    e1_out_ref[...] = jnp.swapaxes(e1_ref[...], 0, 1)


def _retrieve(x1t, wq_t, k1, k2, tq):
    d, t = x1t.shape
    tq = min(tq, t)
    flat, valid = _cand_constants()
    full = lambda a: pl.BlockSpec(a.shape, lambda i: (0,) * a.ndim)
    out_keys = jax.ShapeDtypeStruct((N_KEYS, PEER_HEADS, t), F32)
    key_spec = pl.BlockSpec((N_KEYS, PEER_HEADS, tq), lambda i: (0, 0, i))
    out_packed = jax.ShapeDtypeStruct((PEER_HEADS, N_KEYS, t), BF16)
    out_spec = pl.BlockSpec((PEER_HEADS, N_KEYS, tq), lambda i: (0, 0, i))
    return pl.pallas_call(
        functools.partial(_retrieve_kernel, tq=tq),
        grid=(t // tq,),
        in_specs=[full(wq_t), pl.BlockSpec((d, tq), lambda i: (0, i)), full(k1), full(k2), full(flat), full(valid)],
        out_specs=[key_spec, key_spec, out_spec, out_spec],
        out_shape=[out_keys, out_keys, out_packed, out_packed],
        scratch_shapes=[pltpu.VMEM((wq_t.shape[0], tq), F32)] + [pltpu.VMEM((PEER_HEADS, N_KEYS, tq), F32)] * 2,
        compiler_params=_params("parallel"),
        name="peer_retrieve",
    )(wq_t, x1t, k1, k2, flat, valid)


def _gelu_exact(x):
    return 0.5 * x * (1.0 + lax.erf(x * (1.0 / math.sqrt(2.0))))


def _experts_kernel(xb_ref, u_ref, vt_ref, lim_ref, e1_ref, r2_in_ref, e2_in_ref, o_ref,
                    h_ref, a_ref, rows_ref, gate_ref, r2_ref, e2_ref, *, tm, te):
    e = pl.program_id(1)

    @pl.when(e == 0)
    def _():
        o_ref[...] = jnp.zeros(o_ref.shape, F32)
        r2_ref[:, 0:tm] = r2_in_ref[...]
        e2_ref[:, LANES:LANES + tm] = e2_in_ref[...]

    n_chunks = te // EXPERT_CHUNK

    def rows_of(p):
        return slice(p * EXPERT_CHUNK, (p + 1) * EXPERT_CHUNK)

    def gates(first_key):
        keys = range(first_key, first_key + GATE_KEYS)
        blocks = range(0, N_KEYS, GATE_ROWS)
        for i in keys:
            for hd in range(PEER_HEADS):
                for q, ref in enumerate((lim_ref, e1_ref)):
                    row = jnp.broadcast_to(ref[i, hd:hd + 1, :], (GATE_ROWS, tm)).astype(BF16)
                    rows_ref[q, i, hd, :, q * LANES:q * LANES + tm] = row
        for c in range(tm // LANES):
            lanes = slice(c * LANES, (c + 1) * LANES)
            lanes1 = slice((c + 1) * LANES, (c + 2) * LANES)
            gate = {i: {jb: jnp.zeros((GATE_ROWS, LANES), BF16) for jb in blocks} for i in keys}
            for hd in range(PEER_HEADS):
                lim = {i: rows_ref[0, i, hd, :, lanes] for i in keys}
                e1 = {i: rows_ref[1, i, hd, :, lanes1] for i in keys}
                for jb in blocks:
                    r2 = r2_ref[hd * N_KEYS + jb:hd * N_KEYS + jb + GATE_ROWS, lanes]
                    e2 = e2_ref[hd * N_KEYS + jb:hd * N_KEYS + jb + GATE_ROWS, lanes1]
                    for i in keys:
                        gate[i][jb] = gate[i][jb] + jnp.where(r2 < lim[i], e2 * e1[i], jnp.zeros_like(e2))
            for i in keys:
                for jb in blocks:
                    gate_ref[i * N_KEYS + jb:i * N_KEYS + jb + GATE_ROWS, lanes] = gate[i][jb]

    for first_key in range(0, te // N_KEYS, GATE_KEYS):
        gates(first_key)
    h_ref[...] = jnp.dot(u_ref[...], xb_ref[...], preferred_element_type=F32)
    for p in range(n_chunks):
        a_ref[rows_of(p), :] = gate_ref[rows_of(p), :] * _gelu_exact(h_ref[rows_of(p), :]).astype(BF16)
        o_ref[...] += jnp.dot(vt_ref[:, rows_of(p)], a_ref[rows_of(p), :], preferred_element_type=F32)


def _experts(xbt, u, vt, sel, tm, te):
    d, t = xbt.shape
    n_exp = u.shape[0]
    tm, te = min(tm, t), min(te, n_exp)
    tok = pl.BlockSpec((d, tm), lambda i, j: (0, i))
    sel_spec = pl.BlockSpec((PEER_HEADS * N_KEYS, tm), lambda i, j: (0, i))
    key_spec = pl.BlockSpec((te // N_KEYS, PEER_HEADS, tm), lambda i, j: (j, 0, i))
    lim, e1n, rank2, e2 = sel
    sel = (lim, e1n, rank2.reshape(-1, t), e2.reshape(-1, t))
    return pl.pallas_call(
        functools.partial(_experts_kernel, tm=tm, te=te),
        grid=(t // tm, n_exp // te),
        in_specs=[tok, pl.BlockSpec((te, d), lambda i, j: (j, 0)), pl.BlockSpec((d, te), lambda i, j: (0, j)),
                  key_spec, key_spec, sel_spec, sel_spec],
        out_specs=tok,
        out_shape=jax.ShapeDtypeStruct((d, t), F32),
        scratch_shapes=[pltpu.VMEM((te, tm), F32), pltpu.VMEM((te, tm), BF16),
                        pltpu.VMEM((2, te // N_KEYS, PEER_HEADS, GATE_ROWS, tm + LANES), BF16),
                        pltpu.VMEM((te, tm), BF16),
                        pltpu.VMEM((PEER_HEADS * N_KEYS, tm + LANES), BF16),
                        pltpu.VMEM((PEER_HEADS * N_KEYS, tm + LANES), BF16)],
        compiler_params=_params("parallel", "arbitrary"),
        name="peer_experts",
    )(xbt, u, vt, *sel)


def _ln_t_kernel(x_ref, f_ref, g_ref, b_ref, o_ref, *, alpha):
    z = (alpha * x_ref[...] + f_ref[...]).T
    mu = jnp.mean(z, axis=1, keepdims=True)
    zc = z - mu
    var = jnp.mean(zc * zc, axis=1, keepdims=True)
    o_ref[...] = zc * lax.rsqrt(var + LN_EPS) * g_ref[...] + b_ref[...]


def _residual_ln_t(x1t, fft, g, b, alpha, tm):
    d, t = x1t.shape
    tm = min(tm, t)
    tok = pl.BlockSpec((d, tm), lambda i: (0, i))
    vec = pl.BlockSpec((1, d), lambda i: (0, 0))
    return pl.pallas_call(
        functools.partial(_ln_t_kernel, alpha=alpha),
        grid=(t // tm,),
        in_specs=[tok, tok, vec, vec],
        out_specs=pl.BlockSpec((tm, d), lambda i: (i, 0)),
        out_shape=jax.ShapeDtypeStruct((t, d), F32),
        compiler_params=_params("parallel"),
        name="residual_ln",
    )(x1t, fft, g, b)


def kernel(x, mem, positions, w_in, w_mem_kv, w_pool, pool_scale, attn_sinks, w_out, ln1_g, ln1_b,
           w_peer_q, sub_keys_1, sub_keys_2, expert_u, expert_v, ln2_g, ln2_b):
    bsz, seq, d = x.shape
    depth = w_in.shape[0]
    t = bsz * seq
    alpha = (2.0 * depth) ** 0.25
    for l in range(depth):
        x2 = x.reshape(t, d)
        h = _inproj(x2, w_in[l].astype(BF16), positions, 512).reshape(bsz, seq, -1)
        mem2 = mem.reshape(-1, d).astype(BF16)
        kvm = _matmul(mem2, w_mem_kv[l].astype(BF16), BF16, 512, 512).reshape(bsz, mem.shape[1], -1)
        pool_o = _pool(h, w_pool[l].astype(BF16), pool_scale[l].reshape(1, -1), 512)
        swa_o = _swa(h, attn_sinks[l])
        mem_o = _mem_attention(h, kvm, 512)
        x1t, x1bt = _outproj_ln(pool_o.reshape(t, -1), swa_o.reshape(t, -1), mem_o.reshape(t, -1),
                          w_out[l].astype(BF16), x2, ln1_g[l].reshape(1, d), ln1_b[l].reshape(1, d), alpha, 512)
        sel = _retrieve(x1bt, w_peer_q[l].T.astype(BF16), sub_keys_1[l].astype(BF16),
                        sub_keys_2[l].astype(BF16), 512)
        fft = _experts(x1bt, expert_u[l].astype(BF16), expert_v[l].T.astype(BF16), sel, 1024, 512)
        x = _residual_ln_t(x1t, fft, ln2_g[l].reshape(1, d), ln2_b[l].reshape(1, d), alpha, 512).reshape(bsz, seq, d)
    return x
```

```python
import functools
import math

import numpy as np
import jax
import jax.numpy as jnp
from jax import lax
from jax.experimental import pallas as pl
from jax.experimental.pallas import tpu as pltpu

F32 = jnp.float32
BF16 = jnp.bfloat16

LANES = 128
SUBLANES = 8
VMEM_LIMIT_BYTES = 56 * 1024 * 1024

POOL_WINDOWS = (2, 4, 8, 16)
POOL_GROUP = 128
POOL_HALO = 16
SWA_HEAD_DIM = 64
SWA_HEADS = 16
SWA_KV_HEADS = 4
SWA_BLOCK = 128
ROPE_THETA = 500000.0
ROPE_DIM = 16
MEM_HEADS = 4
MEM_HEAD_DIM = 128
PEER_HEADS = 8
N_KEYS = 128
PEER_TOPK = 16
HALF_DIM = 128
LN_EPS = 1e-5
NEG = -1e30

EXPERT_CHUNK = 512
GATE_KEYS = 4
GATE_ROWS = 16
SELECT_TOKENS = 512


def _params(*semantics):
    return pltpu.CompilerParams(dimension_semantics=semantics, vmem_limit_bytes=VMEM_LIMIT_BYTES)


def _matmul_kernel(a_ref, b_ref, o_ref):
    o_ref[...] = jnp.dot(a_ref[...], b_ref[...], preferred_element_type=F32).astype(o_ref.dtype)


def _matmul(a, b, out_dtype, tm, tn):
    m, k = a.shape
    n = b.shape[1]
    tm, tn = min(tm, m), min(tn, n)
    return pl.pallas_call(
        _matmul_kernel,
        grid=(m // tm, n // tn),
        in_specs=[pl.BlockSpec((tm, k), lambda i, j: (i, 0)), pl.BlockSpec((k, tn), lambda i, j: (0, j))],
        out_specs=pl.BlockSpec((tm, tn), lambda i, j: (i, j)),
        out_shape=jax.ShapeDtypeStruct((m, n), out_dtype),
        compiler_params=_params("parallel", "arbitrary"),
        name="matmul",
    )(a, b)


IN_BLOCK = 512
_Q_BLOCKS = (1, 2)
_KV_BLOCK = 3


def _rope(x, c, sa, sb):
    half = ROPE_DIM // 2
    return x * c + pltpu.roll(x, LANES - half, 1) * sa + pltpu.roll(x, half, 1) * sb


def _inproj_kernel(x_ref, w_ref, pos_ref, freq_ref, sa_ref, sb_ref, o_ref):
    xb = x_ref[...].astype(BF16)
    ang = pos_ref[...].astype(F32) * freq_ref[...]
    s = jnp.sin(ang)
    c, sa, sb = jnp.cos(ang), s * sa_ref[...], s * sb_ref[...]
    for j in range(w_ref.shape[1] // IN_BLOCK):
        y = jnp.dot(xb, w_ref[:, j * IN_BLOCK:(j + 1) * IN_BLOCK], preferred_element_type=F32)
        for k in range(IN_BLOCK // LANES):
            piece = y[:, k * LANES:(k + 1) * LANES]
            if j in _Q_BLOCKS:
                piece = _rope(piece, c, sa, sb) * SWA_HEAD_DIM ** -0.5
            elif j == _KV_BLOCK and k < IN_BLOCK // LANES // 2:
                piece = _rope(piece, c, sa, sb)
            o_ref[:, j * IN_BLOCK + k * LANES:j * IN_BLOCK + (k + 1) * LANES] = piece.astype(o_ref.dtype)


def _rope_constants():
    lane = np.arange(LANES)
    d = lane % SWA_HEAD_DIM
    half = ROPE_DIM // 2
    inv_freq = np.float32(ROPE_THETA) ** (-np.arange(0, ROPE_DIM, 2, dtype=np.float32) / np.float32(ROPE_DIM))
    freq = np.where(d < ROPE_DIM, inv_freq[d % half], 0.0).astype(np.float32)
    sa = np.where(d < half, -1.0, 0.0).astype(np.float32)
    sb = np.where((d >= half) & (d < ROPE_DIM), 1.0, 0.0).astype(np.float32)
    return [jnp.asarray(a.reshape(1, LANES)) for a in (freq, sa, sb)]


def _inproj(x2, w, positions, tm):
    t, d = x2.shape
    n = w.shape[1]
    tm = min(tm, t)
    freq, sa, sb = _rope_constants()
    const = pl.BlockSpec((1, LANES), lambda i: (0, 0))
    return pl.pallas_call(
        _inproj_kernel,
        grid=(t // tm,),
        in_specs=[pl.BlockSpec((tm, d), lambda i: (i, 0)), pl.BlockSpec((d, n), lambda i: (0, 0)),
                  pl.BlockSpec((tm, 1), lambda i: (i, 0)), const, const, const],
        out_specs=pl.BlockSpec((tm, n), lambda i: (i, 0)),
        out_shape=jax.ShapeDtypeStruct((t, n), BF16),
        compiler_params=_params("parallel"),
        name="inproj_rope",
    )(x2, w, positions.reshape(t, 1), freq, sa, sb)


def _pool_kernel(v_ref, w_ref, scale_ref, o_ref, ext_ref, *, ts):
    s = pl.program_id(1)

    @pl.when(s == 0)
    def _():
        ext_ref[0:POOL_HALO, :] = jnp.zeros((POOL_HALO, ext_ref.shape[1]), F32)

    ext_ref[POOL_HALO:POOL_HALO + ts, :] = v_ref[0].astype(F32)
    pos = s * ts + lax.broadcasted_iota(jnp.int32, (ts, 1), 0)
    for g, w in enumerate(POOL_WINDOWS):
        cols = slice(g * POOL_GROUP, (g + 1) * POOL_GROUP)
        acc = ext_ref[POOL_HALO:POOL_HALO + ts, cols]
        for k in range(1, w):
            acc = acc + ext_ref[POOL_HALO - k:POOL_HALO - k + ts, cols]
        count = jnp.minimum(pos + 1, w).astype(F32)
        pooled = acc / count - ext_ref[POOL_HALO:POOL_HALO + ts, cols]
        y = jnp.dot(pooled.astype(BF16), w_ref[g], preferred_element_type=F32)
        o_ref[0, :, cols] = (y * scale_ref[:, cols]).astype(o_ref.dtype)
    ext_ref[0:POOL_HALO, :] = ext_ref[ts:ts + POOL_HALO, :]


def _pool(h, w_pool, pool_scale, ts):
    b, s, _ = h.shape
    width = POOL_GROUP * len(POOL_WINDOWS)
    ts = min(ts, s)
    return pl.pallas_call(
        functools.partial(_pool_kernel, ts=ts),
        grid=(b, s // ts),
        in_specs=[
            pl.BlockSpec((1, ts, width), lambda i, j: (i, j, 0)),
            pl.BlockSpec(w_pool.shape, lambda i, j: (0, 0, 0)),
            pl.BlockSpec((1, width), lambda i, j: (0, 0)),
        ],
        out_specs=pl.BlockSpec((1, ts, width), lambda i, j: (i, j, 0)),
        out_shape=jax.ShapeDtypeStruct((b, s, width), BF16),
        scratch_shapes=[pltpu.VMEM((ts + POOL_HALO, width), F32)],
        compiler_params=_params("arbitrary", "arbitrary"),
        name="pool",
    )(h, w_pool, pool_scale)


def _swa_kernel(sink_ref, q0_ref, q1_ref, kv_ref, kvp_ref, o_ref):
    n = pl.program_id(1)
    kvw = SWA_KV_HEADS * SWA_HEAD_DIM
    q = jnp.concatenate([q0_ref[0], q1_ref[0]], axis=1)
    k = jnp.concatenate([kvp_ref[0, :, 0:kvw], kv_ref[0, :, 0:kvw]], axis=0)
    v = jnp.concatenate([kvp_ref[0, :, kvw:2 * kvw], kv_ref[0, :, kvw:2 * kvw]], axis=0)
    row = lax.broadcasted_iota(jnp.int32, (SWA_BLOCK, 2 * SWA_BLOCK), 0)
    col = lax.broadcasted_iota(jnp.int32, (SWA_BLOCK, 2 * SWA_BLOCK), 1)
    rel = row + SWA_BLOCK - col
    valid = (rel >= 0) & (rel < SWA_BLOCK) & ((col >= SWA_BLOCK) | (n > 0))
    group = SWA_HEADS // SWA_KV_HEADS
    outs = []
    for hq in range(SWA_HEADS):
        kv = hq // group
        qh = q[:, hq * SWA_HEAD_DIM:(hq + 1) * SWA_HEAD_DIM]
        kh = k[:, kv * SWA_HEAD_DIM:(kv + 1) * SWA_HEAD_DIM]
        vh = v[:, kv * SWA_HEAD_DIM:(kv + 1) * SWA_HEAD_DIM]
        sc = lax.dot_general(qh, kh, (((1,), (1,)), ((), ())), preferred_element_type=F32)
        sc = jnp.where(valid, sc, NEG)
        sink = sink_ref[hq]
        m = jnp.maximum(jnp.max(sc, axis=1, keepdims=True), sink)
        p = jnp.exp(sc - m)
        denom = jnp.sum(p, axis=1, keepdims=True) + jnp.exp(sink - m)
        o = jnp.dot(p.astype(BF16), vh, preferred_element_type=F32)
        outs.append(o / denom)
    o_ref[0] = jnp.concatenate(outs, axis=1).astype(o_ref.dtype)


def _swa(h, sinks):
    b, s, _ = h.shape
    nb = s // SWA_BLOCK
    blk = lambda c: pl.BlockSpec((1, SWA_BLOCK, IN_BLOCK), lambda i, j: (i, j, c))
    blk_prev = lambda c: pl.BlockSpec((1, SWA_BLOCK, IN_BLOCK), lambda i, j: (i, jnp.maximum(j - 1, 0), c))
    return pl.pallas_call(
        _swa_kernel,
        grid=(b, nb),
        in_specs=[pl.BlockSpec(memory_space=pltpu.SMEM),
                  blk(_Q_BLOCKS[0]), blk(_Q_BLOCKS[1]), blk(_KV_BLOCK), blk_prev(_KV_BLOCK)],
        out_specs=pl.BlockSpec((1, SWA_BLOCK, SWA_HEADS * SWA_HEAD_DIM), lambda i, j: (i, j, 0)),
        out_shape=jax.ShapeDtypeStruct((b, s, SWA_HEADS * SWA_HEAD_DIM), BF16),
        compiler_params=_params("parallel", "arbitrary"),
        name="swa",
    )(sinks, h, h, h, h)


def _mem_kernel(q_ref, kv_ref, o_ref):
    scale = MEM_HEAD_DIM ** -0.5
    width = MEM_HEADS * MEM_HEAD_DIM
    for hm in range(MEM_HEADS):
        cols = slice(hm * MEM_HEAD_DIM, (hm + 1) * MEM_HEAD_DIM)
        km = kv_ref[0, :, cols]
        vm = kv_ref[0, :, width + hm * MEM_HEAD_DIM:width + (hm + 1) * MEM_HEAD_DIM]
        sc = lax.dot_general(q_ref[0, :, cols], km, (((1,), (1,)), ((), ())), preferred_element_type=F32) * scale
        m = jnp.max(sc, axis=1, keepdims=True)
        p = jnp.exp(sc - m)
        denom = jnp.sum(p, axis=1, keepdims=True)
        o = jnp.dot(p.astype(BF16), vm, preferred_element_type=F32)
        o_ref[0, :, cols] = (o / denom).astype(o_ref.dtype)


def _mem_attention(h, kvm, tq):
    b, s, _ = h.shape
    width = MEM_HEADS * MEM_HEAD_DIM
    tq = min(tq, s)
    return pl.pallas_call(
        _mem_kernel,
        grid=(b, s // tq),
        in_specs=[
            pl.BlockSpec((1, tq, width), lambda i, j: (i, j, 4)),
            pl.BlockSpec((1,) + kvm.shape[1:], lambda i, j: (i, 0, 0)),
        ],
        out_specs=pl.BlockSpec((1, tq, width), lambda i, j: (i, j, 0)),
        out_shape=jax.ShapeDtypeStruct((b, s, width), BF16),
        compiler_params=_params("parallel", "arbitrary"),
        name="mem_attention",
    )(h, kvm)


def _outproj_kernel(pool_ref, swa_ref, mem_ref, wp_ref, ws_ref, wm_ref, x_ref, g_ref, b_ref, o_ref, ob_ref,
                    *, alpha):
    mix = jnp.dot(pool_ref[...], wp_ref[...], preferred_element_type=F32)
    mix += jnp.dot(swa_ref[...], ws_ref[...], preferred_element_type=F32)
    mix += jnp.dot(mem_ref[...], wm_ref[...], preferred_element_type=F32)
    z = alpha * x_ref[...] + mix
    mu = jnp.mean(z, axis=1, keepdims=True)
    zc = z - mu
    var = jnp.mean(zc * zc, axis=1, keepdims=True)
    y = zc * lax.rsqrt(var + LN_EPS) * g_ref[...] + b_ref[...]
    yt = y.T
    o_ref[...] = yt
    ob_ref[...] = yt.astype(BF16)


def _outproj_ln(pool_o, swa_o, mem_o, w_out, x2, g, b, alpha, tm):
    t, d = x2.shape
    tm = min(tm, t)
    wp, ws, wm = pool_o.shape[1], swa_o.shape[1], mem_o.shape[1]
    w_p, w_s, w_m = w_out[:wp], w_out[wp:wp + ws], w_out[wp + ws:]
    row = lambda w: pl.BlockSpec((tm, w), lambda i: (i, 0))
    full = lambda a: pl.BlockSpec(a.shape, lambda i: (0, 0))
    return pl.pallas_call(
        functools.partial(_outproj_kernel, alpha=alpha),
        grid=(t // tm,),
        in_specs=[row(wp), row(ws), row(wm), full(w_p), full(w_s), full(w_m), row(d),
                  pl.BlockSpec((1, d), lambda i: (0, 0)), pl.BlockSpec((1, d), lambda i: (0, 0))],
        out_specs=[pl.BlockSpec((d, tm), lambda i: (0, i))] * 2,
        out_shape=[jax.ShapeDtypeStruct((d, t), F32), jax.ShapeDtypeStruct((d, t), BF16)],
        compiler_params=_params("parallel"),
        name="outproj_ln",
    )(pool_o, swa_o, mem_o, w_p, w_s, w_m, x2, g, b)


def _merge_exchange_pairs(n):
    t = (n - 1).bit_length()
    pairs = []
    p = 1 << (t - 1)
    while p >= 1:
        q, r, d = 1 << (t - 1), 0, p
        while True:
            pairs += [(i, i + d) for i in range(n - d) if (i & p) == r]
            if q == p:
                break
            d, q, r = q - p, q // 2, p
        p //= 2
    return pairs


def _top16_rows_distinct(s, with_rank):
    k = PEER_TOPK
    lists = [s[SUBLANES * g:SUBLANES * (g + 1), :] for g in range(s.shape[0] // SUBLANES)]
    for i, j in _merge_exchange_pairs(len(lists)):
        lists[i], lists[j] = jnp.maximum(lists[i], lists[j]), jnp.minimum(lists[i], lists[j])
    vals = []
    for r in range(k):
        m = jnp.max(lists[0], axis=0, keepdims=True)
        vals.append(m)
        pop = lists[0] == m
        for d in range(k - 1 - r):
            lists[d] = jnp.where(pop, lists[d + 1], lists[d])
    rank = None
    if with_rank:
        rank = jnp.zeros(s.shape, F32)
        for r in range(k):
            rank = rank + jnp.where(vals[r] > s, 1.0, 0.0)
    count = jnp.sum(jnp.where(s >= vals[k - 1], 1.0, 0.0), axis=0, keepdims=True)
    strict = jnp.zeros_like(count)
    for r in range(k - 1):
        strict = strict + jnp.where(vals[r] > vals[r + 1], 1.0, 0.0)
    distinct = (count == float(k)) & (strict == float(k - 1))
    return rank, vals, jnp.where(distinct, float(k), 0.0)


def _top16_rows(s, break_ties, with_rank=True):
    if not break_ties:
        return _top16_rows_distinct(s, with_rank)
    n = s.shape[0]
    iota = lax.broadcasted_iota(jnp.int32, s.shape, 0).astype(F32)
    rank = jnp.full(s.shape, float(PEER_TOPK), F32)
    vals = []
    for r in range(PEER_TOPK):
        m = jnp.max(s, axis=0, keepdims=True)
        idx = jnp.min(jnp.where(s == m, iota, float(n)), axis=0, keepdims=True)
        hit = iota == idx
        rank = jnp.where(hit, float(r), rank)
        s = jnp.where(hit, -jnp.inf, s)
        vals.append(m)
    count = jnp.sum(jnp.where(rank < float(PEER_TOPK), 1.0, 0.0), axis=0, keepdims=True)
    return rank, vals, count


_CAND_SMALL_A = PEER_TOPK // 2
_CAND_ROWS = PEER_TOPK + (_CAND_SMALL_A - 1) * SUBLANES + SUBLANES


def _cand_constants():
    flat = np.full((_CAND_ROWS, 1), 1e9, np.float32)
    valid = np.zeros((_CAND_ROWS, 1), np.float32)
    for b in range(PEER_TOPK):
        flat[b, 0], valid[b, 0] = b, 1.0
    for a in range(1, _CAND_SMALL_A):
        base = PEER_TOPK + (a - 1) * SUBLANES
        for b in range(PEER_TOPK // (a + 1)):
            flat[base + b, 0], valid[base + b, 0] = a * PEER_TOPK + b, 1.0
    base = PEER_TOPK + (_CAND_SMALL_A - 1) * SUBLANES
    for k in range(SUBLANES):
        flat[base + k, 0], valid[base + k, 0] = (_CAND_SMALL_A + k) * PEER_TOPK, 1.0
    return jnp.asarray(flat), jnp.asarray(valid)


def _select_experts(s1, s2, flat, valid, break_ties):
    t = s1.shape[1]
    rank1, v1, count1 = _top16_rows(s1, break_ties, with_rank=break_ties)
    rank2, v2, count2 = _top16_rows(s2, break_ties)
    v2_lo = jnp.concatenate(v2[:SUBLANES], axis=0)
    v2_all = jnp.concatenate(v2, axis=0)
    v1_hi = jnp.concatenate(v1[_CAND_SMALL_A:], axis=0)
    groups = [v1[0] + v2_all]
    for a in range(1, _CAND_SMALL_A):
        groups.append(v1[a] + v2_lo)
    groups.append(v1_hi + v2[0])
    cand = jnp.concatenate(groups, axis=0)
    cand = jnp.where(valid > 0.5, cand, -jnp.inf)
    flat_b = jnp.broadcast_to(flat, cand.shape) if break_ties else None
    hits = jnp.zeros(cand.shape, F32)
    top = []
    for r in range(PEER_TOPK):
        m = jnp.max(cand, axis=0, keepdims=True)
        hit = cand == m
        if break_ties:
            pick = jnp.min(jnp.where(hit, flat_b, 2e9), axis=0, keepdims=True)
            hit = flat_b == pick
        hits = jnp.where(hit, 1.0, hits)
        cand = jnp.where(hit, -jnp.inf, cand)
        top.append(m)
    z = jnp.ones((1, t), F32)
    for r in range(1, PEER_TOPK):
        z = z + jnp.exp(top[r] - top[0])
    counts = [jnp.sum(hits[0:PEER_TOPK], axis=0, keepdims=True)]
    for a in range(1, _CAND_SMALL_A):
        base = PEER_TOPK + (a - 1) * SUBLANES
        counts.append(jnp.sum(hits[base:base + SUBLANES], axis=0, keepdims=True))
    base = PEER_TOPK + (_CAND_SMALL_A - 1) * SUBLANES
    for k in range(SUBLANES):
        counts.append(hits[base + k:base + k + 1])
    lim = jnp.zeros(s1.shape, F32)
    if rank1 is not None:
        for a in range(PEER_TOPK):
            lim = jnp.where(rank1 == float(a), counts[a], lim)
    else:
        for a in range(PEER_TOPK):
            step = counts[a] - counts[a + 1] if a + 1 < PEER_TOPK else counts[a]
            lim = lim + jnp.where(s1 >= v1[a], step, 0.0)
    e1n = jnp.exp(s1 - v1[0]) / z
    e2 = jnp.exp(s2 - v2[0])
    count3 = jnp.sum(hits, axis=0, keepdims=True)
    k = float(PEER_TOPK)
    ok = jnp.where((count1 == k) & (count2 == k) & (count3 == k), 1.0, 0.0)
    return lim, e1n, rank2, e2, ok


def _retrieve_kernel(wq_ref, x_ref, k1_ref, k2_ref, flat_ref, valid_ref,
                     lim_out_ref, e1_out_ref, r2_ref, e2_ref, q_ref, lim_ref, e1_ref, *, tq):
    q_ref[...] = jnp.dot(wq_ref[...], x_ref[...], preferred_element_type=F32)
    flat, valid = flat_ref[...], valid_ref[...]

    def head(h, carry):
        for c in range(tq // SELECT_TOKENS):
            lanes = slice(c * SELECT_TOKENS, (c + 1) * SELECT_TOKENS)
            r0 = pl.multiple_of(h * 2 * HALF_DIM, 2 * HALF_DIM)
            q1 = q_ref[pl.ds(r0, HALF_DIM), lanes].astype(BF16)
            q2 = q_ref[pl.ds(r0 + HALF_DIM, HALF_DIM), lanes].astype(BF16)
            s1 = jnp.dot(k1_ref[...], q1, preferred_element_type=F32)
            s2 = jnp.dot(k2_ref[...], q2, preferred_element_type=F32)

            def emit(break_ties):
                lim, e1n, rank2, e2, ok = _select_experts(s1, s2, flat, valid, break_ties)
                lim_ref[h, :, lanes] = lim
                e1_ref[h, :, lanes] = e1n
                r2_ref[h, :, lanes] = rank2.astype(BF16)
                e2_ref[h, :, lanes] = e2.astype(BF16)
                return ok

            ok = emit(False)

            @pl.when(jnp.min(ok) < 0.5)
            def _():
                emit(True)
        return carry

    lax.fori_loop(0, PEER_HEADS, head, 0)
    lim_out_ref[...] = jnp.swapaxes(lim_ref[...], 0, 1)
    e1_out_ref[...] = jnp.swapaxes(e1_ref[...], 0, 1)


def _retrieve(x1t, wq_t, k1, k2, tq):
    d, t = x1t.shape
    tq = min(tq, t)
    flat, valid = _cand_constants()
    full = lambda a: pl.BlockSpec(a.shape, lambda i: (0,) * a.ndim)
    out_keys = jax.ShapeDtypeStruct((N_KEYS, PEER_HEADS, t), F32)
    key_spec = pl.BlockSpec((N_KEYS, PEER_HEADS, tq), lambda i: (0, 0, i))
    out_packed = jax.ShapeDtypeStruct((PEER_HEADS, N_KEYS, t), BF16)
    out_spec = pl.BlockSpec((PEER_HEADS, N_KEYS, tq), lambda i: (0, 0, i))
    return pl.pallas_call(
        functools.partial(_retrieve_kernel, tq=tq),
        grid=(t // tq,),
        in_specs=[full(wq_t), pl.BlockSpec((d, tq), lambda i: (0, i)), full(k1), full(k2), full(flat), full(valid)],
        out_specs=[key_spec, key_spec, out_spec, out_spec],
        out_shape=[out_keys, out_keys, out_packed, out_packed],
        scratch_shapes=[pltpu.VMEM((wq_t.shape[0], tq), F32)] + [pltpu.VMEM((PEER_HEADS, N_KEYS, tq), F32)] * 2,
        compiler_params=_params("parallel"),
        name="peer_retrieve",
    )(wq_t, x1t, k1, k2, flat, valid)


def _gelu_exact(x):
    return 0.5 * x * (1.0 + lax.erf(x * (1.0 / math.sqrt(2.0))))


def _experts_kernel(xb_ref, u_ref, vt_ref, lim_ref, e1_ref, r2_in_ref, e2_in_ref, o_ref,
                    h_ref, a_ref, rows_ref, gate_ref, r2_ref, e2_ref, *, tm, te):
    e = pl.program_id(1)

    @pl.when(e == 0)
    def _():
        o_ref[...] = jnp.zeros(o_ref.shape, F32)
        r2_ref[:, 0:tm] = r2_in_ref[...]
        e2_ref[:, LANES:LANES + tm] = e2_in_ref[...]

    n_chunks = te // EXPERT_CHUNK

    def rows_of(p):
        return slice(p * EXPERT_CHUNK, (p + 1) * EXPERT_CHUNK)

    def gates(first_key):
        keys = range(first_key, first_key + GATE_KEYS)
        blocks = range(0, N_KEYS, GATE_ROWS)
        for i in keys:
            for hd in range(PEER_HEADS):
                for q, ref in enumerate((lim_ref, e1_ref)):
                    row = jnp.broadcast_to(ref[i, hd:hd + 1, :], (GATE_ROWS, tm)).astype(BF16)
                    rows_ref[q, i, hd, :, q * LANES:q * LANES + tm] = row
        for c in range(tm // LANES):
            lanes = slice(c * LANES, (c + 1) * LANES)
            lanes1 = slice((c + 1) * LANES, (c + 2) * LANES)
            gate = {i: {jb: jnp.zeros((GATE_ROWS, LANES), BF16) for jb in blocks} for i in keys}
            for hd in range(PEER_HEADS):
                lim = {i: rows_ref[0, i, hd, :, lanes] for i in keys}
                e1 = {i: rows_ref[1, i, hd, :, lanes1] for i in keys}
                for jb in blocks:
                    r2 = r2_ref[hd * N_KEYS + jb:hd * N_KEYS + jb + GATE_ROWS, lanes]
                    e2 = e2_ref[hd * N_KEYS + jb:hd * N_KEYS + jb + GATE_ROWS, lanes1]
                    for i in keys:
                        gate[i][jb] = gate[i][jb] + jnp.where(r2 < lim[i], e2 * e1[i], jnp.zeros_like(e2))
            for i in keys:
                for jb in blocks:
                    gate_ref[i * N_KEYS + jb:i * N_KEYS + jb + GATE_ROWS, lanes] = gate[i][jb]

    for first_key in range(0, te // N_KEYS, GATE_KEYS):
        gates(first_key)
    h_ref[...] = jnp.dot(u_ref[...], xb_ref[...], preferred_element_type=F32)
    for p in range(n_chunks):
        a_ref[rows_of(p), :] = gate_ref[rows_of(p), :] * _gelu_exact(h_ref[rows_of(p), :]).astype(BF16)
        o_ref[...] += jnp.dot(vt_ref[:, rows_of(p)], a_ref[rows_of(p), :], preferred_element_type=F32)


def _experts(xbt, u, vt, sel, tm, te):
    d, t = xbt.shape
    n_exp = u.shape[0]
    tm, te = min(tm, t), min(te, n_exp)
    tok = pl.BlockSpec((d, tm), lambda i, j: (0, i))
    sel_spec = pl.BlockSpec((PEER_HEADS * N_KEYS, tm), lambda i, j: (0, i))
    key_spec = pl.BlockSpec((te // N_KEYS, PEER_HEADS, tm), lambda i, j: (j, 0, i))
    lim, e1n, rank2, e2 = sel
    sel = (lim, e1n, rank2.reshape(-1, t), e2.reshape(-1, t))
    return pl.pallas_call(
        functools.partial(_experts_kernel, tm=tm, te=te),
        grid=(t // tm, n_exp // te),
        in_specs=[tok, pl.BlockSpec((te, d), lambda i, j: (j, 0)), pl.BlockSpec((d, te), lambda i, j: (0, j)),
                  key_spec, key_spec, sel_spec, sel_spec],
        out_specs=tok,
        out_shape=jax.ShapeDtypeStruct((d, t), F32),
        scratch_shapes=[pltpu.VMEM((te, tm), F32), pltpu.VMEM((te, tm), BF16),
                        pltpu.VMEM((2, te // N_KEYS, PEER_HEADS, GATE_ROWS, tm + LANES), BF16),
                        pltpu.VMEM((te, tm), BF16),
                        pltpu.VMEM((PEER_HEADS * N_KEYS, tm + LANES), BF16),
                        pltpu.VMEM((PEER_HEADS * N_KEYS, tm + LANES), BF16)],
        compiler_params=_params("parallel", "arbitrary"),
        name="peer_experts",
    )(xbt, u, vt, *sel)


def _ln_t_kernel(x_ref, f_ref, g_ref, b_ref, o_ref, *, alpha):
    z = (alpha * x_ref[...] + f_ref[...]).T
    mu = jnp.mean(z, axis=1, keepdims=True)
    zc = z - mu
    var = jnp.mean(zc * zc, axis=1, keepdims=True)
    o_ref[...] = zc * lax.rsqrt(var + LN_EPS) * g_ref[...] + b_ref[...]


def _residual_ln_t(x1t, fft, g, b, alpha, tm):
    d, t = x1t.shape
    tm = min(tm, t)
    tok = pl.BlockSpec((d, tm), lambda i: (0, i))
    vec = pl.BlockSpec((1, d), lambda i: (0, 0))
    return pl.pallas_call(
        functools.partial(_ln_t_kernel, alpha=alpha),
        grid=(t // tm,),
        in_specs=[tok, tok, vec, vec],
        out_specs=pl.BlockSpec((tm, d), lambda i: (i, 0)),
        out_shape=jax.ShapeDtypeStruct((t, d), F32),
        compiler_params=_params("parallel"),
        name="residual_ln",
    )(x1t, fft, g, b)


def kernel(x, mem, positions, w_in, w_mem_kv, w_pool, pool_scale, attn_sinks, w_out, ln1_g, ln1_b,
           w_peer_q, sub_keys_1, sub_keys_2, expert_u, expert_v, ln2_g, ln2_b):
    bsz, seq, d = x.shape
    depth = w_in.shape[0]
    t = bsz * seq
    alpha = (2.0 * depth) ** 0.25
    for l in range(depth):
        x2 = x.reshape(t, d)
        h = _inproj(x2, w_in[l].astype(BF16), positions, 512).reshape(bsz, seq, -1)
        mem2 = mem.reshape(-1, d).astype(BF16)
        kvm = _matmul(mem2, w_mem_kv[l].astype(BF16), BF16, 512, 512).reshape(bsz, mem.shape[1], -1)
        pool_o = _pool(h, w_pool[l].astype(BF16), pool_scale[l].reshape(1, -1), 512)
        swa_o = _swa(h, attn_sinks[l])
        mem_o = _mem_attention(h, kvm, 512)
        x1t, x1bt = _outproj_ln(pool_o.reshape(t, -1), swa_o.reshape(t, -1), mem_o.reshape(t, -1),
                          w_out[l].astype(BF16), x2, ln1_g[l].reshape(1, d), ln1_b[l].reshape(1, d), alpha, 512)
        sel = _retrieve(x1bt, w_peer_q[l].T.astype(BF16), sub_keys_1[l].astype(BF16),
                        sub_keys_2[l].astype(BF16), 512)
        fft = _experts(x1bt, expert_u[l].astype(BF16), expert_v[l].T.astype(BF16), sel, 1024, 512)
        x = _residual_ln_t(x1t, fft, ln2_g[l].reshape(1, d), ln2_b[l].reshape(1, d), alpha, 512).reshape(bsz, seq, d)
    return x
```

```python
import functools
import math

import numpy as np
import jax
import jax.numpy as jnp
from jax import lax
from jax.experimental import pallas as pl
from jax.experimental.pallas import tpu as pltpu

F32 = jnp.float32
BF16 = jnp.bfloat16

LANES = 128
SUBLANES = 8
VMEM_LIMIT_BYTES = 56 * 1024 * 1024

POOL_WINDOWS = (2, 4, 8, 16)
POOL_GROUP = 128
POOL_HALO = 16
SWA_HEAD_DIM = 64
SWA_HEADS = 16
SWA_KV_HEADS = 4
SWA_BLOCK = 128
ROPE_THETA = 500000.0
ROPE_DIM = 16
MEM_HEADS = 4
MEM_HEAD_DIM = 128
PEER_HEADS = 8
N_KEYS = 128
PEER_TOPK = 16
HALF_DIM = 128
LN_EPS = 1e-5
NEG = -1e30

EXPERT_CHUNK = 512
GATE_KEYS = 4
GATE_ROWS = 16
SELECT_TOKENS = 512


def _params(*semantics):
    return pltpu.CompilerParams(dimension_semantics=semantics, vmem_limit_bytes=VMEM_LIMIT_BYTES)


def _matmul_kernel(a_ref, b_ref, o_ref):
    o_ref[...] = jnp.dot(a_ref[...], b_ref[...], preferred_element_type=F32).astype(o_ref.dtype)


def _matmul(a, b, out_dtype, tm, tn):
    m, k = a.shape
    n = b.shape[1]
    tm, tn = min(tm, m), min(tn, n)
    return pl.pallas_call(
        _matmul_kernel,
        grid=(m // tm, n // tn),
        in_specs=[pl.BlockSpec((tm, k), lambda i, j: (i, 0)), pl.BlockSpec((k, tn), lambda i, j: (0, j))],
        out_specs=pl.BlockSpec((tm, tn), lambda i, j: (i, j)),
        out_shape=jax.ShapeDtypeStruct((m, n), out_dtype),
        compiler_params=_params("parallel", "arbitrary"),
        name="matmul",
    )(a, b)


IN_BLOCK = 512
_Q_BLOCKS = (1, 2)
_KV_BLOCK = 3


def _rope(x, c, sa, sb):
    half = ROPE_DIM // 2
    return x * c + pltpu.roll(x, LANES - half, 1) * sa + pltpu.roll(x, half, 1) * sb


def _inproj_kernel(x_ref, w_ref, pos_ref, freq_ref, sa_ref, sb_ref, o_ref):
    xb = x_ref[...].astype(BF16)
    ang = pos_ref[...].astype(F32) * freq_ref[...]
    s = jnp.sin(ang)
    c, sa, sb = jnp.cos(ang), s * sa_ref[...], s * sb_ref[...]
    for j in range(w_ref.shape[1] // IN_BLOCK):
        y = jnp.dot(xb, w_ref[:, j * IN_BLOCK:(j + 1) * IN_BLOCK], preferred_element_type=F32)
        for k in range(IN_BLOCK // LANES):
            piece = y[:, k * LANES:(k + 1) * LANES]
            if j in _Q_BLOCKS:
                piece = _rope(piece, c, sa, sb) * SWA_HEAD_DIM ** -0.5
            elif j == _KV_BLOCK and k < IN_BLOCK // LANES // 2:
                piece = _rope(piece, c, sa, sb)
            o_ref[:, j * IN_BLOCK + k * LANES:j * IN_BLOCK + (k + 1) * LANES] = piece.astype(o_ref.dtype)


def _rope_constants():
    lane = np.arange(LANES)
    d = lane % SWA_HEAD_DIM
    half = ROPE_DIM // 2
    inv_freq = np.float32(ROPE_THETA) ** (-np.arange(0, ROPE_DIM, 2, dtype=np.float32) / np.float32(ROPE_DIM))
    freq = np.where(d < ROPE_DIM, inv_freq[d % half], 0.0).astype(np.float32)
    sa = np.where(d < half, -1.0, 0.0).astype(np.float32)
    sb = np.where((d >= half) & (d < ROPE_DIM), 1.0, 0.0).astype(np.float32)
    return [jnp.asarray(a.reshape(1, LANES)) for a in (freq, sa, sb)]


def _inproj(x2, w, positions, tm):
    t, d = x2.shape
    n = w.shape[1]
    tm = min(tm, t)
    freq, sa, sb = _rope_constants()
    const = pl.BlockSpec((1, LANES), lambda i: (0, 0))
    return pl.pallas_call(
        _inproj_kernel,
        grid=(t // tm,),
        in_specs=[pl.BlockSpec((tm, d), lambda i: (i, 0)), pl.BlockSpec((d, n), lambda i: (0, 0)),
                  pl.BlockSpec((tm, 1), lambda i: (i, 0)), const, const, const],
        out_specs=pl.BlockSpec((tm, n), lambda i: (i, 0)),
        out_shape=jax.ShapeDtypeStruct((t, n), BF16),
        compiler_params=_params("parallel"),
        name="inproj_rope",
    )(x2, w, positions.reshape(t, 1), freq, sa, sb)


def _pool_kernel(v_ref, w_ref, scale_ref, o_ref, ext_ref, *, ts):
    s = pl.program_id(1)

    @pl.when(s == 0)
    def _():
        ext_ref[0:POOL_HALO, :] = jnp.zeros((POOL_HALO, ext_ref.shape[1]), F32)

    ext_ref[POOL_HALO:POOL_HALO + ts, :] = v_ref[0].astype(F32)
    pos = s * ts + lax.broadcasted_iota(jnp.int32, (ts, 1), 0)
    for g, w in enumerate(POOL_WINDOWS):
        cols = slice(g * POOL_GROUP, (g + 1) * POOL_GROUP)
        acc = ext_ref[POOL_HALO:POOL_HALO + ts, cols]
        for k in range(1, w):
            acc = acc + ext_ref[POOL_HALO - k:POOL_HALO - k + ts, cols]
        count = jnp.minimum(pos + 1, w).astype(F32)
        pooled = acc / count - ext_ref[POOL_HALO:POOL_HALO + ts, cols]
        y = jnp.dot(pooled.astype(BF16), w_ref[g], preferred_element_type=F32)
        o_ref[0, :, cols] = (y * scale_ref[:, cols]).astype(o_ref.dtype)
    ext_ref[0:POOL_HALO, :] = ext_ref[ts:ts + POOL_HALO, :]


def _pool(h, w_pool, pool_scale, ts):
    b, s, _ = h.shape
    width = POOL_GROUP * len(POOL_WINDOWS)
    ts = min(ts, s)
    return pl.pallas_call(
        functools.partial(_pool_kernel, ts=ts),
        grid=(b, s // ts),
        in_specs=[
            pl.BlockSpec((1, ts, width), lambda i, j: (i, j, 0)),
            pl.BlockSpec(w_pool.shape, lambda i, j: (0, 0, 0)),
            pl.BlockSpec((1, width), lambda i, j: (0, 0)),
        ],
        out_specs=pl.BlockSpec((1, ts, width), lambda i, j: (i, j, 0)),
        out_shape=jax.ShapeDtypeStruct((b, s, width), BF16),
        scratch_shapes=[pltpu.VMEM((ts + POOL_HALO, width), F32)],
        compiler_params=_params("arbitrary", "arbitrary"),
        name="pool",
    )(h, w_pool, pool_scale)


def _swa_kernel(sink_ref, q0_ref, q1_ref, kv_ref, kvp_ref, o_ref):
    n = pl.program_id(1)
    kvw = SWA_KV_HEADS * SWA_HEAD_DIM
    q = jnp.concatenate([q0_ref[0], q1_ref[0]], axis=1)
    k = jnp.concatenate([kvp_ref[0, :, 0:kvw], kv_ref[0, :, 0:kvw]], axis=0)
    v = jnp.concatenate([kvp_ref[0, :, kvw:2 * kvw], kv_ref[0, :, kvw:2 * kvw]], axis=0)
    row = lax.broadcasted_iota(jnp.int32, (SWA_BLOCK, 2 * SWA_BLOCK), 0)
    col = lax.broadcasted_iota(jnp.int32, (SWA_BLOCK, 2 * SWA_BLOCK), 1)
    rel = row + SWA_BLOCK - col
    valid = (rel >= 0) & (rel < SWA_BLOCK) & ((col >= SWA_BLOCK) | (n > 0))
    group = SWA_HEADS // SWA_KV_HEADS
    outs = []
    for hq in range(SWA_HEADS):
        kv = hq // group
        qh = q[:, hq * SWA_HEAD_DIM:(hq + 1) * SWA_HEAD_DIM]
        kh = k[:, kv * SWA_HEAD_DIM:(kv + 1) * SWA_HEAD_DIM]
        vh = v[:, kv * SWA_HEAD_DIM:(kv + 1) * SWA_HEAD_DIM]
        sc = lax.dot_general(qh, kh, (((1,), (1,)), ((), ())), preferred_element_type=F32)
        sc = jnp.where(valid, sc, NEG)
        sink = sink_ref[hq]
        m = jnp.maximum(jnp.max(sc, axis=1, keepdims=True), sink)
        p = jnp.exp(sc - m)
        denom = jnp.sum(p, axis=1, keepdims=True) + jnp.exp(sink - m)
        o = jnp.dot(p.astype(BF16), vh, preferred_element_type=F32)
        outs.append(o / denom)
    o_ref[0] = jnp.concatenate(outs, axis=1).astype(o_ref.dtype)


def _swa(h, sinks):
    b, s, _ = h.shape
    nb = s // SWA_BLOCK
    blk = lambda c: pl.BlockSpec((1, SWA_BLOCK, IN_BLOCK), lambda i, j: (i, j, c))
    blk_prev = lambda c: pl.BlockSpec((1, SWA_BLOCK, IN_BLOCK), lambda i, j: (i, jnp.maximum(j - 1, 0), c))
    return pl.pallas_call(
        _swa_kernel,
        grid=(b, nb),
        in_specs=[pl.BlockSpec(memory_space=pltpu.SMEM),
                  blk(_Q_BLOCKS[0]), blk(_Q_BLOCKS[1]), blk(_KV_BLOCK), blk_prev(_KV_BLOCK)],
        out_specs=pl.BlockSpec((1, SWA_BLOCK, SWA_HEADS * SWA_HEAD_DIM), lambda i, j: (i, j, 0)),
        out_shape=jax.ShapeDtypeStruct((b, s, SWA_HEADS * SWA_HEAD_DIM), BF16),
        compiler_params=_params("parallel", "arbitrary"),
        name="swa",
    )(sinks, h, h, h, h)


def _mem_kernel(q_ref, kv_ref, o_ref):
    scale = MEM_HEAD_DIM ** -0.5
    width = MEM_HEADS * MEM_HEAD_DIM
    for hm in range(MEM_HEADS):
        cols = slice(hm * MEM_HEAD_DIM, (hm + 1) * MEM_HEAD_DIM)
        km = kv_ref[0, :, cols]
        vm = kv_ref[0, :, width + hm * MEM_HEAD_DIM:width + (hm + 1) * MEM_HEAD_DIM]
        sc = lax.dot_general(q_ref[0, :, cols], km, (((1,), (1,)), ((), ())), preferred_element_type=F32) * scale
        m = jnp.max(sc, axis=1, keepdims=True)
        p = jnp.exp(sc - m)
        denom = jnp.sum(p, axis=1, keepdims=True)
        o = jnp.dot(p.astype(BF16), vm, preferred_element_type=F32)
        o_ref[0, :, cols] = (o / denom).astype(o_ref.dtype)


def _mem_attention(h, kvm, tq):
    b, s, _ = h.shape
    width = MEM_HEADS * MEM_HEAD_DIM
    tq = min(tq, s)
    return pl.pallas_call(
        _mem_kernel,
        grid=(b, s // tq),
        in_specs=[
            pl.BlockSpec((1, tq, width), lambda i, j: (i, j, 4)),
            pl.BlockSpec((1,) + kvm.shape[1:], lambda i, j: (i, 0, 0)),
        ],
        out_specs=pl.BlockSpec((1, tq, width), lambda i, j: (i, j, 0)),
        out_shape=jax.ShapeDtypeStruct((b, s, width), BF16),
        compiler_params=_params("parallel", "arbitrary"),
        name="mem_attention",
    )(h, kvm)


def _outproj_kernel(pool_ref, swa_ref, mem_ref, wp_ref, ws_ref, wm_ref, x_ref, g_ref, b_ref, o_ref, ob_ref,
                    *, alpha):
    mix = jnp.dot(pool_ref[...], wp_ref[...], preferred_element_type=F32)
    mix += jnp.dot(swa_ref[...], ws_ref[...], preferred_element_type=F32)
    mix += jnp.dot(mem_ref[...], wm_ref[...], preferred_element_type=F32)
    z = alpha * x_ref[...] + mix
    mu = jnp.mean(z, axis=1, keepdims=True)
    zc = z - mu
    var = jnp.mean(zc * zc, axis=1, keepdims=True)
    y = zc * lax.rsqrt(var + LN_EPS) * g_ref[...] + b_ref[...]
    yt = y.T
    o_ref[...] = yt
    ob_ref[...] = yt.astype(BF16)


def _outproj_ln(pool_o, swa_o, mem_o, w_out, x2, g, b, alpha, tm):
    t, d = x2.shape
    tm = min(tm, t)
    wp, ws, wm = pool_o.shape[1], swa_o.shape[1], mem_o.shape[1]
    w_p, w_s, w_m = w_out[:wp], w_out[wp:wp + ws], w_out[wp + ws:]
    row = lambda w: pl.BlockSpec((tm, w), lambda i: (i, 0))
    full = lambda a: pl.BlockSpec(a.shape, lambda i: (0, 0))
    return pl.pallas_call(
        functools.partial(_outproj_kernel, alpha=alpha),
        grid=(t // tm,),
        in_specs=[row(wp), row(ws), row(wm), full(w_p), full(w_s), full(w_m), row(d),
                  pl.BlockSpec((1, d), lambda i: (0, 0)), pl.BlockSpec((1, d), lambda i: (0, 0))],
        out_specs=[pl.BlockSpec((d, tm), lambda i: (0, i))] * 2,
        out_shape=[jax.ShapeDtypeStruct((d, t), F32), jax.ShapeDtypeStruct((d, t), BF16)],
        compiler_params=_params("parallel"),
        name="outproj_ln",
    )(pool_o, swa_o, mem_o, w_p, w_s, w_m, x2, g, b)


def _merge_exchange_pairs(n):
    t = (n - 1).bit_length()
    pairs = []
    p = 1 << (t - 1)
    while p >= 1:
        q, r, d = 1 << (t - 1), 0, p
        while True:
            pairs += [(i, i + d) for i in range(n - d) if (i & p) == r]
            if q == p:
                break
            d, q, r = q - p, q // 2, p
        p //= 2
    return pairs


def _top16_rows_distinct(s, with_rank):
    k = PEER_TOPK
    lists = [s[SUBLANES * g:SUBLANES * (g + 1), :] for g in range(s.shape[0] // SUBLANES)]
    for i, j in _merge_exchange_pairs(len(lists)):
        lists[i], lists[j] = jnp.maximum(lists[i], lists[j]), jnp.minimum(lists[i], lists[j])
    vals = []
    for r in range(k):
        m = jnp.max(lists[0], axis=0, keepdims=True)
        vals.append(m)
        pop = lists[0] == m
        for d in range(k - 1 - r):
            lists[d] = jnp.where(pop, lists[d + 1], lists[d])
    rank = None
    if with_rank:
        rank = jnp.zeros(s.shape, F32)
        for r in range(k):
            rank = rank + jnp.where(vals[r] > s, 1.0, 0.0)
    count = jnp.sum(jnp.where(s >= vals[k - 1], 1.0, 0.0), axis=0, keepdims=True)
    strict = jnp.zeros_like(count)
    for r in range(k - 1):
        strict = strict + jnp.where(vals[r] > vals[r + 1], 1.0, 0.0)
    distinct = (count == float(k)) & (strict == float(k - 1))
    return rank, vals, jnp.where(distinct, float(k), 0.0)


def _top16_rows(s, break_ties, with_rank=True):
    if not break_ties:
        return _top16_rows_distinct(s, with_rank)
    n = s.shape[0]
    iota = lax.broadcasted_iota(jnp.int32, s.shape, 0).astype(F32)
    rank = jnp.full(s.shape, float(PEER_TOPK), F32)
    vals = []
    for r in range(PEER_TOPK):
        m = jnp.max(s, axis=0, keepdims=True)
        idx = jnp.min(jnp.where(s == m, iota, float(n)), axis=0, keepdims=True)
        hit = iota == idx
        rank = jnp.where(hit, float(r), rank)
        s = jnp.where(hit, -jnp.inf, s)
        vals.append(m)
    count = jnp.sum(jnp.where(rank < float(PEER_TOPK), 1.0, 0.0), axis=0, keepdims=True)
    return rank, vals, count


_CAND_SMALL_A = PEER_TOPK // 2
_CAND_ROWS = PEER_TOPK + (_CAND_SMALL_A - 1) * SUBLANES + SUBLANES


def _cand_constants():
    flat = np.full((_CAND_ROWS, 1), 1e9, np.float32)
    valid = np.zeros((_CAND_ROWS, 1), np.float32)
    for b in range(PEER_TOPK):
        flat[b, 0], valid[b, 0] = b, 1.0
    for a in range(1, _CAND_SMALL_A):
        base = PEER_TOPK + (a - 1) * SUBLANES
        for b in range(PEER_TOPK // (a + 1)):
            flat[base + b, 0], valid[base + b, 0] = a * PEER_TOPK + b, 1.0
    base = PEER_TOPK + (_CAND_SMALL_A - 1) * SUBLANES
    for k in range(SUBLANES):
        flat[base + k, 0], valid[base + k, 0] = (_CAND_SMALL_A + k) * PEER_TOPK, 1.0
    return jnp.asarray(flat), jnp.asarray(valid)


def _select_experts(s1, s2, flat, valid, break_ties):
    t = s1.shape[1]
    rank1, v1, count1 = _top16_rows(s1, break_ties, with_rank=break_ties)
    rank2, v2, count2 = _top16_rows(s2, break_ties)
    v2_lo = jnp.concatenate(v2[:SUBLANES], axis=0)
    v2_all = jnp.concatenate(v2, axis=0)
    v1_hi = jnp.concatenate(v1[_CAND_SMALL_A:], axis=0)
    groups = [v1[0] + v2_all]
    for a in range(1, _CAND_SMALL_A):
        groups.append(v1[a] + v2_lo)
    groups.append(v1_hi + v2[0])
    cand = jnp.concatenate(groups, axis=0)
    cand = jnp.where(valid > 0.5, cand, -jnp.inf)
    flat_b = jnp.broadcast_to(flat, cand.shape) if break_ties else None
    all_cand = cand
    hits = jnp.zeros(cand.shape, F32)
    top = []
    for r in range(PEER_TOPK):
        m = jnp.max(cand, axis=0, keepdims=True)
        hit = cand == m
        if break_ties:
            pick = jnp.min(jnp.where(hit, flat_b, 2e9), axis=0, keepdims=True)
            hit = flat_b == pick
            hits = jnp.where(hit, 1.0, hits)
        cand = jnp.where(hit, -jnp.inf, cand)
        top.append(m)
    if not break_ties:
        hits = jnp.where(all_cand >= top[PEER_TOPK - 1], 1.0, 0.0)
    z = jnp.ones((1, t), F32)
    for r in range(1, PEER_TOPK):
        z = z + jnp.exp(top[r] - top[0])
    counts = [jnp.sum(hits[0:PEER_TOPK], axis=0, keepdims=True)]
    for a in range(1, _CAND_SMALL_A):
        base = PEER_TOPK + (a - 1) * SUBLANES
        counts.append(jnp.sum(hits[base:base + SUBLANES], axis=0, keepdims=True))
    base = PEER_TOPK + (_CAND_SMALL_A - 1) * SUBLANES
    for k in range(SUBLANES):
        counts.append(hits[base + k:base + k + 1])
    lim = jnp.zeros(s1.shape, F32)
    if rank1 is not None:
        for a in range(PEER_TOPK):
            lim = jnp.where(rank1 == float(a), counts[a], lim)
    else:
        for a in range(PEER_TOPK):
            step = counts[a] - counts[a + 1] if a + 1 < PEER_TOPK else counts[a]
            lim = lim + jnp.where(s1 >= v1[a], step, 0.0)
    e1n = jnp.exp(s1 - v1[0]) * (1.0 / z)
    e2 = jnp.exp(s2 - v2[0])
    count3 = jnp.sum(hits, axis=0, keepdims=True)
    k = float(PEER_TOPK)
    ok = jnp.where((count1 == k) & (count2 == k) & (count3 == k), 1.0, 0.0)
    return lim, e1n, rank2, e2, ok


def _retrieve_kernel(wq_ref, x_ref, k1_ref, k2_ref, flat_ref, valid_ref,
                     lim_out_ref, e1_out_ref, r2_ref, e2_ref, q_ref, lim_ref, e1_ref, *, tq):
    q_ref[...] = jnp.dot(wq_ref[...], x_ref[...], preferred_element_type=F32)
    flat, valid = flat_ref[...], valid_ref[...]

    def head(h, carry):
        for c in range(tq // SELECT_TOKENS):
            lanes = slice(c * SELECT_TOKENS, (c + 1) * SELECT_TOKENS)
            r0 = pl.multiple_of(h * 2 * HALF_DIM, 2 * HALF_DIM)
            q1 = q_ref[pl.ds(r0, HALF_DIM), lanes].astype(BF16)
            q2 = q_ref[pl.ds(r0 + HALF_DIM, HALF_DIM), lanes].astype(BF16)
            s1 = jnp.dot(k1_ref[...], q1, preferred_element_type=F32)
            s2 = jnp.dot(k2_ref[...], q2, preferred_element_type=F32)

            def emit(break_ties):
                lim, e1n, rank2, e2, ok = _select_experts(s1, s2, flat, valid, break_ties)
                lim_ref[h, :, lanes] = lim
                e1_ref[h, :, lanes] = e1n
                r2_ref[h, :, lanes] = rank2.astype(BF16)
                e2_ref[h, :, lanes] = e2.astype(BF16)
                return ok

            ok = emit(False)

            @pl.when(jnp.min(ok) < 0.5)
            def _():
                emit(True)
        return carry

    lax.fori_loop(0, PEER_HEADS, head, 0)
    lim_out_ref[...] = jnp.swapaxes(lim_ref[...], 0, 1)
    e1_out_ref[...] = jnp.swapaxes(e1_ref[...], 0, 1)


def _retrieve(x1t, wq_t, k1, k2, tq):
    d, t = x1t.shape
    tq = min(tq, t)
    flat, valid = _cand_constants()
    full = lambda a: pl.BlockSpec(a.shape, lambda i: (0,) * a.ndim)
    out_keys = jax.ShapeDtypeStruct((N_KEYS, PEER_HEADS, t), F32)
    key_spec = pl.BlockSpec((N_KEYS, PEER_HEADS, tq), lambda i: (0, 0, i))
    out_packed = jax.ShapeDtypeStruct((PEER_HEADS, N_KEYS, t), BF16)
    out_spec = pl.BlockSpec((PEER_HEADS, N_KEYS, tq), lambda i: (0, 0, i))
    return pl.pallas_call(
        functools.partial(_retrieve_kernel, tq=tq),
        grid=(t // tq,),
        in_specs=[full(wq_t), pl.BlockSpec((d, tq), lambda i: (0, i)), full(k1), full(k2), full(flat), full(valid)],
        out_specs=[key_spec, key_spec, out_spec, out_spec],
        out_shape=[out_keys, out_keys, out_packed, out_packed],
        scratch_shapes=[pltpu.VMEM((wq_t.shape[0], tq), F32)] + [pltpu.VMEM((PEER_HEADS, N_KEYS, tq), F32)] * 2,
        compiler_params=_params("parallel"),
        name="peer_retrieve",
    )(wq_t, x1t, k1, k2, flat, valid)


def _gelu_exact(x):
    return 0.5 * x * (1.0 + lax.erf(x * (1.0 / math.sqrt(2.0))))


def _experts_kernel(xb_ref, u_ref, vt_ref, lim_ref, e1_ref, r2_in_ref, e2_in_ref, o_ref,
                    h_ref, a_ref, rows_ref, gate_ref, r2_ref, e2_ref, *, tm, te):
    e = pl.program_id(1)

    @pl.when(e == 0)
    def _():
        o_ref[...] = jnp.zeros(o_ref.shape, F32)
        r2_ref[:, 0:tm] = r2_in_ref[...]
        e2_ref[:, LANES:LANES + tm] = e2_in_ref[...]

    n_chunks = te // EXPERT_CHUNK

    def rows_of(p):
        return slice(p * EXPERT_CHUNK, (p + 1) * EXPERT_CHUNK)

    def gates(first_key):
        keys = range(first_key, first_key + GATE_KEYS)
        blocks = range(0, N_KEYS, GATE_ROWS)
        for i in keys:
            for hd in range(PEER_HEADS):
                for q, ref in enumerate((lim_ref, e1_ref)):
                    row = jnp.broadcast_to(ref[i, hd:hd + 1, :], (GATE_ROWS, tm)).astype(BF16)
                    rows_ref[q, i, hd, :, q * LANES:q * LANES + tm] = row
        for c in range(tm // LANES):
            lanes = slice(c * LANES, (c + 1) * LANES)
            lanes1 = slice((c + 1) * LANES, (c + 2) * LANES)
            gate = {i: {jb: jnp.zeros((GATE_ROWS, LANES), BF16) for jb in blocks} for i in keys}
            for hd in range(PEER_HEADS):
                lim = {i: rows_ref[0, i, hd, :, lanes] for i in keys}
                e1 = {i: rows_ref[1, i, hd, :, lanes1] for i in keys}
                for jb in blocks:
                    r2 = r2_ref[hd * N_KEYS + jb:hd * N_KEYS + jb + GATE_ROWS, lanes]
                    e2 = e2_ref[hd * N_KEYS + jb:hd * N_KEYS + jb + GATE_ROWS, lanes1]
                    for i in keys:
                        gate[i][jb] = gate[i][jb] + jnp.where(r2 < lim[i], e2 * e1[i], jnp.zeros_like(e2))
            for i in keys:
                for jb in blocks:
                    gate_ref[i * N_KEYS + jb:i * N_KEYS + jb + GATE_ROWS, lanes] = gate[i][jb]

    for first_key in range(0, te // N_KEYS, GATE_KEYS):
        gates(first_key)
    h_ref[...] = jnp.dot(u_ref[...], xb_ref[...], preferred_element_type=F32)
    for p in range(n_chunks):
        a_ref[rows_of(p), :] = gate_ref[rows_of(p), :] * _gelu_exact(h_ref[rows_of(p), :]).astype(BF16)
        o_ref[...] += jnp.dot(vt_ref[:, rows_of(p)], a_ref[rows_of(p), :], preferred_element_type=F32)


def _experts(xbt, u, vt, sel, tm, te):
    d, t = xbt.shape
    n_exp = u.shape[0]
    tm, te = min(tm, t), min(te, n_exp)
    tok = pl.BlockSpec((d, tm), lambda i, j: (0, i))
    sel_spec = pl.BlockSpec((PEER_HEADS * N_KEYS, tm), lambda i, j: (0, i))
    key_spec = pl.BlockSpec((te // N_KEYS, PEER_HEADS, tm), lambda i, j: (j, 0, i))
    lim, e1n, rank2, e2 = sel
    sel = (lim, e1n, rank2.reshape(-1, t), e2.reshape(-1, t))
    return pl.pallas_call(
        functools.partial(_experts_kernel, tm=tm, te=te),
        grid=(t // tm, n_exp // te),
        in_specs=[tok, pl.BlockSpec((te, d), lambda i, j: (j, 0)), pl.BlockSpec((d, te), lambda i, j: (0, j)),
                  key_spec, key_spec, sel_spec, sel_spec],
        out_specs=tok,
        out_shape=jax.ShapeDtypeStruct((d, t), F32),
        scratch_shapes=[pltpu.VMEM((te, tm), F32), pltpu.VMEM((te, tm), BF16),
                        pltpu.VMEM((2, te // N_KEYS, PEER_HEADS, GATE_ROWS, tm + LANES), BF16),
                        pltpu.VMEM((te, tm), BF16),
                        pltpu.VMEM((PEER_HEADS * N_KEYS, tm + LANES), BF16),
                        pltpu.VMEM((PEER_HEADS * N_KEYS, tm + LANES), BF16)],
        compiler_params=_params("parallel", "arbitrary"),
        name="peer_experts",
    )(xbt, u, vt, *sel)


def _ln_t_kernel(x_ref, f_ref, g_ref, b_ref, o_ref, *, alpha):
    z = (alpha * x_ref[...] + f_ref[...]).T
    mu = jnp.mean(z, axis=1, keepdims=True)
    zc = z - mu
    var = jnp.mean(zc * zc, axis=1, keepdims=True)
    o_ref[...] = zc * lax.rsqrt(var + LN_EPS) * g_ref[...] + b_ref[...]


def _residual_ln_t(x1t, fft, g, b, alpha, tm):
    d, t = x1t.shape
    tm = min(tm, t)
    tok = pl.BlockSpec((d, tm), lambda i: (0, i))
    vec = pl.BlockSpec((1, d), lambda i: (0, 0))
    return pl.pallas_call(
        functools.partial(_ln_t_kernel, alpha=alpha),
        grid=(t // tm,),
        in_specs=[tok, tok, vec, vec],
        out_specs=pl.BlockSpec((tm, d), lambda i: (i, 0)),
        out_shape=jax.ShapeDtypeStruct((t, d), F32),
        compiler_params=_params("parallel"),
        name="residual_ln",
    )(x1t, fft, g, b)


def kernel(x, mem, positions, w_in, w_mem_kv, w_pool, pool_scale, attn_sinks, w_out, ln1_g, ln1_b,
           w_peer_q, sub_keys_1, sub_keys_2, expert_u, expert_v, ln2_g, ln2_b):
    bsz, seq, d = x.shape
    depth = w_in.shape[0]
    t = bsz * seq
    alpha = (2.0 * depth) ** 0.25
    for l in range(depth):
        x2 = x.reshape(t, d)
        h = _inproj(x2, w_in[l].astype(BF16), positions, 512).reshape(bsz, seq, -1)
        mem2 = mem.reshape(-1, d).astype(BF16)
        kvm = _matmul(mem2, w_mem_kv[l].astype(BF16), BF16, 512, 512).reshape(bsz, mem.shape[1], -1)
        pool_o = _pool(h, w_pool[l].astype(BF16), pool_scale[l].reshape(1, -1), 512)
        swa_o = _swa(h, attn_sinks[l])
        mem_o = _mem_attention(h, kvm, 512)
        x1t, x1bt = _outproj_ln(pool_o.reshape(t, -1), swa_o.reshape(t, -1), mem_o.reshape(t, -1),
                          w_out[l].astype(BF16), x2, ln1_g[l].reshape(1, d), ln1_b[l].reshape(1, d), alpha, 512)
        sel = _retrieve(x1bt, w_peer_q[l].T.astype(BF16), sub_keys_1[l].astype(BF16),
                        sub_keys_2[l].astype(BF16), 512)
        fft = _experts(x1bt, expert_u[l].astype(BF16), expert_v[l].T.astype(BF16), sel, 1024, 512)
        x = _residual_ln_t(x1t, fft, ln2_g[l].reshape(1, d), ln2_b[l].reshape(1, d), alpha, 512).reshape(bsz, seq, d)
    return x
```

```python
import functools
import math

import numpy as np
import jax
import jax.numpy as jnp
from jax import lax
from jax.experimental import pallas as pl
from jax.experimental.pallas import tpu as pltpu

F32 = jnp.float32
BF16 = jnp.bfloat16

LANES = 128
SUBLANES = 8
VMEM_LIMIT_BYTES = 56 * 1024 * 1024

POOL_WINDOWS = (2, 4, 8, 16)
POOL_GROUP = 128
POOL_HALO = 16
SWA_HEAD_DIM = 64
SWA_HEADS = 16
SWA_KV_HEADS = 4
SWA_BLOCK = 128
ROPE_THETA = 500000.0
ROPE_DIM = 16
MEM_HEADS = 4
MEM_HEAD_DIM = 128
PEER_HEADS = 8
N_KEYS = 128
PEER_TOPK = 16
HALF_DIM = 128
LN_EPS = 1e-5
NEG = -1e30

EXPERT_CHUNK = 512
GATE_KEYS = 4
GATE_ROWS = 16
SELECT_TOKENS = 512


def _params(*semantics):
    return pltpu.CompilerParams(dimension_semantics=semantics, vmem_limit_bytes=VMEM_LIMIT_BYTES)


def _matmul_kernel(a_ref, b_ref, o_ref):
    o_ref[...] = jnp.dot(a_ref[...], b_ref[...], preferred_element_type=F32).astype(o_ref.dtype)


def _matmul(a, b, out_dtype, tm, tn):
    m, k = a.shape
    n = b.shape[1]
    tm, tn = min(tm, m), min(tn, n)
    return pl.pallas_call(
        _matmul_kernel,
        grid=(m // tm, n // tn),
        in_specs=[pl.BlockSpec((tm, k), lambda i, j: (i, 0)), pl.BlockSpec((k, tn), lambda i, j: (0, j))],
        out_specs=pl.BlockSpec((tm, tn), lambda i, j: (i, j)),
        out_shape=jax.ShapeDtypeStruct((m, n), out_dtype),
        compiler_params=_params("parallel", "arbitrary"),
        name="matmul",
    )(a, b)


IN_BLOCK = 512
_Q_BLOCKS = (1, 2)
_KV_BLOCK = 3


def _rope(x, c, sa, sb):
    half = ROPE_DIM // 2
    return x * c + pltpu.roll(x, LANES - half, 1) * sa + pltpu.roll(x, half, 1) * sb


def _inproj_kernel(x_ref, w_ref, pos_ref, freq_ref, sa_ref, sb_ref, o_ref):
    xb = x_ref[...].astype(BF16)
    ang = pos_ref[...].astype(F32) * freq_ref[...]
    s = jnp.sin(ang)
    c, sa, sb = jnp.cos(ang), s * sa_ref[...], s * sb_ref[...]
    for j in range(w_ref.shape[1] // IN_BLOCK):
        y = jnp.dot(xb, w_ref[:, j * IN_BLOCK:(j + 1) * IN_BLOCK], preferred_element_type=F32)
        for k in range(IN_BLOCK // LANES):
            piece = y[:, k * LANES:(k + 1) * LANES]
            if j in _Q_BLOCKS:
                piece = _rope(piece, c, sa, sb) * SWA_HEAD_DIM ** -0.5
            elif j == _KV_BLOCK and k < IN_BLOCK // LANES // 2:
                piece = _rope(piece, c, sa, sb)
            o_ref[:, j * IN_BLOCK + k * LANES:j * IN_BLOCK + (k + 1) * LANES] = piece.astype(o_ref.dtype)


def _rope_constants():
    lane = np.arange(LANES)
    d = lane % SWA_HEAD_DIM
    half = ROPE_DIM // 2
    inv_freq = np.float32(ROPE_THETA) ** (-np.arange(0, ROPE_DIM, 2, dtype=np.float32) / np.float32(ROPE_DIM))
    freq = np.where(d < ROPE_DIM, inv_freq[d % half], 0.0).astype(np.float32)
    sa = np.where(d < half, -1.0, 0.0).astype(np.float32)
    sb = np.where((d >= half) & (d < ROPE_DIM), 1.0, 0.0).astype(np.float32)
    return [jnp.asarray(a.reshape(1, LANES)) for a in (freq, sa, sb)]


def _inproj(x2, w, positions, tm):
    t, d = x2.shape
    n = w.shape[1]
    tm = min(tm, t)
    freq, sa, sb = _rope_constants()
    const = pl.BlockSpec((1, LANES), lambda i: (0, 0))
    return pl.pallas_call(
        _inproj_kernel,
        grid=(t // tm,),
        in_specs=[pl.BlockSpec((tm, d), lambda i: (i, 0)), pl.BlockSpec((d, n), lambda i: (0, 0)),
                  pl.BlockSpec((tm, 1), lambda i: (i, 0)), const, const, const],
        out_specs=pl.BlockSpec((tm, n), lambda i: (i, 0)),
        out_shape=jax.ShapeDtypeStruct((t, n), BF16),
        compiler_params=_params("parallel"),
        name="inproj_rope",
    )(x2, w, positions.reshape(t, 1), freq, sa, sb)


def _pool_kernel(v_ref, w_ref, scale_ref, o_ref, ext_ref, *, ts):
    s = pl.program_id(1)

    @pl.when(s == 0)
    def _():
        ext_ref[0:POOL_HALO, :] = jnp.zeros((POOL_HALO, ext_ref.shape[1]), F32)

    ext_ref[POOL_HALO:POOL_HALO + ts, :] = v_ref[0].astype(F32)
    pos = s * ts + lax.broadcasted_iota(jnp.int32, (ts, 1), 0)
    for g, w in enumerate(POOL_WINDOWS):
        cols = slice(g * POOL_GROUP, (g + 1) * POOL_GROUP)
        acc = ext_ref[POOL_HALO:POOL_HALO + ts, cols]
        for k in range(1, w):
            acc = acc + ext_ref[POOL_HALO - k:POOL_HALO - k + ts, cols]
        count = jnp.minimum(pos + 1, w).astype(F32)
        pooled = acc / count - ext_ref[POOL_HALO:POOL_HALO + ts, cols]
        y = jnp.dot(pooled.astype(BF16), w_ref[g], preferred_element_type=F32)
        o_ref[0, :, cols] = (y * scale_ref[:, cols]).astype(o_ref.dtype)
    ext_ref[0:POOL_HALO, :] = ext_ref[ts:ts + POOL_HALO, :]


def _pool(h, w_pool, pool_scale, ts):
    b, s, _ = h.shape
    width = POOL_GROUP * len(POOL_WINDOWS)
    ts = min(ts, s)
    return pl.pallas_call(
        functools.partial(_pool_kernel, ts=ts),
        grid=(b, s // ts),
        in_specs=[
            pl.BlockSpec((1, ts, width), lambda i, j: (i, j, 0)),
            pl.BlockSpec(w_pool.shape, lambda i, j: (0, 0, 0)),
            pl.BlockSpec((1, width), lambda i, j: (0, 0)),
        ],
        out_specs=pl.BlockSpec((1, ts, width), lambda i, j: (i, j, 0)),
        out_shape=jax.ShapeDtypeStruct((b, s, width), BF16),
        scratch_shapes=[pltpu.VMEM((ts + POOL_HALO, width), F32)],
        compiler_params=_params("arbitrary", "arbitrary"),
        name="pool",
    )(h, w_pool, pool_scale)


def _swa_kernel(sink_ref, q0_ref, q1_ref, kv_ref, kvp_ref, o_ref):
    n = pl.program_id(1)
    kvw = SWA_KV_HEADS * SWA_HEAD_DIM
    q = jnp.concatenate([q0_ref[0], q1_ref[0]], axis=1)
    k = jnp.concatenate([kvp_ref[0, :, 0:kvw], kv_ref[0, :, 0:kvw]], axis=0)
    v = jnp.concatenate([kvp_ref[0, :, kvw:2 * kvw], kv_ref[0, :, kvw:2 * kvw]], axis=0)
    row = lax.broadcasted_iota(jnp.int32, (SWA_BLOCK, 2 * SWA_BLOCK), 0)
    col = lax.broadcasted_iota(jnp.int32, (SWA_BLOCK, 2 * SWA_BLOCK), 1)
    rel = row + SWA_BLOCK - col
    valid = (rel >= 0) & (rel < SWA_BLOCK) & ((col >= SWA_BLOCK) | (n > 0))
    group = SWA_HEADS // SWA_KV_HEADS
    outs = []
    for hq in range(SWA_HEADS):
        kv = hq // group
        qh = q[:, hq * SWA_HEAD_DIM:(hq + 1) * SWA_HEAD_DIM]
        kh = k[:, kv * SWA_HEAD_DIM:(kv + 1) * SWA_HEAD_DIM]
        vh = v[:, kv * SWA_HEAD_DIM:(kv + 1) * SWA_HEAD_DIM]
        sc = lax.dot_general(qh, kh, (((1,), (1,)), ((), ())), preferred_element_type=F32)
        sc = jnp.where(valid, sc, NEG)
        sink = sink_ref[hq]
        m = jnp.maximum(jnp.max(sc, axis=1, keepdims=True), sink)
        p = jnp.exp(sc - m)
        denom = jnp.sum(p, axis=1, keepdims=True) + jnp.exp(sink - m)
        o = jnp.dot(p.astype(BF16), vh, preferred_element_type=F32)
        outs.append(o / denom)
    o_ref[0] = jnp.concatenate(outs, axis=1).astype(o_ref.dtype)


def _swa(h, sinks):
    b, s, _ = h.shape
    nb = s // SWA_BLOCK
    blk = lambda c: pl.BlockSpec((1, SWA_BLOCK, IN_BLOCK), lambda i, j: (i, j, c))
    blk_prev = lambda c: pl.BlockSpec((1, SWA_BLOCK, IN_BLOCK), lambda i, j: (i, jnp.maximum(j - 1, 0), c))
    return pl.pallas_call(
        _swa_kernel,
        grid=(b, nb),
        in_specs=[pl.BlockSpec(memory_space=pltpu.SMEM),
                  blk(_Q_BLOCKS[0]), blk(_Q_BLOCKS[1]), blk(_KV_BLOCK), blk_prev(_KV_BLOCK)],
        out_specs=pl.BlockSpec((1, SWA_BLOCK, SWA_HEADS * SWA_HEAD_DIM), lambda i, j: (i, j, 0)),
        out_shape=jax.ShapeDtypeStruct((b, s, SWA_HEADS * SWA_HEAD_DIM), BF16),
        compiler_params=_params("parallel", "arbitrary"),
        name="swa",
    )(sinks, h, h, h, h)


def _mem_kernel(q_ref, kv_ref, o_ref):
    scale = MEM_HEAD_DIM ** -0.5
    width = MEM_HEADS * MEM_HEAD_DIM
    for hm in range(MEM_HEADS):
        cols = slice(hm * MEM_HEAD_DIM, (hm + 1) * MEM_HEAD_DIM)
        km = kv_ref[0, :, cols]
        vm = kv_ref[0, :, width + hm * MEM_HEAD_DIM:width + (hm + 1) * MEM_HEAD_DIM]
        sc = lax.dot_general(q_ref[0, :, cols], km, (((1,), (1,)), ((), ())), preferred_element_type=F32) * scale
        m = jnp.max(sc, axis=1, keepdims=True)
        p = jnp.exp(sc - m)
        denom = jnp.sum(p, axis=1, keepdims=True)
        o = jnp.dot(p.astype(BF16), vm, preferred_element_type=F32)
        o_ref[0, :, cols] = (o / denom).astype(o_ref.dtype)


def _mem_attention(h, kvm, tq):
    b, s, _ = h.shape
    width = MEM_HEADS * MEM_HEAD_DIM
    tq = min(tq, s)
    return pl.pallas_call(
        _mem_kernel,
        grid=(b, s // tq),
        in_specs=[
            pl.BlockSpec((1, tq, width), lambda i, j: (i, j, 4)),
            pl.BlockSpec((1,) + kvm.shape[1:], lambda i, j: (i, 0, 0)),
        ],
        out_specs=pl.BlockSpec((1, tq, width), lambda i, j: (i, j, 0)),
        out_shape=jax.ShapeDtypeStruct((b, s, width), BF16),
        compiler_params=_params("parallel", "arbitrary"),
        name="mem_attention",
    )(h, kvm)


def _outproj_kernel(pool_ref, swa_ref, mem_ref, wp_ref, ws_ref, wm_ref, x_ref, g_ref, b_ref, o_ref, ob_ref,
                    *, alpha):
    mix = jnp.dot(pool_ref[...], wp_ref[...], preferred_element_type=F32)
    mix += jnp.dot(swa_ref[...], ws_ref[...], preferred_element_type=F32)
    mix += jnp.dot(mem_ref[...], wm_ref[...], preferred_element_type=F32)
    z = alpha * x_ref[...] + mix
    mu = jnp.mean(z, axis=1, keepdims=True)
    zc = z - mu
    var = jnp.mean(zc * zc, axis=1, keepdims=True)
    y = zc * lax.rsqrt(var + LN_EPS) * g_ref[...] + b_ref[...]
    yt = y.T
    o_ref[...] = yt
    ob_ref[...] = yt.astype(BF16)


def _outproj_ln(pool_o, swa_o, mem_o, w_out, x2, g, b, alpha, tm):
    t, d = x2.shape
    tm = min(tm, t)
    wp, ws, wm = pool_o.shape[1], swa_o.shape[1], mem_o.shape[1]
    w_p, w_s, w_m = w_out[:wp], w_out[wp:wp + ws], w_out[wp + ws:]
    row = lambda w: pl.BlockSpec((tm, w), lambda i: (i, 0))
    full = lambda a: pl.BlockSpec(a.shape, lambda i: (0, 0))
    return pl.pallas_call(
        functools.partial(_outproj_kernel, alpha=alpha),
        grid=(t // tm,),
        in_specs=[row(wp), row(ws), row(wm), full(w_p), full(w_s), full(w_m), row(d),
                  pl.BlockSpec((1, d), lambda i: (0, 0)), pl.BlockSpec((1, d), lambda i: (0, 0))],
        out_specs=[pl.BlockSpec((d, tm), lambda i: (0, i))] * 2,
        out_shape=[jax.ShapeDtypeStruct((d, t), F32), jax.ShapeDtypeStruct((d, t), BF16)],
        compiler_params=_params("parallel"),
        name="outproj_ln",
    )(pool_o, swa_o, mem_o, w_p, w_s, w_m, x2, g, b)


def _merge_exchange_pairs(n):
    t = (n - 1).bit_length()
    pairs = []
    p = 1 << (t - 1)
    while p >= 1:
        q, r, d = 1 << (t - 1), 0, p
        while True:
            pairs += [(i, i + d) for i in range(n - d) if (i & p) == r]
            if q == p:
                break
            d, q, r = q - p, q // 2, p
        p //= 2
    return pairs


def _top16_rows_distinct(s, with_rank):
    k = PEER_TOPK
    lists = [s[SUBLANES * g:SUBLANES * (g + 1), :] for g in range(s.shape[0] // SUBLANES)]
    for i, j in _merge_exchange_pairs(len(lists)):
        lists[i], lists[j] = jnp.maximum(lists[i], lists[j]), jnp.minimum(lists[i], lists[j])
    vals = []
    for r in range(k):
        m = jnp.max(lists[0], axis=0, keepdims=True)
        vals.append(m)
        pop = lists[0] == m
        for d in range(k - 1 - r):
            lists[d] = jnp.where(pop, lists[d + 1], lists[d])
    rank = None
    if with_rank:
        rank = jnp.zeros(s.shape, F32)
        for r in range(k):
            rank = rank + jnp.where(vals[r] > s, 1.0, 0.0)
    count = jnp.sum(jnp.where(s >= vals[k - 1], 1.0, 0.0), axis=0, keepdims=True)
    strict = jnp.zeros_like(count)
    for r in range(k - 1):
        strict = strict + jnp.where(vals[r] > vals[r + 1], 1.0, 0.0)
    distinct = (count == float(k)) & (strict == float(k - 1))
    return rank, vals, jnp.where(distinct, float(k), 0.0)


def _top16_rows(s, break_ties, with_rank=True):
    if not break_ties:
        return _top16_rows_distinct(s, with_rank)
    n = s.shape[0]
    iota = lax.broadcasted_iota(jnp.int32, s.shape, 0).astype(F32)
    rank = jnp.full(s.shape, float(PEER_TOPK), F32)
    vals = []
    for r in range(PEER_TOPK):
        m = jnp.max(s, axis=0, keepdims=True)
        idx = jnp.min(jnp.where(s == m, iota, float(n)), axis=0, keepdims=True)
        hit = iota == idx
        rank = jnp.where(hit, float(r), rank)
        s = jnp.where(hit, -jnp.inf, s)
        vals.append(m)
    count = jnp.sum(jnp.where(rank < float(PEER_TOPK), 1.0, 0.0), axis=0, keepdims=True)
    return rank, vals, count


_CAND_SMALL_A = PEER_TOPK // 2
_CAND_ROWS = PEER_TOPK + (_CAND_SMALL_A - 1) * SUBLANES + SUBLANES


def _cand_constants():
    flat = np.full((_CAND_ROWS, 1), 1e9, np.float32)
    valid = np.zeros((_CAND_ROWS, 1), np.float32)
    for b in range(PEER_TOPK):
        flat[b, 0], valid[b, 0] = b, 1.0
    for a in range(1, _CAND_SMALL_A):
        base = PEER_TOPK + (a - 1) * SUBLANES
        for b in range(PEER_TOPK // (a + 1)):
            flat[base + b, 0], valid[base + b, 0] = a * PEER_TOPK + b, 1.0
    base = PEER_TOPK + (_CAND_SMALL_A - 1) * SUBLANES
    for k in range(SUBLANES):
        flat[base + k, 0], valid[base + k, 0] = (_CAND_SMALL_A + k) * PEER_TOPK, 1.0
    return jnp.asarray(flat), jnp.asarray(valid)


def _select_experts(s1, s2, flat, valid, break_ties):
    t = s1.shape[1]
    rank1, v1, count1 = _top16_rows(s1, break_ties, with_rank=break_ties)
    rank2, v2, count2 = _top16_rows(s2, break_ties)
    v2_lo = jnp.concatenate(v2[:SUBLANES], axis=0)
    v2_all = jnp.concatenate(v2, axis=0)
    v1_hi = jnp.concatenate(v1[_CAND_SMALL_A:], axis=0)
    groups = [v1[0] + v2_all]
    for a in range(1, _CAND_SMALL_A):
        groups.append(v1[a] + v2_lo)
    groups.append(v1_hi + v2[0])
    cand = jnp.concatenate(groups, axis=0)
    cand = jnp.where(valid > 0.5, cand, -jnp.inf)
    flat_b = jnp.broadcast_to(flat, cand.shape) if break_ties else None
    all_cand = cand
    hits = jnp.zeros(cand.shape, F32)
    top = []
    for r in range(PEER_TOPK):
        m = jnp.max(cand, axis=0, keepdims=True)
        hit = cand == m
        if break_ties:
            pick = jnp.min(jnp.where(hit, flat_b, 2e9), axis=0, keepdims=True)
            hit = flat_b == pick
            hits = jnp.where(hit, 1.0, hits)
        cand = jnp.where(hit, -jnp.inf, cand)
        top.append(m)
    if not break_ties:
        hits = jnp.where(all_cand >= top[PEER_TOPK - 1], 1.0, 0.0)
    z = jnp.ones((1, t), F32)
    for r in range(1, PEER_TOPK):
        z = z + jnp.exp(top[r] - top[0])
    counts = [jnp.sum(hits[0:PEER_TOPK], axis=0, keepdims=True)]
    for a in range(1, _CAND_SMALL_A):
        base = PEER_TOPK + (a - 1) * SUBLANES
        counts.append(jnp.sum(hits[base:base + SUBLANES], axis=0, keepdims=True))
    base = PEER_TOPK + (_CAND_SMALL_A - 1) * SUBLANES
    for k in range(SUBLANES):
        counts.append(hits[base + k:base + k + 1])
    lim = jnp.zeros(s1.shape, F32)
    if rank1 is not None:
        for a in range(PEER_TOPK):
            lim = jnp.where(rank1 == float(a), counts[a], lim)
    else:
        for a in range(PEER_TOPK):
            step = counts[a] - counts[a + 1] if a + 1 < PEER_TOPK else counts[a]
            lim = lim + jnp.where(s1 >= v1[a], step, 0.0)
    e1n = jnp.exp(s1 - v1[0]) * (1.0 / z)
    e2 = jnp.exp(s2 - v2[0])
    count3 = jnp.sum(hits, axis=0, keepdims=True)
    k = float(PEER_TOPK)
    ok = jnp.where((count1 == k) & (count2 == k) & (count3 == k), 1.0, 0.0)
    return lim, e1n, rank2, e2, ok


def _retrieve_kernel(wq_ref, x_ref, k1_ref, k2_ref, flat_ref, valid_ref,
                     lim_out_ref, e1_out_ref, r2_ref, e2_ref, q_ref, lim_ref, e1_ref, *, tq):
    q_ref[...] = jnp.dot(wq_ref[...], x_ref[...], preferred_element_type=F32)
    flat, valid = flat_ref[...], valid_ref[...]

    def head(h, carry):
        for c in range(tq // SELECT_TOKENS):
            lanes = slice(c * SELECT_TOKENS, (c + 1) * SELECT_TOKENS)
            r0 = pl.multiple_of(h * 2 * HALF_DIM, 2 * HALF_DIM)
            q1 = q_ref[pl.ds(r0, HALF_DIM), lanes].astype(BF16)
            q2 = q_ref[pl.ds(r0 + HALF_DIM, HALF_DIM), lanes].astype(BF16)
            s1 = jnp.dot(k1_ref[...], q1, preferred_element_type=F32)
            s2 = jnp.dot(k2_ref[...], q2, preferred_element_type=F32)

            def emit(break_ties):
                lim, e1n, rank2, e2, ok = _select_experts(s1, s2, flat, valid, break_ties)
                lim_ref[h, :, lanes] = lim
                e1_ref[h, :, lanes] = e1n
                r2_ref[h, :, lanes] = rank2.astype(BF16)
                e2_ref[h, :, lanes] = e2.astype(BF16)
                return ok

            ok = emit(False)

            @pl.when(jnp.min(ok) < 0.5)
            def _():
                emit(True)
        return carry

    lax.fori_loop(0, PEER_HEADS, head, 0)
    lim_out_ref[...] = jnp.swapaxes(lim_ref[...], 0, 1)
    e1_out_ref[...] = jnp.swapaxes(e1_ref[...], 0, 1)


def _retrieve(x1t, wq_t, k1, k2, tq):
    d, t = x1t.shape
    tq = min(tq, t)
    flat, valid = _cand_constants()
    full = lambda a: pl.BlockSpec(a.shape, lambda i: (0,) * a.ndim)
    out_keys = jax.ShapeDtypeStruct((N_KEYS, PEER_HEADS, t), F32)
    key_spec = pl.BlockSpec((N_KEYS, PEER_HEADS, tq), lambda i: (0, 0, i))
    out_packed = jax.ShapeDtypeStruct((PEER_HEADS, N_KEYS, t), BF16)
    out_spec = pl.BlockSpec((PEER_HEADS, N_KEYS, tq), lambda i: (0, 0, i))
    return pl.pallas_call(
        functools.partial(_retrieve_kernel, tq=tq),
        grid=(t // tq,),
        in_specs=[full(wq_t), pl.BlockSpec((d, tq), lambda i: (0, i)), full(k1), full(k2), full(flat), full(valid)],
        out_specs=[key_spec, key_spec, out_spec, out_spec],
        out_shape=[out_keys, out_keys, out_packed, out_packed],
        scratch_shapes=[pltpu.VMEM((wq_t.shape[0], tq), F32)] + [pltpu.VMEM((PEER_HEADS, N_KEYS, tq), F32)] * 2,
        compiler_params=_params("parallel"),
        name="peer_retrieve",
    )(wq_t, x1t, k1, k2, flat, valid)


def _transpose_cast_kernel(v_ref, o_ref):
    o_ref[...] = v_ref[...].T.astype(o_ref.dtype)


def _transpose_cast(v, rows):
    n, d = v.shape
    rows = min(rows, n)
    return pl.pallas_call(
        _transpose_cast_kernel,
        grid=(n // rows,),
        in_specs=[pl.BlockSpec((rows, d), lambda i: (i, 0))],
        out_specs=pl.BlockSpec((d, rows), lambda i: (0, i)),
        out_shape=jax.ShapeDtypeStruct((d, n), BF16),
        compiler_params=_params("parallel"),
        name="transpose_cast",
    )(v)


def _gelu_exact(x):
    return 0.5 * x * (1.0 + lax.erf(x * (1.0 / math.sqrt(2.0))))


def _experts_kernel(xb_ref, u_ref, vt_ref, lim_ref, e1_ref, r2_in_ref, e2_in_ref, o_ref,
                    h_ref, a_ref, rows_ref, gate_ref, r2_ref, e2_ref, *, tm, te):
    e = pl.program_id(1)

    @pl.when(e == 0)
    def _():
        o_ref[...] = jnp.zeros(o_ref.shape, F32)
        r2_ref[:, 0:tm] = r2_in_ref[...]
        e2_ref[:, LANES:LANES + tm] = e2_in_ref[...]

    n_chunks = te // EXPERT_CHUNK

    def rows_of(p):
        return slice(p * EXPERT_CHUNK, (p + 1) * EXPERT_CHUNK)

    def gates(first_key):
        keys = range(first_key, first_key + GATE_KEYS)
        blocks = range(0, N_KEYS, GATE_ROWS)
        for i in keys:
            for hd in range(PEER_HEADS):
                for q, ref in enumerate((lim_ref, e1_ref)):
                    row = jnp.broadcast_to(ref[i, hd:hd + 1, :], (GATE_ROWS, tm)).astype(BF16)
                    rows_ref[q, i, hd, :, q * LANES:q * LANES + tm] = row
        for c in range(tm // LANES):
            lanes = slice(c * LANES, (c + 1) * LANES)
            lanes1 = slice((c + 1) * LANES, (c + 2) * LANES)
            gate = {i: {jb: jnp.zeros((GATE_ROWS, LANES), BF16) for jb in blocks} for i in keys}
            for hd in range(PEER_HEADS):
                lim = {i: rows_ref[0, i, hd, :, lanes] for i in keys}
                e1 = {i: rows_ref[1, i, hd, :, lanes1] for i in keys}
                for jb in blocks:
                    r2 = r2_ref[hd * N_KEYS + jb:hd * N_KEYS + jb + GATE_ROWS, lanes]
                    e2 = e2_ref[hd * N_KEYS + jb:hd * N_KEYS + jb + GATE_ROWS, lanes1]
                    for i in keys:
                        gate[i][jb] = gate[i][jb] + jnp.where(r2 < lim[i], e2 * e1[i], jnp.zeros_like(e2))
            for i in keys:
                for jb in blocks:
                    gate_ref[i * N_KEYS + jb:i * N_KEYS + jb + GATE_ROWS, lanes] = gate[i][jb]

    for first_key in range(0, te // N_KEYS, GATE_KEYS):
        gates(first_key)
    h_ref[...] = jnp.dot(u_ref[...], xb_ref[...], preferred_element_type=F32)
    for p in range(n_chunks):
        a_ref[rows_of(p), :] = gate_ref[rows_of(p), :] * _gelu_exact(h_ref[rows_of(p), :]).astype(BF16)
        o_ref[...] += jnp.dot(vt_ref[:, rows_of(p)], a_ref[rows_of(p), :], preferred_element_type=F32)


def _experts(xbt, u, vt, sel, tm, te):
    d, t = xbt.shape
    n_exp = u.shape[0]
    tm, te = min(tm, t), min(te, n_exp)
    tok = pl.BlockSpec((d, tm), lambda i, j: (0, i))
    sel_spec = pl.BlockSpec((PEER_HEADS * N_KEYS, tm), lambda i, j: (0, i))
    key_spec = pl.BlockSpec((te // N_KEYS, PEER_HEADS, tm), lambda i, j: (j, 0, i))
    lim, e1n, rank2, e2 = sel
    sel = (lim, e1n, rank2.reshape(-1, t), e2.reshape(-1, t))
    return pl.pallas_call(
        functools.partial(_experts_kernel, tm=tm, te=te),
        grid=(t // tm, n_exp // te),
        in_specs=[tok, pl.BlockSpec((te, d), lambda i, j: (j, 0)), pl.BlockSpec((d, te), lambda i, j: (0, j)),
                  key_spec, key_spec, sel_spec, sel_spec],
        out_specs=tok,
        out_shape=jax.ShapeDtypeStruct((d, t), F32),
        scratch_shapes=[pltpu.VMEM((te, tm), F32), pltpu.VMEM((te, tm), BF16),
                        pltpu.VMEM((2, te // N_KEYS, PEER_HEADS, GATE_ROWS, tm + LANES), BF16),
                        pltpu.VMEM((te, tm), BF16),
                        pltpu.VMEM((PEER_HEADS * N_KEYS, tm + LANES), BF16),
                        pltpu.VMEM((PEER_HEADS * N_KEYS, tm + LANES), BF16)],
        compiler_params=_params("parallel", "arbitrary"),
        name="peer_experts",
    )(xbt, u, vt, *sel)


def _ln_t_kernel(x_ref, f_ref, g_ref, b_ref, o_ref, *, alpha):
    z = (alpha * x_ref[...] + f_ref[...]).T
    mu = jnp.mean(z, axis=1, keepdims=True)
    zc = z - mu
    var = jnp.mean(zc * zc, axis=1, keepdims=True)
    o_ref[...] = zc * lax.rsqrt(var + LN_EPS) * g_ref[...] + b_ref[...]


def _residual_ln_t(x1t, fft, g, b, alpha, tm):
    d, t = x1t.shape
    tm = min(tm, t)
    tok = pl.BlockSpec((d, tm), lambda i: (0, i))
    vec = pl.BlockSpec((1, d), lambda i: (0, 0))
    return pl.pallas_call(
        functools.partial(_ln_t_kernel, alpha=alpha),
        grid=(t // tm,),
        in_specs=[tok, tok, vec, vec],
        out_specs=pl.BlockSpec((tm, d), lambda i: (i, 0)),
        out_shape=jax.ShapeDtypeStruct((t, d), F32),
        compiler_params=_params("parallel"),
        name="residual_ln",
    )(x1t, fft, g, b)


def kernel(x, mem, positions, w_in, w_mem_kv, w_pool, pool_scale, attn_sinks, w_out, ln1_g, ln1_b,
           w_peer_q, sub_keys_1, sub_keys_2, expert_u, expert_v, ln2_g, ln2_b):
    bsz, seq, d = x.shape
    depth = w_in.shape[0]
    t = bsz * seq
    alpha = (2.0 * depth) ** 0.25
    for l in range(depth):
        x2 = x.reshape(t, d)
        h = _inproj(x2, w_in[l].astype(BF16), positions, 512).reshape(bsz, seq, -1)
        mem2 = mem.reshape(-1, d).astype(BF16)
        kvm = _matmul(mem2, w_mem_kv[l].astype(BF16), BF16, 512, 512).reshape(bsz, mem.shape[1], -1)
        pool_o = _pool(h, w_pool[l].astype(BF16), pool_scale[l].reshape(1, -1), 512)
        swa_o = _swa(h, attn_sinks[l])
        mem_o = _mem_attention(h, kvm, 512)
        x1t, x1bt = _outproj_ln(pool_o.reshape(t, -1), swa_o.reshape(t, -1), mem_o.reshape(t, -1),
                          w_out[l].astype(BF16), x2, ln1_g[l].reshape(1, d), ln1_b[l].reshape(1, d), alpha, 512)
        sel = _retrieve(x1bt, w_peer_q[l].T.astype(BF16), sub_keys_1[l].astype(BF16),
                        sub_keys_2[l].astype(BF16), 512)
        fft = _experts(x1bt, expert_u[l].astype(BF16), _transpose_cast(expert_v[l], 512), sel, 1024, 512)
        x = _residual_ln_t(x1t, fft, ln2_g[l].reshape(1, d), ln2_b[l].reshape(1, d), alpha, 512).reshape(bsz, seq, d)
    return x
```

```python
import functools
import math

import numpy as np
import jax
import jax.numpy as jnp
from jax import lax
from jax.experimental import pallas as pl
from jax.experimental.pallas import tpu as pltpu

F32 = jnp.float32
BF16 = jnp.bfloat16

LANES = 128
SUBLANES = 8
VMEM_LIMIT_BYTES = 56 * 1024 * 1024

POOL_WINDOWS = (2, 4, 8, 16)
POOL_GROUP = 128
POOL_HALO = 16
SWA_HEAD_DIM = 64
SWA_HEADS = 16
SWA_KV_HEADS = 4
SWA_BLOCK = 128
ROPE_THETA = 500000.0
ROPE_DIM = 16
MEM_HEADS = 4
MEM_HEAD_DIM = 128
PEER_HEADS = 8
N_KEYS = 128
PEER_TOPK = 16
HALF_DIM = 128
LN_EPS = 1e-5
NEG = -1e30

EXPERT_CHUNK = 512
GATE_KEYS = 4
GATE_ROWS = 16
SELECT_TOKENS = 512


def _params(*semantics):
    return pltpu.CompilerParams(dimension_semantics=semantics, vmem_limit_bytes=VMEM_LIMIT_BYTES)


def _matmul_kernel(a_ref, b_ref, o_ref):
    o_ref[...] = jnp.dot(a_ref[...], b_ref[...], preferred_element_type=F32).astype(o_ref.dtype)


def _matmul(a, b, out_dtype, tm, tn):
    m, k = a.shape
    n = b.shape[1]
    tm, tn = min(tm, m), min(tn, n)
    return pl.pallas_call(
        _matmul_kernel,
        grid=(m // tm, n // tn),
        in_specs=[pl.BlockSpec((tm, k), lambda i, j: (i, 0)), pl.BlockSpec((k, tn), lambda i, j: (0, j))],
        out_specs=pl.BlockSpec((tm, tn), lambda i, j: (i, j)),
        out_shape=jax.ShapeDtypeStruct((m, n), out_dtype),
        compiler_params=_params("parallel", "arbitrary"),
        name="matmul",
    )(a, b)


IN_BLOCK = 512
_Q_BLOCKS = (1, 2)
_KV_BLOCK = 3


def _rope(x, c, sa, sb):
    half = ROPE_DIM // 2
    return x * c + pltpu.roll(x, LANES - half, 1) * sa + pltpu.roll(x, half, 1) * sb


def _inproj_kernel(x_ref, w_ref, pos_ref, freq_ref, sa_ref, sb_ref, o_ref):
    xb = x_ref[...].astype(BF16)
    ang = pos_ref[...].astype(F32) * freq_ref[...]
    s = jnp.sin(ang)
    c, sa, sb = jnp.cos(ang), s * sa_ref[...], s * sb_ref[...]
    for j in range(w_ref.shape[1] // IN_BLOCK):
        y = jnp.dot(xb, w_ref[:, j * IN_BLOCK:(j + 1) * IN_BLOCK], preferred_element_type=F32)
        for k in range(IN_BLOCK // LANES):
            piece = y[:, k * LANES:(k + 1) * LANES]
            if j in _Q_BLOCKS:
                piece = _rope(piece, c, sa, sb) * SWA_HEAD_DIM ** -0.5
            elif j == _KV_BLOCK and k < IN_BLOCK // LANES // 2:
                piece = _rope(piece, c, sa, sb)
            o_ref[:, j * IN_BLOCK + k * LANES:j * IN_BLOCK + (k + 1) * LANES] = piece.astype(o_ref.dtype)


def _rope_constants():
    lane = np.arange(LANES)
    d = lane % SWA_HEAD_DIM
    half = ROPE_DIM // 2
    inv_freq = np.float32(ROPE_THETA) ** (-np.arange(0, ROPE_DIM, 2, dtype=np.float32) / np.float32(ROPE_DIM))
    freq = np.where(d < ROPE_DIM, inv_freq[d % half], 0.0).astype(np.float32)
    sa = np.where(d < half, -1.0, 0.0).astype(np.float32)
    sb = np.where((d >= half) & (d < ROPE_DIM), 1.0, 0.0).astype(np.float32)
    return [jnp.asarray(a.reshape(1, LANES)) for a in (freq, sa, sb)]


def _inproj(x2, w, positions, tm):
    t, d = x2.shape
    n = w.shape[1]
    tm = min(tm, t)
    freq, sa, sb = _rope_constants()
    const = pl.BlockSpec((1, LANES), lambda i: (0, 0))
    return pl.pallas_call(
        _inproj_kernel,
        grid=(t // tm,),
        in_specs=[pl.BlockSpec((tm, d), lambda i: (i, 0)), pl.BlockSpec((d, n), lambda i: (0, 0)),
                  pl.BlockSpec((tm, 1), lambda i: (i, 0)), const, const, const],
        out_specs=pl.BlockSpec((tm, n), lambda i: (i, 0)),
        out_shape=jax.ShapeDtypeStruct((t, n), BF16),
        compiler_params=_params("parallel"),
        name="inproj_rope",
    )(x2, w, positions.reshape(t, 1), freq, sa, sb)


def _pool_kernel(v_ref, w_ref, scale_ref, o_ref, ext_ref, *, ts):
    s = pl.program_id(1)

    @pl.when(s == 0)
    def _():
        ext_ref[0:POOL_HALO, :] = jnp.zeros((POOL_HALO, ext_ref.shape[1]), F32)

    ext_ref[POOL_HALO:POOL_HALO + ts, :] = v_ref[0].astype(F32)
    pos = s * ts + lax.broadcasted_iota(jnp.int32, (ts, 1), 0)
    for g, w in enumerate(POOL_WINDOWS):
        cols = slice(g * POOL_GROUP, (g + 1) * POOL_GROUP)
        acc = ext_ref[POOL_HALO:POOL_HALO + ts, cols]
        for k in range(1, w):
            acc = acc + ext_ref[POOL_HALO - k:POOL_HALO - k + ts, cols]
        count = jnp.minimum(pos + 1, w).astype(F32)
        pooled = acc / count - ext_ref[POOL_HALO:POOL_HALO + ts, cols]
        y = jnp.dot(pooled.astype(BF16), w_ref[g], preferred_element_type=F32)
        o_ref[0, :, cols] = (y * scale_ref[:, cols]).astype(o_ref.dtype)
    ext_ref[0:POOL_HALO, :] = ext_ref[ts:ts + POOL_HALO, :]


def _pool(h, w_pool, pool_scale, ts):
    b, s, _ = h.shape
    width = POOL_GROUP * len(POOL_WINDOWS)
    ts = min(ts, s)
    return pl.pallas_call(
        functools.partial(_pool_kernel, ts=ts),
        grid=(b, s // ts),
        in_specs=[
            pl.BlockSpec((1, ts, width), lambda i, j: (i, j, 0)),
            pl.BlockSpec(w_pool.shape, lambda i, j: (0, 0, 0)),
            pl.BlockSpec((1, width), lambda i, j: (0, 0)),
        ],
        out_specs=pl.BlockSpec((1, ts, width), lambda i, j: (i, j, 0)),
        out_shape=jax.ShapeDtypeStruct((b, s, width), BF16),
        scratch_shapes=[pltpu.VMEM((ts + POOL_HALO, width), F32)],
        compiler_params=_params("arbitrary", "arbitrary"),
        name="pool",
    )(h, w_pool, pool_scale)


def _swa_kernel(sink_ref, q0_ref, q1_ref, kv_ref, kvp_ref, o_ref):
    n = pl.program_id(1)
    kvw = SWA_KV_HEADS * SWA_HEAD_DIM
    q = jnp.concatenate([q0_ref[0], q1_ref[0]], axis=1)
    k = jnp.concatenate([kvp_ref[0, :, 0:kvw], kv_ref[0, :, 0:kvw]], axis=0)
    v = jnp.concatenate([kvp_ref[0, :, kvw:2 * kvw], kv_ref[0, :, kvw:2 * kvw]], axis=0)
    row = lax.broadcasted_iota(jnp.int32, (SWA_BLOCK, 2 * SWA_BLOCK), 0)
    col = lax.broadcasted_iota(jnp.int32, (SWA_BLOCK, 2 * SWA_BLOCK), 1)
    rel = row + SWA_BLOCK - col
    valid = (rel >= 0) & (rel < SWA_BLOCK) & ((col >= SWA_BLOCK) | (n > 0))
    group = SWA_HEADS // SWA_KV_HEADS
    outs = []
    for hq in range(SWA_HEADS):
        kv = hq // group
        qh = q[:, hq * SWA_HEAD_DIM:(hq + 1) * SWA_HEAD_DIM]
        kh = k[:, kv * SWA_HEAD_DIM:(kv + 1) * SWA_HEAD_DIM]
        vh = v[:, kv * SWA_HEAD_DIM:(kv + 1) * SWA_HEAD_DIM]
        sc = lax.dot_general(qh, kh, (((1,), (1,)), ((), ())), preferred_element_type=F32)
        sc = jnp.where(valid, sc, NEG)
        sink = sink_ref[hq]
        m = jnp.maximum(jnp.max(sc, axis=1, keepdims=True), sink)
        p = jnp.exp(sc - m)
        denom = jnp.sum(p, axis=1, keepdims=True) + jnp.exp(sink - m)
        o = jnp.dot(p.astype(BF16), vh, preferred_element_type=F32)
        outs.append(o / denom)
    o_ref[0] = jnp.concatenate(outs, axis=1).astype(o_ref.dtype)


def _swa(h, sinks):
    b, s, _ = h.shape
    nb = s // SWA_BLOCK
    blk = lambda c: pl.BlockSpec((1, SWA_BLOCK, IN_BLOCK), lambda i, j: (i, j, c))
    blk_prev = lambda c: pl.BlockSpec((1, SWA_BLOCK, IN_BLOCK), lambda i, j: (i, jnp.maximum(j - 1, 0), c))
    return pl.pallas_call(
        _swa_kernel,
        grid=(b, nb),
        in_specs=[pl.BlockSpec(memory_space=pltpu.SMEM),
                  blk(_Q_BLOCKS[0]), blk(_Q_BLOCKS[1]), blk(_KV_BLOCK), blk_prev(_KV_BLOCK)],
        out_specs=pl.BlockSpec((1, SWA_BLOCK, SWA_HEADS * SWA_HEAD_DIM), lambda i, j: (i, j, 0)),
        out_shape=jax.ShapeDtypeStruct((b, s, SWA_HEADS * SWA_HEAD_DIM), BF16),
        compiler_params=_params("parallel", "arbitrary"),
        name="swa",
    )(sinks, h, h, h, h)


def _mem_kernel(q_ref, kv_ref, o_ref):
    scale = MEM_HEAD_DIM ** -0.5
    width = MEM_HEADS * MEM_HEAD_DIM
    for hm in range(MEM_HEADS):
        cols = slice(hm * MEM_HEAD_DIM, (hm + 1) * MEM_HEAD_DIM)
        km = kv_ref[0, :, cols]
        vm = kv_ref[0, :, width + hm * MEM_HEAD_DIM:width + (hm + 1) * MEM_HEAD_DIM]
        sc = lax.dot_general(q_ref[0, :, cols], km, (((1,), (1,)), ((), ())), preferred_element_type=F32) * scale
        m = jnp.max(sc, axis=1, keepdims=True)
        p = jnp.exp(sc - m)
        denom = jnp.sum(p, axis=1, keepdims=True)
        o = jnp.dot(p.astype(BF16), vm, preferred_element_type=F32)
        o_ref[0, :, cols] = (o / denom).astype(o_ref.dtype)


def _mem_attention(h, kvm, tq):
    b, s, _ = h.shape
    width = MEM_HEADS * MEM_HEAD_DIM
    tq = min(tq, s)
    return pl.pallas_call(
        _mem_kernel,
        grid=(b, s // tq),
        in_specs=[
            pl.BlockSpec((1, tq, width), lambda i, j: (i, j, 4)),
            pl.BlockSpec((1,) + kvm.shape[1:], lambda i, j: (i, 0, 0)),
        ],
        out_specs=pl.BlockSpec((1, tq, width), lambda i, j: (i, j, 0)),
        out_shape=jax.ShapeDtypeStruct((b, s, width), BF16),
        compiler_params=_params("parallel", "arbitrary"),
        name="mem_attention",
    )(h, kvm)


def _outproj_kernel(pool_ref, swa_ref, mem_ref, wp_ref, ws_ref, wm_ref, x_ref, g_ref, b_ref, o_ref, ob_ref,
                    *, alpha):
    mix = jnp.dot(pool_ref[...], wp_ref[...], preferred_element_type=F32)
    mix += jnp.dot(swa_ref[...], ws_ref[...], preferred_element_type=F32)
    mix += jnp.dot(mem_ref[...], wm_ref[...], preferred_element_type=F32)
    z = alpha * x_ref[...] + mix
    mu = jnp.mean(z, axis=1, keepdims=True)
    zc = z - mu
    var = jnp.mean(zc * zc, axis=1, keepdims=True)
    y = zc * lax.rsqrt(var + LN_EPS) * g_ref[...] + b_ref[...]
    yt = y.T
    o_ref[...] = yt
    ob_ref[...] = yt.astype(BF16)


def _outproj_ln(pool_o, swa_o, mem_o, w_out, x2, g, b, alpha, tm):
    t, d = x2.shape
    tm = min(tm, t)
    wp, ws, wm = pool_o.shape[1], swa_o.shape[1], mem_o.shape[1]
    w_p, w_s, w_m = w_out[:wp], w_out[wp:wp + ws], w_out[wp + ws:]
    row = lambda w: pl.BlockSpec((tm, w), lambda i: (i, 0))
    full = lambda a: pl.BlockSpec(a.shape, lambda i: (0, 0))
    return pl.pallas_call(
        functools.partial(_outproj_kernel, alpha=alpha),
        grid=(t // tm,),
        in_specs=[row(wp), row(ws), row(wm), full(w_p), full(w_s), full(w_m), row(d),
                  pl.BlockSpec((1, d), lambda i: (0, 0)), pl.BlockSpec((1, d), lambda i: (0, 0))],
        out_specs=[pl.BlockSpec((d, tm), lambda i: (0, i))] * 2,
        out_shape=[jax.ShapeDtypeStruct((d, t), F32), jax.ShapeDtypeStruct((d, t), BF16)],
        compiler_params=_params("parallel"),
        name="outproj_ln",
    )(pool_o, swa_o, mem_o, w_p, w_s, w_m, x2, g, b)


def _merge_exchange_pairs(n):
    t = (n - 1).bit_length()
    pairs = []
    p = 1 << (t - 1)
    while p >= 1:
        q, r, d = 1 << (t - 1), 0, p
        while True:
            pairs += [(i, i + d) for i in range(n - d) if (i & p) == r]
            if q == p:
                break
            d, q, r = q - p, q // 2, p
        p //= 2
    return pairs


def _top16_rows_distinct(s, with_rank):
    k = PEER_TOPK
    lists = [s[SUBLANES * g:SUBLANES * (g + 1), :] for g in range(s.shape[0] // SUBLANES)]
    for i, j in _merge_exchange_pairs(len(lists)):
        lists[i], lists[j] = jnp.maximum(lists[i], lists[j]), jnp.minimum(lists[i], lists[j])
    vals = []
    for r in range(k):
        m = jnp.max(lists[0], axis=0, keepdims=True)
        vals.append(m)
        pop = lists[0] == m
        for d in range(k - 1 - r):
            lists[d] = jnp.where(pop, lists[d + 1], lists[d])
    rank = None
    if with_rank:
        rank = jnp.zeros(s.shape, F32)
        for r in range(k):
            rank = rank + jnp.where(vals[r] > s, 1.0, 0.0)
    count = jnp.sum(jnp.where(s >= vals[k - 1], 1.0, 0.0), axis=0, keepdims=True)
    strict = jnp.zeros_like(count)
    for r in range(k - 1):
        strict = strict + jnp.where(vals[r] > vals[r + 1], 1.0, 0.0)
    distinct = (count == float(k)) & (strict == float(k - 1))
    return rank, vals, jnp.where(distinct, float(k), 0.0)


def _top16_rows(s, break_ties, with_rank=True):
    if not break_ties:
        return _top16_rows_distinct(s, with_rank)
    n = s.shape[0]
    iota = lax.broadcasted_iota(jnp.int32, s.shape, 0).astype(F32)
    rank = jnp.full(s.shape, float(PEER_TOPK), F32)
    vals = []
    for r in range(PEER_TOPK):
        m = jnp.max(s, axis=0, keepdims=True)
        idx = jnp.min(jnp.where(s == m, iota, float(n)), axis=0, keepdims=True)
        hit = iota == idx
        rank = jnp.where(hit, float(r), rank)
        s = jnp.where(hit, -jnp.inf, s)
        vals.append(m)
    count = jnp.sum(jnp.where(rank < float(PEER_TOPK), 1.0, 0.0), axis=0, keepdims=True)
    return rank, vals, count


_CAND_SMALL_A = PEER_TOPK // 2
_CAND_ROWS = PEER_TOPK + (_CAND_SMALL_A - 1) * SUBLANES + SUBLANES


def _cand_constants():
    flat = np.full((_CAND_ROWS, 1), 1e9, np.float32)
    valid = np.zeros((_CAND_ROWS, 1), np.float32)
    for b in range(PEER_TOPK):
        flat[b, 0], valid[b, 0] = b, 1.0
    for a in range(1, _CAND_SMALL_A):
        base = PEER_TOPK + (a - 1) * SUBLANES
        for b in range(PEER_TOPK // (a + 1)):
            flat[base + b, 0], valid[base + b, 0] = a * PEER_TOPK + b, 1.0
    base = PEER_TOPK + (_CAND_SMALL_A - 1) * SUBLANES
    for k in range(SUBLANES):
        flat[base + k, 0], valid[base + k, 0] = (_CAND_SMALL_A + k) * PEER_TOPK, 1.0
    return jnp.asarray(flat), jnp.asarray(valid)


def _select_experts(s1, s2, flat, valid, break_ties):
    t = s1.shape[1]
    rank1, v1, count1 = _top16_rows(s1, break_ties, with_rank=break_ties)
    rank2, v2, count2 = _top16_rows(s2, break_ties)
    v2_lo = jnp.concatenate(v2[:SUBLANES], axis=0)
    v2_all = jnp.concatenate(v2, axis=0)
    v1_hi = jnp.concatenate(v1[_CAND_SMALL_A:], axis=0)
    groups = [v1[0] + v2_all]
    for a in range(1, _CAND_SMALL_A):
        groups.append(v1[a] + v2_lo)
    groups.append(v1_hi + v2[0])
    cand = jnp.concatenate(groups, axis=0)
    cand = jnp.where(valid > 0.5, cand, -jnp.inf)
    flat_b = jnp.broadcast_to(flat, cand.shape) if break_ties else None
    all_cand = cand
    hits = jnp.zeros(cand.shape, F32)
    top = []
    for r in range(PEER_TOPK):
        m = jnp.max(cand, axis=0, keepdims=True)
        hit = cand == m
        if break_ties:
            pick = jnp.min(jnp.where(hit, flat_b, 2e9), axis=0, keepdims=True)
            hit = flat_b == pick
            hits = jnp.where(hit, 1.0, hits)
        cand = jnp.where(hit, -jnp.inf, cand)
        top.append(m)
    if not break_ties:
        hits = jnp.where(all_cand >= top[PEER_TOPK - 1], 1.0, 0.0)
    z = jnp.ones((1, t), F32)
    for r in range(1, PEER_TOPK):
        z = z + jnp.exp(top[r] - top[0])
    counts = [jnp.sum(hits[0:PEER_TOPK], axis=0, keepdims=True)]
    for a in range(1, _CAND_SMALL_A):
        base = PEER_TOPK + (a - 1) * SUBLANES
        counts.append(jnp.sum(hits[base:base + SUBLANES], axis=0, keepdims=True))
    base = PEER_TOPK + (_CAND_SMALL_A - 1) * SUBLANES
    for k in range(SUBLANES):
        counts.append(hits[base + k:base + k + 1])
    lim = jnp.zeros(s1.shape, F32)
    if rank1 is not None:
        for a in range(PEER_TOPK):
            lim = jnp.where(rank1 == float(a), counts[a], lim)
    else:
        for a in range(PEER_TOPK):
            step = counts[a] - counts[a + 1] if a + 1 < PEER_TOPK else counts[a]
            lim = lim + jnp.where(s1 >= v1[a], step, 0.0)
    e1n = jnp.exp(s1 - v1[0]) * (1.0 / z)
    e2 = jnp.exp(s2 - v2[0])
    count3 = jnp.sum(hits, axis=0, keepdims=True)
    k = float(PEER_TOPK)
    ok = jnp.where((count1 == k) & (count2 == k) & (count3 == k), 1.0, 0.0)
    return lim, e1n, rank2, e2, ok


def _retrieve_kernel(wq_ref, x_ref, k1_ref, k2_ref, flat_ref, valid_ref,
                     lim_out_ref, e1_out_ref, r2_ref, e2_ref, q_ref, lim_ref, e1_ref, *, tq):
    q_ref[...] = jnp.dot(wq_ref[...], x_ref[...], preferred_element_type=F32)
    flat, valid = flat_ref[...], valid_ref[...]

    def head(h, carry):
        for c in range(tq // SELECT_TOKENS):
            lanes = slice(c * SELECT_TOKENS, (c + 1) * SELECT_TOKENS)
            r0 = pl.multiple_of(h * 2 * HALF_DIM, 2 * HALF_DIM)
            q1 = q_ref[pl.ds(r0, HALF_DIM), lanes].astype(BF16)
            q2 = q_ref[pl.ds(r0 + HALF_DIM, HALF_DIM), lanes].astype(BF16)
            s1 = jnp.dot(k1_ref[...], q1, preferred_element_type=F32)
            s2 = jnp.dot(k2_ref[...], q2, preferred_element_type=F32)

            def emit(break_ties):
                lim, e1n, rank2, e2, ok = _select_experts(s1, s2, flat, valid, break_ties)
                lim_ref[h, :, lanes] = lim
                e1_ref[h, :, lanes] = e1n
                r2_ref[h, :, lanes] = rank2.astype(BF16)
                e2_ref[h, :, lanes] = e2.astype(BF16)
                return ok

            ok = emit(False)

            @pl.when(jnp.min(ok) < 0.5)
            def _():
                emit(True)
        return carry

    lax.fori_loop(0, PEER_HEADS, head, 0)
    lim_out_ref[...] = jnp.swapaxes(lim_ref[...], 0, 1)
    e1_out_ref[...] = jnp.swapaxes(e1_ref[...], 0, 1)


def _retrieve(x1t, wq_t, k1, k2, tq):
    d, t = x1t.shape
    tq = min(tq, t)
    flat, valid = _cand_constants()
    full = lambda a: pl.BlockSpec(a.shape, lambda i: (0,) * a.ndim)
    out_keys = jax.ShapeDtypeStruct((N_KEYS, PEER_HEADS, t), F32)
    key_spec = pl.BlockSpec((N_KEYS, PEER_HEADS, tq), lambda i: (0, 0, i))
    out_packed = jax.ShapeDtypeStruct((PEER_HEADS, N_KEYS, t), BF16)
    out_spec = pl.BlockSpec((PEER_HEADS, N_KEYS, tq), lambda i: (0, 0, i))
    return pl.pallas_call(
        functools.partial(_retrieve_kernel, tq=tq),
        grid=(t // tq,),
        in_specs=[full(wq_t), pl.BlockSpec((d, tq), lambda i: (0, i)), full(k1), full(k2), full(flat), full(valid)],
        out_specs=[key_spec, key_spec, out_spec, out_spec],
        out_shape=[out_keys, out_keys, out_packed, out_packed],
        scratch_shapes=[pltpu.VMEM((wq_t.shape[0], tq), F32)] + [pltpu.VMEM((PEER_HEADS, N_KEYS, tq), F32)] * 2,
        compiler_params=_params("parallel"),
        name="peer_retrieve",
    )(wq_t, x1t, k1, k2, flat, valid)


def _transpose_cast_kernel(v_ref, o_ref):
    o_ref[...] = v_ref[...].T.astype(o_ref.dtype)


def _transpose_cast(v, rows):
    n, d = v.shape
    rows = min(rows, n)
    return pl.pallas_call(
        _transpose_cast_kernel,
        grid=(n // rows,),
        in_specs=[pl.BlockSpec((rows, d), lambda i: (i, 0))],
        out_specs=pl.BlockSpec((d, rows), lambda i: (0, i)),
        out_shape=jax.ShapeDtypeStruct((d, n), BF16),
        compiler_params=_params("parallel"),
        name="transpose_cast",
    )(v)


def _gelu_exact(x):
    return 0.5 * x * (1.0 + lax.erf(x * (1.0 / math.sqrt(2.0))))


def _experts_kernel(xb_ref, u_ref, vt_ref, lim_ref, e1_ref, r2_in_ref, e2_in_ref, o_ref,
                    h_ref, a_ref, rows_ref, gate_ref, r2_ref, e2_ref, *, tm, te):
    e = pl.program_id(1)

    @pl.when(e == 0)
    def _():
        o_ref[...] = jnp.zeros(o_ref.shape, F32)
        r2_ref[:, 0:tm] = r2_in_ref[...]
        e2_ref[:, LANES:LANES + tm] = e2_in_ref[...]

    n_chunks = te // EXPERT_CHUNK

    def rows_of(p):
        return slice(p * EXPERT_CHUNK, (p + 1) * EXPERT_CHUNK)

    def gates(first_key):
        keys = range(first_key, first_key + GATE_KEYS)
        blocks = range(0, N_KEYS, GATE_ROWS)
        for i in keys:
            for hd in range(PEER_HEADS):
                for q, ref in enumerate((lim_ref, e1_ref)):
                    row = jnp.broadcast_to(ref[i, hd:hd + 1, :], (GATE_ROWS, tm)).astype(BF16)
                    rows_ref[q, i, hd, :, q * LANES:q * LANES + tm] = row
        for c in range(tm // LANES):
            lanes = slice(c * LANES, (c + 1) * LANES)
            lanes1 = slice((c + 1) * LANES, (c + 2) * LANES)
            gate = {i: {jb: jnp.zeros((GATE_ROWS, LANES), BF16) for jb in blocks} for i in keys}
            for hd in range(PEER_HEADS):
                lim = {i: rows_ref[0, i, hd, :, lanes] for i in keys}
                e1 = {i: rows_ref[1, i, hd, :, lanes1] for i in keys}
                for jb in blocks:
                    r2 = r2_ref[hd * N_KEYS + jb:hd * N_KEYS + jb + GATE_ROWS, lanes]
                    e2 = e2_ref[hd * N_KEYS + jb:hd * N_KEYS + jb + GATE_ROWS, lanes1]
                    for i in keys:
                        gate[i][jb] = gate[i][jb] + jnp.where(r2 < lim[i], e2 * e1[i], jnp.zeros_like(e2))
            for i in keys:
                for jb in blocks:
                    gate_ref[i * N_KEYS + jb:i * N_KEYS + jb + GATE_ROWS, lanes] = gate[i][jb]

    for first_key in range(0, te // N_KEYS, GATE_KEYS):
        gates(first_key)
    h_ref[...] = jnp.dot(u_ref[...], xb_ref[...], preferred_element_type=F32)
    for p in range(n_chunks):
        a_ref[rows_of(p), :] = gate_ref[rows_of(p), :] * _gelu_exact(h_ref[rows_of(p), :]).astype(BF16)
        o_ref[...] += jnp.dot(vt_ref[:, rows_of(p)], a_ref[rows_of(p), :], preferred_element_type=F32)


def _experts(xbt, u, vt, sel, tm, te):
    d, t = xbt.shape
    n_exp = u.shape[0]
    tm, te = min(tm, t), min(te, n_exp)
    tok = pl.BlockSpec((d, tm), lambda i, j: (0, i))
    sel_spec = pl.BlockSpec((PEER_HEADS * N_KEYS, tm), lambda i, j: (0, i))
    key_spec = pl.BlockSpec((te // N_KEYS, PEER_HEADS, tm), lambda i, j: (j, 0, i))
    lim, e1n, rank2, e2 = sel
    sel = (lim, e1n, rank2.reshape(-1, t), e2.reshape(-1, t))
    return pl.pallas_call(
        functools.partial(_experts_kernel, tm=tm, te=te),
        grid=(t // tm, n_exp // te),
        in_specs=[tok, pl.BlockSpec((te, d), lambda i, j: (j, 0)), pl.BlockSpec((d, te), lambda i, j: (0, j)),
                  key_spec, key_spec, sel_spec, sel_spec],
        out_specs=tok,
        out_shape=jax.ShapeDtypeStruct((d, t), F32),
        scratch_shapes=[pltpu.VMEM((te, tm), F32), pltpu.VMEM((te, tm), BF16),
                        pltpu.VMEM((2, te // N_KEYS, PEER_HEADS, GATE_ROWS, tm + LANES), BF16),
                        pltpu.VMEM((te, tm), BF16),
                        pltpu.VMEM((PEER_HEADS * N_KEYS, tm + LANES), BF16),
                        pltpu.VMEM((PEER_HEADS * N_KEYS, tm + LANES), BF16)],
        compiler_params=_params("parallel", "arbitrary"),
        name="peer_experts",
    )(xbt, u, vt, *sel)


def _ln_t_kernel(x_ref, f_ref, g_ref, b_ref, o_ref, *, alpha):
    z = (alpha * x_ref[...] + f_ref[...]).T
    mu = jnp.mean(z, axis=1, keepdims=True)
    zc = z - mu
    var = jnp.mean(zc * zc, axis=1, keepdims=True)
    o_ref[...] = zc * lax.rsqrt(var + LN_EPS) * g_ref[...] + b_ref[...]


def _residual_ln_t(x1t, fft, g, b, alpha, tm):
    d, t = x1t.shape
    tm = min(tm, t)
    tok = pl.BlockSpec((d, tm), lambda i: (0, i))
    vec = pl.BlockSpec((1, d), lambda i: (0, 0))
    return pl.pallas_call(
        functools.partial(_ln_t_kernel, alpha=alpha),
        grid=(t // tm,),
        in_specs=[tok, tok, vec, vec],
        out_specs=pl.BlockSpec((tm, d), lambda i: (i, 0)),
        out_shape=jax.ShapeDtypeStruct((t, d), F32),
        compiler_params=_params("parallel"),
        name="residual_ln",
    )(x1t, fft, g, b)


def kernel(x, mem, positions, w_in, w_mem_kv, w_pool, pool_scale, attn_sinks, w_out, ln1_g, ln1_b,
           w_peer_q, sub_keys_1, sub_keys_2, expert_u, expert_v, ln2_g, ln2_b):
    bsz, seq, d = x.shape
    depth = w_in.shape[0]
    t = bsz * seq
    alpha = (2.0 * depth) ** 0.25
    for l in range(depth):
        x2 = x.reshape(t, d)
        h = _inproj(x2, w_in[l].astype(BF16), positions, 512).reshape(bsz, seq, -1)
        mem2 = mem.reshape(-1, d).astype(BF16)
        kvm = _matmul(mem2, w_mem_kv[l].astype(BF16), BF16, 512, 512).reshape(bsz, mem.shape[1], -1)
        pool_o = _pool(h, w_pool[l].astype(BF16), pool_scale[l].reshape(1, -1), 512)
        swa_o = _swa(h, attn_sinks[l])
        mem_o = _mem_attention(h, kvm, 512)
        x1t, x1bt = _outproj_ln(pool_o.reshape(t, -1), swa_o.reshape(t, -1), mem_o.reshape(t, -1),
                          w_out[l].astype(BF16), x2, ln1_g[l].reshape(1, d), ln1_b[l].reshape(1, d), alpha, 512)
        sel = _retrieve(x1bt, _transpose_cast(w_peer_q[l], 512), sub_keys_1[l].astype(BF16),
                        sub_keys_2[l].astype(BF16), 512)
        fft = _experts(x1bt, expert_u[l].astype(BF16), _transpose_cast(expert_v[l], 512), sel, 1024, 512)
        x = _residual_ln_t(x1t, fft, ln2_g[l].reshape(1, d), ln2_b[l].reshape(1, d), alpha, 512).reshape(bsz, seq, d)
    return x
```

```python
import functools
import math

import numpy as np
import jax
import jax.numpy as jnp
from jax import lax
from jax.experimental import pallas as pl
from jax.experimental.pallas import tpu as pltpu

F32 = jnp.float32
BF16 = jnp.bfloat16

LANES = 128
SUBLANES = 8
VMEM_LIMIT_BYTES = 56 * 1024 * 1024

POOL_WINDOWS = (2, 4, 8, 16)
POOL_GROUP = 128
POOL_HALO = 16
SWA_HEAD_DIM = 64
SWA_HEADS = 16
SWA_KV_HEADS = 4
SWA_BLOCK = 128
ROPE_THETA = 500000.0
ROPE_DIM = 16
MEM_HEADS = 4
MEM_HEAD_DIM = 128
PEER_HEADS = 8
N_KEYS = 128
PEER_TOPK = 16
HALF_DIM = 128
LN_EPS = 1e-5
NEG = -1e30

EXPERT_CHUNK = 512
GATE_KEYS = 4
GATE_ROWS = 16
SELECT_TOKENS = 512


def _params(*semantics):
    return pltpu.CompilerParams(dimension_semantics=semantics, vmem_limit_bytes=VMEM_LIMIT_BYTES)


def _matmul_kernel(a_ref, b_ref, o_ref):
    o_ref[...] = jnp.dot(a_ref[...], b_ref[...], preferred_element_type=F32).astype(o_ref.dtype)


def _matmul(a, b, out_dtype, tm, tn):
    m, k = a.shape
    n = b.shape[1]
    tm, tn = min(tm, m), min(tn, n)
    return pl.pallas_call(
        _matmul_kernel,
        grid=(m // tm, n // tn),
        in_specs=[pl.BlockSpec((tm, k), lambda i, j: (i, 0)), pl.BlockSpec((k, tn), lambda i, j: (0, j))],
        out_specs=pl.BlockSpec((tm, tn), lambda i, j: (i, j)),
        out_shape=jax.ShapeDtypeStruct((m, n), out_dtype),
        compiler_params=_params("parallel", "arbitrary"),
        name="matmul",
    )(a, b)


IN_BLOCK = 512
_Q_BLOCKS = (1, 2)
_KV_BLOCK = 3


def _rope(x, c, sa, sb):
    half = ROPE_DIM // 2
    return x * c + pltpu.roll(x, LANES - half, 1) * sa + pltpu.roll(x, half, 1) * sb


def _inproj_kernel(x_ref, w_ref, pos_ref, freq_ref, sa_ref, sb_ref, o_ref):
    xb = x_ref[...].astype(BF16)
    ang = pos_ref[...].astype(F32) * freq_ref[...]
    s = jnp.sin(ang)
    c, sa, sb = jnp.cos(ang), s * sa_ref[...], s * sb_ref[...]
    for j in range(w_ref.shape[1] // IN_BLOCK):
        y = jnp.dot(xb, w_ref[:, j * IN_BLOCK:(j + 1) * IN_BLOCK], preferred_element_type=F32)
        for k in range(IN_BLOCK // LANES):
            piece = y[:, k * LANES:(k + 1) * LANES]
            if j in _Q_BLOCKS:
                piece = _rope(piece, c, sa, sb) * SWA_HEAD_DIM ** -0.5
            elif j == _KV_BLOCK and k < IN_BLOCK // LANES // 2:
                piece = _rope(piece, c, sa, sb)
            o_ref[:, j * IN_BLOCK + k * LANES:j * IN_BLOCK + (k + 1) * LANES] = piece.astype(o_ref.dtype)


def _rope_constants():
    lane = np.arange(LANES)
    d = lane % SWA_HEAD_DIM
    half = ROPE_DIM // 2
    inv_freq = np.float32(ROPE_THETA) ** (-np.arange(0, ROPE_DIM, 2, dtype=np.float32) / np.float32(ROPE_DIM))
    freq = np.where(d < ROPE_DIM, inv_freq[d % half], 0.0).astype(np.float32)
    sa = np.where(d < half, -1.0, 0.0).astype(np.float32)
    sb = np.where((d >= half) & (d < ROPE_DIM), 1.0, 0.0).astype(np.float32)
    return [jnp.asarray(a.reshape(1, LANES)) for a in (freq, sa, sb)]


def _inproj(x2, w, positions, tm):
    t, d = x2.shape
    n = w.shape[1]
    tm = min(tm, t)
    freq, sa, sb = _rope_constants()
    const = pl.BlockSpec((1, LANES), lambda i: (0, 0))
    return pl.pallas_call(
        _inproj_kernel,
        grid=(t // tm,),
        in_specs=[pl.BlockSpec((tm, d), lambda i: (i, 0)), pl.BlockSpec((d, n), lambda i: (0, 0)),
                  pl.BlockSpec((tm, 1), lambda i: (i, 0)), const, const, const],
        out_specs=pl.BlockSpec((tm, n), lambda i: (i, 0)),
        out_shape=jax.ShapeDtypeStruct((t, n), BF16),
        compiler_params=_params("parallel"),
        name="inproj_rope",
    )(x2, w, positions.reshape(t, 1), freq, sa, sb)


def _pool_kernel(v_ref, w_ref, scale_ref, o_ref, ext_ref, *, ts):
    s = pl.program_id(1)

    @pl.when(s == 0)
    def _():
        ext_ref[0:POOL_HALO, :] = jnp.zeros((POOL_HALO, ext_ref.shape[1]), F32)

    ext_ref[POOL_HALO:POOL_HALO + ts, :] = v_ref[0].astype(F32)
    pos = s * ts + lax.broadcasted_iota(jnp.int32, (ts, 1), 0)
    for g, w in enumerate(POOL_WINDOWS):
        cols = slice(g * POOL_GROUP, (g + 1) * POOL_GROUP)
        acc = ext_ref[POOL_HALO:POOL_HALO + ts, cols]
        for k in range(1, w):
            acc = acc + ext_ref[POOL_HALO - k:POOL_HALO - k + ts, cols]
        count = jnp.minimum(pos + 1, w).astype(F32)
        pooled = acc / count - ext_ref[POOL_HALO:POOL_HALO + ts, cols]
        y = jnp.dot(pooled.astype(BF16), w_ref[g], preferred_element_type=F32)
        o_ref[0, :, cols] = (y * scale_ref[:, cols]).astype(o_ref.dtype)
    ext_ref[0:POOL_HALO, :] = ext_ref[ts:ts + POOL_HALO, :]


def _pool(h, w_pool, pool_scale, ts):
    b, s, _ = h.shape
    width = POOL_GROUP * len(POOL_WINDOWS)
    ts = min(ts, s)
    return pl.pallas_call(
        functools.partial(_pool_kernel, ts=ts),
        grid=(b, s // ts),
        in_specs=[
            pl.BlockSpec((1, ts, width), lambda i, j: (i, j, 0)),
            pl.BlockSpec(w_pool.shape, lambda i, j: (0, 0, 0)),
            pl.BlockSpec((1, width), lambda i, j: (0, 0)),
        ],
        out_specs=pl.BlockSpec((1, ts, width), lambda i, j: (i, j, 0)),
        out_shape=jax.ShapeDtypeStruct((b, s, width), BF16),
        scratch_shapes=[pltpu.VMEM((ts + POOL_HALO, width), F32)],
        compiler_params=_params("arbitrary", "arbitrary"),
        name="pool",
    )(h, w_pool, pool_scale)


def _swa_kernel(sink_ref, q0_ref, q1_ref, kv_ref, kvp_ref, o_ref):
    n = pl.program_id(1)
    kvw = SWA_KV_HEADS * SWA_HEAD_DIM
    q = jnp.concatenate([q0_ref[0], q1_ref[0]], axis=1)
    k = jnp.concatenate([kvp_ref[0, :, 0:kvw], kv_ref[0, :, 0:kvw]], axis=0)
    v = jnp.concatenate([kvp_ref[0, :, kvw:2 * kvw], kv_ref[0, :, kvw:2 * kvw]], axis=0)
    row = lax.broadcasted_iota(jnp.int32, (SWA_BLOCK, 2 * SWA_BLOCK), 0)
    col = lax.broadcasted_iota(jnp.int32, (SWA_BLOCK, 2 * SWA_BLOCK), 1)
    rel = row + SWA_BLOCK - col
    valid = (rel >= 0) & (rel < SWA_BLOCK) & ((col >= SWA_BLOCK) | (n > 0))
    group = SWA_HEADS // SWA_KV_HEADS
    outs = []
    for hq in range(SWA_HEADS):
        kv = hq // group
        qh = q[:, hq * SWA_HEAD_DIM:(hq + 1) * SWA_HEAD_DIM]
        kh = k[:, kv * SWA_HEAD_DIM:(kv + 1) * SWA_HEAD_DIM]
        vh = v[:, kv * SWA_HEAD_DIM:(kv + 1) * SWA_HEAD_DIM]
        sc = lax.dot_general(qh, kh, (((1,), (1,)), ((), ())), preferred_element_type=F32)
        sc = jnp.where(valid, sc, NEG)
        sink = sink_ref[hq]
        m = jnp.maximum(jnp.max(sc, axis=1, keepdims=True), sink)
        p = jnp.exp(sc - m)
        denom = jnp.sum(p, axis=1, keepdims=True) + jnp.exp(sink - m)
        o = jnp.dot(p.astype(BF16), vh, preferred_element_type=F32)
        outs.append(o / denom)
    o_ref[0] = jnp.concatenate(outs, axis=1).astype(o_ref.dtype)


def _swa(h, sinks):
    b, s, _ = h.shape
    nb = s // SWA_BLOCK
    blk = lambda c: pl.BlockSpec((1, SWA_BLOCK, IN_BLOCK), lambda i, j: (i, j, c))
    blk_prev = lambda c: pl.BlockSpec((1, SWA_BLOCK, IN_BLOCK), lambda i, j: (i, jnp.maximum(j - 1, 0), c))
    return pl.pallas_call(
        _swa_kernel,
        grid=(b, nb),
        in_specs=[pl.BlockSpec(memory_space=pltpu.SMEM),
                  blk(_Q_BLOCKS[0]), blk(_Q_BLOCKS[1]), blk(_KV_BLOCK), blk_prev(_KV_BLOCK)],
        out_specs=pl.BlockSpec((1, SWA_BLOCK, SWA_HEADS * SWA_HEAD_DIM), lambda i, j: (i, j, 0)),
        out_shape=jax.ShapeDtypeStruct((b, s, SWA_HEADS * SWA_HEAD_DIM), BF16),
        compiler_params=_params("parallel", "arbitrary"),
        name="swa",
    )(sinks, h, h, h, h)


def _mem_kernel(q_ref, kv_ref, o_ref):
    scale = MEM_HEAD_DIM ** -0.5
    width = MEM_HEADS * MEM_HEAD_DIM
    for hm in range(MEM_HEADS):
        cols = slice(hm * MEM_HEAD_DIM, (hm + 1) * MEM_HEAD_DIM)
        km = kv_ref[0, :, cols]
        vm = kv_ref[0, :, width + hm * MEM_HEAD_DIM:width + (hm + 1) * MEM_HEAD_DIM]
        sc = lax.dot_general(q_ref[0, :, cols], km, (((1,), (1,)), ((), ())), preferred_element_type=F32) * scale
        m = jnp.max(sc, axis=1, keepdims=True)
        p = jnp.exp(sc - m)
        denom = jnp.sum(p, axis=1, keepdims=True)
        o = jnp.dot(p.astype(BF16), vm, preferred_element_type=F32)
        o_ref[0, :, cols] = (o / denom).astype(o_ref.dtype)


def _mem_attention(h, kvm, tq):
    b, s, _ = h.shape
    width = MEM_HEADS * MEM_HEAD_DIM
    tq = min(tq, s)
    return pl.pallas_call(
        _mem_kernel,
        grid=(b, s // tq),
        in_specs=[
            pl.BlockSpec((1, tq, width), lambda i, j: (i, j, 4)),
            pl.BlockSpec((1,) + kvm.shape[1:], lambda i, j: (i, 0, 0)),
        ],
        out_specs=pl.BlockSpec((1, tq, width), lambda i, j: (i, j, 0)),
        out_shape=jax.ShapeDtypeStruct((b, s, width), BF16),
        compiler_params=_params("parallel", "arbitrary"),
        name="mem_attention",
    )(h, kvm)


def _outproj_kernel(pool_ref, swa_ref, mem_ref, wp_ref, ws_ref, wm_ref, x_ref, g_ref, b_ref, o_ref, ob_ref,
                    *, alpha):
    mix = jnp.dot(pool_ref[...], wp_ref[...], preferred_element_type=F32)
    mix += jnp.dot(swa_ref[...], ws_ref[...], preferred_element_type=F32)
    mix += jnp.dot(mem_ref[...], wm_ref[...], preferred_element_type=F32)
    z = alpha * x_ref[...] + mix
    mu = jnp.mean(z, axis=1, keepdims=True)
    zc = z - mu
    var = jnp.mean(zc * zc, axis=1, keepdims=True)
    y = zc * lax.rsqrt(var + LN_EPS) * g_ref[...] + b_ref[...]
    yt = y.T
    o_ref[...] = yt
    ob_ref[...] = yt.astype(BF16)


def _outproj_ln(pool_o, swa_o, mem_o, w_out, x2, g, b, alpha, tm):
    t, d = x2.shape
    tm = min(tm, t)
    wp, ws, wm = pool_o.shape[1], swa_o.shape[1], mem_o.shape[1]
    w_p, w_s, w_m = w_out[:wp], w_out[wp:wp + ws], w_out[wp + ws:]
    row = lambda w: pl.BlockSpec((tm, w), lambda i: (i, 0))
    full = lambda a: pl.BlockSpec(a.shape, lambda i: (0, 0))
    return pl.pallas_call(
        functools.partial(_outproj_kernel, alpha=alpha),
        grid=(t // tm,),
        in_specs=[row(wp), row(ws), row(wm), full(w_p), full(w_s), full(w_m), row(d),
                  pl.BlockSpec((1, d), lambda i: (0, 0)), pl.BlockSpec((1, d), lambda i: (0, 0))],
        out_specs=[pl.BlockSpec((d, tm), lambda i: (0, i))] * 2,
        out_shape=[jax.ShapeDtypeStruct((d, t), F32), jax.ShapeDtypeStruct((d, t), BF16)],
        compiler_params=_params("parallel"),
        name="outproj_ln",
    )(pool_o, swa_o, mem_o, w_p, w_s, w_m, x2, g, b)


def _merge_exchange_pairs(n):
    t = (n - 1).bit_length()
    pairs = []
    p = 1 << (t - 1)
    while p >= 1:
        q, r, d = 1 << (t - 1), 0, p
        while True:
            pairs += [(i, i + d) for i in range(n - d) if (i & p) == r]
            if q == p:
                break
            d, q, r = q - p, q // 2, p
        p //= 2
    return pairs


def _top16_rows_distinct(s, with_rank):
    k = PEER_TOPK
    lists = [s[SUBLANES * g:SUBLANES * (g + 1), :] for g in range(s.shape[0] // SUBLANES)]
    for i, j in _merge_exchange_pairs(len(lists)):
        lists[i], lists[j] = jnp.maximum(lists[i], lists[j]), jnp.minimum(lists[i], lists[j])
    vals = []
    for r in range(k):
        m = jnp.max(lists[0], axis=0, keepdims=True)
        vals.append(m)
        pop = lists[0] == m
        for d in range(k - 1 - r):
            lists[d] = jnp.where(pop, lists[d + 1], lists[d])
    rank = None
    if with_rank:
        rank = jnp.zeros(s.shape, F32)
        for r in range(k):
            rank = rank + jnp.where(vals[r] > s, 1.0, 0.0)
    count = jnp.sum(jnp.where(s >= vals[k - 1], 1.0, 0.0), axis=0, keepdims=True)
    strict = jnp.zeros_like(count)
    for r in range(k - 1):
        strict = strict + jnp.where(vals[r] > vals[r + 1], 1.0, 0.0)
    distinct = (count == float(k)) & (strict == float(k - 1))
    return rank, vals, jnp.where(distinct, float(k), 0.0)


def _top16_rows(s, break_ties, with_rank=True):
    if not break_ties:
        return _top16_rows_distinct(s, with_rank)
    n = s.shape[0]
    iota = lax.broadcasted_iota(jnp.int32, s.shape, 0).astype(F32)
    rank = jnp.full(s.shape, float(PEER_TOPK), F32)
    vals = []
    for r in range(PEER_TOPK):
        m = jnp.max(s, axis=0, keepdims=True)
        idx = jnp.min(jnp.where(s == m, iota, float(n)), axis=0, keepdims=True)
        hit = iota == idx
        rank = jnp.where(hit, float(r), rank)
        s = jnp.where(hit, -jnp.inf, s)
        vals.append(m)
    count = jnp.sum(jnp.where(rank < float(PEER_TOPK), 1.0, 0.0), axis=0, keepdims=True)
    return rank, vals, count


_CAND_SMALL_A = PEER_TOPK // 2
_CAND_ROWS = PEER_TOPK + (_CAND_SMALL_A - 1) * SUBLANES + SUBLANES


def _cand_constants():
    flat = np.full((_CAND_ROWS, 1), 1e9, np.float32)
    valid = np.zeros((_CAND_ROWS, 1), np.float32)
    for b in range(PEER_TOPK):
        flat[b, 0], valid[b, 0] = b, 1.0
    for a in range(1, _CAND_SMALL_A):
        base = PEER_TOPK + (a - 1) * SUBLANES
        for b in range(PEER_TOPK // (a + 1)):
            flat[base + b, 0], valid[base + b, 0] = a * PEER_TOPK + b, 1.0
    base = PEER_TOPK + (_CAND_SMALL_A - 1) * SUBLANES
    for k in range(SUBLANES):
        flat[base + k, 0], valid[base + k, 0] = (_CAND_SMALL_A + k) * PEER_TOPK, 1.0
    return jnp.asarray(flat), jnp.asarray(valid)


def _select_experts(s1, s2, flat, valid, break_ties):
    t = s1.shape[1]
    rank1, v1, count1 = _top16_rows(s1, break_ties, with_rank=break_ties)
    rank2, v2, count2 = _top16_rows(s2, break_ties)
    v2_lo = jnp.concatenate(v2[:SUBLANES], axis=0)
    v2_all = jnp.concatenate(v2, axis=0)
    v1_hi = jnp.concatenate(v1[_CAND_SMALL_A:], axis=0)
    groups = [v1[0] + v2_all]
    for a in range(1, _CAND_SMALL_A):
        groups.append(v1[a] + v2_lo)
    groups.append(v1_hi + v2[0])
    cand = jnp.concatenate(groups, axis=0)
    cand = jnp.where(valid > 0.5, cand, -jnp.inf)
    flat_b = jnp.broadcast_to(flat, cand.shape) if break_ties else None
    all_cand = cand
    hits = jnp.zeros(cand.shape, F32)
    top = []
    for r in range(PEER_TOPK):
        m = jnp.max(cand, axis=0, keepdims=True)
        hit = cand == m
        if break_ties:
            pick = jnp.min(jnp.where(hit, flat_b, 2e9), axis=0, keepdims=True)
            hit = flat_b == pick
            hits = jnp.where(hit, 1.0, hits)
        cand = jnp.where(hit, -jnp.inf, cand)
        top.append(m)
    if not break_ties:
        hits = jnp.where(all_cand >= top[PEER_TOPK - 1], 1.0, 0.0)
    z = jnp.ones((1, t), F32)
    for r in range(1, PEER_TOPK):
        z = z + jnp.exp(top[r] - top[0])
    counts = [jnp.sum(hits[0:PEER_TOPK], axis=0, keepdims=True)]
    for a in range(1, _CAND_SMALL_A):
        base = PEER_TOPK + (a - 1) * SUBLANES
        counts.append(jnp.sum(hits[base:base + SUBLANES], axis=0, keepdims=True))
    base = PEER_TOPK + (_CAND_SMALL_A - 1) * SUBLANES
    for k in range(SUBLANES):
        counts.append(hits[base + k:base + k + 1])
    lim = jnp.zeros(s1.shape, F32)
    if rank1 is not None:
        for a in range(PEER_TOPK):
            lim = jnp.where(rank1 == float(a), counts[a], lim)
    else:
        for a in range(PEER_TOPK):
            step = counts[a] - counts[a + 1] if a + 1 < PEER_TOPK else counts[a]
            lim = lim + jnp.where(s1 >= v1[a], step, 0.0)
    e1n = jnp.exp(s1 - v1[0]) * (1.0 / z)
    e2 = jnp.exp(s2 - v2[0])
    count3 = jnp.sum(hits, axis=0, keepdims=True)
    k = float(PEER_TOPK)
    ok = jnp.where((count1 == k) & (count2 == k) & (count3 == k), 1.0, 0.0)
    return lim, e1n, rank2, e2, ok


def _retrieve_kernel(wq_ref, x_ref, k1_ref, k2_ref, flat_ref, valid_ref,
                     lim_out_ref, e1_out_ref, r2_ref, e2_ref, q_ref, lim_ref, e1_ref, *, tq):
    q_ref[...] = jnp.dot(wq_ref[...], x_ref[...], preferred_element_type=F32)
    flat, valid = flat_ref[...], valid_ref[...]

    def head(h, carry):
        for c in range(tq // SELECT_TOKENS):
            lanes = slice(c * SELECT_TOKENS, (c + 1) * SELECT_TOKENS)
            r0 = pl.multiple_of(h * 2 * HALF_DIM, 2 * HALF_DIM)
            q1 = q_ref[pl.ds(r0, HALF_DIM), lanes].astype(BF16)
            q2 = q_ref[pl.ds(r0 + HALF_DIM, HALF_DIM), lanes].astype(BF16)
            s1 = jnp.dot(k1_ref[...], q1, preferred_element_type=F32)
            s2 = jnp.dot(k2_ref[...], q2, preferred_element_type=F32)

            def emit(break_ties):
                lim, e1n, rank2, e2, ok = _select_experts(s1, s2, flat, valid, break_ties)
                lim_ref[h, :, lanes] = lim
                e1_ref[h, :, lanes] = e1n
                r2_ref[h, :, lanes] = rank2.astype(BF16)
                e2_ref[h, :, lanes] = e2.astype(BF16)
                return ok

            ok = emit(False)

            @pl.when(jnp.min(ok) < 0.5)
            def _():
                emit(True)
        return carry

    lax.fori_loop(0, PEER_HEADS, head, 0)
    lim_out_ref[...] = jnp.swapaxes(lim_ref[...], 0, 1)
    e1_out_ref[...] = jnp.swapaxes(e1_ref[...], 0, 1)


def _retrieve(x1t, wq_t, k1, k2, tq):
    d, t = x1t.shape
    tq = min(tq, t)
    flat, valid = _cand_constants()
    full = lambda a: pl.BlockSpec(a.shape, lambda i: (0,) * a.ndim)
    out_keys = jax.ShapeDtypeStruct((N_KEYS, PEER_HEADS, t), F32)
    key_spec = pl.BlockSpec((N_KEYS, PEER_HEADS, tq), lambda i: (0, 0, i))
    out_packed = jax.ShapeDtypeStruct((PEER_HEADS, N_KEYS, t), BF16)
    out_spec = pl.BlockSpec((PEER_HEADS, N_KEYS, tq), lambda i: (0, 0, i))
    return pl.pallas_call(
        functools.partial(_retrieve_kernel, tq=tq),
        grid=(t // tq,),
        in_specs=[full(wq_t), pl.BlockSpec((d, tq), lambda i: (0, i)), full(k1), full(k2), full(flat), full(valid)],
        out_specs=[key_spec, key_spec, out_spec, out_spec],
        out_shape=[out_keys, out_keys, out_packed, out_packed],
        scratch_shapes=[pltpu.VMEM((wq_t.shape[0], tq), F32)] + [pltpu.VMEM((PEER_HEADS, N_KEYS, tq), F32)] * 2,
        compiler_params=_params("parallel"),
        name="peer_retrieve",
    )(wq_t, x1t, k1, k2, flat, valid)


def _transpose_cast_kernel(v_ref, o_ref):
    o_ref[...] = v_ref[...].T.astype(o_ref.dtype).reshape(o_ref.shape)


def _transpose_cast(v, rows, tiled=False):
    n, d = v.shape
    rows = min(rows, n)
    if tiled:
        out_spec = pl.BlockSpec((1, d, rows), lambda i: (i, 0, 0))
        out_shape = jax.ShapeDtypeStruct((n // rows, d, rows), BF16)
    else:
        out_spec = pl.BlockSpec((d, rows), lambda i: (0, i))
        out_shape = jax.ShapeDtypeStruct((d, n), BF16)
    return pl.pallas_call(
        _transpose_cast_kernel,
        grid=(n // rows,),
        in_specs=[pl.BlockSpec((rows, d), lambda i: (i, 0))],
        out_specs=out_spec,
        out_shape=out_shape,
        compiler_params=_params("parallel"),
        name="transpose_cast",
    )(v)


def _gelu_exact(x):
    return 0.5 * x * (1.0 + lax.erf(x * (1.0 / math.sqrt(2.0))))


def _experts_kernel(xb_ref, u_ref, vt_ref, lim_ref, e1_ref, r2_in_ref, e2_in_ref, o_ref,
                    h_ref, a_ref, rows_ref, gate_ref, r2_ref, e2_ref, *, tm, te):
    e = pl.program_id(1)

    @pl.when(e == 0)
    def _():
        o_ref[...] = jnp.zeros(o_ref.shape, F32)
        r2_ref[:, 0:tm] = r2_in_ref[...]
        e2_ref[:, LANES:LANES + tm] = e2_in_ref[...]

    n_chunks = te // EXPERT_CHUNK

    def rows_of(p):
        return slice(p * EXPERT_CHUNK, (p + 1) * EXPERT_CHUNK)

    def gates(first_key):
        keys = range(first_key, first_key + GATE_KEYS)
        blocks = range(0, N_KEYS, GATE_ROWS)
        for i in keys:
            for hd in range(PEER_HEADS):
                for q, ref in enumerate((lim_ref, e1_ref)):
                    row = jnp.broadcast_to(ref[i, hd:hd + 1, :], (GATE_ROWS, tm)).astype(BF16)
                    rows_ref[q, i, hd, :, q * LANES:q * LANES + tm] = row
        for c in range(tm // LANES):
            lanes = slice(c * LANES, (c + 1) * LANES)
            lanes1 = slice((c + 1) * LANES, (c + 2) * LANES)
            gate = {i: {jb: jnp.zeros((GATE_ROWS, LANES), BF16) for jb in blocks} for i in keys}
            for hd in range(PEER_HEADS):
                lim = {i: rows_ref[0, i, hd, :, lanes] for i in keys}
                e1 = {i: rows_ref[1, i, hd, :, lanes1] for i in keys}
                for jb in blocks:
                    r2 = r2_ref[hd * N_KEYS + jb:hd * N_KEYS + jb + GATE_ROWS, lanes]
                    e2 = e2_ref[hd * N_KEYS + jb:hd * N_KEYS + jb + GATE_ROWS, lanes1]
                    for i in keys:
                        gate[i][jb] = gate[i][jb] + jnp.where(r2 < lim[i], e2 * e1[i], jnp.zeros_like(e2))
            for i in keys:
                for jb in blocks:
                    gate_ref[i * N_KEYS + jb:i * N_KEYS + jb + GATE_ROWS, lanes] = gate[i][jb]

    for first_key in range(0, te // N_KEYS, GATE_KEYS):
        gates(first_key)
    h_ref[...] = jnp.dot(u_ref[...], xb_ref[...], preferred_element_type=F32)
    for p in range(n_chunks):
        a_ref[rows_of(p), :] = gate_ref[rows_of(p), :] * _gelu_exact(h_ref[rows_of(p), :]).astype(BF16)
        o_ref[...] += jnp.dot(vt_ref[0, :, rows_of(p)], a_ref[rows_of(p), :], preferred_element_type=F32)


def _experts(xbt, u, vt, sel, tm, te):
    d, t = xbt.shape
    n_exp = u.shape[0]
    tm, te = min(tm, t), min(te, n_exp)
    assert vt.shape == (n_exp // te, d, te), vt.shape
    tok = pl.BlockSpec((d, tm), lambda i, j: (0, i))
    sel_spec = pl.BlockSpec((PEER_HEADS * N_KEYS, tm), lambda i, j: (0, i))
    key_spec = pl.BlockSpec((te // N_KEYS, PEER_HEADS, tm), lambda i, j: (j, 0, i))
    lim, e1n, rank2, e2 = sel
    sel = (lim, e1n, rank2.reshape(-1, t), e2.reshape(-1, t))
    return pl.pallas_call(
        functools.partial(_experts_kernel, tm=tm, te=te),
        grid=(t // tm, n_exp // te),
        in_specs=[tok, pl.BlockSpec((te, d), lambda i, j: (j, 0)), pl.BlockSpec((1, d, te), lambda i, j: (j, 0, 0)),
                  key_spec, key_spec, sel_spec, sel_spec],
        out_specs=tok,
        out_shape=jax.ShapeDtypeStruct((d, t), F32),
        scratch_shapes=[pltpu.VMEM((te, tm), F32), pltpu.VMEM((te, tm), BF16),
                        pltpu.VMEM((2, te // N_KEYS, PEER_HEADS, GATE_ROWS, tm + LANES), BF16),
                        pltpu.VMEM((te, tm), BF16),
                        pltpu.VMEM((PEER_HEADS * N_KEYS, tm + LANES), BF16),
                        pltpu.VMEM((PEER_HEADS * N_KEYS, tm + LANES), BF16)],
        compiler_params=_params("parallel", "arbitrary"),
        name="peer_experts",
    )(xbt, u, vt, *sel)


def _ln_t_kernel(x_ref, f_ref, g_ref, b_ref, o_ref, *, alpha):
    z = (alpha * x_ref[...] + f_ref[...]).T
    mu = jnp.mean(z, axis=1, keepdims=True)
    zc = z - mu
    var = jnp.mean(zc * zc, axis=1, keepdims=True)
    o_ref[...] = zc * lax.rsqrt(var + LN_EPS) * g_ref[...] + b_ref[...]


def _residual_ln_t(x1t, fft, g, b, alpha, tm):
    d, t = x1t.shape
    tm = min(tm, t)
    tok = pl.BlockSpec((d, tm), lambda i: (0, i))
    vec = pl.BlockSpec((1, d), lambda i: (0, 0))
    return pl.pallas_call(
        functools.partial(_ln_t_kernel, alpha=alpha),
        grid=(t // tm,),
        in_specs=[tok, tok, vec, vec],
        out_specs=pl.BlockSpec((tm, d), lambda i: (i, 0)),
        out_shape=jax.ShapeDtypeStruct((t, d), F32),
        compiler_params=_params("parallel"),
        name="residual_ln",
    )(x1t, fft, g, b)


def kernel(x, mem, positions, w_in, w_mem_kv, w_pool, pool_scale, attn_sinks, w_out, ln1_g, ln1_b,
           w_peer_q, sub_keys_1, sub_keys_2, expert_u, expert_v, ln2_g, ln2_b):
    bsz, seq, d = x.shape
    depth = w_in.shape[0]
    t = bsz * seq
    alpha = (2.0 * depth) ** 0.25
    for l in range(depth):
        x2 = x.reshape(t, d)
        h = _inproj(x2, w_in[l].astype(BF16), positions, 512).reshape(bsz, seq, -1)
        mem2 = mem.reshape(-1, d).astype(BF16)
        kvm = _matmul(mem2, w_mem_kv[l].astype(BF16), BF16, 512, 512).reshape(bsz, mem.shape[1], -1)
        pool_o = _pool(h, w_pool[l].astype(BF16), pool_scale[l].reshape(1, -1), 512)
        swa_o = _swa(h, attn_sinks[l])
        mem_o = _mem_attention(h, kvm, 512)
        x1t, x1bt = _outproj_ln(pool_o.reshape(t, -1), swa_o.reshape(t, -1), mem_o.reshape(t, -1),
                          w_out[l].astype(BF16), x2, ln1_g[l].reshape(1, d), ln1_b[l].reshape(1, d), alpha, 512)
        sel = _retrieve(x1bt, _transpose_cast(w_peer_q[l], 512), sub_keys_1[l].astype(BF16),
                        sub_keys_2[l].astype(BF16), 512)
        fft = _experts(x1bt, expert_u[l].astype(BF16), _transpose_cast(expert_v[l], 512, tiled=True), sel, 1024, 512)
        x = _residual_ln_t(x1t, fft, ln2_g[l].reshape(1, d), ln2_b[l].reshape(1, d), alpha, 512).reshape(bsz, seq, d)
    return x
```

```python
import functools
import math

import numpy as np
import jax
import jax.numpy as jnp
from jax import lax
from jax.experimental import pallas as pl
from jax.experimental.pallas import tpu as pltpu

F32 = jnp.float32
BF16 = jnp.bfloat16

LANES = 128
SUBLANES = 8
VMEM_LIMIT_BYTES = 56 * 1024 * 1024

POOL_WINDOWS = (2, 4, 8, 16)
POOL_GROUP = 128
POOL_HALO = 16
SWA_HEAD_DIM = 64
SWA_HEADS = 16
SWA_KV_HEADS = 4
SWA_BLOCK = 128
ROPE_THETA = 500000.0
ROPE_DIM = 16
MEM_HEADS = 4
MEM_HEAD_DIM = 128
PEER_HEADS = 8
N_KEYS = 128
PEER_TOPK = 16
HALF_DIM = 128
LN_EPS = 1e-5
NEG = -1e30

EXPERT_CHUNK = 512
GATE_KEYS = 4
GATE_ROWS = 16
SELECT_TOKENS = 512


def _params(*semantics):
    return pltpu.CompilerParams(dimension_semantics=semantics, vmem_limit_bytes=VMEM_LIMIT_BYTES)


def _matmul_kernel(a_ref, b_ref, o_ref):
    o_ref[...] = jnp.dot(a_ref[...], b_ref[...], preferred_element_type=F32).astype(o_ref.dtype)


def _matmul(a, b, out_dtype, tm, tn):
    m, k = a.shape
    n = b.shape[1]
    tm, tn = min(tm, m), min(tn, n)
    return pl.pallas_call(
        _matmul_kernel,
        grid=(m // tm, n // tn),
        in_specs=[pl.BlockSpec((tm, k), lambda i, j: (i, 0)), pl.BlockSpec((k, tn), lambda i, j: (0, j))],
        out_specs=pl.BlockSpec((tm, tn), lambda i, j: (i, j)),
        out_shape=jax.ShapeDtypeStruct((m, n), out_dtype),
        compiler_params=_params("parallel", "arbitrary"),
        name="matmul",
    )(a, b)


IN_BLOCK = 512
_Q_BLOCKS = (1, 2)
_KV_BLOCK = 3


def _rope(x, c, sa, sb):
    half = ROPE_DIM // 2
    return x * c + pltpu.roll(x, LANES - half, 1) * sa + pltpu.roll(x, half, 1) * sb


def _inproj_kernel(x_ref, w_ref, pos_ref, freq_ref, sa_ref, sb_ref, o_ref):
    xb = x_ref[...].astype(BF16)
    ang = pos_ref[...].astype(F32) * freq_ref[...]
    s = jnp.sin(ang)
    c, sa, sb = jnp.cos(ang), s * sa_ref[...], s * sb_ref[...]
    for j in range(w_ref.shape[1] // IN_BLOCK):
        y = jnp.dot(xb, w_ref[:, j * IN_BLOCK:(j + 1) * IN_BLOCK], preferred_element_type=F32)
        for k in range(IN_BLOCK // LANES):
            piece = y[:, k * LANES:(k + 1) * LANES]
            if j in _Q_BLOCKS:
                piece = _rope(piece, c, sa, sb) * SWA_HEAD_DIM ** -0.5
            elif j == _KV_BLOCK and k < IN_BLOCK // LANES // 2:
                piece = _rope(piece, c, sa, sb)
            o_ref[:, j * IN_BLOCK + k * LANES:j * IN_BLOCK + (k + 1) * LANES] = piece.astype(o_ref.dtype)


def _rope_constants():
    lane = np.arange(LANES)
    d = lane % SWA_HEAD_DIM
    half = ROPE_DIM // 2
    inv_freq = np.float32(ROPE_THETA) ** (-np.arange(0, ROPE_DIM, 2, dtype=np.float32) / np.float32(ROPE_DIM))
    freq = np.where(d < ROPE_DIM, inv_freq[d % half], 0.0).astype(np.float32)
    sa = np.where(d < half, -1.0, 0.0).astype(np.float32)
    sb = np.where((d >= half) & (d < ROPE_DIM), 1.0, 0.0).astype(np.float32)
    return [jnp.asarray(a.reshape(1, LANES)) for a in (freq, sa, sb)]


def _inproj(x2, w, positions, tm):
    t, d = x2.shape
    n = w.shape[1]
    tm = min(tm, t)
    freq, sa, sb = _rope_constants()
    const = pl.BlockSpec((1, LANES), lambda i: (0, 0))
    return pl.pallas_call(
        _inproj_kernel,
        grid=(t // tm,),
        in_specs=[pl.BlockSpec((tm, d), lambda i: (i, 0)), pl.BlockSpec((d, n), lambda i: (0, 0)),
                  pl.BlockSpec((tm, 1), lambda i: (i, 0)), const, const, const],
        out_specs=pl.BlockSpec((tm, n), lambda i: (i, 0)),
        out_shape=jax.ShapeDtypeStruct((t, n), BF16),
        compiler_params=_params("parallel"),
        name="inproj_rope",
    )(x2, w, positions.reshape(t, 1), freq, sa, sb)


def _pool_kernel(v_ref, w_ref, scale_ref, o_ref, ext_ref, *, ts):
    s = pl.program_id(1)

    @pl.when(s == 0)
    def _():
        ext_ref[0:POOL_HALO, :] = jnp.zeros((POOL_HALO, ext_ref.shape[1]), F32)

    ext_ref[POOL_HALO:POOL_HALO + ts, :] = v_ref[0].astype(F32)
    pos = s * ts + lax.broadcasted_iota(jnp.int32, (ts, 1), 0)
    for g, w in enumerate(POOL_WINDOWS):
        cols = slice(g * POOL_GROUP, (g + 1) * POOL_GROUP)
        acc = ext_ref[POOL_HALO:POOL_HALO + ts, cols]
        for k in range(1, w):
            acc = acc + ext_ref[POOL_HALO - k:POOL_HALO - k + ts, cols]
        count = jnp.minimum(pos + 1, w).astype(F32)
        pooled = acc / count - ext_ref[POOL_HALO:POOL_HALO + ts, cols]
        y = jnp.dot(pooled.astype(BF16), w_ref[g], preferred_element_type=F32)
        o_ref[0, :, cols] = (y * scale_ref[:, cols]).astype(o_ref.dtype)
    ext_ref[0:POOL_HALO, :] = ext_ref[ts:ts + POOL_HALO, :]


def _pool(h, w_pool, pool_scale, ts):
    b, s, _ = h.shape
    width = POOL_GROUP * len(POOL_WINDOWS)
    ts = min(ts, s)
    return pl.pallas_call(
        functools.partial(_pool_kernel, ts=ts),
        grid=(b, s // ts),
        in_specs=[
            pl.BlockSpec((1, ts, width), lambda i, j: (i, j, 0)),
            pl.BlockSpec(w_pool.shape, lambda i, j: (0, 0, 0)),
            pl.BlockSpec((1, width), lambda i, j: (0, 0)),
        ],
        out_specs=pl.BlockSpec((1, ts, width), lambda i, j: (i, j, 0)),
        out_shape=jax.ShapeDtypeStruct((b, s, width), BF16),
        scratch_shapes=[pltpu.VMEM((ts + POOL_HALO, width), F32)],
        compiler_params=_params("arbitrary", "arbitrary"),
        name="pool",
    )(h, w_pool, pool_scale)


def _swa_kernel(sink_ref, q0_ref, q1_ref, kv_ref, kvp_ref, o_ref):
    n = pl.program_id(1)
    kvw = SWA_KV_HEADS * SWA_HEAD_DIM
    q = jnp.concatenate([q0_ref[0], q1_ref[0]], axis=1)
    k = jnp.concatenate([kvp_ref[0, :, 0:kvw], kv_ref[0, :, 0:kvw]], axis=0)
    v = jnp.concatenate([kvp_ref[0, :, kvw:2 * kvw], kv_ref[0, :, kvw:2 * kvw]], axis=0)
    row = lax.broadcasted_iota(jnp.int32, (SWA_BLOCK, 2 * SWA_BLOCK), 0)
    col = lax.broadcasted_iota(jnp.int32, (SWA_BLOCK, 2 * SWA_BLOCK), 1)
    rel = row + SWA_BLOCK - col
    valid = (rel >= 0) & (rel < SWA_BLOCK) & ((col >= SWA_BLOCK) | (n > 0))
    group = SWA_HEADS // SWA_KV_HEADS
    outs = []
    for hq in range(SWA_HEADS):
        kv = hq // group
        qh = q[:, hq * SWA_HEAD_DIM:(hq + 1) * SWA_HEAD_DIM]
        kh = k[:, kv * SWA_HEAD_DIM:(kv + 1) * SWA_HEAD_DIM]
        vh = v[:, kv * SWA_HEAD_DIM:(kv + 1) * SWA_HEAD_DIM]
        sc = lax.dot_general(qh, kh, (((1,), (1,)), ((), ())), preferred_element_type=F32)
        sc = jnp.where(valid, sc, NEG)
        sink = sink_ref[hq]
        m = jnp.maximum(jnp.max(sc, axis=1, keepdims=True), sink)
        p = jnp.exp(sc - m)
        denom = jnp.sum(p, axis=1, keepdims=True) + jnp.exp(sink - m)
        o = jnp.dot(p.astype(BF16), vh, preferred_element_type=F32)
        outs.append(o / denom)
    o_ref[0] = jnp.concatenate(outs, axis=1).astype(o_ref.dtype)


def _swa(h, sinks):
    b, s, _ = h.shape
    nb = s // SWA_BLOCK
    blk = lambda c: pl.BlockSpec((1, SWA_BLOCK, IN_BLOCK), lambda i, j: (i, j, c))
    blk_prev = lambda c: pl.BlockSpec((1, SWA_BLOCK, IN_BLOCK), lambda i, j: (i, jnp.maximum(j - 1, 0), c))
    return pl.pallas_call(
        _swa_kernel,
        grid=(b, nb),
        in_specs=[pl.BlockSpec(memory_space=pltpu.SMEM),
                  blk(_Q_BLOCKS[0]), blk(_Q_BLOCKS[1]), blk(_KV_BLOCK), blk_prev(_KV_BLOCK)],
        out_specs=pl.BlockSpec((1, SWA_BLOCK, SWA_HEADS * SWA_HEAD_DIM), lambda i, j: (i, j, 0)),
        out_shape=jax.ShapeDtypeStruct((b, s, SWA_HEADS * SWA_HEAD_DIM), BF16),
        compiler_params=_params("parallel", "arbitrary"),
        name="swa",
    )(sinks, h, h, h, h)


def _mem_kernel(q_ref, kv_ref, o_ref):
    scale = MEM_HEAD_DIM ** -0.5
    width = MEM_HEADS * MEM_HEAD_DIM
    for hm in range(MEM_HEADS):
        cols = slice(hm * MEM_HEAD_DIM, (hm + 1) * MEM_HEAD_DIM)
        km = kv_ref[0, :, cols]
        vm = kv_ref[0, :, width + hm * MEM_HEAD_DIM:width + (hm + 1) * MEM_HEAD_DIM]
        sc = lax.dot_general(q_ref[0, :, cols], km, (((1,), (1,)), ((), ())), preferred_element_type=F32) * scale
        m = jnp.max(sc, axis=1, keepdims=True)
        p = jnp.exp(sc - m)
        denom = jnp.sum(p, axis=1, keepdims=True)
        o = jnp.dot(p.astype(BF16), vm, preferred_element_type=F32)
        o_ref[0, :, cols] = (o / denom).astype(o_ref.dtype)


def _mem_attention(h, kvm, tq):
    b, s, _ = h.shape
    width = MEM_HEADS * MEM_HEAD_DIM
    tq = min(tq, s)
    return pl.pallas_call(
        _mem_kernel,
        grid=(b, s // tq),
        in_specs=[
            pl.BlockSpec((1, tq, width), lambda i, j: (i, j, 4)),
            pl.BlockSpec((1,) + kvm.shape[1:], lambda i, j: (i, 0, 0)),
        ],
        out_specs=pl.BlockSpec((1, tq, width), lambda i, j: (i, j, 0)),
        out_shape=jax.ShapeDtypeStruct((b, s, width), BF16),
        compiler_params=_params("parallel", "arbitrary"),
        name="mem_attention",
    )(h, kvm)


def _outproj_kernel(pool_ref, swa_ref, mem_ref, wp_ref, ws_ref, wm_ref, x_ref, g_ref, b_ref, o_ref, ob_ref,
                    *, alpha):
    mix = jnp.dot(pool_ref[...], wp_ref[...], preferred_element_type=F32)
    mix += jnp.dot(swa_ref[...], ws_ref[...], preferred_element_type=F32)
    mix += jnp.dot(mem_ref[...], wm_ref[...], preferred_element_type=F32)
    z = alpha * x_ref[...] + mix
    mu = jnp.mean(z, axis=1, keepdims=True)
    zc = z - mu
    var = jnp.mean(zc * zc, axis=1, keepdims=True)
    y = zc * lax.rsqrt(var + LN_EPS) * g_ref[...] + b_ref[...]
    yt = y.T
    o_ref[...] = yt
    ob_ref[...] = yt.astype(BF16)


def _outproj_ln(pool_o, swa_o, mem_o, w_out, x2, g, b, alpha, tm):
    t, d = x2.shape
    tm = min(tm, t)
    wp, ws, wm = pool_o.shape[1], swa_o.shape[1], mem_o.shape[1]
    w_p, w_s, w_m = w_out[:wp], w_out[wp:wp + ws], w_out[wp + ws:]
    row = lambda w: pl.BlockSpec((tm, w), lambda i: (i, 0))
    full = lambda a: pl.BlockSpec(a.shape, lambda i: (0, 0))
    return pl.pallas_call(
        functools.partial(_outproj_kernel, alpha=alpha),
        grid=(t // tm,),
        in_specs=[row(wp), row(ws), row(wm), full(w_p), full(w_s), full(w_m), row(d),
                  pl.BlockSpec((1, d), lambda i: (0, 0)), pl.BlockSpec((1, d), lambda i: (0, 0))],
        out_specs=[pl.BlockSpec((d, tm), lambda i: (0, i))] * 2,
        out_shape=[jax.ShapeDtypeStruct((d, t), F32), jax.ShapeDtypeStruct((d, t), BF16)],
        compiler_params=_params("parallel"),
        name="outproj_ln",
    )(pool_o, swa_o, mem_o, w_p, w_s, w_m, x2, g, b)


def _merge_exchange_pairs(n):
    t = (n - 1).bit_length()
    pairs = []
    p = 1 << (t - 1)
    while p >= 1:
        q, r, d = 1 << (t - 1), 0, p
        while True:
            pairs += [(i, i + d) for i in range(n - d) if (i & p) == r]
            if q == p:
                break
            d, q, r = q - p, q // 2, p
        p //= 2
    return pairs


def _top16_rows_distinct(s, with_rank):
    k = PEER_TOPK
    lists = [s[SUBLANES * g:SUBLANES * (g + 1), :] for g in range(s.shape[0] // SUBLANES)]
    for i, j in _merge_exchange_pairs(len(lists)):
        lists[i], lists[j] = jnp.maximum(lists[i], lists[j]), jnp.minimum(lists[i], lists[j])
    vals = []
    for r in range(k):
        m = jnp.max(lists[0], axis=0, keepdims=True)
        vals.append(m)
        pop = lists[0] == m
        for d in range(k - 1 - r):
            lists[d] = jnp.where(pop, lists[d + 1], lists[d])
    rank = None
    if with_rank:
        rank = jnp.zeros(s.shape, F32)
        for r in range(k):
            rank = rank + jnp.where(vals[r] > s, 1.0, 0.0)
    count = jnp.sum(jnp.where(s >= vals[k - 1], 1.0, 0.0), axis=0, keepdims=True)
    strict = jnp.zeros_like(count)
    for r in range(k - 1):
        strict = strict + jnp.where(vals[r] > vals[r + 1], 1.0, 0.0)
    distinct = (count == float(k)) & (strict == float(k - 1))
    return rank, vals, jnp.where(distinct, float(k), 0.0)


def _top16_rows(s, break_ties, with_rank=True):
    if not break_ties:
        return _top16_rows_distinct(s, with_rank)
    n = s.shape[0]
    iota = lax.broadcasted_iota(jnp.int32, s.shape, 0).astype(F32)
    rank = jnp.full(s.shape, float(PEER_TOPK), F32)
    vals = []
    for r in range(PEER_TOPK):
        m = jnp.max(s, axis=0, keepdims=True)
        idx = jnp.min(jnp.where(s == m, iota, float(n)), axis=0, keepdims=True)
        hit = iota == idx
        rank = jnp.where(hit, float(r), rank)
        s = jnp.where(hit, -jnp.inf, s)
        vals.append(m)
    count = jnp.sum(jnp.where(rank < float(PEER_TOPK), 1.0, 0.0), axis=0, keepdims=True)
    return rank, vals, count


_CAND_SMALL_A = PEER_TOPK // 2
_CAND_ROWS = PEER_TOPK + (_CAND_SMALL_A - 1) * SUBLANES + SUBLANES


def _cand_constants():
    flat = np.full((_CAND_ROWS, 1), 1e9, np.float32)
    valid = np.zeros((_CAND_ROWS, 1), np.float32)
    for b in range(PEER_TOPK):
        flat[b, 0], valid[b, 0] = b, 1.0
    for a in range(1, _CAND_SMALL_A):
        base = PEER_TOPK + (a - 1) * SUBLANES
        for b in range(PEER_TOPK // (a + 1)):
            flat[base + b, 0], valid[base + b, 0] = a * PEER_TOPK + b, 1.0
    base = PEER_TOPK + (_CAND_SMALL_A - 1) * SUBLANES
    for k in range(SUBLANES):
        flat[base + k, 0], valid[base + k, 0] = (_CAND_SMALL_A + k) * PEER_TOPK, 1.0
    return jnp.asarray(flat), jnp.asarray(valid)


def _select_experts(s1, s2, flat, valid, break_ties):
    t = s1.shape[1]
    rank1, v1, count1 = _top16_rows(s1, break_ties, with_rank=break_ties)
    rank2, v2, count2 = _top16_rows(s2, break_ties)
    v2_lo = jnp.concatenate(v2[:SUBLANES], axis=0)
    v2_all = jnp.concatenate(v2, axis=0)
    v1_hi = jnp.concatenate(v1[_CAND_SMALL_A:], axis=0)
    groups = [v1[0] + v2_all]
    for a in range(1, _CAND_SMALL_A):
        groups.append(v1[a] + v2_lo)
    groups.append(v1_hi + v2[0])
    cand = jnp.concatenate(groups, axis=0)
    cand = jnp.where(valid > 0.5, cand, -jnp.inf)
    flat_b = jnp.broadcast_to(flat, cand.shape) if break_ties else None
    all_cand = cand
    hits = jnp.zeros(cand.shape, F32)
    top = []
    for r in range(PEER_TOPK):
        m = jnp.max(cand, axis=0, keepdims=True)
        hit = cand == m
        if break_ties:
            pick = jnp.min(jnp.where(hit, flat_b, 2e9), axis=0, keepdims=True)
            hit = flat_b == pick
            hits = jnp.where(hit, 1.0, hits)
        cand = jnp.where(hit, -jnp.inf, cand)
        top.append(m)
    if not break_ties:
        hits = jnp.where(all_cand >= top[PEER_TOPK - 1], 1.0, 0.0)
    z = jnp.ones((1, t), F32)
    for r in range(1, PEER_TOPK):
        z = z + jnp.exp(top[r] - top[0])
    counts = [jnp.sum(hits[0:PEER_TOPK], axis=0, keepdims=True)]
    for a in range(1, _CAND_SMALL_A):
        base = PEER_TOPK + (a - 1) * SUBLANES
        counts.append(jnp.sum(hits[base:base + SUBLANES], axis=0, keepdims=True))
    base = PEER_TOPK + (_CAND_SMALL_A - 1) * SUBLANES
    for k in range(SUBLANES):
        counts.append(hits[base + k:base + k + 1])
    lim = jnp.zeros(s1.shape, F32)
    if rank1 is not None:
        for a in range(PEER_TOPK):
            lim = jnp.where(rank1 == float(a), counts[a], lim)
    else:
        for a in range(PEER_TOPK):
            step = counts[a] - counts[a + 1] if a + 1 < PEER_TOPK else counts[a]
            lim = lim + jnp.where(s1 >= v1[a], step, 0.0)
    e1n = jnp.exp(s1 - v1[0]) * (1.0 / z)
    e2 = jnp.exp(s2 - v2[0])
    count3 = jnp.sum(hits, axis=0, keepdims=True)
    k = float(PEER_TOPK)
    ok = jnp.where((count1 == k) & (count2 == k) & (count3 == k), 1.0, 0.0)
    return lim, e1n, rank2, e2, ok


def _retrieve_kernel(wq_ref, x_ref, k1_ref, k2_ref, flat_ref, valid_ref,
                     lim_out_ref, e1_out_ref, r2_ref, e2_ref, q_ref, lim_ref, e1_ref, *, tq):
    q_ref[...] = jnp.dot(wq_ref[...], x_ref[...], preferred_element_type=F32)
    flat, valid = flat_ref[...], valid_ref[...]

    def head(h, carry):
        for c in range(tq // SELECT_TOKENS):
            lanes = slice(c * SELECT_TOKENS, (c + 1) * SELECT_TOKENS)
            r0 = pl.multiple_of(h * 2 * HALF_DIM, 2 * HALF_DIM)
            q1 = q_ref[pl.ds(r0, HALF_DIM), lanes].astype(BF16)
            q2 = q_ref[pl.ds(r0 + HALF_DIM, HALF_DIM), lanes].astype(BF16)
            s1 = jnp.dot(k1_ref[...], q1, preferred_element_type=F32)
            s2 = jnp.dot(k2_ref[...], q2, preferred_element_type=F32)

            def emit(break_ties):
                lim, e1n, rank2, e2, ok = _select_experts(s1, s2, flat, valid, break_ties)
                lim_ref[h, :, lanes] = lim
                e1_ref[h, :, lanes] = e1n
                r2_ref[h, :, lanes] = rank2.astype(BF16)
                e2_ref[h, :, lanes] = e2.astype(BF16)
                return ok

            ok = emit(False)

            @pl.when(jnp.min(ok) < 0.5)
            def _():
                emit(True)
        return carry

    lax.fori_loop(0, PEER_HEADS, head, 0)
    lim_out_ref[...] = jnp.swapaxes(lim_ref[...], 0, 1)
    e1_out_ref[...] = jnp.swapaxes(e1_ref[...], 0, 1)


def _retrieve(x1t, wq_t, k1, k2, tq):
    d, t = x1t.shape
    tq = min(tq, t)
    flat, valid = _cand_constants()
    full = lambda a: pl.BlockSpec(a.shape, lambda i: (0,) * a.ndim)
    out_keys = jax.ShapeDtypeStruct((N_KEYS, PEER_HEADS, t), F32)
    key_spec = pl.BlockSpec((N_KEYS, PEER_HEADS, tq), lambda i: (0, 0, i))
    out_packed = jax.ShapeDtypeStruct((PEER_HEADS, N_KEYS, t), BF16)
    out_spec = pl.BlockSpec((PEER_HEADS, N_KEYS, tq), lambda i: (0, 0, i))
    return pl.pallas_call(
        functools.partial(_retrieve_kernel, tq=tq),
        grid=(t // tq,),
        in_specs=[full(wq_t), pl.BlockSpec((d, tq), lambda i: (0, i)), full(k1), full(k2), full(flat), full(valid)],
        out_specs=[key_spec, key_spec, out_spec, out_spec],
        out_shape=[out_keys, out_keys, out_packed, out_packed],
        scratch_shapes=[pltpu.VMEM((wq_t.shape[0], tq), F32)] + [pltpu.VMEM((PEER_HEADS, N_KEYS, tq), F32)] * 2,
        compiler_params=_params("parallel"),
        name="peer_retrieve",
    )(wq_t, x1t, k1, k2, flat, valid)


def _transpose_cast_kernel(v_ref, o_ref):
    o_ref[...] = v_ref[...].T.astype(o_ref.dtype)


def _transpose_cast(v, rows):
    n, d = v.shape
    rows = min(rows, n)
    return pl.pallas_call(
        _transpose_cast_kernel,
        grid=(n // rows,),
        in_specs=[pl.BlockSpec((rows, d), lambda i: (i, 0))],
        out_specs=pl.BlockSpec((d, rows), lambda i: (0, i)),
        out_shape=jax.ShapeDtypeStruct((d, n), BF16),
        compiler_params=_params("parallel"),
        name="transpose_cast",
    )(v)


def _gelu_exact(x):
    return 0.5 * x * (1.0 + lax.erf(x * (1.0 / math.sqrt(2.0))))


def _experts_kernel(xb_ref, u_ref, vt_ref, lim_ref, e1_ref, r2_in_ref, e2_in_ref, o_ref,
                    h_ref, a_ref, rows_ref, gate_ref, r2_ref, e2_ref, *, tm, te):
    e = pl.program_id(1)

    @pl.when(e == 0)
    def _():
        o_ref[...] = jnp.zeros(o_ref.shape, F32)
        r2_ref[:, 2 * LANES:2 * LANES + tm] = r2_in_ref[...]
        e2_ref[:, 3 * LANES:3 * LANES + tm] = e2_in_ref[...]

    n_chunks = te // EXPERT_CHUNK

    def rows_of(p):
        return slice(p * EXPERT_CHUNK, (p + 1) * EXPERT_CHUNK)

    def gates(first_key):
        keys = range(first_key, first_key + GATE_KEYS)
        blocks = range(0, N_KEYS, GATE_ROWS)
        for i in keys:
            for hd in range(PEER_HEADS):
                for q, ref in enumerate((lim_ref, e1_ref)):
                    row = jnp.broadcast_to(ref[i, hd:hd + 1, :], (GATE_ROWS, tm)).astype(BF16)
                    rows_ref[q, i, hd, :, q * LANES:q * LANES + tm] = row
        for c in range(tm // LANES):
            lanes = slice(c * LANES, (c + 1) * LANES)
            lanes1 = slice((c + 1) * LANES, (c + 2) * LANES)
            gate = {i: {jb: jnp.zeros((GATE_ROWS, LANES), BF16) for jb in blocks} for i in keys}
            for hd in range(PEER_HEADS):
                lim = {i: rows_ref[0, i, hd, :, lanes] for i in keys}
                e1 = {i: rows_ref[1, i, hd, :, lanes1] for i in keys}
                for jb in blocks:
                    r2 = r2_ref[hd * N_KEYS + jb:hd * N_KEYS + jb + GATE_ROWS, (c + 2) * LANES:(c + 3) * LANES]
                    e2 = e2_ref[hd * N_KEYS + jb:hd * N_KEYS + jb + GATE_ROWS, (c + 3) * LANES:(c + 4) * LANES]
                    for i in keys:
                        gate[i][jb] = gate[i][jb] + jnp.where(r2 < lim[i], e2 * e1[i], jnp.zeros_like(e2))
            for i in keys:
                for jb in blocks:
                    gate_ref[i * N_KEYS + jb:i * N_KEYS + jb + GATE_ROWS, lanes] = gate[i][jb]

    for first_key in range(0, te // N_KEYS, GATE_KEYS):
        gates(first_key)
    h_ref[...] = jnp.dot(u_ref[...], xb_ref[...], preferred_element_type=F32)
    for p in range(n_chunks):
        a_ref[rows_of(p), :] = gate_ref[rows_of(p), :] * _gelu_exact(h_ref[rows_of(p), :]).astype(BF16)
        o_ref[...] += jnp.dot(vt_ref[:, rows_of(p)], a_ref[rows_of(p), :], preferred_element_type=F32)


def _experts(xbt, u, vt, sel, tm, te):
    d, t = xbt.shape
    n_exp = u.shape[0]
    tm, te = min(tm, t), min(te, n_exp)
    tok = pl.BlockSpec((d, tm), lambda i, j: (0, i))
    sel_spec = pl.BlockSpec((PEER_HEADS * N_KEYS, tm), lambda i, j: (0, i))
    key_spec = pl.BlockSpec((te // N_KEYS, PEER_HEADS, tm), lambda i, j: (j, 0, i))
    lim, e1n, rank2, e2 = sel
    sel = (lim, e1n, rank2.reshape(-1, t), e2.reshape(-1, t))
    return pl.pallas_call(
        functools.partial(_experts_kernel, tm=tm, te=te),
        grid=(t // tm, n_exp // te),
        in_specs=[tok, pl.BlockSpec((te, d), lambda i, j: (j, 0)), pl.BlockSpec((d, te), lambda i, j: (0, j)),
                  key_spec, key_spec, sel_spec, sel_spec],
        out_specs=tok,
        out_shape=jax.ShapeDtypeStruct((d, t), F32),
        scratch_shapes=[pltpu.VMEM((te, tm), F32), pltpu.VMEM((te, tm), BF16),
                        pltpu.VMEM((2, te // N_KEYS, PEER_HEADS, GATE_ROWS, tm + LANES), BF16),
                        pltpu.VMEM((te, tm), BF16),
                        pltpu.VMEM((PEER_HEADS * N_KEYS, tm + 3 * LANES), BF16),
                        pltpu.VMEM((PEER_HEADS * N_KEYS, tm + 3 * LANES), BF16)],
        compiler_params=_params("parallel", "arbitrary"),
        name="peer_experts",
    )(xbt, u, vt, *sel)


def _ln_t_kernel(x_ref, f_ref, g_ref, b_ref, o_ref, *, alpha):
    z = (alpha * x_ref[...] + f_ref[...]).T
    mu = jnp.mean(z, axis=1, keepdims=True)
    zc = z - mu
    var = jnp.mean(zc * zc, axis=1, keepdims=True)
    o_ref[...] = zc * lax.rsqrt(var + LN_EPS) * g_ref[...] + b_ref[...]


def _residual_ln_t(x1t, fft, g, b, alpha, tm):
    d, t = x1t.shape
    tm = min(tm, t)
    tok = pl.BlockSpec((d, tm), lambda i: (0, i))
    vec = pl.BlockSpec((1, d), lambda i: (0, 0))
    return pl.pallas_call(
        functools.partial(_ln_t_kernel, alpha=alpha),
        grid=(t // tm,),
        in_specs=[tok, tok, vec, vec],
        out_specs=pl.BlockSpec((tm, d), lambda i: (i, 0)),
        out_shape=jax.ShapeDtypeStruct((t, d), F32),
        compiler_params=_params("parallel"),
        name="residual_ln",
    )(x1t, fft, g, b)


def kernel(x, mem, positions, w_in, w_mem_kv, w_pool, pool_scale, attn_sinks, w_out, ln1_g, ln1_b,
           w_peer_q, sub_keys_1, sub_keys_2, expert_u, expert_v, ln2_g, ln2_b):
    bsz, seq, d = x.shape
    depth = w_in.shape[0]
    t = bsz * seq
    alpha = (2.0 * depth) ** 0.25
    for l in range(depth):
        x2 = x.reshape(t, d)
        h = _inproj(x2, w_in[l].astype(BF16), positions, 512).reshape(bsz, seq, -1)
        mem2 = mem.reshape(-1, d).astype(BF16)
        kvm = _matmul(mem2, w_mem_kv[l].astype(BF16), BF16, 512, 512).reshape(bsz, mem.shape[1], -1)
        pool_o = _pool(h, w_pool[l].astype(BF16), pool_scale[l].reshape(1, -1), 512)
        swa_o = _swa(h, attn_sinks[l])
        mem_o = _mem_attention(h, kvm, 512)
        x1t, x1bt = _outproj_ln(pool_o.reshape(t, -1), swa_o.reshape(t, -1), mem_o.reshape(t, -1),
                          w_out[l].astype(BF16), x2, ln1_g[l].reshape(1, d), ln1_b[l].reshape(1, d), alpha, 512)
        sel = _retrieve(x1bt, _transpose_cast(w_peer_q[l], 512), sub_keys_1[l].astype(BF16),
                        sub_keys_2[l].astype(BF16), 512)
        fft = _experts(x1bt, expert_u[l].astype(BF16), _transpose_cast(expert_v[l], 512), sel, 1024, 512)
        x = _residual_ln_t(x1t, fft, ln2_g[l].reshape(1, d), ln2_b[l].reshape(1, d), alpha, 512).reshape(bsz, seq, d)
    return x
```

```python
import functools
import math

import numpy as np
import jax
import jax.numpy as jnp
from jax import lax
from jax.experimental import pallas as pl
from jax.experimental.pallas import tpu as pltpu

F32 = jnp.float32
BF16 = jnp.bfloat16

LANES = 128
SUBLANES = 8
VMEM_LIMIT_BYTES = 56 * 1024 * 1024

POOL_WINDOWS = (2, 4, 8, 16)
POOL_GROUP = 128
POOL_HALO = 16
SWA_HEAD_DIM = 64
SWA_HEADS = 16
SWA_KV_HEADS = 4
SWA_BLOCK = 128
ROPE_THETA = 500000.0
ROPE_DIM = 16
MEM_HEADS = 4
MEM_HEAD_DIM = 128
PEER_HEADS = 8
N_KEYS = 128
PEER_TOPK = 16
HALF_DIM = 128
LN_EPS = 1e-5
NEG = -1e30

EXPERT_CHUNK = 512
GATE_KEYS = 4
GATE_ROWS = 16
SELECT_TOKENS = 512


def _params(*semantics):
    return pltpu.CompilerParams(dimension_semantics=semantics, vmem_limit_bytes=VMEM_LIMIT_BYTES)


def _matmul_kernel(a_ref, b_ref, o_ref):
    o_ref[...] = jnp.dot(a_ref[...], b_ref[...], preferred_element_type=F32).astype(o_ref.dtype)


def _matmul(a, b, out_dtype, tm, tn):
    m, k = a.shape
    n = b.shape[1]
    tm, tn = min(tm, m), min(tn, n)
    return pl.pallas_call(
        _matmul_kernel,
        grid=(m // tm, n // tn),
        in_specs=[pl.BlockSpec((tm, k), lambda i, j: (i, 0)), pl.BlockSpec((k, tn), lambda i, j: (0, j))],
        out_specs=pl.BlockSpec((tm, tn), lambda i, j: (i, j)),
        out_shape=jax.ShapeDtypeStruct((m, n), out_dtype),
        compiler_params=_params("parallel", "arbitrary"),
        name="matmul",
    )(a, b)


IN_BLOCK = 512
_Q_BLOCKS = (1, 2)
_KV_BLOCK = 3


def _rope(x, c, sa, sb):
    half = ROPE_DIM // 2
    return x * c + pltpu.roll(x, LANES - half, 1) * sa + pltpu.roll(x, half, 1) * sb


def _inproj_kernel(x_ref, w_ref, pos_ref, freq_ref, sa_ref, sb_ref, o_ref):
    xb = x_ref[...].astype(BF16)
    ang = pos_ref[...].astype(F32) * freq_ref[...]
    s = jnp.sin(ang)
    c, sa, sb = jnp.cos(ang), s * sa_ref[...], s * sb_ref[...]
    for j in range(w_ref.shape[1] // IN_BLOCK):
        y = jnp.dot(xb, w_ref[:, j * IN_BLOCK:(j + 1) * IN_BLOCK], preferred_element_type=F32)
        for k in range(IN_BLOCK // LANES):
            piece = y[:, k * LANES:(k + 1) * LANES]
            if j in _Q_BLOCKS:
                piece = _rope(piece, c, sa, sb) * SWA_HEAD_DIM ** -0.5
            elif j == _KV_BLOCK and k < IN_BLOCK // LANES // 2:
                piece = _rope(piece, c, sa, sb)
            o_ref[:, j * IN_BLOCK + k * LANES:j * IN_BLOCK + (k + 1) * LANES] = piece.astype(o_ref.dtype)


def _rope_constants():
    lane = np.arange(LANES)
    d = lane % SWA_HEAD_DIM
    half = ROPE_DIM // 2
    inv_freq = np.float32(ROPE_THETA) ** (-np.arange(0, ROPE_DIM, 2, dtype=np.float32) / np.float32(ROPE_DIM))
    freq = np.where(d < ROPE_DIM, inv_freq[d % half], 0.0).astype(np.float32)
    sa = np.where(d < half, -1.0, 0.0).astype(np.float32)
    sb = np.where((d >= half) & (d < ROPE_DIM), 1.0, 0.0).astype(np.float32)
    return [jnp.asarray(a.reshape(1, LANES)) for a in (freq, sa, sb)]


def _inproj(x2, w, positions, tm):
    t, d = x2.shape
    n = w.shape[1]
    tm = min(tm, t)
    freq, sa, sb = _rope_constants()
    const = pl.BlockSpec((1, LANES), lambda i: (0, 0))
    return pl.pallas_call(
        _inproj_kernel,
        grid=(t // tm,),
        in_specs=[pl.BlockSpec((tm, d), lambda i: (i, 0)), pl.BlockSpec((d, n), lambda i: (0, 0)),
                  pl.BlockSpec((tm, 1), lambda i: (i, 0)), const, const, const],
        out_specs=pl.BlockSpec((tm, n), lambda i: (i, 0)),
        out_shape=jax.ShapeDtypeStruct((t, n), BF16),
        compiler_params=_params("parallel"),
        name="inproj_rope",
    )(x2, w, positions.reshape(t, 1), freq, sa, sb)


def _pool_kernel(v_ref, w_ref, scale_ref, o_ref, ext_ref, *, ts):
    s = pl.program_id(1)

    @pl.when(s == 0)
    def _():
        ext_ref[0:POOL_HALO, :] = jnp.zeros((POOL_HALO, ext_ref.shape[1]), F32)

    ext_ref[POOL_HALO:POOL_HALO + ts, :] = v_ref[0].astype(F32)
    pos = s * ts + lax.broadcasted_iota(jnp.int32, (ts, 1), 0)
    for g, w in enumerate(POOL_WINDOWS):
        cols = slice(g * POOL_GROUP, (g + 1) * POOL_GROUP)
        acc = ext_ref[POOL_HALO:POOL_HALO + ts, cols]
        for k in range(1, w):
            acc = acc + ext_ref[POOL_HALO - k:POOL_HALO - k + ts, cols]
        count = jnp.minimum(pos + 1, w).astype(F32)
        pooled = acc / count - ext_ref[POOL_HALO:POOL_HALO + ts, cols]
        y = jnp.dot(pooled.astype(BF16), w_ref[g], preferred_element_type=F32)
        o_ref[0, :, cols] = (y * scale_ref[:, cols]).astype(o_ref.dtype)
    ext_ref[0:POOL_HALO, :] = ext_ref[ts:ts + POOL_HALO, :]


def _pool(h, w_pool, pool_scale, ts):
    b, s, _ = h.shape
    width = POOL_GROUP * len(POOL_WINDOWS)
    ts = min(ts, s)
    return pl.pallas_call(
        functools.partial(_pool_kernel, ts=ts),
        grid=(b, s // ts),
        in_specs=[
            pl.BlockSpec((1, ts, width), lambda i, j: (i, j, 0)),
            pl.BlockSpec(w_pool.shape, lambda i, j: (0, 0, 0)),
            pl.BlockSpec((1, width), lambda i, j: (0, 0)),
        ],
        out_specs=pl.BlockSpec((1, ts, width), lambda i, j: (i, j, 0)),
        out_shape=jax.ShapeDtypeStruct((b, s, width), BF16),
        scratch_shapes=[pltpu.VMEM((ts + POOL_HALO, width), F32)],
        compiler_params=_params("arbitrary", "arbitrary"),
        name="pool",
    )(h, w_pool, pool_scale)


def _swa_kernel(sink_ref, q0_ref, q1_ref, kv_ref, kvp_ref, o_ref):
    n = pl.program_id(1)
    kvw = SWA_KV_HEADS * SWA_HEAD_DIM
    q = jnp.concatenate([q0_ref[0], q1_ref[0]], axis=1)
    k = jnp.concatenate([kvp_ref[0, :, 0:kvw], kv_ref[0, :, 0:kvw]], axis=0)
    v = jnp.concatenate([kvp_ref[0, :, kvw:2 * kvw], kv_ref[0, :, kvw:2 * kvw]], axis=0)
    row = lax.broadcasted_iota(jnp.int32, (SWA_BLOCK, 2 * SWA_BLOCK), 0)
    col = lax.broadcasted_iota(jnp.int32, (SWA_BLOCK, 2 * SWA_BLOCK), 1)
    rel = row + SWA_BLOCK - col
    valid = (rel >= 0) & (rel < SWA_BLOCK) & ((col >= SWA_BLOCK) | (n > 0))
    group = SWA_HEADS // SWA_KV_HEADS
    outs = []
    for hq in range(SWA_HEADS):
        kv = hq // group
        qh = q[:, hq * SWA_HEAD_DIM:(hq + 1) * SWA_HEAD_DIM]
        kh = k[:, kv * SWA_HEAD_DIM:(kv + 1) * SWA_HEAD_DIM]
        vh = v[:, kv * SWA_HEAD_DIM:(kv + 1) * SWA_HEAD_DIM]
        sc = lax.dot_general(qh, kh, (((1,), (1,)), ((), ())), preferred_element_type=F32)
        sc = jnp.where(valid, sc, NEG)
        sink = sink_ref[hq]
        m = jnp.maximum(jnp.max(sc, axis=1, keepdims=True), sink)
        p = jnp.exp(sc - m)
        denom = jnp.sum(p, axis=1, keepdims=True) + jnp.exp(sink - m)
        o = jnp.dot(p.astype(BF16), vh, preferred_element_type=F32)
        outs.append(o / denom)
    o_ref[0] = jnp.concatenate(outs, axis=1).astype(o_ref.dtype)


def _swa(h, sinks):
    b, s, _ = h.shape
    nb = s // SWA_BLOCK
    blk = lambda c: pl.BlockSpec((1, SWA_BLOCK, IN_BLOCK), lambda i, j: (i, j, c))
    blk_prev = lambda c: pl.BlockSpec((1, SWA_BLOCK, IN_BLOCK), lambda i, j: (i, jnp.maximum(j - 1, 0), c))
    return pl.pallas_call(
        _swa_kernel,
        grid=(b, nb),
        in_specs=[pl.BlockSpec(memory_space=pltpu.SMEM),
                  blk(_Q_BLOCKS[0]), blk(_Q_BLOCKS[1]), blk(_KV_BLOCK), blk_prev(_KV_BLOCK)],
        out_specs=pl.BlockSpec((1, SWA_BLOCK, SWA_HEADS * SWA_HEAD_DIM), lambda i, j: (i, j, 0)),
        out_shape=jax.ShapeDtypeStruct((b, s, SWA_HEADS * SWA_HEAD_DIM), BF16),
        compiler_params=_params("parallel", "arbitrary"),
        name="swa",
    )(sinks, h, h, h, h)


def _mem_kernel(q_ref, kv_ref, o_ref):
    scale = MEM_HEAD_DIM ** -0.5
    width = MEM_HEADS * MEM_HEAD_DIM
    for hm in range(MEM_HEADS):
        cols = slice(hm * MEM_HEAD_DIM, (hm + 1) * MEM_HEAD_DIM)
        km = kv_ref[0, :, cols]
        vm = kv_ref[0, :, width + hm * MEM_HEAD_DIM:width + (hm + 1) * MEM_HEAD_DIM]
        sc = lax.dot_general(q_ref[0, :, cols], km, (((1,), (1,)), ((), ())), preferred_element_type=F32) * scale
        m = jnp.max(sc, axis=1, keepdims=True)
        p = jnp.exp(sc - m)
        denom = jnp.sum(p, axis=1, keepdims=True)
        o = jnp.dot(p.astype(BF16), vm, preferred_element_type=F32)
        o_ref[0, :, cols] = (o / denom).astype(o_ref.dtype)


def _mem_attention(h, kvm, tq):
    b, s, _ = h.shape
    width = MEM_HEADS * MEM_HEAD_DIM
    tq = min(tq, s)
    return pl.pallas_call(
        _mem_kernel,
        grid=(b, s // tq),
        in_specs=[
            pl.BlockSpec((1, tq, width), lambda i, j: (i, j, 4)),
            pl.BlockSpec((1,) + kvm.shape[1:], lambda i, j: (i, 0, 0)),
        ],
        out_specs=pl.BlockSpec((1, tq, width), lambda i, j: (i, j, 0)),
        out_shape=jax.ShapeDtypeStruct((b, s, width), BF16),
        compiler_params=_params("parallel", "arbitrary"),
        name="mem_attention",
    )(h, kvm)


def _outproj_kernel(pool_ref, swa_ref, mem_ref, wp_ref, ws_ref, wm_ref, x_ref, g_ref, b_ref, o_ref, ob_ref,
                    *, alpha):
    mix = jnp.dot(pool_ref[...], wp_ref[...], preferred_element_type=F32)
    mix += jnp.dot(swa_ref[...], ws_ref[...], preferred_element_type=F32)
    mix += jnp.dot(mem_ref[...], wm_ref[...], preferred_element_type=F32)
    z = alpha * x_ref[...] + mix
    mu = jnp.mean(z, axis=1, keepdims=True)
    zc = z - mu
    var = jnp.mean(zc * zc, axis=1, keepdims=True)
    y = zc * lax.rsqrt(var + LN_EPS) * g_ref[...] + b_ref[...]
    yt = y.T
    o_ref[...] = yt
    ob_ref[...] = yt.astype(BF16)


def _outproj_ln(pool_o, swa_o, mem_o, w_out, x2, g, b, alpha, tm):
    t, d = x2.shape
    tm = min(tm, t)
    wp, ws, wm = pool_o.shape[1], swa_o.shape[1], mem_o.shape[1]
    w_p, w_s, w_m = w_out[:wp], w_out[wp:wp + ws], w_out[wp + ws:]
    row = lambda w: pl.BlockSpec((tm, w), lambda i: (i, 0))
    full = lambda a: pl.BlockSpec(a.shape, lambda i: (0, 0))
    return pl.pallas_call(
        functools.partial(_outproj_kernel, alpha=alpha),
        grid=(t // tm,),
        in_specs=[row(wp), row(ws), row(wm), full(w_p), full(w_s), full(w_m), row(d),
                  pl.BlockSpec((1, d), lambda i: (0, 0)), pl.BlockSpec((1, d), lambda i: (0, 0))],
        out_specs=[pl.BlockSpec((d, tm), lambda i: (0, i))] * 2,
        out_shape=[jax.ShapeDtypeStruct((d, t), F32), jax.ShapeDtypeStruct((d, t), BF16)],
        compiler_params=_params("parallel"),
        name="outproj_ln",
    )(pool_o, swa_o, mem_o, w_p, w_s, w_m, x2, g, b)


def _merge_exchange_pairs(n):
    t = (n - 1).bit_length()
    pairs = []
    p = 1 << (t - 1)
    while p >= 1:
        q, r, d = 1 << (t - 1), 0, p
        while True:
            pairs += [(i, i + d) for i in range(n - d) if (i & p) == r]
            if q == p:
                break
            d, q, r = q - p, q // 2, p
        p //= 2
    return pairs


def _top16_rows_distinct(s, with_rank):
    k = PEER_TOPK
    lists = [s[SUBLANES * g:SUBLANES * (g + 1), :] for g in range(s.shape[0] // SUBLANES)]
    for i, j in _merge_exchange_pairs(len(lists)):
        lists[i], lists[j] = jnp.maximum(lists[i], lists[j]), jnp.minimum(lists[i], lists[j])
    vals = []
    for r in range(k):
        m = jnp.max(lists[0], axis=0, keepdims=True)
        vals.append(m)
        pop = lists[0] == m
        for d in range(k - 1 - r):
            lists[d] = jnp.where(pop, lists[d + 1], lists[d])
    def threshold(bits, lo, hi):
        if not bits:
            return vals[(lo + hi) // 2 - 1]
        mid = (lo + hi) // 2
        return jnp.where(bits[0], threshold(bits[1:], mid, hi), threshold(bits[1:], lo, mid))

    rank = None
    if with_rank:
        bits = []
        for _ in range(k.bit_length() - 1):
            bits.append(threshold(bits, 0, k) > s)
        rank = sum(jnp.where(b, float(k >> (n + 1)), 0.0) for n, b in enumerate(bits))
        rank = jnp.where(vals[k - 1] > s, float(k), rank)
    count = jnp.sum(jnp.where(s >= vals[k - 1], 1.0, 0.0), axis=0, keepdims=True)
    strict = jnp.zeros_like(count)
    for r in range(k - 1):
        strict = strict + jnp.where(vals[r] > vals[r + 1], 1.0, 0.0)
    distinct = (count == float(k)) & (strict == float(k - 1))
    return rank, vals, jnp.where(distinct, float(k), 0.0)


def _top16_rows(s, break_ties, with_rank=True):
    if not break_ties:
        return _top16_rows_distinct(s, with_rank)
    n = s.shape[0]
    iota = lax.broadcasted_iota(jnp.int32, s.shape, 0).astype(F32)
    rank = jnp.full(s.shape, float(PEER_TOPK), F32)
    vals = []
    for r in range(PEER_TOPK):
        m = jnp.max(s, axis=0, keepdims=True)
        idx = jnp.min(jnp.where(s == m, iota, float(n)), axis=0, keepdims=True)
        hit = iota == idx
        rank = jnp.where(hit, float(r), rank)
        s = jnp.where(hit, -jnp.inf, s)
        vals.append(m)
    count = jnp.sum(jnp.where(rank < float(PEER_TOPK), 1.0, 0.0), axis=0, keepdims=True)
    return rank, vals, count


_CAND_SMALL_A = PEER_TOPK // 2
_CAND_ROWS = PEER_TOPK + (_CAND_SMALL_A - 1) * SUBLANES + SUBLANES


def _cand_constants():
    flat = np.full((_CAND_ROWS, 1), 1e9, np.float32)
    valid = np.zeros((_CAND_ROWS, 1), np.float32)
    for b in range(PEER_TOPK):
        flat[b, 0], valid[b, 0] = b, 1.0
    for a in range(1, _CAND_SMALL_A):
        base = PEER_TOPK + (a - 1) * SUBLANES
        for b in range(PEER_TOPK // (a + 1)):
            flat[base + b, 0], valid[base + b, 0] = a * PEER_TOPK + b, 1.0
    base = PEER_TOPK + (_CAND_SMALL_A - 1) * SUBLANES
    for k in range(SUBLANES):
        flat[base + k, 0], valid[base + k, 0] = (_CAND_SMALL_A + k) * PEER_TOPK, 1.0
    return jnp.asarray(flat), jnp.asarray(valid)


def _select_experts(s1, s2, flat, valid, break_ties):
    t = s1.shape[1]
    rank1, v1, count1 = _top16_rows(s1, break_ties, with_rank=break_ties)
    rank2, v2, count2 = _top16_rows(s2, break_ties)
    v2_lo = jnp.concatenate(v2[:SUBLANES], axis=0)
    v2_all = jnp.concatenate(v2, axis=0)
    v1_hi = jnp.concatenate(v1[_CAND_SMALL_A:], axis=0)
    groups = [v1[0] + v2_all]
    for a in range(1, _CAND_SMALL_A):
        groups.append(v1[a] + v2_lo)
    groups.append(v1_hi + v2[0])
    cand = jnp.concatenate(groups, axis=0)
    cand = jnp.where(valid > 0.5, cand, -jnp.inf)
    flat_b = jnp.broadcast_to(flat, cand.shape) if break_ties else None
    all_cand = cand
    hits = jnp.zeros(cand.shape, F32)
    top = []
    for r in range(PEER_TOPK):
        m = jnp.max(cand, axis=0, keepdims=True)
        hit = cand == m
        if break_ties:
            pick = jnp.min(jnp.where(hit, flat_b, 2e9), axis=0, keepdims=True)
            hit = flat_b == pick
            hits = jnp.where(hit, 1.0, hits)
        cand = jnp.where(hit, -jnp.inf, cand)
        top.append(m)
    if not break_ties:
        hits = jnp.where(all_cand >= top[PEER_TOPK - 1], 1.0, 0.0)
    z = jnp.ones((1, t), F32)
    for r in range(1, PEER_TOPK):
        z = z + jnp.exp(top[r] - top[0])
    counts = [jnp.sum(hits[0:PEER_TOPK], axis=0, keepdims=True)]
    for a in range(1, _CAND_SMALL_A):
        base = PEER_TOPK + (a - 1) * SUBLANES
        counts.append(jnp.sum(hits[base:base + SUBLANES], axis=0, keepdims=True))
    base = PEER_TOPK + (_CAND_SMALL_A - 1) * SUBLANES
    for k in range(SUBLANES):
        counts.append(hits[base + k:base + k + 1])
    lim = jnp.zeros(s1.shape, F32)
    if rank1 is not None:
        for a in range(PEER_TOPK):
            lim = jnp.where(rank1 == float(a), counts[a], lim)
    else:
        for a in range(PEER_TOPK):
            step = counts[a] - counts[a + 1] if a + 1 < PEER_TOPK else counts[a]
            lim = lim + jnp.where(s1 >= v1[a], step, 0.0)
    e1n = jnp.exp(s1 - v1[0]) * (1.0 / z)
    e2 = jnp.exp(s2 - v2[0])
    count3 = jnp.sum(hits, axis=0, keepdims=True)
    k = float(PEER_TOPK)
    ok = jnp.where((count1 == k) & (count2 == k) & (count3 == k), 1.0, 0.0)
    return lim, e1n, rank2, e2, ok


def _retrieve_kernel(wq_ref, x_ref, k1_ref, k2_ref, flat_ref, valid_ref,
                     lim_out_ref, e1_out_ref, r2_ref, e2_ref, q_ref, lim_ref, e1_ref, *, tq):
    q_ref[...] = jnp.dot(wq_ref[...], x_ref[...], preferred_element_type=F32)
    flat, valid = flat_ref[...], valid_ref[...]

    def head(h, carry):
        for c in range(tq // SELECT_TOKENS):
            lanes = slice(c * SELECT_TOKENS, (c + 1) * SELECT_TOKENS)
            r0 = pl.multiple_of(h * 2 * HALF_DIM, 2 * HALF_DIM)
            q1 = q_ref[pl.ds(r0, HALF_DIM), lanes].astype(BF16)
            q2 = q_ref[pl.ds(r0 + HALF_DIM, HALF_DIM), lanes].astype(BF16)
            s1 = jnp.dot(k1_ref[...], q1, preferred_element_type=F32)
            s2 = jnp.dot(k2_ref[...], q2, preferred_element_type=F32)

            def emit(break_ties):
                lim, e1n, rank2, e2, ok = _select_experts(s1, s2, flat, valid, break_ties)
                lim_ref[h, :, lanes] = lim
                e1_ref[h, :, lanes] = e1n
                r2_ref[h, :, lanes] = rank2.astype(BF16)
                e2_ref[h, :, lanes] = e2.astype(BF16)
                return ok

            ok = emit(False)

            @pl.when(jnp.min(ok) < 0.5)
            def _():
                emit(True)
        return carry

    lax.fori_loop(0, PEER_HEADS, head, 0)
    lim_out_ref[...] = jnp.swapaxes(lim_ref[...], 0, 1)
    e1_out_ref[...] = jnp.swapaxes(e1_ref[...], 0, 1)


def _retrieve(x1t, wq_t, k1, k2, tq):
    d, t = x1t.shape
    tq = min(tq, t)
    flat, valid = _cand_constants()
    full = lambda a: pl.BlockSpec(a.shape, lambda i: (0,) * a.ndim)
    out_keys = jax.ShapeDtypeStruct((N_KEYS, PEER_HEADS, t), F32)
    key_spec = pl.BlockSpec((N_KEYS, PEER_HEADS, tq), lambda i: (0, 0, i))
    out_packed = jax.ShapeDtypeStruct((PEER_HEADS, N_KEYS, t), BF16)
    out_spec = pl.BlockSpec((PEER_HEADS, N_KEYS, tq), lambda i: (0, 0, i))
    return pl.pallas_call(
        functools.partial(_retrieve_kernel, tq=tq),
        grid=(t // tq,),
        in_specs=[full(wq_t), pl.BlockSpec((d, tq), lambda i: (0, i)), full(k1), full(k2), full(flat), full(valid)],
        out_specs=[key_spec, key_spec, out_spec, out_spec],
        out_shape=[out_keys, out_keys, out_packed, out_packed],
        scratch_shapes=[pltpu.VMEM((wq_t.shape[0], tq), F32)] + [pltpu.VMEM((PEER_HEADS, N_KEYS, tq), F32)] * 2,
        compiler_params=_params("parallel"),
        name="peer_retrieve",
    )(wq_t, x1t, k1, k2, flat, valid)


def _transpose_cast_kernel(v_ref, o_ref):
    o_ref[...] = v_ref[...].T.astype(o_ref.dtype)


def _transpose_cast(v, rows):
    n, d = v.shape
    rows = min(rows, n)
    return pl.pallas_call(
        _transpose_cast_kernel,
        grid=(n // rows,),
        in_specs=[pl.BlockSpec((rows, d), lambda i: (i, 0))],
        out_specs=pl.BlockSpec((d, rows), lambda i: (0, i)),
        out_shape=jax.ShapeDtypeStruct((d, n), BF16),
        compiler_params=_params("parallel"),
        name="transpose_cast",
    )(v)


def _gelu_exact(x):
    return 0.5 * x * (1.0 + lax.erf(x * (1.0 / math.sqrt(2.0))))


def _experts_kernel(xb_ref, u_ref, vt_ref, lim_ref, e1_ref, r2_in_ref, e2_in_ref, o_ref,
                    h_ref, a_ref, rows_ref, gate_ref, r2_ref, e2_ref, *, tm, te):
    e = pl.program_id(1)

    @pl.when(e == 0)
    def _():
        o_ref[...] = jnp.zeros(o_ref.shape, F32)
        r2_ref[:, 0:tm] = r2_in_ref[...]
        e2_ref[:, LANES:LANES + tm] = e2_in_ref[...]

    n_chunks = te // EXPERT_CHUNK

    def rows_of(p):
        return slice(p * EXPERT_CHUNK, (p + 1) * EXPERT_CHUNK)

    def gates(first_key):
        keys = range(first_key, first_key + GATE_KEYS)
        blocks = range(0, N_KEYS, GATE_ROWS)
        for i in keys:
            for hd in range(PEER_HEADS):
                for q, ref in enumerate((lim_ref, e1_ref)):
                    row = jnp.broadcast_to(ref[i, hd:hd + 1, :], (GATE_ROWS, tm)).astype(BF16)
                    rows_ref[q, i, hd, :, q * LANES:q * LANES + tm] = row
        for c in range(tm // LANES):
            lanes = slice(c * LANES, (c + 1) * LANES)
            lanes1 = slice((c + 1) * LANES, (c + 2) * LANES)
            gate = {i: {jb: jnp.zeros((GATE_ROWS, LANES), BF16) for jb in blocks} for i in keys}
            for hd in range(PEER_HEADS):
                lim = {i: rows_ref[0, i, hd, :, lanes] for i in keys}
                e1 = {i: rows_ref[1, i, hd, :, lanes1] for i in keys}
                for jb in blocks:
                    r2 = r2_ref[hd * N_KEYS + jb:hd * N_KEYS + jb + GATE_ROWS, lanes]
                    e2 = e2_ref[hd * N_KEYS + jb:hd * N_KEYS + jb + GATE_ROWS, lanes1]
                    for i in keys:
                        gate[i][jb] = gate[i][jb] + jnp.where(r2 < lim[i], e2 * e1[i], jnp.zeros_like(e2))
            for i in keys:
                for jb in blocks:
                    gate_ref[i * N_KEYS + jb:i * N_KEYS + jb + GATE_ROWS, lanes] = gate[i][jb]

    for first_key in range(0, te // N_KEYS, GATE_KEYS):
        gates(first_key)
    h_ref[...] = jnp.dot(u_ref[...], xb_ref[...], preferred_element_type=F32)
    for p in range(n_chunks):
        a_ref[rows_of(p), :] = gate_ref[rows_of(p), :] * _gelu_exact(h_ref[rows_of(p), :]).astype(BF16)
        o_ref[...] += jnp.dot(vt_ref[:, rows_of(p)], a_ref[rows_of(p), :], preferred_element_type=F32)


def _experts(xbt, u, vt, sel, tm, te):
    d, t = xbt.shape
    n_exp = u.shape[0]
    tm, te = min(tm, t), min(te, n_exp)
    tok = pl.BlockSpec((d, tm), lambda i, j: (0, i))
    sel_spec = pl.BlockSpec((PEER_HEADS * N_KEYS, tm), lambda i, j: (0, i))
    key_spec = pl.BlockSpec((te // N_KEYS, PEER_HEADS, tm), lambda i, j: (j, 0, i))
    lim, e1n, rank2, e2 = sel
    sel = (lim, e1n, rank2.reshape(-1, t), e2.reshape(-1, t))
    return pl.pallas_call(
        functools.partial(_experts_kernel, tm=tm, te=te),
        grid=(t // tm, n_exp // te),
        in_specs=[tok, pl.BlockSpec((te, d), lambda i, j: (j, 0)), pl.BlockSpec((d, te), lambda i, j: (0, j)),
                  key_spec, key_spec, sel_spec, sel_spec],
        out_specs=tok,
        out_shape=jax.ShapeDtypeStruct((d, t), F32),
        scratch_shapes=[pltpu.VMEM((te, tm), F32), pltpu.VMEM((te, tm), BF16),
                        pltpu.VMEM((2, te // N_KEYS, PEER_HEADS, GATE_ROWS, tm + LANES), BF16),
                        pltpu.VMEM((te, tm), BF16),
                        pltpu.VMEM((PEER_HEADS * N_KEYS, tm + LANES), BF16),
                        pltpu.VMEM((PEER_HEADS * N_KEYS, tm + LANES), BF16)],
        compiler_params=_params("parallel", "arbitrary"),
        name="peer_experts",
    )(xbt, u, vt, *sel)


def _ln_t_kernel(x_ref, f_ref, g_ref, b_ref, o_ref, *, alpha):
    z = (alpha * x_ref[...] + f_ref[...]).T
    mu = jnp.mean(z, axis=1, keepdims=True)
    zc = z - mu
    var = jnp.mean(zc * zc, axis=1, keepdims=True)
    o_ref[...] = zc * lax.rsqrt(var + LN_EPS) * g_ref[...] + b_ref[...]


def _residual_ln_t(x1t, fft, g, b, alpha, tm):
    d, t = x1t.shape
    tm = min(tm, t)
    tok = pl.BlockSpec((d, tm), lambda i: (0, i))
    vec = pl.BlockSpec((1, d), lambda i: (0, 0))
    return pl.pallas_call(
        functools.partial(_ln_t_kernel, alpha=alpha),
        grid=(t // tm,),
        in_specs=[tok, tok, vec, vec],
        out_specs=pl.BlockSpec((tm, d), lambda i: (i, 0)),
        out_shape=jax.ShapeDtypeStruct((t, d), F32),
        compiler_params=_params("parallel"),
        name="residual_ln",
    )(x1t, fft, g, b)


def kernel(x, mem, positions, w_in, w_mem_kv, w_pool, pool_scale, attn_sinks, w_out, ln1_g, ln1_b,
           w_peer_q, sub_keys_1, sub_keys_2, expert_u, expert_v, ln2_g, ln2_b):
    bsz, seq, d = x.shape
    depth = w_in.shape[0]
    t = bsz * seq
    alpha = (2.0 * depth) ** 0.25
    for l in range(depth):
        x2 = x.reshape(t, d)
        h = _inproj(x2, w_in[l].astype(BF16), positions, 512).reshape(bsz, seq, -1)
        mem2 = mem.reshape(-1, d).astype(BF16)
        kvm = _matmul(mem2, w_mem_kv[l].astype(BF16), BF16, 512, 512).reshape(bsz, mem.shape[1], -1)
        pool_o = _pool(h, w_pool[l].astype(BF16), pool_scale[l].reshape(1, -1), 512)
        swa_o = _swa(h, attn_sinks[l])
        mem_o = _mem_attention(h, kvm, 512)
        x1t, x1bt = _outproj_ln(pool_o.reshape(t, -1), swa_o.reshape(t, -1), mem_o.reshape(t, -1),
                          w_out[l].astype(BF16), x2, ln1_g[l].reshape(1, d), ln1_b[l].reshape(1, d), alpha, 512)
        sel = _retrieve(x1bt, _transpose_cast(w_peer_q[l], 512), sub_keys_1[l].astype(BF16),
                        sub_keys_2[l].astype(BF16), 512)
        fft = _experts(x1bt, expert_u[l].astype(BF16), _transpose_cast(expert_v[l], 512), sel, 1024, 512)
        x = _residual_ln_t(x1t, fft, ln2_g[l].reshape(1, d), ln2_b[l].reshape(1, d), alpha, 512).reshape(bsz, seq, d)
    return x
```

```python
import functools
import math

import numpy as np
import jax
import jax.numpy as jnp
from jax import lax
from jax.experimental import pallas as pl
from jax.experimental.pallas import tpu as pltpu

F32 = jnp.float32
BF16 = jnp.bfloat16

LANES = 128
SUBLANES = 8
VMEM_LIMIT_BYTES = 56 * 1024 * 1024

POOL_WINDOWS = (2, 4, 8, 16)
POOL_GROUP = 128
POOL_HALO = 16
SWA_HEAD_DIM = 64
SWA_HEADS = 16
SWA_KV_HEADS = 4
SWA_BLOCK = 128
ROPE_THETA = 500000.0
ROPE_DIM = 16
MEM_HEADS = 4
MEM_HEAD_DIM = 128
PEER_HEADS = 8
N_KEYS = 128
PEER_TOPK = 16
HALF_DIM = 128
LN_EPS = 1e-5
NEG = -1e30

EXPERT_CHUNK = 512
GATE_KEYS = 4
GATE_ROWS = 16
SELECT_TOKENS = 512


def _params(*semantics):
    return pltpu.CompilerParams(dimension_semantics=semantics, vmem_limit_bytes=VMEM_LIMIT_BYTES)


def _matmul_kernel(a_ref, b_ref, o_ref):
    o_ref[...] = jnp.dot(a_ref[...], b_ref[...], preferred_element_type=F32).astype(o_ref.dtype)


def _matmul(a, b, out_dtype, tm, tn):
    m, k = a.shape
    n = b.shape[1]
    tm, tn = min(tm, m), min(tn, n)
    return pl.pallas_call(
        _matmul_kernel,
        grid=(m // tm, n // tn),
        in_specs=[pl.BlockSpec((tm, k), lambda i, j: (i, 0)), pl.BlockSpec((k, tn), lambda i, j: (0, j))],
        out_specs=pl.BlockSpec((tm, tn), lambda i, j: (i, j)),
        out_shape=jax.ShapeDtypeStruct((m, n), out_dtype),
        compiler_params=_params("parallel", "arbitrary"),
        name="matmul",
    )(a, b)


IN_BLOCK = 512
_Q_BLOCKS = (1, 2)
_KV_BLOCK = 3


def _rope(x, c, sa, sb):
    half = ROPE_DIM // 2
    return x * c + pltpu.roll(x, LANES - half, 1) * sa + pltpu.roll(x, half, 1) * sb


def _inproj_kernel(x_ref, w_ref, pos_ref, freq_ref, sa_ref, sb_ref, o_ref):
    xb = x_ref[...].astype(BF16)
    ang = pos_ref[...].astype(F32) * freq_ref[...]
    s = jnp.sin(ang)
    c, sa, sb = jnp.cos(ang), s * sa_ref[...], s * sb_ref[...]
    for j in range(w_ref.shape[1] // IN_BLOCK):
        y = jnp.dot(xb, w_ref[:, j * IN_BLOCK:(j + 1) * IN_BLOCK], preferred_element_type=F32)
        for k in range(IN_BLOCK // LANES):
            piece = y[:, k * LANES:(k + 1) * LANES]
            if j in _Q_BLOCKS:
                piece = _rope(piece, c, sa, sb) * SWA_HEAD_DIM ** -0.5
            elif j == _KV_BLOCK and k < IN_BLOCK // LANES // 2:
                piece = _rope(piece, c, sa, sb)
            o_ref[:, j * IN_BLOCK + k * LANES:j * IN_BLOCK + (k + 1) * LANES] = piece.astype(o_ref.dtype)


def _rope_constants():
    lane = np.arange(LANES)
    d = lane % SWA_HEAD_DIM
    half = ROPE_DIM // 2
    inv_freq = np.float32(ROPE_THETA) ** (-np.arange(0, ROPE_DIM, 2, dtype=np.float32) / np.float32(ROPE_DIM))
    freq = np.where(d < ROPE_DIM, inv_freq[d % half], 0.0).astype(np.float32)
    sa = np.where(d < half, -1.0, 0.0).astype(np.float32)
    sb = np.where((d >= half) & (d < ROPE_DIM), 1.0, 0.0).astype(np.float32)
    return [jnp.asarray(a.reshape(1, LANES)) for a in (freq, sa, sb)]


def _inproj(x2, w, positions, tm):
    t, d = x2.shape
    n = w.shape[1]
    tm = min(tm, t)
    freq, sa, sb = _rope_constants()
    const = pl.BlockSpec((1, LANES), lambda i: (0, 0))
    return pl.pallas_call(
        _inproj_kernel,
        grid=(t // tm,),
        in_specs=[pl.BlockSpec((tm, d), lambda i: (i, 0)), pl.BlockSpec((d, n), lambda i: (0, 0)),
                  pl.BlockSpec((tm, 1), lambda i: (i, 0)), const, const, const],
        out_specs=pl.BlockSpec((tm, n), lambda i: (i, 0)),
        out_shape=jax.ShapeDtypeStruct((t, n), BF16),
        compiler_params=_params("parallel"),
        name="inproj_rope",
    )(x2, w, positions.reshape(t, 1), freq, sa, sb)


def _pool_kernel(v_ref, w_ref, scale_ref, o_ref, ext_ref, *, ts):
    s = pl.program_id(1)

    @pl.when(s == 0)
    def _():
        ext_ref[0:POOL_HALO, :] = jnp.zeros((POOL_HALO, ext_ref.shape[1]), F32)

    ext_ref[POOL_HALO:POOL_HALO + ts, :] = v_ref[0].astype(F32)
    pos = s * ts + lax.broadcasted_iota(jnp.int32, (ts, 1), 0)
    for g, w in enumerate(POOL_WINDOWS):
        cols = slice(g * POOL_GROUP, (g + 1) * POOL_GROUP)
        acc = ext_ref[POOL_HALO:POOL_HALO + ts, cols]
        for k in range(1, w):
            acc = acc + ext_ref[POOL_HALO - k:POOL_HALO - k + ts, cols]
        count = jnp.minimum(pos + 1, w).astype(F32)
        pooled = acc / count - ext_ref[POOL_HALO:POOL_HALO + ts, cols]
        y = jnp.dot(pooled.astype(BF16), w_ref[g], preferred_element_type=F32)
        o_ref[0, :, cols] = (y * scale_ref[:, cols]).astype(o_ref.dtype)
    ext_ref[0:POOL_HALO, :] = ext_ref[ts:ts + POOL_HALO, :]


def _pool(h, w_pool, pool_scale, ts):
    b, s, _ = h.shape
    width = POOL_GROUP * len(POOL_WINDOWS)
    ts = min(ts, s)
    return pl.pallas_call(
        functools.partial(_pool_kernel, ts=ts),
        grid=(b, s // ts),
        in_specs=[
            pl.BlockSpec((1, ts, width), lambda i, j: (i, j, 0)),
            pl.BlockSpec(w_pool.shape, lambda i, j: (0, 0, 0)),
            pl.BlockSpec((1, width), lambda i, j: (0, 0)),
        ],
        out_specs=pl.BlockSpec((1, ts, width), lambda i, j: (i, j, 0)),
        out_shape=jax.ShapeDtypeStruct((b, s, width), BF16),
        scratch_shapes=[pltpu.VMEM((ts + POOL_HALO, width), F32)],
        compiler_params=_params("arbitrary", "arbitrary"),
        name="pool",
    )(h, w_pool, pool_scale)


def _swa_kernel(sink_ref, q0_ref, q1_ref, kv_ref, kvp_ref, o_ref):
    n = pl.program_id(1)
    kvw = SWA_KV_HEADS * SWA_HEAD_DIM
    q = jnp.concatenate([q0_ref[0], q1_ref[0]], axis=1)
    k = jnp.concatenate([kvp_ref[0, :, 0:kvw], kv_ref[0, :, 0:kvw]], axis=0)
    v = jnp.concatenate([kvp_ref[0, :, kvw:2 * kvw], kv_ref[0, :, kvw:2 * kvw]], axis=0)
    row = lax.broadcasted_iota(jnp.int32, (SWA_BLOCK, 2 * SWA_BLOCK), 0)
    col = lax.broadcasted_iota(jnp.int32, (SWA_BLOCK, 2 * SWA_BLOCK), 1)
    rel = row + SWA_BLOCK - col
    valid = (rel >= 0) & (rel < SWA_BLOCK) & ((col >= SWA_BLOCK) | (n > 0))
    group = SWA_HEADS // SWA_KV_HEADS
    outs = []
    for hq in range(SWA_HEADS):
        kv = hq // group
        qh = q[:, hq * SWA_HEAD_DIM:(hq + 1) * SWA_HEAD_DIM]
        kh = k[:, kv * SWA_HEAD_DIM:(kv + 1) * SWA_HEAD_DIM]
        vh = v[:, kv * SWA_HEAD_DIM:(kv + 1) * SWA_HEAD_DIM]
        sc = lax.dot_general(qh, kh, (((1,), (1,)), ((), ())), preferred_element_type=F32)
        sc = jnp.where(valid, sc, NEG)
        sink = sink_ref[hq]
        m = jnp.maximum(jnp.max(sc, axis=1, keepdims=True), sink)
        p = jnp.exp(sc - m)
        denom = jnp.sum(p, axis=1, keepdims=True) + jnp.exp(sink - m)
        o = jnp.dot(p.astype(BF16), vh, preferred_element_type=F32)
        outs.append(o / denom)
    o_ref[0] = jnp.concatenate(outs, axis=1).astype(o_ref.dtype)


def _swa(h, sinks):
    b, s, _ = h.shape
    nb = s // SWA_BLOCK
    blk = lambda c: pl.BlockSpec((1, SWA_BLOCK, IN_BLOCK), lambda i, j: (i, j, c))
    blk_prev = lambda c: pl.BlockSpec((1, SWA_BLOCK, IN_BLOCK), lambda i, j: (i, jnp.maximum(j - 1, 0), c))
    return pl.pallas_call(
        _swa_kernel,
        grid=(b, nb),
        in_specs=[pl.BlockSpec(memory_space=pltpu.SMEM),
                  blk(_Q_BLOCKS[0]), blk(_Q_BLOCKS[1]), blk(_KV_BLOCK), blk_prev(_KV_BLOCK)],
        out_specs=pl.BlockSpec((1, SWA_BLOCK, SWA_HEADS * SWA_HEAD_DIM), lambda i, j: (i, j, 0)),
        out_shape=jax.ShapeDtypeStruct((b, s, SWA_HEADS * SWA_HEAD_DIM), BF16),
        compiler_params=_params("parallel", "arbitrary"),
        name="swa",
    )(sinks, h, h, h, h)


def _mem_kernel(q_ref, kv_ref, o_ref):
    scale = MEM_HEAD_DIM ** -0.5
    width = MEM_HEADS * MEM_HEAD_DIM
    for hm in range(MEM_HEADS):
        cols = slice(hm * MEM_HEAD_DIM, (hm + 1) * MEM_HEAD_DIM)
        km = kv_ref[0, :, cols]
        vm = kv_ref[0, :, width + hm * MEM_HEAD_DIM:width + (hm + 1) * MEM_HEAD_DIM]
        sc = lax.dot_general(q_ref[0, :, cols], km, (((1,), (1,)), ((), ())), preferred_element_type=F32) * scale
        m = jnp.max(sc, axis=1, keepdims=True)
        p = jnp.exp(sc - m)
        denom = jnp.sum(p, axis=1, keepdims=True)
        o = jnp.dot(p.astype(BF16), vm, preferred_element_type=F32)
        o_ref[0, :, cols] = (o / denom).astype(o_ref.dtype)


def _mem_attention(h, kvm, tq):
    b, s, _ = h.shape
    width = MEM_HEADS * MEM_HEAD_DIM
    tq = min(tq, s)
    return pl.pallas_call(
        _mem_kernel,
        grid=(b, s // tq),
        in_specs=[
            pl.BlockSpec((1, tq, width), lambda i, j: (i, j, 4)),
            pl.BlockSpec((1,) + kvm.shape[1:], lambda i, j: (i, 0, 0)),
        ],
        out_specs=pl.BlockSpec((1, tq, width), lambda i, j: (i, j, 0)),
        out_shape=jax.ShapeDtypeStruct((b, s, width), BF16),
        compiler_params=_params("parallel", "arbitrary"),
        name="mem_attention",
    )(h, kvm)


def _outproj_kernel(pool_ref, swa_ref, mem_ref, wp_ref, ws_ref, wm_ref, x_ref, g_ref, b_ref, o_ref, ob_ref,
                    *, alpha):
    mix = jnp.dot(pool_ref[...], wp_ref[...], preferred_element_type=F32)
    mix += jnp.dot(swa_ref[...], ws_ref[...], preferred_element_type=F32)
    mix += jnp.dot(mem_ref[...], wm_ref[...], preferred_element_type=F32)
    z = alpha * x_ref[...] + mix
    mu = jnp.mean(z, axis=1, keepdims=True)
    zc = z - mu
    var = jnp.mean(zc * zc, axis=1, keepdims=True)
    y = zc * lax.rsqrt(var + LN_EPS) * g_ref[...] + b_ref[...]
    yt = y.T
    o_ref[...] = yt
    ob_ref[...] = yt.astype(BF16)


def _outproj_ln(pool_o, swa_o, mem_o, w_out, x2, g, b, alpha, tm):
    t, d = x2.shape
    tm = min(tm, t)
    wp, ws, wm = pool_o.shape[1], swa_o.shape[1], mem_o.shape[1]
    w_p, w_s, w_m = w_out[:wp], w_out[wp:wp + ws], w_out[wp + ws:]
    row = lambda w: pl.BlockSpec((tm, w), lambda i: (i, 0))
    full = lambda a: pl.BlockSpec(a.shape, lambda i: (0, 0))
    return pl.pallas_call(
        functools.partial(_outproj_kernel, alpha=alpha),
        grid=(t // tm,),
        in_specs=[row(wp), row(ws), row(wm), full(w_p), full(w_s), full(w_m), row(d),
                  pl.BlockSpec((1, d), lambda i: (0, 0)), pl.BlockSpec((1, d), lambda i: (0, 0))],
        out_specs=[pl.BlockSpec((d, tm), lambda i: (0, i))] * 2,
        out_shape=[jax.ShapeDtypeStruct((d, t), F32), jax.ShapeDtypeStruct((d, t), BF16)],
        compiler_params=_params("parallel"),
        name="outproj_ln",
    )(pool_o, swa_o, mem_o, w_p, w_s, w_m, x2, g, b)


def _merge_exchange_pairs(n):
    t = (n - 1).bit_length()
    pairs = []
    p = 1 << (t - 1)
    while p >= 1:
        q, r, d = 1 << (t - 1), 0, p
        while True:
            pairs += [(i, i + d) for i in range(n - d) if (i & p) == r]
            if q == p:
                break
            d, q, r = q - p, q // 2, p
        p //= 2
    return pairs


def _pop_largest(lists, k):
    lists = list(lists)
    n = len(lists)
    for i, j in _merge_exchange_pairs(n):
        lists[i], lists[j] = jnp.maximum(lists[i], lists[j]), jnp.minimum(lists[i], lists[j])
    vals = []
    for r in range(k):
        m = jnp.max(lists[0], axis=0, keepdims=True)
        vals.append(m)
        pop = lists[0] == m
        for d in range(min(n - 1, k - 2 - r) + 1):
            below = lists[d + 1] if d + 1 < n else -jnp.inf
            lists[d] = jnp.where(pop, below, lists[d])
    return vals


def _top16_rows_distinct(s, with_rank):
    k = PEER_TOPK
    vals = _pop_largest([s[SUBLANES * g:SUBLANES * (g + 1), :] for g in range(s.shape[0] // SUBLANES)], k)
    def threshold(bits, lo, hi):
        if not bits:
            return vals[(lo + hi) // 2 - 1]
        mid = (lo + hi) // 2
        return jnp.where(bits[0], threshold(bits[1:], mid, hi), threshold(bits[1:], lo, mid))

    rank = None
    if with_rank:
        bits = []
        for _ in range(k.bit_length() - 1):
            bits.append(threshold(bits, 0, k) > s)
        rank = sum(jnp.where(b, float(k >> (n + 1)), 0.0) for n, b in enumerate(bits))
        rank = jnp.where(vals[k - 1] > s, float(k), rank)
    count = jnp.sum(jnp.where(s >= vals[k - 1], 1.0, 0.0), axis=0, keepdims=True)
    strict = jnp.zeros_like(count)
    for r in range(k - 1):
        strict = strict + jnp.where(vals[r] > vals[r + 1], 1.0, 0.0)
    distinct = (count == float(k)) & (strict == float(k - 1))
    return rank, vals, jnp.where(distinct, float(k), 0.0)


def _top16_rows(s, break_ties, with_rank=True):
    if not break_ties:
        return _top16_rows_distinct(s, with_rank)
    n = s.shape[0]
    iota = lax.broadcasted_iota(jnp.int32, s.shape, 0).astype(F32)
    rank = jnp.full(s.shape, float(PEER_TOPK), F32)
    vals = []
    for r in range(PEER_TOPK):
        m = jnp.max(s, axis=0, keepdims=True)
        idx = jnp.min(jnp.where(s == m, iota, float(n)), axis=0, keepdims=True)
        hit = iota == idx
        rank = jnp.where(hit, float(r), rank)
        s = jnp.where(hit, -jnp.inf, s)
        vals.append(m)
    count = jnp.sum(jnp.where(rank < float(PEER_TOPK), 1.0, 0.0), axis=0, keepdims=True)
    return rank, vals, count


_CAND_SMALL_A = PEER_TOPK // 2
_CAND_ROWS = PEER_TOPK + (_CAND_SMALL_A - 1) * SUBLANES + SUBLANES


def _cand_constants():
    flat = np.full((_CAND_ROWS, 1), 1e9, np.float32)
    valid = np.zeros((_CAND_ROWS, 1), np.float32)
    for b in range(PEER_TOPK):
        flat[b, 0], valid[b, 0] = b, 1.0
    for a in range(1, _CAND_SMALL_A):
        base = PEER_TOPK + (a - 1) * SUBLANES
        for b in range(PEER_TOPK // (a + 1)):
            flat[base + b, 0], valid[base + b, 0] = a * PEER_TOPK + b, 1.0
    base = PEER_TOPK + (_CAND_SMALL_A - 1) * SUBLANES
    for k in range(SUBLANES):
        flat[base + k, 0], valid[base + k, 0] = (_CAND_SMALL_A + k) * PEER_TOPK, 1.0
    return jnp.asarray(flat), jnp.asarray(valid)


def _select_experts(s1, s2, flat, valid, break_ties):
    t = s1.shape[1]
    rank1, v1, count1 = _top16_rows(s1, break_ties, with_rank=break_ties)
    rank2, v2, count2 = _top16_rows(s2, break_ties)
    v2_lo = jnp.concatenate(v2[:SUBLANES], axis=0)
    v2_all = jnp.concatenate(v2, axis=0)
    v1_hi = jnp.concatenate(v1[_CAND_SMALL_A:], axis=0)
    groups = [v1[0] + v2_all]
    for a in range(1, _CAND_SMALL_A):
        groups.append(v1[a] + v2_lo)
    groups.append(v1_hi + v2[0])
    cand = jnp.concatenate(groups, axis=0)
    cand = jnp.where(valid > 0.5, cand, -jnp.inf)
    if break_ties:
        flat_b = jnp.broadcast_to(flat, cand.shape)
        hits = jnp.zeros(cand.shape, F32)
        top = []
        for r in range(PEER_TOPK):
            m = jnp.max(cand, axis=0, keepdims=True)
            pick = jnp.min(jnp.where(cand == m, flat_b, 2e9), axis=0, keepdims=True)
            hit = flat_b == pick
            hits = jnp.where(hit, 1.0, hits)
            cand = jnp.where(hit, -jnp.inf, cand)
            top.append(m)
    else:
        top = _pop_largest([cand[SUBLANES * g:SUBLANES * (g + 1), :] for g in range(_CAND_ROWS // SUBLANES)],
                           PEER_TOPK)
        hits = jnp.where(cand >= top[PEER_TOPK - 1], 1.0, 0.0)
    z = jnp.ones((1, t), F32)
    for r in range(1, PEER_TOPK):
        z = z + jnp.exp(top[r] - top[0])
    counts = [jnp.sum(hits[0:PEER_TOPK], axis=0, keepdims=True)]
    for a in range(1, _CAND_SMALL_A):
        base = PEER_TOPK + (a - 1) * SUBLANES
        counts.append(jnp.sum(hits[base:base + SUBLANES], axis=0, keepdims=True))
    base = PEER_TOPK + (_CAND_SMALL_A - 1) * SUBLANES
    for k in range(SUBLANES):
        counts.append(hits[base + k:base + k + 1])
    lim = jnp.zeros(s1.shape, F32)
    if rank1 is not None:
        for a in range(PEER_TOPK):
            lim = jnp.where(rank1 == float(a), counts[a], lim)
    else:
        for a in range(PEER_TOPK):
            step = counts[a] - counts[a + 1] if a + 1 < PEER_TOPK else counts[a]
            lim = lim + jnp.where(s1 >= v1[a], step, 0.0)
    e1n = jnp.exp(s1 - v1[0]) * (1.0 / z)
    e2 = jnp.exp(s2 - v2[0])
    count3 = jnp.sum(hits, axis=0, keepdims=True)
    k = float(PEER_TOPK)
    ok = jnp.where((count1 == k) & (count2 == k) & (count3 == k), 1.0, 0.0)
    return lim, e1n, rank2, e2, ok


def _retrieve_kernel(wq_ref, x_ref, k1_ref, k2_ref, flat_ref, valid_ref,
                     lim_out_ref, e1_out_ref, r2_ref, e2_ref, q_ref, lim_ref, e1_ref, *, tq):
    q_ref[...] = jnp.dot(wq_ref[...], x_ref[...], preferred_element_type=F32)
    flat, valid = flat_ref[...], valid_ref[...]

    def head(h, carry):
        for c in range(tq // SELECT_TOKENS):
            lanes = slice(c * SELECT_TOKENS, (c + 1) * SELECT_TOKENS)
            r0 = pl.multiple_of(h * 2 * HALF_DIM, 2 * HALF_DIM)
            q1 = q_ref[pl.ds(r0, HALF_DIM), lanes].astype(BF16)
            q2 = q_ref[pl.ds(r0 + HALF_DIM, HALF_DIM), lanes].astype(BF16)
            s1 = jnp.dot(k1_ref[...], q1, preferred_element_type=F32)
            s2 = jnp.dot(k2_ref[...], q2, preferred_element_type=F32)

            def emit(break_ties):
                lim, e1n, rank2, e2, ok = _select_experts(s1, s2, flat, valid, break_ties)
                lim_ref[h, :, lanes] = lim
                e1_ref[h, :, lanes] = e1n
                r2_ref[h, :, lanes] = rank2.astype(BF16)
                e2_ref[h, :, lanes] = e2.astype(BF16)
                return ok

            ok = emit(False)

            @pl.when(jnp.min(ok) < 0.5)
            def _():
                emit(True)
        return carry

    lax.fori_loop(0, PEER_HEADS, head, 0)
    lim_out_ref[...] = jnp.swapaxes(lim_ref[...], 0, 1)
    e1_out_ref[...] = jnp.swapaxes(e1_ref[...], 0, 1)


def _retrieve(x1t, wq_t, k1, k2, tq):
    d, t = x1t.shape
    tq = min(tq, t)
    flat, valid = _cand_constants()
    full = lambda a: pl.BlockSpec(a.shape, lambda i: (0,) * a.ndim)
    out_keys = jax.ShapeDtypeStruct((N_KEYS, PEER_HEADS, t), F32)
    key_spec = pl.BlockSpec((N_KEYS, PEER_HEADS, tq), lambda i: (0, 0, i))
    out_packed = jax.ShapeDtypeStruct((PEER_HEADS, N_KEYS, t), BF16)
    out_spec = pl.BlockSpec((PEER_HEADS, N_KEYS, tq), lambda i: (0, 0, i))
    return pl.pallas_call(
        functools.partial(_retrieve_kernel, tq=tq),
        grid=(t // tq,),
        in_specs=[full(wq_t), pl.BlockSpec((d, tq), lambda i: (0, i)), full(k1), full(k2), full(flat), full(valid)],
        out_specs=[key_spec, key_spec, out_spec, out_spec],
        out_shape=[out_keys, out_keys, out_packed, out_packed],
        scratch_shapes=[pltpu.VMEM((wq_t.shape[0], tq), F32)] + [pltpu.VMEM((PEER_HEADS, N_KEYS, tq), F32)] * 2,
        compiler_params=_params("parallel"),
        name="peer_retrieve",
    )(wq_t, x1t, k1, k2, flat, valid)


def _transpose_cast_kernel(v_ref, o_ref):
    o_ref[...] = v_ref[...].T.astype(o_ref.dtype)


def _transpose_cast(v, rows):
    n, d = v.shape
    rows = min(rows, n)
    return pl.pallas_call(
        _transpose_cast_kernel,
        grid=(n // rows,),
        in_specs=[pl.BlockSpec((rows, d), lambda i: (i, 0))],
        out_specs=pl.BlockSpec((d, rows), lambda i: (0, i)),
        out_shape=jax.ShapeDtypeStruct((d, n), BF16),
        compiler_params=_params("parallel"),
        name="transpose_cast",
    )(v)


def _gelu_exact(x):
    return 0.5 * x * (1.0 + lax.erf(x * (1.0 / math.sqrt(2.0))))


def _experts_kernel(xb_ref, u_ref, vt_ref, lim_ref, e1_ref, r2_in_ref, e2_in_ref, o_ref,
                    h_ref, a_ref, rows_ref, gate_ref, r2_ref, e2_ref, *, tm, te):
    e = pl.program_id(1)

    @pl.when(e == 0)
    def _():
        o_ref[...] = jnp.zeros(o_ref.shape, F32)
        r2_ref[:, 0:tm] = r2_in_ref[...]
        e2_ref[:, LANES:LANES + tm] = e2_in_ref[...]

    n_chunks = te // EXPERT_CHUNK

    def rows_of(p):
        return slice(p * EXPERT_CHUNK, (p + 1) * EXPERT_CHUNK)

    def gates(first_key):
        keys = range(first_key, first_key + GATE_KEYS)
        blocks = range(0, N_KEYS, GATE_ROWS)
        for i in keys:
            for hd in range(PEER_HEADS):
                for q, ref in enumerate((lim_ref, e1_ref)):
                    row = jnp.broadcast_to(ref[i, hd:hd + 1, :], (GATE_ROWS, tm)).astype(BF16)
                    rows_ref[q, i, hd, :, q * LANES:q * LANES + tm] = row
        for c in range(tm // LANES):
            lanes = slice(c * LANES, (c + 1) * LANES)
            lanes1 = slice((c + 1) * LANES, (c + 2) * LANES)
            gate = {i: {jb: jnp.zeros((GATE_ROWS, LANES), BF16) for jb in blocks} for i in keys}
            for hd in range(PEER_HEADS):
                lim = {i: rows_ref[0, i, hd, :, lanes] for i in keys}
                e1 = {i: rows_ref[1, i, hd, :, lanes1] for i in keys}
                for jb in blocks:
                    r2 = r2_ref[hd * N_KEYS + jb:hd * N_KEYS + jb + GATE_ROWS, lanes]
                    e2 = e2_ref[hd * N_KEYS + jb:hd * N_KEYS + jb + GATE_ROWS, lanes1]
                    for i in keys:
                        gate[i][jb] = gate[i][jb] + jnp.where(r2 < lim[i], e2 * e1[i], jnp.zeros_like(e2))
            for i in keys:
                for jb in blocks:
                    gate_ref[i * N_KEYS + jb:i * N_KEYS + jb + GATE_ROWS, lanes] = gate[i][jb]

    for first_key in range(0, te // N_KEYS, GATE_KEYS):
        gates(first_key)
    h_ref[...] = jnp.dot(u_ref[...], xb_ref[...], preferred_element_type=F32)
    for p in range(n_chunks):
        a_ref[rows_of(p), :] = gate_ref[rows_of(p), :] * _gelu_exact(h_ref[rows_of(p), :]).astype(BF16)
        o_ref[...] += jnp.dot(vt_ref[:, rows_of(p)], a_ref[rows_of(p), :], preferred_element_type=F32)


def _experts(xbt, u, vt, sel, tm, te):
    d, t = xbt.shape
    n_exp = u.shape[0]
    tm, te = min(tm, t), min(te, n_exp)
    tok = pl.BlockSpec((d, tm), lambda i, j: (0, i))
    sel_spec = pl.BlockSpec((PEER_HEADS * N_KEYS, tm), lambda i, j: (0, i))
    key_spec = pl.BlockSpec((te // N_KEYS, PEER_HEADS, tm), lambda i, j: (j, 0, i))
    lim, e1n, rank2, e2 = sel
    sel = (lim, e1n, rank2.reshape(-1, t), e2.reshape(-1, t))
    return pl.pallas_call(
        functools.partial(_experts_kernel, tm=tm, te=te),
        grid=(t // tm, n_exp // te),
        in_specs=[tok, pl.BlockSpec((te, d), lambda i, j: (j, 0)), pl.BlockSpec((d, te), lambda i, j: (0, j)),
                  key_spec, key_spec, sel_spec, sel_spec],
        out_specs=tok,
        out_shape=jax.ShapeDtypeStruct((d, t), F32),
        scratch_shapes=[pltpu.VMEM((te, tm), F32), pltpu.VMEM((te, tm), BF16),
                        pltpu.VMEM((2, te // N_KEYS, PEER_HEADS, GATE_ROWS, tm + LANES), BF16),
                        pltpu.VMEM((te, tm), BF16),
                        pltpu.VMEM((PEER_HEADS * N_KEYS, tm + LANES), BF16),
                        pltpu.VMEM((PEER_HEADS * N_KEYS, tm + LANES), BF16)],
        compiler_params=_params("parallel", "arbitrary"),
        name="peer_experts",
    )(xbt, u, vt, *sel)


def _ln_t_kernel(x_ref, f_ref, g_ref, b_ref, o_ref, *, alpha):
    z = (alpha * x_ref[...] + f_ref[...]).T
    mu = jnp.mean(z, axis=1, keepdims=True)
    zc = z - mu
    var = jnp.mean(zc * zc, axis=1, keepdims=True)
    o_ref[...] = zc * lax.rsqrt(var + LN_EPS) * g_ref[...] + b_ref[...]


def _residual_ln_t(x1t, fft, g, b, alpha, tm):
    d, t = x1t.shape
    tm = min(tm, t)
    tok = pl.BlockSpec((d, tm), lambda i: (0, i))
    vec = pl.BlockSpec((1, d), lambda i: (0, 0))
    return pl.pallas_call(
        functools.partial(_ln_t_kernel, alpha=alpha),
        grid=(t // tm,),
        in_specs=[tok, tok, vec, vec],
        out_specs=pl.BlockSpec((tm, d), lambda i: (i, 0)),
        out_shape=jax.ShapeDtypeStruct((t, d), F32),
        compiler_params=_params("parallel"),
        name="residual_ln",
    )(x1t, fft, g, b)


def kernel(x, mem, positions, w_in, w_mem_kv, w_pool, pool_scale, attn_sinks, w_out, ln1_g, ln1_b,
           w_peer_q, sub_keys_1, sub_keys_2, expert_u, expert_v, ln2_g, ln2_b):
    bsz, seq, d = x.shape
    depth = w_in.shape[0]
    t = bsz * seq
    alpha = (2.0 * depth) ** 0.25
    for l in range(depth):
        x2 = x.reshape(t, d)
        h = _inproj(x2, w_in[l].astype(BF16), positions, 512).reshape(bsz, seq, -1)
        mem2 = mem.reshape(-1, d).astype(BF16)
        kvm = _matmul(mem2, w_mem_kv[l].astype(BF16), BF16, 512, 512).reshape(bsz, mem.shape[1], -1)
        pool_o = _pool(h, w_pool[l].astype(BF16), pool_scale[l].reshape(1, -1), 512)
        swa_o = _swa(h, attn_sinks[l])
        mem_o = _mem_attention(h, kvm, 512)
        x1t, x1bt = _outproj_ln(pool_o.reshape(t, -1), swa_o.reshape(t, -1), mem_o.reshape(t, -1),
                          w_out[l].astype(BF16), x2, ln1_g[l].reshape(1, d), ln1_b[l].reshape(1, d), alpha, 512)
        sel = _retrieve(x1bt, _transpose_cast(w_peer_q[l], 512), sub_keys_1[l].astype(BF16),
                        sub_keys_2[l].astype(BF16), 512)
        fft = _experts(x1bt, expert_u[l].astype(BF16), _transpose_cast(expert_v[l], 512), sel, 1024, 512)
        x = _residual_ln_t(x1t, fft, ln2_g[l].reshape(1, d), ln2_b[l].reshape(1, d), alpha, 512).reshape(bsz, seq, d)
    return x
```

```python
import functools
import math

import numpy as np
import jax
import jax.numpy as jnp
from jax import lax
from jax.experimental import pallas as pl
from jax.experimental.pallas import tpu as pltpu

F32 = jnp.float32
BF16 = jnp.bfloat16

LANES = 128
SUBLANES = 8
VMEM_LIMIT_BYTES = 56 * 1024 * 1024

POOL_WINDOWS = (2, 4, 8, 16)
POOL_GROUP = 128
POOL_HALO = 16
SWA_HEAD_DIM = 64
SWA_HEADS = 16
SWA_KV_HEADS = 4
SWA_BLOCK = 128
ROPE_THETA = 500000.0
ROPE_DIM = 16
MEM_HEADS = 4
MEM_HEAD_DIM = 128
PEER_HEADS = 8
N_KEYS = 128
PEER_TOPK = 16
HALF_DIM = 128
LN_EPS = 1e-5
NEG = -1e30

EXPERT_CHUNK = 512
GATE_KEYS = 4
GATE_ROWS = 16
SELECT_TOKENS = 512


def _params(*semantics):
    return pltpu.CompilerParams(dimension_semantics=semantics, vmem_limit_bytes=VMEM_LIMIT_BYTES)


def _matmul_kernel(a_ref, b_ref, o_ref):
    o_ref[...] = jnp.dot(a_ref[...], b_ref[...], preferred_element_type=F32).astype(o_ref.dtype)


def _matmul(a, b, out_dtype, tm, tn):
    m, k = a.shape
    n = b.shape[1]
    tm, tn = min(tm, m), min(tn, n)
    return pl.pallas_call(
        _matmul_kernel,
        grid=(m // tm, n // tn),
        in_specs=[pl.BlockSpec((tm, k), lambda i, j: (i, 0)), pl.BlockSpec((k, tn), lambda i, j: (0, j))],
        out_specs=pl.BlockSpec((tm, tn), lambda i, j: (i, j)),
        out_shape=jax.ShapeDtypeStruct((m, n), out_dtype),
        compiler_params=_params("parallel", "arbitrary"),
        name="matmul",
    )(a, b)


IN_BLOCK = 512
_Q_BLOCKS = (1, 2)
_KV_BLOCK = 3


def _rope(x, c, sa, sb):
    half = ROPE_DIM // 2
    return x * c + pltpu.roll(x, LANES - half, 1) * sa + pltpu.roll(x, half, 1) * sb


def _inproj_kernel(x_ref, w_ref, pos_ref, freq_ref, sa_ref, sb_ref, o_ref):
    xb = x_ref[...].astype(BF16)
    ang = pos_ref[...].astype(F32) * freq_ref[...]
    s = jnp.sin(ang)
    c, sa, sb = jnp.cos(ang), s * sa_ref[...], s * sb_ref[...]
    for j in range(w_ref.shape[1] // IN_BLOCK):
        y = jnp.dot(xb, w_ref[:, j * IN_BLOCK:(j + 1) * IN_BLOCK], preferred_element_type=F32)
        for k in range(IN_BLOCK // LANES):
            piece = y[:, k * LANES:(k + 1) * LANES]
            if j in _Q_BLOCKS:
                piece = _rope(piece, c, sa, sb) * SWA_HEAD_DIM ** -0.5
            elif j == _KV_BLOCK and k < IN_BLOCK // LANES // 2:
                piece = _rope(piece, c, sa, sb)
            o_ref[:, j * IN_BLOCK + k * LANES:j * IN_BLOCK + (k + 1) * LANES] = piece.astype(o_ref.dtype)


def _rope_constants():
    lane = np.arange(LANES)
    d = lane % SWA_HEAD_DIM
    half = ROPE_DIM // 2
    inv_freq = np.float32(ROPE_THETA) ** (-np.arange(0, ROPE_DIM, 2, dtype=np.float32) / np.float32(ROPE_DIM))
    freq = np.where(d < ROPE_DIM, inv_freq[d % half], 0.0).astype(np.float32)
    sa = np.where(d < half, -1.0, 0.0).astype(np.float32)
    sb = np.where((d >= half) & (d < ROPE_DIM), 1.0, 0.0).astype(np.float32)
    return [jnp.asarray(a.reshape(1, LANES)) for a in (freq, sa, sb)]


def _inproj(x2, w, positions, tm):
    t, d = x2.shape
    n = w.shape[1]
    tm = min(tm, t)
    freq, sa, sb = _rope_constants()
    const = pl.BlockSpec((1, LANES), lambda i: (0, 0))
    return pl.pallas_call(
        _inproj_kernel,
        grid=(t // tm,),
        in_specs=[pl.BlockSpec((tm, d), lambda i: (i, 0)), pl.BlockSpec((d, n), lambda i: (0, 0)),
                  pl.BlockSpec((tm, 1), lambda i: (i, 0)), const, const, const],
        out_specs=pl.BlockSpec((tm, n), lambda i: (i, 0)),
        out_shape=jax.ShapeDtypeStruct((t, n), BF16),
        compiler_params=_params("parallel"),
        name="inproj_rope",
    )(x2, w, positions.reshape(t, 1), freq, sa, sb)


def _pool_kernel(v_ref, w_ref, scale_ref, o_ref, ext_ref, *, ts):
    s = pl.program_id(1)

    @pl.when(s == 0)
    def _():
        ext_ref[0:POOL_HALO, :] = jnp.zeros((POOL_HALO, ext_ref.shape[1]), F32)

    ext_ref[POOL_HALO:POOL_HALO + ts, :] = v_ref[0].astype(F32)
    pos = s * ts + lax.broadcasted_iota(jnp.int32, (ts, 1), 0)
    for g, w in enumerate(POOL_WINDOWS):
        cols = slice(g * POOL_GROUP, (g + 1) * POOL_GROUP)
        acc = ext_ref[POOL_HALO:POOL_HALO + ts, cols]
        for k in range(1, w):
            acc = acc + ext_ref[POOL_HALO - k:POOL_HALO - k + ts, cols]
        count = jnp.minimum(pos + 1, w).astype(F32)
        pooled = acc / count - ext_ref[POOL_HALO:POOL_HALO + ts, cols]
        y = jnp.dot(pooled.astype(BF16), w_ref[g], preferred_element_type=F32)
        o_ref[0, :, cols] = (y * scale_ref[:, cols]).astype(o_ref.dtype)
    ext_ref[0:POOL_HALO, :] = ext_ref[ts:ts + POOL_HALO, :]


def _pool(h, w_pool, pool_scale, ts):
    b, s, _ = h.shape
    width = POOL_GROUP * len(POOL_WINDOWS)
    ts = min(ts, s)
    return pl.pallas_call(
        functools.partial(_pool_kernel, ts=ts),
        grid=(b, s // ts),
        in_specs=[
            pl.BlockSpec((1, ts, width), lambda i, j: (i, j, 0)),
            pl.BlockSpec(w_pool.shape, lambda i, j: (0, 0, 0)),
            pl.BlockSpec((1, width), lambda i, j: (0, 0)),
        ],
        out_specs=pl.BlockSpec((1, ts, width), lambda i, j: (i, j, 0)),
        out_shape=jax.ShapeDtypeStruct((b, s, width), BF16),
        scratch_shapes=[pltpu.VMEM((ts + POOL_HALO, width), F32)],
        compiler_params=_params("arbitrary", "arbitrary"),
        name="pool",
    )(h, w_pool, pool_scale)


def _swa_kernel(sink_ref, q0_ref, q1_ref, kv_ref, kvp_ref, o_ref):
    n = pl.program_id(1)
    kvw = SWA_KV_HEADS * SWA_HEAD_DIM
    q = jnp.concatenate([q0_ref[0], q1_ref[0]], axis=1)
    k = jnp.concatenate([kvp_ref[0, :, 0:kvw], kv_ref[0, :, 0:kvw]], axis=0)
    v = jnp.concatenate([kvp_ref[0, :, kvw:2 * kvw], kv_ref[0, :, kvw:2 * kvw]], axis=0)
    row = lax.broadcasted_iota(jnp.int32, (SWA_BLOCK, 2 * SWA_BLOCK), 0)
    col = lax.broadcasted_iota(jnp.int32, (SWA_BLOCK, 2 * SWA_BLOCK), 1)
    rel = row + SWA_BLOCK - col
    valid = (rel >= 0) & (rel < SWA_BLOCK) & ((col >= SWA_BLOCK) | (n > 0))
    group = SWA_HEADS // SWA_KV_HEADS
    outs = []
    for hq in range(SWA_HEADS):
        kv = hq // group
        qh = q[:, hq * SWA_HEAD_DIM:(hq + 1) * SWA_HEAD_DIM]
        kh = k[:, kv * SWA_HEAD_DIM:(kv + 1) * SWA_HEAD_DIM]
        vh = v[:, kv * SWA_HEAD_DIM:(kv + 1) * SWA_HEAD_DIM]
        sc = lax.dot_general(qh, kh, (((1,), (1,)), ((), ())), preferred_element_type=F32)
        sc = jnp.where(valid, sc, NEG)
        sink = sink_ref[hq]
        m = jnp.maximum(jnp.max(sc, axis=1, keepdims=True), sink)
        p = jnp.exp(sc - m)
        denom = jnp.sum(p, axis=1, keepdims=True) + jnp.exp(sink - m)
        o = jnp.dot(p.astype(BF16), vh, preferred_element_type=F32)
        outs.append(o / denom)
    o_ref[0] = jnp.concatenate(outs, axis=1).astype(o_ref.dtype)


def _swa(h, sinks):
    b, s, _ = h.shape
    nb = s // SWA_BLOCK
    blk = lambda c: pl.BlockSpec((1, SWA_BLOCK, IN_BLOCK), lambda i, j: (i, j, c))
    blk_prev = lambda c: pl.BlockSpec((1, SWA_BLOCK, IN_BLOCK), lambda i, j: (i, jnp.maximum(j - 1, 0), c))
    return pl.pallas_call(
        _swa_kernel,
        grid=(b, nb),
        in_specs=[pl.BlockSpec(memory_space=pltpu.SMEM),
                  blk(_Q_BLOCKS[0]), blk(_Q_BLOCKS[1]), blk(_KV_BLOCK), blk_prev(_KV_BLOCK)],
        out_specs=pl.BlockSpec((1, SWA_BLOCK, SWA_HEADS * SWA_HEAD_DIM), lambda i, j: (i, j, 0)),
        out_shape=jax.ShapeDtypeStruct((b, s, SWA_HEADS * SWA_HEAD_DIM), BF16),
        compiler_params=_params("parallel", "arbitrary"),
        name="swa",
    )(sinks, h, h, h, h)


def _mem_kernel(q_ref, kv_ref, o_ref):
    scale = MEM_HEAD_DIM ** -0.5
    width = MEM_HEADS * MEM_HEAD_DIM
    for hm in range(MEM_HEADS):
        cols = slice(hm * MEM_HEAD_DIM, (hm + 1) * MEM_HEAD_DIM)
        km = kv_ref[0, :, cols]
        vm = kv_ref[0, :, width + hm * MEM_HEAD_DIM:width + (hm + 1) * MEM_HEAD_DIM]
        sc = lax.dot_general(q_ref[0, :, cols], km, (((1,), (1,)), ((), ())), preferred_element_type=F32) * scale
        m = jnp.max(sc, axis=1, keepdims=True)
        p = jnp.exp(sc - m)
        denom = jnp.sum(p, axis=1, keepdims=True)
        o = jnp.dot(p.astype(BF16), vm, preferred_element_type=F32)
        o_ref[0, :, cols] = (o / denom).astype(o_ref.dtype)


def _mem_attention(h, kvm, tq):
    b, s, _ = h.shape
    width = MEM_HEADS * MEM_HEAD_DIM
    tq = min(tq, s)
    return pl.pallas_call(
        _mem_kernel,
        grid=(b, s // tq),
        in_specs=[
            pl.BlockSpec((1, tq, width), lambda i, j: (i, j, 4)),
            pl.BlockSpec((1,) + kvm.shape[1:], lambda i, j: (i, 0, 0)),
        ],
        out_specs=pl.BlockSpec((1, tq, width), lambda i, j: (i, j, 0)),
        out_shape=jax.ShapeDtypeStruct((b, s, width), BF16),
        compiler_params=_params("parallel", "arbitrary"),
        name="mem_attention",
    )(h, kvm)


def _outproj_kernel(pool_ref, swa_ref, mem_ref, wp_ref, ws_ref, wm_ref, x_ref, g_ref, b_ref, o_ref, ob_ref,
                    *, alpha):
    mix = jnp.dot(pool_ref[...], wp_ref[...], preferred_element_type=F32)
    mix += jnp.dot(swa_ref[...], ws_ref[...], preferred_element_type=F32)
    mix += jnp.dot(mem_ref[...], wm_ref[...], preferred_element_type=F32)
    z = alpha * x_ref[...] + mix
    mu = jnp.mean(z, axis=1, keepdims=True)
    zc = z - mu
    var = jnp.mean(zc * zc, axis=1, keepdims=True)
    y = zc * lax.rsqrt(var + LN_EPS) * g_ref[...] + b_ref[...]
    yt = y.T
    o_ref[...] = yt
    ob_ref[...] = yt.astype(BF16)


def _outproj_ln(pool_o, swa_o, mem_o, w_out, x2, g, b, alpha, tm):
    t, d = x2.shape
    tm = min(tm, t)
    wp, ws, wm = pool_o.shape[1], swa_o.shape[1], mem_o.shape[1]
    w_p, w_s, w_m = w_out[:wp], w_out[wp:wp + ws], w_out[wp + ws:]
    row = lambda w: pl.BlockSpec((tm, w), lambda i: (i, 0))
    full = lambda a: pl.BlockSpec(a.shape, lambda i: (0, 0))
    return pl.pallas_call(
        functools.partial(_outproj_kernel, alpha=alpha),
        grid=(t // tm,),
        in_specs=[row(wp), row(ws), row(wm), full(w_p), full(w_s), full(w_m), row(d),
                  pl.BlockSpec((1, d), lambda i: (0, 0)), pl.BlockSpec((1, d), lambda i: (0, 0))],
        out_specs=[pl.BlockSpec((d, tm), lambda i: (0, i))] * 2,
        out_shape=[jax.ShapeDtypeStruct((d, t), F32), jax.ShapeDtypeStruct((d, t), BF16)],
        compiler_params=_params("parallel"),
        name="outproj_ln",
    )(pool_o, swa_o, mem_o, w_p, w_s, w_m, x2, g, b)


def _merge_exchange_pairs(n):
    t = (n - 1).bit_length()
    pairs = []
    p = 1 << (t - 1)
    while p >= 1:
        q, r, d = 1 << (t - 1), 0, p
        while True:
            pairs += [(i, i + d) for i in range(n - d) if (i & p) == r]
            if q == p:
                break
            d, q, r = q - p, q // 2, p
        p //= 2
    return pairs


def _pop_largest(lists, k):
    lists = list(lists)
    n = len(lists)
    for i, j in _merge_exchange_pairs(n):
        lists[i], lists[j] = jnp.maximum(lists[i], lists[j]), jnp.minimum(lists[i], lists[j])
    vals = []
    for r in range(k):
        m = jnp.max(lists[0], axis=0, keepdims=True)
        vals.append(m)
        pop = lists[0] == m
        for d in range(min(n - 1, k - 2 - r) + 1):
            below = lists[d + 1] if d + 1 < n else -jnp.inf
            lists[d] = jnp.where(pop, below, lists[d])
    return vals


def _top16_rows_distinct(s, with_rank):
    k = PEER_TOPK
    vals = _pop_largest([s[SUBLANES * g:SUBLANES * (g + 1), :] for g in range(s.shape[0] // SUBLANES)], k)
    def threshold(bits, lo, hi):
        if not bits:
            return vals[(lo + hi) // 2 - 1]
        mid = (lo + hi) // 2
        return jnp.where(bits[0], threshold(bits[1:], mid, hi), threshold(bits[1:], lo, mid))

    rank = None
    if with_rank:
        bits = []
        for _ in range(k.bit_length() - 1):
            bits.append(threshold(bits, 0, k) > s)
        rank = sum(jnp.where(b, float(k >> (n + 1)), 0.0) for n, b in enumerate(bits))
        rank = jnp.where(vals[k - 1] > s, float(k), rank)
    count = jnp.sum(jnp.where(s >= vals[k - 1], 1.0, 0.0), axis=0, keepdims=True)
    strict = jnp.zeros_like(count)
    for r in range(k - 1):
        strict = strict + jnp.where(vals[r] > vals[r + 1], 1.0, 0.0)
    distinct = (count == float(k)) & (strict == float(k - 1))
    return rank, vals, jnp.where(distinct, float(k), 0.0)


def _top16_rows(s, break_ties, with_rank=True):
    if not break_ties:
        return _top16_rows_distinct(s, with_rank)
    n = s.shape[0]
    iota = lax.broadcasted_iota(jnp.int32, s.shape, 0).astype(F32)
    rank = jnp.full(s.shape, float(PEER_TOPK), F32)
    vals = []
    for r in range(PEER_TOPK):
        m = jnp.max(s, axis=0, keepdims=True)
        idx = jnp.min(jnp.where(s == m, iota, float(n)), axis=0, keepdims=True)
        hit = iota == idx
        rank = jnp.where(hit, float(r), rank)
        s = jnp.where(hit, -jnp.inf, s)
        vals.append(m)
    count = jnp.sum(jnp.where(rank < float(PEER_TOPK), 1.0, 0.0), axis=0, keepdims=True)
    return rank, vals, count


_CAND_SMALL_A = PEER_TOPK // 2
_CAND_ROWS = PEER_TOPK + (_CAND_SMALL_A - 1) * SUBLANES + SUBLANES


def _cand_constants():
    flat = np.full((_CAND_ROWS, 1), 1e9, np.float32)
    valid = np.zeros((_CAND_ROWS, 1), np.float32)
    for b in range(PEER_TOPK):
        flat[b, 0], valid[b, 0] = b, 1.0
    for a in range(1, _CAND_SMALL_A):
        base = PEER_TOPK + (a - 1) * SUBLANES
        for b in range(PEER_TOPK // (a + 1)):
            flat[base + b, 0], valid[base + b, 0] = a * PEER_TOPK + b, 1.0
    base = PEER_TOPK + (_CAND_SMALL_A - 1) * SUBLANES
    for k in range(SUBLANES):
        flat[base + k, 0], valid[base + k, 0] = (_CAND_SMALL_A + k) * PEER_TOPK, 1.0
    return jnp.asarray(flat), jnp.asarray(valid)


def _select_experts(s1, s2, flat, valid, break_ties):
    t = s1.shape[1]
    rank1, v1, count1 = _top16_rows(s1, break_ties, with_rank=break_ties)
    rank2, v2, count2 = _top16_rows(s2, break_ties)
    v2_lo = jnp.concatenate(v2[:SUBLANES], axis=0)
    v2_all = jnp.concatenate(v2, axis=0)
    v1_hi = jnp.concatenate(v1[_CAND_SMALL_A:], axis=0)
    groups = [v1[0] + v2_all]
    for a in range(1, _CAND_SMALL_A):
        groups.append(v1[a] + v2_lo)
    groups.append(v1_hi + v2[0])
    cand = jnp.concatenate(groups, axis=0)
    cand = jnp.where(valid > 0.5, cand, -jnp.inf)
    if break_ties:
        flat_b = jnp.broadcast_to(flat, cand.shape)
        hits = jnp.zeros(cand.shape, F32)
        top = []
        for r in range(PEER_TOPK):
            m = jnp.max(cand, axis=0, keepdims=True)
            pick = jnp.min(jnp.where(cand == m, flat_b, 2e9), axis=0, keepdims=True)
            hit = flat_b == pick
            hits = jnp.where(hit, 1.0, hits)
            cand = jnp.where(hit, -jnp.inf, cand)
            top.append(m)
    else:
        top = _pop_largest([cand[SUBLANES * g:SUBLANES * (g + 1), :] for g in range(_CAND_ROWS // SUBLANES)],
                           PEER_TOPK)
        hits = jnp.where(cand >= top[PEER_TOPK - 1], 1.0, 0.0)
    z = jnp.ones((1, t), F32)
    for r in range(1, PEER_TOPK):
        z = z + jnp.exp(top[r] - top[0])
    counts = [jnp.sum(hits[0:PEER_TOPK], axis=0, keepdims=True)]
    for a in range(1, _CAND_SMALL_A):
        base = PEER_TOPK + (a - 1) * SUBLANES
        counts.append(jnp.sum(hits[base:base + SUBLANES], axis=0, keepdims=True))
    base = PEER_TOPK + (_CAND_SMALL_A - 1) * SUBLANES
    for k in range(SUBLANES):
        counts.append(hits[base + k:base + k + 1])
    lim = jnp.zeros(s1.shape, F32)
    if rank1 is not None:
        for a in range(PEER_TOPK):
            lim = jnp.where(rank1 == float(a), counts[a], lim)
    else:
        for a in range(PEER_TOPK):
            lim = jnp.where(s1 == v1[a], counts[a], lim)
    e1n = jnp.exp(s1 - v1[0]) * (1.0 / z)
    e2 = jnp.exp(s2 - v2[0])
    count3 = jnp.sum(hits, axis=0, keepdims=True)
    k = float(PEER_TOPK)
    ok = jnp.where((count1 == k) & (count2 == k) & (count3 == k), 1.0, 0.0)
    return lim, e1n, rank2, e2, ok


def _retrieve_kernel(wq_ref, x_ref, k1_ref, k2_ref, flat_ref, valid_ref,
                     lim_out_ref, e1_out_ref, r2_ref, e2_ref, q_ref, lim_ref, e1_ref, *, tq):
    q_ref[...] = jnp.dot(wq_ref[...], x_ref[...], preferred_element_type=F32)
    flat, valid = flat_ref[...], valid_ref[...]

    def head(h, carry):
        for c in range(tq // SELECT_TOKENS):
            lanes = slice(c * SELECT_TOKENS, (c + 1) * SELECT_TOKENS)
            r0 = pl.multiple_of(h * 2 * HALF_DIM, 2 * HALF_DIM)
            q1 = q_ref[pl.ds(r0, HALF_DIM), lanes].astype(BF16)
            q2 = q_ref[pl.ds(r0 + HALF_DIM, HALF_DIM), lanes].astype(BF16)
            s1 = jnp.dot(k1_ref[...], q1, preferred_element_type=F32)
            s2 = jnp.dot(k2_ref[...], q2, preferred_element_type=F32)

            def emit(break_ties):
                lim, e1n, rank2, e2, ok = _select_experts(s1, s2, flat, valid, break_ties)
                lim_ref[h, :, lanes] = lim
                e1_ref[h, :, lanes] = e1n
                r2_ref[h, :, lanes] = rank2.astype(BF16)
                e2_ref[h, :, lanes] = e2.astype(BF16)
                return ok

            ok = emit(False)

            @pl.when(jnp.min(ok) < 0.5)
            def _():
                emit(True)
        return carry

    lax.fori_loop(0, PEER_HEADS, head, 0)
    lim_out_ref[...] = jnp.swapaxes(lim_ref[...], 0, 1)
    e1_out_ref[...] = jnp.swapaxes(e1_ref[...], 0, 1)


def _retrieve(x1t, wq_t, k1, k2, tq):
    d, t = x1t.shape
    tq = min(tq, t)
    flat, valid = _cand_constants()
    full = lambda a: pl.BlockSpec(a.shape, lambda i: (0,) * a.ndim)
    out_keys = jax.ShapeDtypeStruct((N_KEYS, PEER_HEADS, t), F32)
    key_spec = pl.BlockSpec((N_KEYS, PEER_HEADS, tq), lambda i: (0, 0, i))
    out_packed = jax.ShapeDtypeStruct((PEER_HEADS, N_KEYS, t), BF16)
    out_spec = pl.BlockSpec((PEER_HEADS, N_KEYS, tq), lambda i: (0, 0, i))
    return pl.pallas_call(
        functools.partial(_retrieve_kernel, tq=tq),
        grid=(t // tq,),
        in_specs=[full(wq_t), pl.BlockSpec((d, tq), lambda i: (0, i)), full(k1), full(k2), full(flat), full(valid)],
        out_specs=[key_spec, key_spec, out_spec, out_spec],
        out_shape=[out_keys, out_keys, out_packed, out_packed],
        scratch_shapes=[pltpu.VMEM((wq_t.shape[0], tq), F32)] + [pltpu.VMEM((PEER_HEADS, N_KEYS, tq), F32)] * 2,
        compiler_params=_params("parallel"),
        name="peer_retrieve",
    )(wq_t, x1t, k1, k2, flat, valid)


def _transpose_cast_kernel(v_ref, o_ref):
    o_ref[...] = v_ref[...].T.astype(o_ref.dtype)


def _transpose_cast(v, rows):
    n, d = v.shape
    rows = min(rows, n)
    return pl.pallas_call(
        _transpose_cast_kernel,
        grid=(n // rows,),
        in_specs=[pl.BlockSpec((rows, d), lambda i: (i, 0))],
        out_specs=pl.BlockSpec((d, rows), lambda i: (0, i)),
        out_shape=jax.ShapeDtypeStruct((d, n), BF16),
        compiler_params=_params("parallel"),
        name="transpose_cast",
    )(v)


def _gelu_exact(x):
    return 0.5 * x * (1.0 + lax.erf(x * (1.0 / math.sqrt(2.0))))


def _experts_kernel(xb_ref, u_ref, vt_ref, lim_ref, e1_ref, r2_in_ref, e2_in_ref, o_ref,
                    h_ref, a_ref, rows_ref, gate_ref, r2_ref, e2_ref, *, tm, te):
    e = pl.program_id(1)

    @pl.when(e == 0)
    def _():
        o_ref[...] = jnp.zeros(o_ref.shape, F32)
        r2_ref[:, 0:tm] = r2_in_ref[...]
        e2_ref[:, LANES:LANES + tm] = e2_in_ref[...]

    n_chunks = te // EXPERT_CHUNK

    def rows_of(p):
        return slice(p * EXPERT_CHUNK, (p + 1) * EXPERT_CHUNK)

    def gates(first_key):
        keys = range(first_key, first_key + GATE_KEYS)
        blocks = range(0, N_KEYS, GATE_ROWS)
        for i in keys:
            for hd in range(PEER_HEADS):
                for q, ref in enumerate((lim_ref, e1_ref)):
                    row = jnp.broadcast_to(ref[i, hd:hd + 1, :], (GATE_ROWS, tm)).astype(BF16)
                    rows_ref[q, i, hd, :, q * LANES:q * LANES + tm] = row
        for c in range(tm // LANES):
            lanes = slice(c * LANES, (c + 1) * LANES)
            lanes1 = slice((c + 1) * LANES, (c + 2) * LANES)
            gate = {i: {jb: jnp.zeros((GATE_ROWS, LANES), BF16) for jb in blocks} for i in keys}
            for hd in range(PEER_HEADS):
                lim = {i: rows_ref[0, i, hd, :, lanes] for i in keys}
                e1 = {i: rows_ref[1, i, hd, :, lanes1] for i in keys}
                for jb in blocks:
                    r2 = r2_ref[hd * N_KEYS + jb:hd * N_KEYS + jb + GATE_ROWS, lanes]
                    e2 = e2_ref[hd * N_KEYS + jb:hd * N_KEYS + jb + GATE_ROWS, lanes1]
                    for i in keys:
                        gate[i][jb] = gate[i][jb] + jnp.where(r2 < lim[i], e2 * e1[i], jnp.zeros_like(e2))
            for i in keys:
                for jb in blocks:
                    gate_ref[i * N_KEYS + jb:i * N_KEYS + jb + GATE_ROWS, lanes] = gate[i][jb]

    for first_key in range(0, te // N_KEYS, GATE_KEYS):
        gates(first_key)
    h_ref[...] = jnp.dot(u_ref[...], xb_ref[...], preferred_element_type=F32)
    for p in range(n_chunks):
        a_ref[rows_of(p), :] = gate_ref[rows_of(p), :] * _gelu_exact(h_ref[rows_of(p), :]).astype(BF16)
        o_ref[...] += jnp.dot(vt_ref[:, rows_of(p)], a_ref[rows_of(p), :], preferred_element_type=F32)


def _experts(xbt, u, vt, sel, tm, te):
    d, t = xbt.shape
    n_exp = u.shape[0]
    tm, te = min(tm, t), min(te, n_exp)
    tok = pl.BlockSpec((d, tm), lambda i, j: (0, i))
    sel_spec = pl.BlockSpec((PEER_HEADS * N_KEYS, tm), lambda i, j: (0, i))
    key_spec = pl.BlockSpec((te // N_KEYS, PEER_HEADS, tm), lambda i, j: (j, 0, i))
    lim, e1n, rank2, e2 = sel
    sel = (lim, e1n, rank2.reshape(-1, t), e2.reshape(-1, t))
    return pl.pallas_call(
        functools.partial(_experts_kernel, tm=tm, te=te),
        grid=(t // tm, n_exp // te),
        in_specs=[tok, pl.BlockSpec((te, d), lambda i, j: (j, 0)), pl.BlockSpec((d, te), lambda i, j: (0, j)),
                  key_spec, key_spec, sel_spec, sel_spec],
        out_specs=tok,
        out_shape=jax.ShapeDtypeStruct((d, t), F32),
        scratch_shapes=[pltpu.VMEM((te, tm), F32), pltpu.VMEM((te, tm), BF16),
                        pltpu.VMEM((2, te // N_KEYS, PEER_HEADS, GATE_ROWS, tm + LANES), BF16),
                        pltpu.VMEM((te, tm), BF16),
                        pltpu.VMEM((PEER_HEADS * N_KEYS, tm + LANES), BF16),
                        pltpu.VMEM((PEER_HEADS * N_KEYS, tm + LANES), BF16)],
        compiler_params=_params("parallel", "arbitrary"),
        name="peer_experts",
    )(xbt, u, vt, *sel)


def _ln_t_kernel(x_ref, f_ref, g_ref, b_ref, o_ref, *, alpha):
    z = (alpha * x_ref[...] + f_ref[...]).T
    mu = jnp.mean(z, axis=1, keepdims=True)
    zc = z - mu
    var = jnp.mean(zc * zc, axis=1, keepdims=True)
    o_ref[...] = zc * lax.rsqrt(var + LN_EPS) * g_ref[...] + b_ref[...]


def _residual_ln_t(x1t, fft, g, b, alpha, tm):
    d, t = x1t.shape
    tm = min(tm, t)
    tok = pl.BlockSpec((d, tm), lambda i: (0, i))
    vec = pl.BlockSpec((1, d), lambda i: (0, 0))
    return pl.pallas_call(
        functools.partial(_ln_t_kernel, alpha=alpha),
        grid=(t // tm,),
        in_specs=[tok, tok, vec, vec],
        out_specs=pl.BlockSpec((tm, d), lambda i: (i, 0)),
        out_shape=jax.ShapeDtypeStruct((t, d), F32),
        compiler_params=_params("parallel"),
        name="residual_ln",
    )(x1t, fft, g, b)


def kernel(x, mem, positions, w_in, w_mem_kv, w_pool, pool_scale, attn_sinks, w_out, ln1_g, ln1_b,
           w_peer_q, sub_keys_1, sub_keys_2, expert_u, expert_v, ln2_g, ln2_b):
    bsz, seq, d = x.shape
    depth = w_in.shape[0]
    t = bsz * seq
    alpha = (2.0 * depth) ** 0.25
    for l in range(depth):
        x2 = x.reshape(t, d)
        h = _inproj(x2, w_in[l].astype(BF16), positions, 512).reshape(bsz, seq, -1)
        mem2 = mem.reshape(-1, d).astype(BF16)
        kvm = _matmul(mem2, w_mem_kv[l].astype(BF16), BF16, 512, 512).reshape(bsz, mem.shape[1], -1)
        pool_o = _pool(h, w_pool[l].astype(BF16), pool_scale[l].reshape(1, -1), 512)
        swa_o = _swa(h, attn_sinks[l])
        mem_o = _mem_attention(h, kvm, 512)
        x1t, x1bt = _outproj_ln(pool_o.reshape(t, -1), swa_o.reshape(t, -1), mem_o.reshape(t, -1),
                          w_out[l].astype(BF16), x2, ln1_g[l].reshape(1, d), ln1_b[l].reshape(1, d), alpha, 512)
        sel = _retrieve(x1bt, _transpose_cast(w_peer_q[l], 512), sub_keys_1[l].astype(BF16),
                        sub_keys_2[l].astype(BF16), 512)
        fft = _experts(x1bt, expert_u[l].astype(BF16), _transpose_cast(expert_v[l], 512), sel, 1024, 512)
        x = _residual_ln_t(x1t, fft, ln2_g[l].reshape(1, d), ln2_b[l].reshape(1, d), alpha, 512).reshape(bsz, seq, d)
    return x
```

```python
import functools
import math

import numpy as np
import jax
import jax.numpy as jnp
from jax import lax
from jax.experimental import pallas as pl
from jax.experimental.pallas import tpu as pltpu

F32 = jnp.float32
BF16 = jnp.bfloat16

LANES = 128
SUBLANES = 8
VMEM_LIMIT_BYTES = 56 * 1024 * 1024

POOL_WINDOWS = (2, 4, 8, 16)
POOL_GROUP = 128
POOL_HALO = 16
SWA_HEAD_DIM = 64
SWA_HEADS = 16
SWA_KV_HEADS = 4
SWA_BLOCK = 128
ROPE_THETA = 500000.0
ROPE_DIM = 16
MEM_HEADS = 4
MEM_HEAD_DIM = 128
PEER_HEADS = 8
N_KEYS = 128
PEER_TOPK = 16
HALF_DIM = 128
LN_EPS = 1e-5
NEG = -1e30

EXPERT_CHUNK = 512
GATE_KEYS = 4
GATE_ROWS = 16
SELECT_TOKENS = 512


def _params(*semantics):
    return pltpu.CompilerParams(dimension_semantics=semantics, vmem_limit_bytes=VMEM_LIMIT_BYTES)


def _matmul_kernel(a_ref, b_ref, o_ref):
    o_ref[...] = jnp.dot(a_ref[...], b_ref[...], preferred_element_type=F32).astype(o_ref.dtype)


def _matmul(a, b, out_dtype, tm, tn):
    m, k = a.shape
    n = b.shape[1]
    tm, tn = min(tm, m), min(tn, n)
    return pl.pallas_call(
        _matmul_kernel,
        grid=(m // tm, n // tn),
        in_specs=[pl.BlockSpec((tm, k), lambda i, j: (i, 0)), pl.BlockSpec((k, tn), lambda i, j: (0, j))],
        out_specs=pl.BlockSpec((tm, tn), lambda i, j: (i, j)),
        out_shape=jax.ShapeDtypeStruct((m, n), out_dtype),
        compiler_params=_params("parallel", "arbitrary"),
        name="matmul",
    )(a, b)


IN_BLOCK = 512
_Q_BLOCKS = (1, 2)
_KV_BLOCK = 3


def _rope(x, c, sa, sb):
    half = ROPE_DIM // 2
    return x * c + pltpu.roll(x, LANES - half, 1) * sa + pltpu.roll(x, half, 1) * sb


def _inproj_kernel(x_ref, w_ref, pos_ref, freq_ref, sa_ref, sb_ref, o_ref):
    xb = x_ref[...].astype(BF16)
    ang = pos_ref[...].astype(F32) * freq_ref[...]
    s = jnp.sin(ang)
    c, sa, sb = jnp.cos(ang), s * sa_ref[...], s * sb_ref[...]
    for j in range(w_ref.shape[1] // IN_BLOCK):
        y = jnp.dot(xb, w_ref[:, j * IN_BLOCK:(j + 1) * IN_BLOCK], preferred_element_type=F32)
        for k in range(IN_BLOCK // LANES):
            piece = y[:, k * LANES:(k + 1) * LANES]
            if j in _Q_BLOCKS:
                piece = _rope(piece, c, sa, sb) * SWA_HEAD_DIM ** -0.5
            elif j == _KV_BLOCK and k < IN_BLOCK // LANES // 2:
                piece = _rope(piece, c, sa, sb)
            o_ref[:, j * IN_BLOCK + k * LANES:j * IN_BLOCK + (k + 1) * LANES] = piece.astype(o_ref.dtype)


def _rope_constants():
    lane = np.arange(LANES)
    d = lane % SWA_HEAD_DIM
    half = ROPE_DIM // 2
    inv_freq = np.float32(ROPE_THETA) ** (-np.arange(0, ROPE_DIM, 2, dtype=np.float32) / np.float32(ROPE_DIM))
    freq = np.where(d < ROPE_DIM, inv_freq[d % half], 0.0).astype(np.float32)
    sa = np.where(d < half, -1.0, 0.0).astype(np.float32)
    sb = np.where((d >= half) & (d < ROPE_DIM), 1.0, 0.0).astype(np.float32)
    return [jnp.asarray(a.reshape(1, LANES)) for a in (freq, sa, sb)]


def _inproj(x2, w, positions, tm):
    t, d = x2.shape
    n = w.shape[1]
    tm = min(tm, t)
    freq, sa, sb = _rope_constants()
    const = pl.BlockSpec((1, LANES), lambda i: (0, 0))
    return pl.pallas_call(
        _inproj_kernel,
        grid=(t // tm,),
        in_specs=[pl.BlockSpec((tm, d), lambda i: (i, 0)), pl.BlockSpec((d, n), lambda i: (0, 0)),
                  pl.BlockSpec((tm, 1), lambda i: (i, 0)), const, const, const],
        out_specs=pl.BlockSpec((tm, n), lambda i: (i, 0)),
        out_shape=jax.ShapeDtypeStruct((t, n), BF16),
        compiler_params=_params("parallel"),
        name="inproj_rope",
    )(x2, w, positions.reshape(t, 1), freq, sa, sb)


def _pool_kernel(v_ref, w_ref, scale_ref, o_ref, ext_ref, *, ts):
    s = pl.program_id(1)

    @pl.when(s == 0)
    def _():
        ext_ref[0:POOL_HALO, :] = jnp.zeros((POOL_HALO, ext_ref.shape[1]), F32)

    ext_ref[POOL_HALO:POOL_HALO + ts, :] = v_ref[0].astype(F32)
    pos = s * ts + lax.broadcasted_iota(jnp.int32, (ts, 1), 0)
    for g, w in enumerate(POOL_WINDOWS):
        cols = slice(g * POOL_GROUP, (g + 1) * POOL_GROUP)
        acc = ext_ref[POOL_HALO:POOL_HALO + ts, cols]
        for k in range(1, w):
            acc = acc + ext_ref[POOL_HALO - k:POOL_HALO - k + ts, cols]
        count = jnp.minimum(pos + 1, w).astype(F32)
        pooled = acc / count - ext_ref[POOL_HALO:POOL_HALO + ts, cols]
        y = jnp.dot(pooled.astype(BF16), w_ref[g], preferred_element_type=F32)
        o_ref[0, :, cols] = (y * scale_ref[:, cols]).astype(o_ref.dtype)
    ext_ref[0:POOL_HALO, :] = ext_ref[ts:ts + POOL_HALO, :]


def _pool(h, w_pool, pool_scale, ts):
    b, s, _ = h.shape
    width = POOL_GROUP * len(POOL_WINDOWS)
    ts = min(ts, s)
    return pl.pallas_call(
        functools.partial(_pool_kernel, ts=ts),
        grid=(b, s // ts),
        in_specs=[
            pl.BlockSpec((1, ts, width), lambda i, j: (i, j, 0)),
            pl.BlockSpec(w_pool.shape, lambda i, j: (0, 0, 0)),
            pl.BlockSpec((1, width), lambda i, j: (0, 0)),
        ],
        out_specs=pl.BlockSpec((1, ts, width), lambda i, j: (i, j, 0)),
        out_shape=jax.ShapeDtypeStruct((b, s, width), BF16),
        scratch_shapes=[pltpu.VMEM((ts + POOL_HALO, width), F32)],
        compiler_params=_params("arbitrary", "arbitrary"),
        name="pool",
    )(h, w_pool, pool_scale)


def _swa_kernel(sink_ref, q0_ref, q1_ref, kv_ref, kvp_ref, o_ref):
    n = pl.program_id(1)
    kvw = SWA_KV_HEADS * SWA_HEAD_DIM
    q = jnp.concatenate([q0_ref[0], q1_ref[0]], axis=1)
    k = jnp.concatenate([kvp_ref[0, :, 0:kvw], kv_ref[0, :, 0:kvw]], axis=0)
    v = jnp.concatenate([kvp_ref[0, :, kvw:2 * kvw], kv_ref[0, :, kvw:2 * kvw]], axis=0)
    row = lax.broadcasted_iota(jnp.int32, (SWA_BLOCK, 2 * SWA_BLOCK), 0)
    col = lax.broadcasted_iota(jnp.int32, (SWA_BLOCK, 2 * SWA_BLOCK), 1)
    rel = row + SWA_BLOCK - col
    valid = (rel >= 0) & (rel < SWA_BLOCK) & ((col >= SWA_BLOCK) | (n > 0))
    group = SWA_HEADS // SWA_KV_HEADS
    outs = []
    for hq in range(SWA_HEADS):
        kv = hq // group
        qh = q[:, hq * SWA_HEAD_DIM:(hq + 1) * SWA_HEAD_DIM]
        kh = k[:, kv * SWA_HEAD_DIM:(kv + 1) * SWA_HEAD_DIM]
        vh = v[:, kv * SWA_HEAD_DIM:(kv + 1) * SWA_HEAD_DIM]
        sc = lax.dot_general(qh, kh, (((1,), (1,)), ((), ())), preferred_element_type=F32)
        sc = jnp.where(valid, sc, NEG)
        sink = sink_ref[hq]
        m = jnp.maximum(jnp.max(sc, axis=1, keepdims=True), sink)
        p = jnp.exp(sc - m)
        denom = jnp.sum(p, axis=1, keepdims=True) + jnp.exp(sink - m)
        o = jnp.dot(p.astype(BF16), vh, preferred_element_type=F32)
        outs.append(o / denom)
    o_ref[0] = jnp.concatenate(outs, axis=1).astype(o_ref.dtype)


def _swa(h, sinks):
    b, s, _ = h.shape
    nb = s // SWA_BLOCK
    blk = lambda c: pl.BlockSpec((1, SWA_BLOCK, IN_BLOCK), lambda i, j: (i, j, c))
    blk_prev = lambda c: pl.BlockSpec((1, SWA_BLOCK, IN_BLOCK), lambda i, j: (i, jnp.maximum(j - 1, 0), c))
    return pl.pallas_call(
        _swa_kernel,
        grid=(b, nb),
        in_specs=[pl.BlockSpec(memory_space=pltpu.SMEM),
                  blk(_Q_BLOCKS[0]), blk(_Q_BLOCKS[1]), blk(_KV_BLOCK), blk_prev(_KV_BLOCK)],
        out_specs=pl.BlockSpec((1, SWA_BLOCK, SWA_HEADS * SWA_HEAD_DIM), lambda i, j: (i, j, 0)),
        out_shape=jax.ShapeDtypeStruct((b, s, SWA_HEADS * SWA_HEAD_DIM), BF16),
        compiler_params=_params("parallel", "arbitrary"),
        name="swa",
    )(sinks, h, h, h, h)


def _mem_kernel(q_ref, kv_ref, o_ref):
    scale = MEM_HEAD_DIM ** -0.5
    width = MEM_HEADS * MEM_HEAD_DIM
    for hm in range(MEM_HEADS):
        cols = slice(hm * MEM_HEAD_DIM, (hm + 1) * MEM_HEAD_DIM)
        km = kv_ref[0, :, cols]
        vm = kv_ref[0, :, width + hm * MEM_HEAD_DIM:width + (hm + 1) * MEM_HEAD_DIM]
        sc = lax.dot_general(q_ref[0, :, cols], km, (((1,), (1,)), ((), ())), preferred_element_type=F32) * scale
        m = jnp.max(sc, axis=1, keepdims=True)
        p = jnp.exp(sc - m)
        denom = jnp.sum(p, axis=1, keepdims=True)
        o = jnp.dot(p.astype(BF16), vm, preferred_element_type=F32)
        o_ref[0, :, cols] = (o / denom).astype(o_ref.dtype)


def _mem_attention(h, kvm, tq):
    b, s, _ = h.shape
    width = MEM_HEADS * MEM_HEAD_DIM
    tq = min(tq, s)
    return pl.pallas_call(
        _mem_kernel,
        grid=(b, s // tq),
        in_specs=[
            pl.BlockSpec((1, tq, width), lambda i, j: (i, j, 4)),
            pl.BlockSpec((1,) + kvm.shape[1:], lambda i, j: (i, 0, 0)),
        ],
        out_specs=pl.BlockSpec((1, tq, width), lambda i, j: (i, j, 0)),
        out_shape=jax.ShapeDtypeStruct((b, s, width), BF16),
        compiler_params=_params("parallel", "arbitrary"),
        name="mem_attention",
    )(h, kvm)


def _pool_mem_kernel(v_ref, w_ref, scale_ref, q_ref, kv_ref, pool_o_ref, mem_o_ref, ext_ref, *, ts):
    _pool_kernel(v_ref, w_ref, scale_ref, pool_o_ref, ext_ref, ts=ts)
    _mem_kernel(q_ref, kv_ref, mem_o_ref)


def _pool_mem(h, w_pool, pool_scale, kvm, ts):
    b, s, _ = h.shape
    width = POOL_GROUP * len(POOL_WINDOWS)
    mem_width = MEM_HEADS * MEM_HEAD_DIM
    ts = min(ts, s)
    row = lambda w, c: pl.BlockSpec((1, ts, w), lambda i, j: (i, j, c))
    return pl.pallas_call(
        functools.partial(_pool_mem_kernel, ts=ts),
        grid=(b, s // ts),
        in_specs=[row(width, 0), pl.BlockSpec(w_pool.shape, lambda i, j: (0, 0, 0)),
                  pl.BlockSpec((1, width), lambda i, j: (0, 0)),
                  row(mem_width, 4),
                  pl.BlockSpec((1,) + kvm.shape[1:], lambda i, j: (i, 0, 0))],
        out_specs=[row(width, 0), row(mem_width, 0)],
        out_shape=[jax.ShapeDtypeStruct((b, s, width), BF16), jax.ShapeDtypeStruct((b, s, mem_width), BF16)],
        scratch_shapes=[pltpu.VMEM((ts + POOL_HALO, width), F32)],
        compiler_params=_params("arbitrary", "arbitrary"),
        name="pool_mem",
    )(h, w_pool, pool_scale, h, kvm)


def _outproj_kernel(pool_ref, swa_ref, mem_ref, wp_ref, ws_ref, wm_ref, x_ref, g_ref, b_ref, o_ref, ob_ref,
                    *, alpha):
    mix = jnp.dot(pool_ref[...], wp_ref[...], preferred_element_type=F32)
    mix += jnp.dot(swa_ref[...], ws_ref[...], preferred_element_type=F32)
    mix += jnp.dot(mem_ref[...], wm_ref[...], preferred_element_type=F32)
    z = alpha * x_ref[...] + mix
    mu = jnp.mean(z, axis=1, keepdims=True)
    zc = z - mu
    var = jnp.mean(zc * zc, axis=1, keepdims=True)
    y = zc * lax.rsqrt(var + LN_EPS) * g_ref[...] + b_ref[...]
    yt = y.T
    o_ref[...] = yt
    ob_ref[...] = yt.astype(BF16)


def _outproj_ln(pool_o, swa_o, mem_o, w_out, x2, g, b, alpha, tm):
    t, d = x2.shape
    tm = min(tm, t)
    wp, ws, wm = pool_o.shape[1], swa_o.shape[1], mem_o.shape[1]
    w_p, w_s, w_m = w_out[:wp], w_out[wp:wp + ws], w_out[wp + ws:]
    row = lambda w: pl.BlockSpec((tm, w), lambda i: (i, 0))
    full = lambda a: pl.BlockSpec(a.shape, lambda i: (0, 0))
    return pl.pallas_call(
        functools.partial(_outproj_kernel, alpha=alpha),
        grid=(t // tm,),
        in_specs=[row(wp), row(ws), row(wm), full(w_p), full(w_s), full(w_m), row(d),
                  pl.BlockSpec((1, d), lambda i: (0, 0)), pl.BlockSpec((1, d), lambda i: (0, 0))],
        out_specs=[pl.BlockSpec((d, tm), lambda i: (0, i))] * 2,
        out_shape=[jax.ShapeDtypeStruct((d, t), F32), jax.ShapeDtypeStruct((d, t), BF16)],
        compiler_params=_params("parallel"),
        name="outproj_ln",
    )(pool_o, swa_o, mem_o, w_p, w_s, w_m, x2, g, b)


def _merge_exchange_pairs(n):
    t = (n - 1).bit_length()
    pairs = []
    p = 1 << (t - 1)
    while p >= 1:
        q, r, d = 1 << (t - 1), 0, p
        while True:
            pairs += [(i, i + d) for i in range(n - d) if (i & p) == r]
            if q == p:
                break
            d, q, r = q - p, q // 2, p
        p //= 2
    return pairs


def _pop_largest(lists, k):
    lists = list(lists)
    n = len(lists)
    for i, j in _merge_exchange_pairs(n):
        lists[i], lists[j] = jnp.maximum(lists[i], lists[j]), jnp.minimum(lists[i], lists[j])
    vals = []
    for r in range(k):
        m = jnp.max(lists[0], axis=0, keepdims=True)
        vals.append(m)
        pop = lists[0] == m
        for d in range(min(n - 1, k - 2 - r) + 1):
            below = lists[d + 1] if d + 1 < n else -jnp.inf
            lists[d] = jnp.where(pop, below, lists[d])
    return vals


def _top16_rows_distinct(s, with_rank):
    k = PEER_TOPK
    vals = _pop_largest([s[SUBLANES * g:SUBLANES * (g + 1), :] for g in range(s.shape[0] // SUBLANES)], k)
    def threshold(bits, lo, hi):
        if not bits:
            return vals[(lo + hi) // 2 - 1]
        mid = (lo + hi) // 2
        return jnp.where(bits[0], threshold(bits[1:], mid, hi), threshold(bits[1:], lo, mid))

    rank = None
    if with_rank:
        bits = []
        for _ in range(k.bit_length() - 1):
            bits.append(threshold(bits, 0, k) > s)
        rank = sum(jnp.where(b, float(k >> (n + 1)), 0.0) for n, b in enumerate(bits))
        rank = jnp.where(vals[k - 1] > s, float(k), rank)
    count = jnp.sum(jnp.where(s >= vals[k - 1], 1.0, 0.0), axis=0, keepdims=True)
    strict = jnp.zeros_like(count)
    for r in range(k - 1):
        strict = strict + jnp.where(vals[r] > vals[r + 1], 1.0, 0.0)
    distinct = (count == float(k)) & (strict == float(k - 1))
    return rank, vals, jnp.where(distinct, float(k), 0.0)


def _top16_rows(s, break_ties, with_rank=True):
    if not break_ties:
        return _top16_rows_distinct(s, with_rank)
    n = s.shape[0]
    iota = lax.broadcasted_iota(jnp.int32, s.shape, 0).astype(F32)
    rank = jnp.full(s.shape, float(PEER_TOPK), F32)
    vals = []
    for r in range(PEER_TOPK):
        m = jnp.max(s, axis=0, keepdims=True)
        idx = jnp.min(jnp.where(s == m, iota, float(n)), axis=0, keepdims=True)
        hit = iota == idx
        rank = jnp.where(hit, float(r), rank)
        s = jnp.where(hit, -jnp.inf, s)
        vals.append(m)
    count = jnp.sum(jnp.where(rank < float(PEER_TOPK), 1.0, 0.0), axis=0, keepdims=True)
    return rank, vals, count


_CAND_SMALL_A = PEER_TOPK // 2
_CAND_ROWS = PEER_TOPK + (_CAND_SMALL_A - 1) * SUBLANES + SUBLANES


def _cand_constants():
    flat = np.full((_CAND_ROWS, 1), 1e9, np.float32)
    valid = np.zeros((_CAND_ROWS, 1), np.float32)
    for b in range(PEER_TOPK):
        flat[b, 0], valid[b, 0] = b, 1.0
    for a in range(1, _CAND_SMALL_A):
        base = PEER_TOPK + (a - 1) * SUBLANES
        for b in range(PEER_TOPK // (a + 1)):
            flat[base + b, 0], valid[base + b, 0] = a * PEER_TOPK + b, 1.0
    base = PEER_TOPK + (_CAND_SMALL_A - 1) * SUBLANES
    for k in range(SUBLANES):
        flat[base + k, 0], valid[base + k, 0] = (_CAND_SMALL_A + k) * PEER_TOPK, 1.0
    return jnp.asarray(flat), jnp.asarray(valid)


def _select_experts(s1, s2, flat, valid, break_ties):
    t = s1.shape[1]
    rank1, v1, count1 = _top16_rows(s1, break_ties, with_rank=break_ties)
    rank2, v2, count2 = _top16_rows(s2, break_ties)
    v2_lo = jnp.concatenate(v2[:SUBLANES], axis=0)
    v2_all = jnp.concatenate(v2, axis=0)
    v1_hi = jnp.concatenate(v1[_CAND_SMALL_A:], axis=0)
    groups = [v1[0] + v2_all]
    for a in range(1, _CAND_SMALL_A):
        groups.append(v1[a] + v2_lo)
    groups.append(v1_hi + v2[0])
    cand = jnp.concatenate(groups, axis=0)
    cand = jnp.where(valid > 0.5, cand, -jnp.inf)
    if break_ties:
        flat_b = jnp.broadcast_to(flat, cand.shape)
        hits = jnp.zeros(cand.shape, F32)
        top = []
        for r in range(PEER_TOPK):
            m = jnp.max(cand, axis=0, keepdims=True)
            pick = jnp.min(jnp.where(cand == m, flat_b, 2e9), axis=0, keepdims=True)
            hit = flat_b == pick
            hits = jnp.where(hit, 1.0, hits)
            cand = jnp.where(hit, -jnp.inf, cand)
            top.append(m)
    else:
        top = _pop_largest([cand[SUBLANES * g:SUBLANES * (g + 1), :] for g in range(_CAND_ROWS // SUBLANES)],
                           PEER_TOPK)
        hits = jnp.where(cand >= top[PEER_TOPK - 1], 1.0, 0.0)
    z = jnp.ones((1, t), F32)
    for r in range(1, PEER_TOPK):
        z = z + jnp.exp(top[r] - top[0])
    counts = [jnp.sum(hits[0:PEER_TOPK], axis=0, keepdims=True)]
    for a in range(1, _CAND_SMALL_A):
        base = PEER_TOPK + (a - 1) * SUBLANES
        counts.append(jnp.sum(hits[base:base + SUBLANES], axis=0, keepdims=True))
    base = PEER_TOPK + (_CAND_SMALL_A - 1) * SUBLANES
    for k in range(SUBLANES):
        counts.append(hits[base + k:base + k + 1])
    lim = jnp.zeros(s1.shape, F32)
    if rank1 is not None:
        for a in range(PEER_TOPK):
            lim = jnp.where(rank1 == float(a), counts[a], lim)
    else:
        for a in range(PEER_TOPK):
            lim = jnp.where(s1 == v1[a], counts[a], lim)
    e1n = jnp.exp(s1 - v1[0]) * (1.0 / z)
    e2 = jnp.exp(s2 - v2[0])
    count3 = jnp.sum(hits, axis=0, keepdims=True)
    k = float(PEER_TOPK)
    ok = jnp.where((count1 == k) & (count2 == k) & (count3 == k), 1.0, 0.0)
    return lim, e1n, rank2, e2, ok


def _retrieve_kernel(wq_ref, x_ref, k1_ref, k2_ref, flat_ref, valid_ref,
                     lim_out_ref, e1_out_ref, r2_ref, e2_ref, q_ref, lim_ref, e1_ref, *, tq):
    q_ref[...] = jnp.dot(wq_ref[...], x_ref[...], preferred_element_type=F32)
    flat, valid = flat_ref[...], valid_ref[...]

    def head(h, carry):
        for c in range(tq // SELECT_TOKENS):
            lanes = slice(c * SELECT_TOKENS, (c + 1) * SELECT_TOKENS)
            r0 = pl.multiple_of(h * 2 * HALF_DIM, 2 * HALF_DIM)
            q1 = q_ref[pl.ds(r0, HALF_DIM), lanes].astype(BF16)
            q2 = q_ref[pl.ds(r0 + HALF_DIM, HALF_DIM), lanes].astype(BF16)
            s1 = jnp.dot(k1_ref[...], q1, preferred_element_type=F32)
            s2 = jnp.dot(k2_ref[...], q2, preferred_element_type=F32)

            def emit(break_ties):
                lim, e1n, rank2, e2, ok = _select_experts(s1, s2, flat, valid, break_ties)
                lim_ref[h, :, lanes] = lim
                e1_ref[h, :, lanes] = e1n
                r2_ref[h, :, lanes] = rank2.astype(BF16)
                e2_ref[h, :, lanes] = e2.astype(BF16)
                return ok

            ok = emit(False)

            @pl.when(jnp.min(ok) < 0.5)
            def _():
                emit(True)
        return carry

    lax.fori_loop(0, PEER_HEADS, head, 0)
    lim_out_ref[...] = jnp.swapaxes(lim_ref[...], 0, 1)
    e1_out_ref[...] = jnp.swapaxes(e1_ref[...], 0, 1)


def _retrieve(x1t, wq_t, k1, k2, tq):
    d, t = x1t.shape
    tq = min(tq, t)
    flat, valid = _cand_constants()
    full = lambda a: pl.BlockSpec(a.shape, lambda i: (0,) * a.ndim)
    out_keys = jax.ShapeDtypeStruct((N_KEYS, PEER_HEADS, t), F32)
    key_spec = pl.BlockSpec((N_KEYS, PEER_HEADS, tq), lambda i: (0, 0, i))
    out_packed = jax.ShapeDtypeStruct((PEER_HEADS, N_KEYS, t), BF16)
    out_spec = pl.BlockSpec((PEER_HEADS, N_KEYS, tq), lambda i: (0, 0, i))
    return pl.pallas_call(
        functools.partial(_retrieve_kernel, tq=tq),
        grid=(t // tq,),
        in_specs=[full(wq_t), pl.BlockSpec((d, tq), lambda i: (0, i)), full(k1), full(k2), full(flat), full(valid)],
        out_specs=[key_spec, key_spec, out_spec, out_spec],
        out_shape=[out_keys, out_keys, out_packed, out_packed],
        scratch_shapes=[pltpu.VMEM((wq_t.shape[0], tq), F32)] + [pltpu.VMEM((PEER_HEADS, N_KEYS, tq), F32)] * 2,
        compiler_params=_params("parallel"),
        name="peer_retrieve",
    )(wq_t, x1t, k1, k2, flat, valid)


def _transpose_cast_kernel(v_ref, o_ref):
    o_ref[...] = v_ref[...].T.astype(o_ref.dtype)


def _transpose_cast(v, rows):
    n, d = v.shape
    rows = min(rows, n)
    return pl.pallas_call(
        _transpose_cast_kernel,
        grid=(n // rows,),
        in_specs=[pl.BlockSpec((rows, d), lambda i: (i, 0))],
        out_specs=pl.BlockSpec((d, rows), lambda i: (0, i)),
        out_shape=jax.ShapeDtypeStruct((d, n), BF16),
        compiler_params=_params("parallel"),
        name="transpose_cast",
    )(v)


def _gelu_exact(x):
    return 0.5 * x * (1.0 + lax.erf(x * (1.0 / math.sqrt(2.0))))


def _experts_kernel(xb_ref, u_ref, vt_ref, lim_ref, e1_ref, r2_in_ref, e2_in_ref, o_ref,
                    h_ref, a_ref, rows_ref, gate_ref, r2_ref, e2_ref, *, tm, te):
    e = pl.program_id(1)

    @pl.when(e == 0)
    def _():
        o_ref[...] = jnp.zeros(o_ref.shape, F32)
        r2_ref[:, 0:tm] = r2_in_ref[...]
        e2_ref[:, LANES:LANES + tm] = e2_in_ref[...]

    n_chunks = te // EXPERT_CHUNK

    def rows_of(p):
        return slice(p * EXPERT_CHUNK, (p + 1) * EXPERT_CHUNK)

    def gates(first_key):
        keys = range(first_key, first_key + GATE_KEYS)
        blocks = range(0, N_KEYS, GATE_ROWS)
        for i in keys:
            for hd in range(PEER_HEADS):
                for q, ref in enumerate((lim_ref, e1_ref)):
                    row = jnp.broadcast_to(ref[i, hd:hd + 1, :], (GATE_ROWS, tm)).astype(BF16)
                    rows_ref[q, i, hd, :, q * LANES:q * LANES + tm] = row
        for c in range(tm // LANES):
            lanes = slice(c * LANES, (c + 1) * LANES)
            lanes1 = slice((c + 1) * LANES, (c + 2) * LANES)
            gate = {i: {jb: jnp.zeros((GATE_ROWS, LANES), BF16) for jb in blocks} for i in keys}
            for hd in range(PEER_HEADS):
                lim = {i: rows_ref[0, i, hd, :, lanes] for i in keys}
                e1 = {i: rows_ref[1, i, hd, :, lanes1] for i in keys}
                for jb in blocks:
                    r2 = r2_ref[hd * N_KEYS + jb:hd * N_KEYS + jb + GATE_ROWS, lanes]
                    e2 = e2_ref[hd * N_KEYS + jb:hd * N_KEYS + jb + GATE_ROWS, lanes1]
                    for i in keys:
                        gate[i][jb] = gate[i][jb] + jnp.where(r2 < lim[i], e2 * e1[i], jnp.zeros_like(e2))
            for i in keys:
                for jb in blocks:
                    gate_ref[i * N_KEYS + jb:i * N_KEYS + jb + GATE_ROWS, lanes] = gate[i][jb]

    for first_key in range(0, te // N_KEYS, GATE_KEYS):
        gates(first_key)
    h_ref[...] = jnp.dot(u_ref[...], xb_ref[...], preferred_element_type=F32)
    for p in range(n_chunks):
        a_ref[rows_of(p), :] = gate_ref[rows_of(p), :] * _gelu_exact(h_ref[rows_of(p), :]).astype(BF16)
        o_ref[...] += jnp.dot(vt_ref[:, rows_of(p)], a_ref[rows_of(p), :], preferred_element_type=F32)


def _experts(xbt, u, vt, sel, tm, te):
    d, t = xbt.shape
    n_exp = u.shape[0]
    tm, te = min(tm, t), min(te, n_exp)
    tok = pl.BlockSpec((d, tm), lambda i, j: (0, i))
    sel_spec = pl.BlockSpec((PEER_HEADS * N_KEYS, tm), lambda i, j: (0, i))
    key_spec = pl.BlockSpec((te // N_KEYS, PEER_HEADS, tm), lambda i, j: (j, 0, i))
    lim, e1n, rank2, e2 = sel
    sel = (lim, e1n, rank2.reshape(-1, t), e2.reshape(-1, t))
    return pl.pallas_call(
        functools.partial(_experts_kernel, tm=tm, te=te),
        grid=(t // tm, n_exp // te),
        in_specs=[tok, pl.BlockSpec((te, d), lambda i, j: (j, 0)), pl.BlockSpec((d, te), lambda i, j: (0, j)),
                  key_spec, key_spec, sel_spec, sel_spec],
        out_specs=tok,
        out_shape=jax.ShapeDtypeStruct((d, t), F32),
        scratch_shapes=[pltpu.VMEM((te, tm), F32), pltpu.VMEM((te, tm), BF16),
                        pltpu.VMEM((2, te // N_KEYS, PEER_HEADS, GATE_ROWS, tm + LANES), BF16),
                        pltpu.VMEM((te, tm), BF16),
                        pltpu.VMEM((PEER_HEADS * N_KEYS, tm + LANES), BF16),
                        pltpu.VMEM((PEER_HEADS * N_KEYS, tm + LANES), BF16)],
        compiler_params=_params("parallel", "arbitrary"),
        name="peer_experts",
    )(xbt, u, vt, *sel)


def _ln_t_kernel(x_ref, f_ref, g_ref, b_ref, o_ref, *, alpha):
    z = (alpha * x_ref[...] + f_ref[...]).T
    mu = jnp.mean(z, axis=1, keepdims=True)
    zc = z - mu
    var = jnp.mean(zc * zc, axis=1, keepdims=True)
    o_ref[...] = zc * lax.rsqrt(var + LN_EPS) * g_ref[...] + b_ref[...]


def _residual_ln_t(x1t, fft, g, b, alpha, tm):
    d, t = x1t.shape
    tm = min(tm, t)
    tok = pl.BlockSpec((d, tm), lambda i: (0, i))
    vec = pl.BlockSpec((1, d), lambda i: (0, 0))
    return pl.pallas_call(
        functools.partial(_ln_t_kernel, alpha=alpha),
        grid=(t // tm,),
        in_specs=[tok, tok, vec, vec],
        out_specs=pl.BlockSpec((tm, d), lambda i: (i, 0)),
        out_shape=jax.ShapeDtypeStruct((t, d), F32),
        compiler_params=_params("parallel"),
        name="residual_ln",
    )(x1t, fft, g, b)


def kernel(x, mem, positions, w_in, w_mem_kv, w_pool, pool_scale, attn_sinks, w_out, ln1_g, ln1_b,
           w_peer_q, sub_keys_1, sub_keys_2, expert_u, expert_v, ln2_g, ln2_b):
    bsz, seq, d = x.shape
    depth = w_in.shape[0]
    t = bsz * seq
    alpha = (2.0 * depth) ** 0.25
    for l in range(depth):
        x2 = x.reshape(t, d)
        h = _inproj(x2, w_in[l].astype(BF16), positions, 512).reshape(bsz, seq, -1)
        mem2 = mem.reshape(-1, d).astype(BF16)
        kvm = _matmul(mem2, w_mem_kv[l].astype(BF16), BF16, 512, 512).reshape(bsz, mem.shape[1], -1)
        pool_o, mem_o = _pool_mem(h, w_pool[l].astype(BF16), pool_scale[l].reshape(1, -1), kvm, 512)
        swa_o = _swa(h, attn_sinks[l])
        x1t, x1bt = _outproj_ln(pool_o.reshape(t, -1), swa_o.reshape(t, -1), mem_o.reshape(t, -1),
                          w_out[l].astype(BF16), x2, ln1_g[l].reshape(1, d), ln1_b[l].reshape(1, d), alpha, 512)
        sel = _retrieve(x1bt, _transpose_cast(w_peer_q[l], 512), sub_keys_1[l].astype(BF16),
                        sub_keys_2[l].astype(BF16), 512)
        fft = _experts(x1bt, expert_u[l].astype(BF16), _transpose_cast(expert_v[l], 512), sel, 1024, 512)
        x = _residual_ln_t(x1t, fft, ln2_g[l].reshape(1, d), ln2_b[l].reshape(1, d), alpha, 512).reshape(bsz, seq, d)
    return x
```
